```python
import math
import jax, jax.numpy as jnp
from jax import lax
import numpy as np

D_MODEL = 1024
BATCH = 16
SEQ = 4096
DEPTH = 1

CHUNK = 64
Q_BLOCK = 128
SSD_INNER = D_MODEL // 2
SSD_HEAD_DIM = 64
SSD_HEADS = SSD_INNER // SSD_HEAD_DIM
SSD_GROUPS = 2
SSD_STATE = 128
SSD_CONV = 4
SSD_BC = SSD_GROUPS * SSD_STATE
XBC_DIM = SSD_INNER + 2 * SSD_BC
MLA_V = 64
MLA_HEADS = (D_MODEL - SSD_INNER) // MLA_V
MLA_NOPE = 64
MLA_ROPE = 32
Q_LORA = 384
KV_LORA = 256
ROPE_BASE = 10000.0
MIX_WIDTH = SSD_INNER + MLA_HEADS * MLA_V
OFF_Z = 0
OFF_XBC = OFF_Z + SSD_INNER
OFF_DT = OFF_XBC + XBC_DIM
OFF_QA = OFF_DT + SSD_HEADS
OFF_KVA = OFF_QA + Q_LORA
OFF_KR = OFF_KVA + KV_LORA
IN_COLS = OFF_KR + MLA_ROPE
N_EXPERTS = 32
TOP_K = 4
D_FF_EXPERT = 1024
SWIGLU_LIMIT = 7.0
SWIGLU_ALPHA = 1.702
MOE_BLOCK = 256
NORM_EPS = 1e-6
MAX_STREAM_OFFSET = 16384

kernel_name = 'hybrid_ssd_mla_moe_stream_block'


def rms_norm(x, gain):
    xf = x.astype(jnp.float32)
    y = xf * lax.rsqrt(jnp.mean(xf * xf, axis=-1, keepdims=True) + NORM_EPS)
    return (y * gain.astype(jnp.float32)).astype(x.dtype)


def causal_depthwise_conv(x, w, b):
    k = w.shape[0]
    y = lax.conv_general_dilated(x, w[:, None, :], window_strides=(1,), padding=[(k - 1, 0)],
                                 dimension_numbers=('NWC', 'WIO', 'NWC'),
                                 feature_group_count=x.shape[-1])
    return y + b


def apply_rope(x, cos, sin):
    half = x.shape[-1] // 2
    x1, x2 = x[..., :half], x[..., half:]
    return jnp.concatenate([x1 * cos - x2 * sin, x2 * cos + x1 * sin], axis=-1)


def ssd_mixer(z, xbc, dt_raw, conv_w, conv_b, dt_bias, a_log, d_skip, norm_gain):
    f32 = jnp.float32
    bsz, seq, _ = xbc.shape
    nc = seq // CHUNK
    G, R, P, N = SSD_GROUPS, SSD_HEADS // SSD_GROUPS, SSD_HEAD_DIM, SSD_STATE
    xbc = jax.nn.silu(causal_depthwise_conv(xbc, conv_w, conv_b)).astype(f32)
    xs = xbc[..., :SSD_INNER].reshape(bsz, nc, CHUNK, G, R, P)
    bm = xbc[..., SSD_INNER:SSD_INNER + SSD_BC].reshape(bsz, nc, CHUNK, G, N)
    cm = xbc[..., SSD_INNER + SSD_BC:].reshape(bsz, nc, CHUNK, G, N)
    dt = jax.nn.softplus(dt_raw.astype(f32) + dt_bias.astype(f32)).reshape(bsz, nc, CHUNK, G, R)
    a = -jnp.exp(a_log.astype(f32)).reshape(G, R)
    cs = jnp.cumsum(jnp.moveaxis(dt * a, 2, -1), axis=-1)
    xdt = xs * dt[..., None]
    causal = jnp.tril(jnp.ones((CHUNK, CHUNK), dtype=bool))
    seg = cs[..., :, None] - cs[..., None, :]
    decay_in = jnp.exp(jnp.where(causal, seg, -jnp.inf))
    cb = jnp.einsum('bclgn,bcsgn->bcgls', cm, bm)
    y_diag = jnp.einsum('bcgls,bcgrls,bcsgrp->bclgrp', cb, decay_in, xdt)
    decay_to_end = jnp.exp(cs[..., -1:] - cs)
    chunk_states = jnp.einsum('bclgn,bcgrl,bclgrp->bcgrpn', bm, decay_to_end, xdt)
    chunk_decay = jnp.exp(cs[..., -1])

    def carry_state(state, inp):
        new_s, dec = inp
        return dec[..., None, None] * state + new_s, state

    init = jnp.zeros((bsz, G, R, P, N), f32)
    _, prev_states = lax.scan(carry_state, init,
                              (jnp.moveaxis(chunk_states, 1, 0), jnp.moveaxis(chunk_decay, 1, 0)))
    prev_states = jnp.moveaxis(prev_states, 0, 1)
    y_off = jnp.einsum('bclgn,bcgrpn,bcgrl->bclgrp', cm, prev_states, jnp.exp(cs))
    y = y_diag + y_off + xs * d_skip.astype(f32).reshape(G, R, 1)
    y = y.reshape(bsz, seq, SSD_INNER) * jax.nn.silu(z.astype(f32))
    yg = y.reshape(bsz, seq, G, SSD_INNER // G)
    yg = yg * lax.rsqrt(jnp.mean(yg * yg, axis=-1, keepdims=True) + NORM_EPS)
    return (yg.reshape(bsz, seq, SSD_INNER) * norm_gain.astype(f32)).astype(z.dtype)


def mla_mixer(q_a, kv_a, k_rope_raw, cos, sin, q_a_norm, w_q_up, kv_a_norm, w_kv_up, out_norm):
    bsz, seq, _ = q_a.shape
    H = MLA_HEADS
    q = (rms_norm(q_a, q_a_norm) @ w_q_up).reshape(bsz, seq, H, MLA_NOPE + MLA_ROPE)
    q_nope = q[..., :MLA_NOPE]
    q_rope = apply_rope(q[..., MLA_NOPE:], cos[:, :, None, :], sin[:, :, None, :])
    kv = (rms_norm(kv_a, kv_a_norm) @ w_kv_up).reshape(bsz, seq, H, MLA_NOPE + MLA_V)
    k_nope, v = kv[..., :MLA_NOPE], kv[..., MLA_NOPE:]
    k_rope = apply_rope(k_rope_raw, cos, sin)
    scale = 1.0 / math.sqrt(MLA_NOPE + MLA_ROPE)
    outs = []
    for qb in range(seq // Q_BLOCK):
        q0 = qb * Q_BLOCK
        k_end = q0 + Q_BLOCK
        s = (jnp.einsum('bqhd,bkhd->bhqk', q_nope[:, q0:k_end], k_nope[:, :k_end])
             + jnp.einsum('bqhd,bkd->bhqk', q_rope[:, q0:k_end], k_rope[:, :k_end]))
        s = s.astype(jnp.float32) * scale
        q_chunk = (q0 + jnp.arange(Q_BLOCK)) // CHUNK
        k_chunk = jnp.arange(k_end) // CHUNK
        s = jnp.where(k_chunk[None, :] <= q_chunk[:, None], s, -jnp.inf)
        p = jax.nn.softmax(s, axis=-1).astype(v.dtype)
        outs.append(jnp.einsum('bhqk,bkhd->bqhd', p, v[:, :k_end]))
    o = jnp.concatenate(outs, axis=1).reshape(bsz, seq, H * MLA_V)
    return rms_norm(o, out_norm)


def moe_ffn(h, w_router, b_router, w_gate_up, b_gate_up, w_down, b_down):
    n_tok, d = h.shape
    logits = (h @ w_router).astype(jnp.float32) + b_router.astype(jnp.float32)
    top_vals, top_idx = lax.top_k(logits, TOP_K)
    gates = jax.nn.softmax(top_vals, axis=-1)
    n_assign = n_tok * TOP_K
    flat_e = top_idx.reshape(-1).astype(jnp.int32)
    flat_tok = jnp.repeat(jnp.arange(n_tok, dtype=jnp.int32), TOP_K)
    flat_g = gates.reshape(-1)
    order = jnp.argsort(flat_e)
    sorted_e = flat_e[order]
    counts = jnp.bincount(flat_e, length=N_EXPERTS).astype(jnp.int32)
    padded = ((counts + MOE_BLOCK - 1) // MOE_BLOCK) * MOE_BLOCK
    starts = jnp.cumsum(counts) - counts
    padded_end = jnp.cumsum(padded)
    padded_start = padded_end - padded
    rank = jnp.arange(n_assign, dtype=jnp.int32) - starts[sorted_e]
    dest = padded_start[sorted_e] + rank
    n_slots = ((n_assign + MOE_BLOCK - 1) // MOE_BLOCK + N_EXPERTS) * MOE_BLOCK
    n_blocks = n_slots // MOE_BLOCK
    slot_tok = jnp.zeros((n_slots,), jnp.int32).at[dest].set(flat_tok[order])
    slot_gate = jnp.zeros((n_slots,), jnp.float32).at[dest].set(flat_g[order])
    block_start = jnp.arange(n_blocks, dtype=jnp.int32) * MOE_BLOCK
    block_expert = jnp.minimum(jnp.searchsorted(padded_end, block_start, side='right'),
                               N_EXPERTS - 1).astype(jnp.int32)

    def expert_block(args):
        tok, g, e = args
        xb = h[tok]
        gu = xb @ w_gate_up[e] + b_gate_up[e]
        glu = jnp.minimum(gu[:, :D_FF_EXPERT], SWIGLU_LIMIT)
        lin = jnp.clip(gu[:, D_FF_EXPERT:], -SWIGLU_LIMIT, SWIGLU_LIMIT)
        act = glu * jax.nn.sigmoid(SWIGLU_ALPHA * glu) * (lin + 1.0)
        y = act @ w_down[e] + b_down[e]
        return y * g[:, None].astype(y.dtype)

    y = lax.map(expert_block, (slot_tok.reshape(n_blocks, MOE_BLOCK),
                               slot_gate.reshape(n_blocks, MOE_BLOCK), block_expert))
    return jax.ops.segment_sum(y.reshape(n_slots, d), slot_tok, num_segments=n_tok)


def setup_inputs(seed: int = 0) -> dict:
    key = jax.random.key(seed)
    ks = jax.random.split(key, 40)
    f32 = jnp.float32
    nrm = lambda k, shape, s: jax.random.normal(k, shape, f32) * s
    gain = lambda k, shape: 1.0 + 0.05 * jax.random.normal(k, shape, f32)
    L, D, E, F = DEPTH, D_MODEL, N_EXPERTS, D_FF_EXPERT
    x = jax.random.normal(ks[0], (BATCH, SEQ, D), f32)
    c = jax.random.normal(ks[1], (BATCH, D), f32)
    offset = jax.random.randint(ks[2], (BATCH, 1), 0, MAX_STREAM_OFFSET, dtype=jnp.int32)
    positions = offset + jnp.arange(SEQ, dtype=jnp.int32)[None, :]
    dt0 = jnp.exp(jax.random.uniform(ks[9], (L, SSD_HEADS), f32, math.log(1e-3), math.log(1e-1)))
    return {
        'x': x,
        'c': c,
        'positions': positions,
        'w_ada': nrm(ks[3], (L, D, 6 * D), 0.5 * D ** -0.5),
        'b_ada': nrm(ks[4], (L, 6 * D), 0.02),
        'pre_mix_norm': gain(ks[5], (L, D)),
        'w_in': nrm(ks[6], (L, D, IN_COLS), D ** -0.5),
        'conv_w': nrm(ks[7], (L, SSD_CONV, XBC_DIM), SSD_CONV ** -0.5),
        'conv_b': nrm(ks[8], (L, XBC_DIM), 0.02),
        'dt_bias': dt0 + jnp.log(-jnp.expm1(-dt0)),
        'a_log': jnp.log(jax.random.uniform(ks[10], (L, SSD_HEADS), f32, 1.0, 16.0)),
        'd_skip': gain(ks[11], (L, SSD_HEADS)),
        'ssd_norm': gain(ks[12], (L, SSD_INNER)),
        'q_a_norm': gain(ks[13], (L, Q_LORA)),
        'w_q_up': nrm(ks[14], (L, Q_LORA, MLA_HEADS * (MLA_NOPE + MLA_ROPE)), Q_LORA ** -0.5),
        'kv_a_norm': gain(ks[15], (L, KV_LORA)),
        'w_kv_up': nrm(ks[16], (L, KV_LORA, MLA_HEADS * (MLA_NOPE + MLA_V)), KV_LORA ** -0.5),
        'mla_norm': gain(ks[17], (L, MLA_HEADS * MLA_V)),
        'w_out': nrm(ks[18], (L, MIX_WIDTH, D), MIX_WIDTH ** -0.5),
        'post_mix_norm': gain(ks[19], (L, D)),
        'pre_ffn_norm': gain(ks[20], (L, D)),
        'w_router': nrm(ks[21], (L, D, E), D ** -0.5),
        'b_router': nrm(ks[22], (L, E), 0.01),
        'w_gate_up': nrm(ks[23], (L, E, D, 2 * F), D ** -0.5),
        'b_gate_up': nrm(ks[24], (L, E, 2 * F), 0.02),
        'w_down': nrm(ks[25], (L, E, F, D), F ** -0.5),
        'b_down': nrm(ks[26], (L, E, D), 0.02),
        'post_ffn_norm': gain(ks[27], (L, D)),
    }


def reference(x, c, positions, w_ada, b_ada, pre_mix_norm, w_in, conv_w, conv_b, dt_bias, a_log,
              d_skip, ssd_norm, q_a_norm, w_q_up, kv_a_norm, w_kv_up, mla_norm, w_out,
              post_mix_norm, pre_ffn_norm, w_router, b_router, w_gate_up, b_gate_up, w_down,
              b_down, post_ffn_norm):
    bsz, seq, d = x.shape
    inv_freq = ROPE_BASE ** (-(jnp.arange(MLA_ROPE // 2, dtype=jnp.float32) * 2.0 / MLA_ROPE))
    angles = positions.astype(jnp.float32)[..., None] * inv_freq
    cos = jnp.cos(angles).astype(x.dtype)
    sin = jnp.sin(angles).astype(x.dtype)
    c_act = jax.nn.silu(c)
    for l in range(DEPTH):
        mod = c_act @ w_ada[l] + b_ada[l]
        sh1, sc1, g1, sh2, sc2, g2 = jnp.split(mod[:, None, :], 6, axis=-1)
        h = rms_norm(x, pre_mix_norm[l]) * (1.0 + sc1) + sh1
        proj = h @ w_in[l]
        y_ssd = ssd_mixer(proj[..., OFF_Z:OFF_XBC], proj[..., OFF_XBC:OFF_DT], proj[..., OFF_DT:OFF_QA],
                          conv_w[l], conv_b[l], dt_bias[l], a_log[l], d_skip[l], ssd_norm[l])
        y_mla = mla_mixer(proj[..., OFF_QA:OFF_KVA], proj[..., OFF_KVA:OFF_KR], proj[..., OFF_KR:IN_COLS],
                          cos, sin, q_a_norm[l], w_q_up[l], kv_a_norm[l], w_kv_up[l], mla_norm[l])
        mix = jnp.concatenate([y_ssd, y_mla], axis=-1) @ w_out[l]
        x = x + g1 * rms_norm(mix, post_mix_norm[l])
        h = rms_norm(x, pre_ffn_norm[l]) * (1.0 + sc2) + sh2
        f = moe_ffn(h.reshape(bsz * seq, d), w_router[l], b_router[l], w_gate_up[l], b_gate_up[l],
                    w_down[l], b_down[l]).reshape(bsz, seq, d)
        x = x + g2 * rms_norm(f, post_ffn_norm[l])
    return x
```

```python
import functools
import math

import jax
import jax.numpy as jnp
from jax import lax
from jax.experimental import pallas as pl
from jax.experimental.pallas import tpu as pltpu

F32 = jnp.float32
BF16 = jnp.bfloat16

D_MODEL = 1024
CHUNK = 64
SSD_INNER = 512
SSD_HEAD_DIM = 64
SSD_HEADS = 8
SSD_GROUPS = 2
SSD_STATE = 128
SSD_CONV = 4
SSD_BC = SSD_GROUPS * SSD_STATE
XBC_DIM = SSD_INNER + 2 * SSD_BC
MLA_V = 64
MLA_HEADS = 8
MLA_NOPE = 64
MLA_ROPE = 32
Q_LORA = 384
KV_LORA = 256
ROPE_BASE = 10000.0
OFF_Z = 0
OFF_XBC = OFF_Z + SSD_INNER
OFF_DT = OFF_XBC + XBC_DIM
OFF_QA = OFF_DT + SSD_HEADS
OFF_KVA = OFF_QA + Q_LORA
OFF_KR = OFF_KVA + KV_LORA
IN_COLS = OFF_KR + MLA_ROPE
N_EXPERTS = 32
TOP_K = 4
D_FF_EXPERT = 1024
SWIGLU_LIMIT = 7.0
SWIGLU_ALPHA = 1.702
NORM_EPS = 1e-6

LANES = 128
HALF_ROPE = MLA_ROPE // 2
ROPE_LO = MLA_NOPE
ROPE_HI = MLA_NOPE + HALF_ROPE

TS_PROJ = 512
SSD_L = 256
ATT_TQ = 256
ATT_TK = 256
MOE_TB = 512
VMEM_LIMIT = 56 * 1024 * 1024


def _dot(a, b):
    return jnp.dot(a, b, preferred_element_type=F32)


def _dot_nt(a, b):
    return lax.dot_general(a, b, (((1,), (1,)), ((), ())), preferred_element_type=F32)


def _dot_tn(a, b):
    return lax.dot_general(a, b, (((0,), (0,)), ((), ())), preferred_element_type=F32)


def _split2(x):
    hi = x.astype(BF16)
    lo = (x - hi.astype(F32)).astype(BF16)
    return hi, lo


def _split3(x):
    h1 = x.astype(BF16)
    r1 = x - h1.astype(F32)
    h2 = r1.astype(BF16)
    h3 = (r1 - h2.astype(F32)).astype(BF16)
    return h1, h2, h3


def _dot3(a, b):
    ah, al = _split2(a)
    bh, bl = _split2(b)
    return _dot(ah, bh) + _dot(ah, bl) + _dot(al, bh)


def _rms(x):
    return x * lax.rsqrt(jnp.mean(x * x, axis=-1, keepdims=True) + NORM_EPS)


def _silu(x):
    return x * jax.nn.sigmoid(x)


def _const_spec(shape):
    nd = len(shape)
    return pl.BlockSpec(shape, lambda *_: (0,) * nd)


def _adaln_body(c_ref, w_ref, b_ref, o_ref):
    o_ref[...] = _dot3(_silu(c_ref[...]), w_ref[...]) + b_ref[...]


def _adaln(c, w_ada, b_ada):
    bsz, d = c.shape
    n = w_ada.shape[1]
    tn = 1024
    return pl.pallas_call(
        _adaln_body,
        grid=(n // tn,),
        in_specs=[_const_spec((bsz, d)),
                  pl.BlockSpec((d, tn), lambda j: (0, j)),
                  pl.BlockSpec((1, tn), lambda j: (0, j))],
        out_specs=pl.BlockSpec((bsz, tn), lambda j: (0, j)),
        out_shape=jax.ShapeDtypeStruct((bsz, n), F32),
        compiler_params=pltpu.CompilerParams(dimension_semantics=("arbitrary",),
                                             vmem_limit_bytes=VMEM_LIMIT),
        name="adaln",
    )(c, w_ada, b_ada.reshape(1, n))


def _rope_block(xb, ct, st, lane):
    partner = jnp.where(lane < ROPE_HI, pltpu.roll(xb, LANES - HALF_ROPE, 1), pltpu.roll(xb, HALF_ROPE, 1))
    return xb * ct + partner * st


def _inproj_body(x_ref, sc_ref, sh_ref, g_ref, ct_ref, st_ref, wz_ref, wxbc_ref, wsm_ref, wqa_ref, wkva_ref,
                 qn_ref, kvn_ref, wqup_ref, wkup_ref, wvup_ref,
                 z_ref, xbc_ref, dt_ref, q_ref, k_ref, v_ref):
    x = x_ref[0]
    h = _rms(x) * (g_ref[...] * (1.0 + sc_ref[0])) + sh_ref[0]
    hb = h.astype(BF16)
    z_ref[0] = _dot(hb, wz_ref[...]).astype(BF16)
    xbc_ref[0] = _dot(hb, wxbc_ref[...]).astype(BF16)
    sm = _dot(hb, wsm_ref[...])
    dt_ref[0] = sm[:, LANES:]
    ct = ct_ref[0]
    st = st_ref[0]
    lane = lax.broadcasted_iota(jnp.int32, ct.shape, 1)
    kr = _rope_block(sm[:, :LANES], ct, st, lane)
    qan = (_rms(_dot(hb, wqa_ref[...])) * qn_ref[...]).astype(BF16)
    q = _dot(qan, wqup_ref[...])
    for hh in range(MLA_HEADS):
        blk = slice(hh * LANES, (hh + 1) * LANES)
        q_ref[0, :, blk] = _rope_block(q[:, blk], ct, st, lane).astype(BF16)
    kvn = (_rms(_dot(hb, wkva_ref[...])) * kvn_ref[...]).astype(BF16)
    k = _dot(kvn, wkup_ref[...])
    for hh in range(MLA_HEADS):
        blk = slice(hh * LANES, (hh + 1) * LANES)
        k_ref[0, :, blk] = (k[:, blk] + kr).astype(BF16)
    v_ref[0] = _dot(kvn, wvup_ref[...]).astype(BF16)


def _inproj(x, sc1, sh1, gain, ct, st, wz, wxbc, wsm, wqa, wkva, qn, kvn, wqup, wkup, wvup):
    bsz, seq, d = x.shape
    ts = min(TS_PROJ, seq)
    hw = MLA_HEADS * LANES

    def tok(width):
        return pl.BlockSpec((1, ts, width), lambda b, i: (b, i, 0))

    def per_batch(width):
        return pl.BlockSpec((1, 1, width), lambda b, i: (b, 0, 0))

    weights = (wz, wxbc, wsm, wqa, wkva, qn, kvn, wqup, wkup, wvup)
    out_widths = (SSD_INNER, XBC_DIM, LANES, hw, hw, MLA_HEADS * MLA_V)
    out_dtypes = (BF16, BF16, F32, BF16, BF16, BF16)
    return pl.pallas_call(
        _inproj_body,
        grid=(bsz, seq // ts),
        in_specs=[tok(d), per_batch(d), per_batch(d), _const_spec((1, d)), tok(LANES), tok(LANES)]
                 + [_const_spec(w.shape) for w in weights],
        out_specs=[tok(w) for w in out_widths],
        out_shape=[jax.ShapeDtypeStruct((bsz, seq, w), dt) for w, dt in zip(out_widths, out_dtypes)],
        compiler_params=pltpu.CompilerParams(dimension_semantics=("parallel", "parallel"),
                                             vmem_limit_bytes=VMEM_LIMIT),
        name="inproj",
    )(x, sc1, sh1, gain, ct, st, *weights)


CONV_HALO = 16


def _ssd_body(xc_ref, xp_ref, z_ref, dt_ref, cw_ref, cb_ref, dtb_ref, alog_ref, dsk_ref, ng_ref,
              y_ref, xs_scr, st_scr, *, blk):
    i = pl.program_id(1)

    @pl.when(i == 0)
    def _():
        st_scr[...] = jnp.zeros_like(st_scr)

    xs_scr[0:CONV_HALO, :] = jnp.where(i > 0, xp_ref[0].astype(F32), 0.0)
    xs_scr[CONV_HALO:CONV_HALO + blk, :] = xc_ref[0].astype(F32)
    conv = cb_ref[...]
    for kk in range(SSD_CONV):
        off = CONV_HALO - (SSD_CONV - 1) + kk
        conv = conv + cw_ref[kk:kk + 1, :] * xs_scr[off:off + blk, :]
    xa = _silu(conv)
    xs = xa[:, :SSD_INNER]
    bm = xa[:, SSD_INNER:SSD_INNER + SSD_BC]
    cm = xa[:, SSD_INNER + SSD_BC:]

    hl = lax.broadcasted_iota(jnp.int32, (1, LANES), 1)
    dtr = dt_ref[0] + dtb_ref[...]
    dt = jnp.maximum(dtr, 0.0) + jnp.log(1.0 + jnp.exp(-jnp.abs(dtr)))
    a = jnp.where(hl < SSD_HEADS, -jnp.exp(alog_ref[...]), 0.0)
    dta = dt * a
    row = lax.broadcasted_iota(jnp.int32, (blk, blk), 0)
    col = lax.broadcasted_iota(jnp.int32, (blk, blk), 1)
    tril = row >= col
    trilb = jnp.where(tril, 1.0, 0.0).astype(BF16)
    d1, d2, d3 = _split3(dta)
    cs = _dot(trilb, d1) + _dot(trilb, d2) + _dot(trilb, d3)
    cs_last = cs[blk - 1:blk, :]
    ecs = jnp.exp(cs)
    dte = jnp.exp(cs_last - cs)
    cs_t = cs.T

    er = lax.broadcasted_iota(jnp.int32, (LANES, SSD_INNER), 0)
    ec = lax.broadcasted_iota(jnp.int32, (LANES, SSD_INNER), 1)
    expand = jnp.where(ec // SSD_HEAD_DIM == er, 1.0, 0.0).astype(BF16)

    def per_channel(v):
        vh, vl = _split2(v)
        return _dot(vh, expand) + _dot(vl, expand)

    dt_e = per_channel(dt)
    ecs_e = per_channel(ecs)
    dte_e = per_channel(dte)
    xdt = xs * dt_e
    xdt_b = xdt.astype(BF16)
    xw_b = (xdt * dte_e).astype(BF16)

    gw = SSD_INNER // SSD_GROUPS
    heads_per_group = SSD_HEADS // SSD_GROUPS
    lane = lax.broadcasted_iota(jnp.int32, (blk, LANES), 1)
    y_groups = []
    for g in range(SSD_GROUPS):
        bg = bm[:, g * SSD_STATE:(g + 1) * SSD_STATE].astype(BF16)
        cg = cm[:, g * SSD_STATE:(g + 1) * SSD_STATE].astype(BF16)
        cb = _dot_nt(cg, bg)
        state = st_scr[g]
        y_off = _dot(cg, state.astype(BF16))
        pairs = []
        for j in range(heads_per_group // 2):
            xp = xdt_b[:, g * gw + j * LANES:g * gw + (j + 1) * LANES]
            halves = []
            for u in range(2):
                hidx = g * heads_per_group + 2 * j + u
                seg = cs[:, hidx:hidx + 1] - cs_t[hidx:hidx + 1, :]
                dec = jnp.exp(jnp.where(tril, seg, -jnp.inf))
                halves.append(_dot((cb * dec).astype(BF16), xp))
            pairs.append(jnp.where(lane < SSD_HEAD_DIM, halves[0], halves[1]))
        y_diag = jnp.concatenate(pairs, axis=1)
        y_groups.append(y_diag + y_off * ecs_e[:, g * gw:(g + 1) * gw])
        st_scr[g] = (state * ecs_e[blk - 1:blk, g * gw:(g + 1) * gw]
                     + _dot_tn(bg, xw_b[:, g * gw:(g + 1) * gw]))
    y = jnp.concatenate(y_groups, axis=1) + xs * dsk_ref[...]
    y = y * _silu(z_ref[0].astype(F32))
    y = jnp.concatenate([_rms(y[:, g * gw:(g + 1) * gw]) for g in range(SSD_GROUPS)], axis=1)
    y_ref[0] = (y * ng_ref[...]).astype(BF16)


def _ssd(xbc, z, dt, conv_w, conv_b, dt_bias, a_log, d_skip_e, norm_gain):
    bsz, seq, _ = xbc.shape
    blk = min(SSD_L, seq)
    halo_per_blk = blk // CONV_HALO
    body = functools.partial(_ssd_body, blk=blk)
    return pl.pallas_call(
        body,
        grid=(bsz, seq // blk),
        in_specs=[pl.BlockSpec((1, blk, XBC_DIM), lambda b, i: (b, i, 0)),
                  pl.BlockSpec((1, CONV_HALO, XBC_DIM), lambda b, i: (b, jnp.maximum(i * halo_per_blk - 1, 0), 0)),
                  pl.BlockSpec((1, blk, SSD_INNER), lambda b, i: (b, i, 0)),
                  pl.BlockSpec((1, blk, LANES), lambda b, i: (b, i, 0)),
                  _const_spec((SSD_CONV, XBC_DIM)), _const_spec((1, XBC_DIM)),
                  _const_spec((1, LANES)), _const_spec((1, LANES)),
                  _const_spec((1, SSD_INNER)), _const_spec((1, SSD_INNER))],
        out_specs=pl.BlockSpec((1, blk, SSD_INNER), lambda b, i: (b, i, 0)),
        out_shape=jax.ShapeDtypeStruct((bsz, seq, SSD_INNER), BF16),
        scratch_shapes=[pltpu.VMEM((CONV_HALO + blk, XBC_DIM), F32),
                        pltpu.VMEM((SSD_GROUPS, SSD_STATE, SSD_INNER // SSD_GROUPS), F32)],
        compiler_params=pltpu.CompilerParams(dimension_semantics=("parallel", "arbitrary"),
                                             vmem_limit_bytes=VMEM_LIMIT),
        name="ssd",
    )(xbc, xbc, z, dt, conv_w, conv_b, dt_bias, a_log, d_skip_e, norm_gain)


def _attn_body(q_ref, k_ref, v_ref, o_ref, *, tq, tk):
    qi = pl.program_id(2)
    n_full = qi * (tq // tk)
    q_chunk = (qi * tq + lax.broadcasted_iota(jnp.int32, (tq, tk), 0)) // CHUNK
    k_col = lax.broadcasted_iota(jnp.int32, (tq, tk), 1)
    lane = lax.broadcasted_iota(jnp.int32, (tq, LANES), 1)
    outs = []
    for u in range(2):
        hs = slice(u * LANES, (u + 1) * LANES)
        q = q_ref[0, :, hs]

        def step(ki, carry, masked, hs=hs, q=q):
            m, l, acc = carry
            start = pl.multiple_of(ki * tk, tk)
            k = k_ref[0, pl.ds(start, tk), hs]
            v = v_ref[0, pl.ds(start, tk), :]
            s = _dot_nt(q, k)
            if masked:
                s = jnp.where((ki * tk + k_col) // CHUNK <= q_chunk, s, -jnp.inf)
            m_new = jnp.maximum(m, jnp.max(s, axis=-1, keepdims=True))
            alpha = jnp.exp(m - m_new)
            p = jnp.exp(s - m_new)
            l = alpha * l + jnp.sum(p, axis=-1, keepdims=True)
            acc = alpha * acc + _dot(p.astype(BF16), v)
            return m_new, l, acc

        carry = (jnp.full((tq, 1), -jnp.inf, F32), jnp.zeros((tq, 1), F32), jnp.zeros((tq, LANES), F32))
        carry = lax.fori_loop(0, n_full, functools.partial(step, masked=False), carry)
        for dd in range(tq // tk):
            carry = step(n_full + dd, carry, True)
        _, l, acc = carry
        outs.append(acc / l)
    o_ref[0] = jnp.where(lane < MLA_V, outs[0], outs[1]).astype(BF16)


def _attn(q, k, v):
    bsz, seq, _ = q.shape
    tq = min(ATT_TQ, seq)
    tk = min(ATT_TK, seq)
    body = functools.partial(_attn_body, tq=tq, tk=tk)
    return pl.pallas_call(
        body,
        grid=(bsz, MLA_HEADS // 2, seq // tq),
        in_specs=[pl.BlockSpec((1, tq, 2 * LANES), lambda b, hp, i: (b, i, hp)),
                  pl.BlockSpec((1, seq, 2 * LANES), lambda b, hp, i: (b, 0, hp)),
                  pl.BlockSpec((1, seq, LANES), lambda b, hp, i: (b, 0, hp))],
        out_specs=pl.BlockSpec((1, tq, LANES), lambda b, hp, i: (b, i, hp)),
        out_shape=jax.ShapeDtypeStruct((bsz, seq, MLA_HEADS * MLA_V), BF16),
        compiler_params=pltpu.CompilerParams(dimension_semantics=("parallel", "parallel", "arbitrary"),
                                             vmem_limit_bytes=VMEM_LIMIT),
        name="attn",
    )(q, k, v)


def _outproj_body(ys_ref, ya_ref, x_ref, g1_ref, sc2_ref, sh2_ref, mn_ref, wo1_ref, wo2_ref, pmn_ref, pfn_ref,
                  wr_ref, br_ref, x1_ref, h2_ref, gate_ref, idx_ref):
    yan = (_rms(ya_ref[0].astype(F32)) * mn_ref[...]).astype(BF16)
    mix = _dot(ys_ref[0], wo1_ref[...]) + _dot(yan, wo2_ref[...])
    x1 = x_ref[0] + g1_ref[0] * (_rms(mix) * pmn_ref[...])
    x1_ref[0] = x1
    h2 = _rms(x1) * (pfn_ref[...] * (1.0 + sc2_ref[0])) + sh2_ref[0]
    h2_ref[0] = h2.astype(BF16)
    logits = _dot3(h2, wr_ref[...]) + br_ref[...]
    lane = lax.broadcasted_iota(jnp.int32, logits.shape, 1)
    cur = jnp.where(lane < N_EXPERTS, logits, -jnp.inf)
    vals, idxs = [], []
    for _ in range(TOP_K):
        m = jnp.max(cur, axis=-1, keepdims=True)
        ix = jnp.min(jnp.where(cur == m, lane, LANES), axis=-1, keepdims=True)
        vals.append(m)
        idxs.append(ix)
        cur = jnp.where(lane == ix, -jnp.inf, cur)
    es = [jnp.exp(v - vals[0]) for v in vals]
    denom = es[0]
    for e in es[1:]:
        denom = denom + e
    gate_out = jnp.zeros(logits.shape, F32)
    idx_out = jnp.zeros(logits.shape, jnp.int32)
    for kk in range(TOP_K):
        gate_out = jnp.where(lane == kk, es[kk] / denom, gate_out)
        idx_out = jnp.where(lane == kk, idxs[kk], idx_out)
    gate_ref[0] = gate_out
    idx_ref[0] = idx_out


def _outproj(y_ssd, y_att, x, g1, sc2, sh2, mla_norm, wo1, wo2, post_mix_norm, pre_ffn_norm, wr, br):
    bsz, seq, d = x.shape
    ts = min(TS_PROJ, seq)

    def tok(width):
        return pl.BlockSpec((1, ts, width), lambda b, i: (b, i, 0))

    def per_batch(width):
        return pl.BlockSpec((1, 1, width), lambda b, i: (b, 0, 0))

    consts = (mla_norm, wo1, wo2, post_mix_norm, pre_ffn_norm, wr, br)
    return pl.pallas_call(
        _outproj_body,
        grid=(bsz, seq // ts),
        in_specs=[tok(SSD_INNER), tok(MLA_HEADS * MLA_V), tok(d), per_batch(d), per_batch(d), per_batch(d)]
                 + [_const_spec(w.shape) for w in consts],
        out_specs=[tok(d), tok(d), tok(LANES), tok(LANES)],
        out_shape=[jax.ShapeDtypeStruct((bsz, seq, d), F32), jax.ShapeDtypeStruct((bsz, seq, d), BF16),
                   jax.ShapeDtypeStruct((bsz, seq, LANES), F32), jax.ShapeDtypeStruct((bsz, seq, LANES), jnp.int32)],
        compiler_params=pltpu.CompilerParams(dimension_semantics=("parallel", "parallel"),
                                             vmem_limit_bytes=VMEM_LIMIT),
        name="outproj",
    )(y_ssd, y_att, x, g1, sc2, sh2, *consts)


def _moe_body(be_ref, na_ref, x_ref, wgu_ref, bgu_ref, wd_ref, bd_ref, y_ref):
    i = pl.program_id(0)

    @pl.when(i < na_ref[0])
    def _():
        gu = _dot(x_ref[...], wgu_ref[0]) + bgu_ref[0]
        glu = jnp.minimum(gu[:, :D_FF_EXPERT], SWIGLU_LIMIT)
        lin = jnp.clip(gu[:, D_FF_EXPERT:], -SWIGLU_LIMIT, SWIGLU_LIMIT)
        act = glu * jax.nn.sigmoid(SWIGLU_ALPHA * glu) * (lin + 1.0)
        y_ref[...] = (_dot(act.astype(BF16), wd_ref[0]) + bd_ref[0]).astype(BF16)

    @pl.when(i >= na_ref[0])
    def _():
        y_ref[...] = jnp.zeros_like(y_ref)


def _moe(block_expert, n_active, xg, wgu, bgu, wd, bd):
    n_slots, d = xg.shape
    n_blocks = n_slots // MOE_TB
    f2 = wgu.shape[2]
    return pl.pallas_call(
        _moe_body,
        grid_spec=pltpu.PrefetchScalarGridSpec(
            num_scalar_prefetch=2,
            grid=(n_blocks,),
            in_specs=[pl.BlockSpec((MOE_TB, d), lambda i, be, na: (i, 0)),
                      pl.BlockSpec((1, d, f2), lambda i, be, na: (be[i], 0, 0)),
                      pl.BlockSpec((1, 1, f2), lambda i, be, na: (be[i], 0, 0)),
                      pl.BlockSpec((1, f2 // 2, d), lambda i, be, na: (be[i], 0, 0)),
                      pl.BlockSpec((1, 1, d), lambda i, be, na: (be[i], 0, 0))],
            out_specs=pl.BlockSpec((MOE_TB, d), lambda i, be, na: (i, 0)),
        ),
        out_shape=jax.ShapeDtypeStruct((n_slots, d), BF16),
        compiler_params=pltpu.CompilerParams(dimension_semantics=("arbitrary",),
                                             vmem_limit_bytes=VMEM_LIMIT),
        name="moe",
    )(block_expert, n_active, xg, wgu, bgu, wd, bd)


def _final_body(x1_ref, f_ref, g2_ref, gain_ref, o_ref):
    o_ref[0] = x1_ref[0] + g2_ref[0] * (_rms(f_ref[0]) * gain_ref[...])


def _final(x1, f, g2, gain):
    bsz, seq, d = x1.shape
    ts = min(TS_PROJ, seq)
    tok = pl.BlockSpec((1, ts, d), lambda b, i: (b, i, 0))
    return pl.pallas_call(
        _final_body,
        grid=(bsz, seq // ts),
        in_specs=[tok, tok, pl.BlockSpec((1, 1, d), lambda b, i: (b, 0, 0)), _const_spec((1, d))],
        out_specs=tok,
        out_shape=jax.ShapeDtypeStruct((bsz, seq, d), F32),
        compiler_params=pltpu.CompilerParams(dimension_semantics=("parallel", "parallel"),
                                             vmem_limit_bytes=VMEM_LIMIT),
        name="final",
    )(x1, f, g2, gain)


def _head_blocks(cols):
    out = []
    for c in cols:
        pad = LANES - c.shape[1]
        out.append(jnp.pad(c, ((0, 0), (0, pad))) if pad else c)
    return jnp.concatenate(out, axis=1)


def _prep_mixer_weights(w_in, w_q_up, w_kv_up):
    d = w_in.shape[0]
    wz = w_in[:, OFF_Z:OFF_XBC]
    wxbc = w_in[:, OFF_XBC:OFF_DT]
    wdt = w_in[:, OFF_DT:OFF_QA]
    wqa = w_in[:, OFF_QA:OFF_KVA]
    wkva = w_in[:, OFF_KVA:OFF_KR]
    wkr = w_in[:, OFF_KR:IN_COLS]
    kr_blk = jnp.concatenate([jnp.zeros((d, ROPE_LO), F32), wkr, jnp.zeros((d, LANES - ROPE_LO - MLA_ROPE), F32)], axis=1)
    dt_blk = jnp.pad(wdt, ((0, 0), (0, LANES - SSD_HEADS)))
    wsm = jnp.concatenate([kr_blk, dt_blk], axis=1)
    qh = MLA_NOPE + MLA_ROPE
    scale = 1.0 / math.sqrt(qh)
    wqup = _head_blocks([w_q_up[:, h * qh:(h + 1) * qh] for h in range(MLA_HEADS)]) * scale
    kvh = MLA_NOPE + MLA_V
    wkup = _head_blocks([w_kv_up[:, h * kvh:h * kvh + MLA_NOPE] for h in range(MLA_HEADS)])
    wvup = jnp.concatenate([w_kv_up[:, h * kvh + MLA_NOPE:(h + 1) * kvh] for h in range(MLA_HEADS)], axis=1)
    return tuple(w.astype(BF16) for w in (wz, wxbc, wsm, wqa, wkva)) + tuple(w.astype(BF16) for w in (wqup, wkup, wvup))


def _rope_tables(positions):
    inv_freq = ROPE_BASE ** (-(jnp.arange(HALF_ROPE, dtype=F32) * 2.0 / MLA_ROPE))
    angles = positions.astype(F32)[..., None] * inv_freq
    cos = jnp.cos(angles)
    sin = jnp.sin(angles)
    shp = angles.shape[:-1]
    ct = jnp.concatenate([jnp.ones(shp + (ROPE_LO,), F32), cos, cos,
                          jnp.zeros(shp + (LANES - ROPE_LO - MLA_ROPE,), F32)], axis=-1)
    st = jnp.concatenate([jnp.zeros(shp + (ROPE_LO,), F32), -sin, sin,
                          jnp.zeros(shp + (LANES - ROPE_LO - MLA_ROPE,), F32)], axis=-1)
    return ct, st


def _route(idx, n_tok):
    n_assign = n_tok * TOP_K
    flat_e = idx.reshape(-1)
    onehot = (flat_e[:, None] == jnp.arange(N_EXPERTS, dtype=jnp.int32)[None, :]).astype(jnp.int32)
    incl = jnp.cumsum(onehot, axis=0)
    rank = jnp.sum((incl - onehot) * onehot, axis=1)
    counts = incl[-1]
    padded = ((counts + MOE_TB - 1) // MOE_TB) * MOE_TB
    padded_end = jnp.cumsum(padded)
    padded_start = padded_end - padded
    dest = padded_start[flat_e] + rank
    n_blocks = n_assign // MOE_TB + N_EXPERTS
    n_slots = n_blocks * MOE_TB
    flat_tok = jnp.arange(n_assign, dtype=jnp.int32) // TOP_K
    slot_tok = jnp.zeros((n_slots,), jnp.int32).at[dest].set(flat_tok)
    block_start = jnp.arange(n_blocks, dtype=jnp.int32) * MOE_TB
    block_expert = jnp.minimum(jnp.searchsorted(padded_end, block_start, side='right'),
                               N_EXPERTS - 1).astype(jnp.int32)
    n_active = (padded_end[-1] // MOE_TB).astype(jnp.int32).reshape(1)
    return dest, slot_tok, block_expert, n_active


def kernel(x, c, positions, w_ada, b_ada, pre_mix_norm, w_in, conv_w, conv_b, dt_bias, a_log, d_skip, ssd_norm, q_a_norm, w_q_up, kv_a_norm, w_kv_up, mla_norm, w_out, post_mix_norm, pre_ffn_norm, w_router, b_router, w_gate_up, b_gate_up, w_down, b_down, post_ffn_norm):
    bsz, seq, d = x.shape
    n_tok = bsz * seq
    ct, st = _rope_tables(positions)
    for l in range(w_ada.shape[0]):
        mod = _adaln(c, w_ada[l], b_ada[l])
        sh1, sc1, g1, sh2, sc2, g2 = [m.reshape(bsz, 1, d) for m in jnp.split(mod, 6, axis=-1)]

        mixer_w = _prep_mixer_weights(w_in[l], w_q_up[l], w_kv_up[l])
        z, xbc, dt, q, k, v = _inproj(x, sc1, sh1, pre_mix_norm[l].reshape(1, d), ct, st, *mixer_w[:5],
                                      q_a_norm[l].reshape(1, -1), kv_a_norm[l].reshape(1, -1), *mixer_w[5:])
        pad_h = LANES - SSD_HEADS
        y_ssd = _ssd(xbc, z, dt, conv_w[l], conv_b[l].reshape(1, -1),
                     jnp.pad(dt_bias[l], (0, pad_h)).reshape(1, LANES), jnp.pad(a_log[l], (0, pad_h)).reshape(1, LANES),
                     jnp.repeat(d_skip[l], SSD_HEAD_DIM).reshape(1, -1), ssd_norm[l].reshape(1, -1))
        y_att = _attn(q, k, v)

        wo = w_out[l].astype(BF16)
        wr = jnp.pad(w_router[l], ((0, 0), (0, LANES - N_EXPERTS)))
        br = jnp.pad(b_router[l], (0, LANES - N_EXPERTS)).reshape(1, LANES)
        x1, h2, gates, idx = _outproj(y_ssd, y_att, x, g1, sc2, sh2, mla_norm[l].reshape(1, -1),
                                      wo[:SSD_INNER], wo[SSD_INNER:], post_mix_norm[l].reshape(1, d),
                                      pre_ffn_norm[l].reshape(1, d), wr, br)

        gates = gates.reshape(n_tok, LANES)[:, :TOP_K]
        idx = idx.reshape(n_tok, LANES)[:, :TOP_K]
        dest, slot_tok, block_expert, n_active = _route(idx, n_tok)
        xg = h2.reshape(n_tok, d)[slot_tok]
        y = _moe(block_expert, n_active, xg, w_gate_up[l].astype(BF16), b_gate_up[l][:, None, :],
                 w_down[l].astype(BF16), b_down[l][:, None, :])
        pos = dest.reshape(n_tok, TOP_K)
        f = jnp.zeros((n_tok, d), F32)
        for kk in range(TOP_K):
            f = f + gates[:, kk:kk + 1] * y[pos[:, kk]].astype(F32)
        x = _final(x1, f.reshape(bsz, seq, d), g2, post_ffn_norm[l].reshape(1, d))
    return x
```

```python
import functools
import math

import jax
import jax.numpy as jnp
from jax import lax
from jax.experimental import pallas as pl
from jax.experimental.pallas import tpu as pltpu

F32 = jnp.float32
BF16 = jnp.bfloat16

D_MODEL = 1024
CHUNK = 64
SSD_INNER = 512
SSD_HEAD_DIM = 64
SSD_HEADS = 8
SSD_GROUPS = 2
SSD_STATE = 128
SSD_CONV = 4
SSD_BC = SSD_GROUPS * SSD_STATE
XBC_DIM = SSD_INNER + 2 * SSD_BC
MLA_V = 64
MLA_HEADS = 8
MLA_NOPE = 64
MLA_ROPE = 32
Q_LORA = 384
KV_LORA = 256
ROPE_BASE = 10000.0
OFF_Z = 0
OFF_XBC = OFF_Z + SSD_INNER
OFF_DT = OFF_XBC + XBC_DIM
OFF_QA = OFF_DT + SSD_HEADS
OFF_KVA = OFF_QA + Q_LORA
OFF_KR = OFF_KVA + KV_LORA
IN_COLS = OFF_KR + MLA_ROPE
N_EXPERTS = 32
TOP_K = 4
D_FF_EXPERT = 1024
SWIGLU_LIMIT = 7.0
SWIGLU_ALPHA = 1.702
NORM_EPS = 1e-6

LANES = 128
HALF_ROPE = MLA_ROPE // 2
ROPE_LO = MLA_NOPE
ROPE_HI = MLA_NOPE + HALF_ROPE

TS_PROJ = 512
SSD_L = 256
ATT_TQ = 512
ATT_TK = 256
MOE_TB = 512
VMEM_LIMIT = 56 * 1024 * 1024


def _dot(a, b):
    return jnp.dot(a, b, preferred_element_type=F32)


def _dot_nt(a, b):
    return lax.dot_general(a, b, (((1,), (1,)), ((), ())), preferred_element_type=F32)


def _dot_tn(a, b):
    return lax.dot_general(a, b, (((0,), (0,)), ((), ())), preferred_element_type=F32)


def _split2(x):
    hi = x.astype(BF16)
    lo = (x - hi.astype(F32)).astype(BF16)
    return hi, lo


def _split3(x):
    h1 = x.astype(BF16)
    r1 = x - h1.astype(F32)
    h2 = r1.astype(BF16)
    h3 = (r1 - h2.astype(F32)).astype(BF16)
    return h1, h2, h3


def _dot3(a, b):
    ah, al = _split2(a)
    bh, bl = _split2(b)
    return _dot(ah, bh) + _dot(ah, bl) + _dot(al, bh)


def _rms(x):
    return x * lax.rsqrt(jnp.mean(x * x, axis=-1, keepdims=True) + NORM_EPS)


def _silu(x):
    return x * jax.nn.sigmoid(x)


def _const_spec(shape):
    nd = len(shape)
    return pl.BlockSpec(shape, lambda *_: (0,) * nd)


def _adaln_body(c_ref, w_ref, b_ref, o_ref):
    o_ref[...] = _dot3(_silu(c_ref[...]), w_ref[...]) + b_ref[...]


def _adaln(c, w_ada, b_ada):
    bsz, d = c.shape
    n = w_ada.shape[1]
    tn = 1024
    return pl.pallas_call(
        _adaln_body,
        grid=(n // tn,),
        in_specs=[_const_spec((bsz, d)),
                  pl.BlockSpec((d, tn), lambda j: (0, j)),
                  pl.BlockSpec((1, tn), lambda j: (0, j))],
        out_specs=pl.BlockSpec((bsz, tn), lambda j: (0, j)),
        out_shape=jax.ShapeDtypeStruct((bsz, n), F32),
        compiler_params=pltpu.CompilerParams(dimension_semantics=("arbitrary",),
                                             vmem_limit_bytes=VMEM_LIMIT),
        name="adaln",
    )(c, w_ada, b_ada.reshape(1, n))


def _rope_block(xb, ct, st, lane):
    partner = jnp.where(lane < ROPE_HI, pltpu.roll(xb, LANES - HALF_ROPE, 1), pltpu.roll(xb, HALF_ROPE, 1))
    return xb * ct + partner * st


def _inproj_body(x_ref, sc_ref, sh_ref, g_ref, ct_ref, st_ref, wz_ref, wxbc_ref, wsm_ref, wqa_ref, wkva_ref,
                 qn_ref, kvn_ref, wqup_ref, wkup_ref, wvup_ref,
                 z_ref, xbc_ref, dt_ref, q_ref, k_ref, v_ref):
    x = x_ref[0]
    h = _rms(x) * (g_ref[...] * (1.0 + sc_ref[0])) + sh_ref[0]
    hb = h.astype(BF16)
    z_ref[0] = _dot(hb, wz_ref[...]).astype(BF16)
    xbc_ref[0] = _dot(hb, wxbc_ref[...]).astype(BF16)
    sm = _dot(hb, wsm_ref[...])
    dt_ref[0] = sm[:, LANES:]
    ct = ct_ref[0]
    st = st_ref[0]
    lane = lax.broadcasted_iota(jnp.int32, ct.shape, 1)
    kr = _rope_block(sm[:, :LANES], ct, st, lane)
    qan = (_rms(_dot(hb, wqa_ref[...])) * qn_ref[...]).astype(BF16)
    q = _dot(qan, wqup_ref[...])
    for hh in range(MLA_HEADS):
        blk = slice(hh * LANES, (hh + 1) * LANES)
        q_ref[0, :, blk] = _rope_block(q[:, blk], ct, st, lane).astype(BF16)
    kvn = (_rms(_dot(hb, wkva_ref[...])) * kvn_ref[...]).astype(BF16)
    k = _dot(kvn, wkup_ref[...])
    for hh in range(MLA_HEADS):
        blk = slice(hh * LANES, (hh + 1) * LANES)
        k_ref[0, :, blk] = (k[:, blk] + kr).astype(BF16)
    v_ref[0] = _dot_nt(wvup_ref[...], kvn).astype(BF16)


def _inproj(x, sc1, sh1, gain, ct, st, wz, wxbc, wsm, wqa, wkva, qn, kvn, wqup, wkup, wvup):
    bsz, seq, d = x.shape
    ts = min(TS_PROJ, seq)
    hw = MLA_HEADS * LANES

    def tok(width):
        return pl.BlockSpec((1, ts, width), lambda b, i: (b, i, 0))

    def per_batch(width):
        return pl.BlockSpec((1, 1, width), lambda b, i: (b, 0, 0))

    weights = (wz, wxbc, wsm, wqa, wkva, qn, kvn, wqup, wkup, wvup)
    out_widths = (SSD_INNER, XBC_DIM, LANES, hw, hw)
    out_dtypes = (BF16, BF16, F32, BF16, BF16)
    vdim = MLA_HEADS * MLA_V
    return pl.pallas_call(
        _inproj_body,
        grid=(bsz, seq // ts),
        in_specs=[tok(d), per_batch(d), per_batch(d), _const_spec((1, d)), tok(LANES), tok(LANES)]
                 + [_const_spec(w.shape) for w in weights],
        out_specs=[tok(w) for w in out_widths] + [pl.BlockSpec((1, vdim, ts), lambda b, i: (b, 0, i))],
        out_shape=[jax.ShapeDtypeStruct((bsz, seq, w), dt) for w, dt in zip(out_widths, out_dtypes)]
                  + [jax.ShapeDtypeStruct((bsz, vdim, seq), BF16)],
        compiler_params=pltpu.CompilerParams(dimension_semantics=("parallel", "parallel"),
                                             vmem_limit_bytes=VMEM_LIMIT),
        name="inproj",
    )(x, sc1, sh1, gain, ct, st, *weights)


CONV_HALO = 16


def _ssd_body(xc_ref, xp_ref, z_ref, dt_ref, cw_ref, cb_ref, dtb_ref, alog_ref, dsk_ref, ng_ref,
              y_ref, xs_scr, st_scr, *, blk):
    i = pl.program_id(1)

    @pl.when(i == 0)
    def _():
        st_scr[...] = jnp.zeros_like(st_scr)

    xs_scr[0:CONV_HALO, :] = jnp.where(i > 0, xp_ref[0].astype(F32), 0.0)
    xs_scr[CONV_HALO:CONV_HALO + blk, :] = xc_ref[0].astype(F32)
    conv = cb_ref[...]
    for kk in range(SSD_CONV):
        off = CONV_HALO - (SSD_CONV - 1) + kk
        conv = conv + cw_ref[kk:kk + 1, :] * xs_scr[off:off + blk, :]
    xa = _silu(conv)
    xs = xa[:, :SSD_INNER]
    bm = xa[:, SSD_INNER:SSD_INNER + SSD_BC]
    cm = xa[:, SSD_INNER + SSD_BC:]

    hl = lax.broadcasted_iota(jnp.int32, (1, LANES), 1)
    dtr = dt_ref[0] + dtb_ref[...]
    dt = jnp.maximum(dtr, 0.0) + jnp.log(1.0 + jnp.exp(-jnp.abs(dtr)))
    a = jnp.where(hl < SSD_HEADS, -jnp.exp(alog_ref[...]), 0.0)
    dta = dt * a
    row = lax.broadcasted_iota(jnp.int32, (blk, blk), 0)
    col = lax.broadcasted_iota(jnp.int32, (blk, blk), 1)
    tril = row >= col
    trilb = jnp.where(tril, 1.0, 0.0).astype(BF16)
    d1, d2, d3 = _split3(dta)
    cs = _dot(trilb, d1) + _dot(trilb, d2) + _dot(trilb, d3)
    cs_last = cs[blk - 1:blk, :]
    ecs = jnp.exp(cs)
    dte = jnp.exp(cs_last - cs)
    cs_t = cs.T

    er = lax.broadcasted_iota(jnp.int32, (LANES, SSD_INNER), 0)
    ec = lax.broadcasted_iota(jnp.int32, (LANES, SSD_INNER), 1)
    expand = jnp.where(ec // SSD_HEAD_DIM == er, 1.0, 0.0).astype(BF16)

    def per_channel(v):
        vh, vl = _split2(v)
        return _dot(vh, expand) + _dot(vl, expand)

    dt_e = per_channel(dt)
    ecs_e = per_channel(ecs)
    dte_e = per_channel(dte)
    xdt = xs * dt_e
    xdt_b = xdt.astype(BF16)
    xw_b = (xdt * dte_e).astype(BF16)

    gw = SSD_INNER // SSD_GROUPS
    heads_per_group = SSD_HEADS // SSD_GROUPS
    lane = lax.broadcasted_iota(jnp.int32, (blk, LANES), 1)
    y_groups = []
    for g in range(SSD_GROUPS):
        bg = bm[:, g * SSD_STATE:(g + 1) * SSD_STATE].astype(BF16)
        cg = cm[:, g * SSD_STATE:(g + 1) * SSD_STATE].astype(BF16)
        cb = _dot_nt(cg, bg)
        state = st_scr[g]
        y_off = _dot(cg, state.astype(BF16))
        pairs = []
        for j in range(heads_per_group // 2):
            xp = xdt_b[:, g * gw + j * LANES:g * gw + (j + 1) * LANES]
            halves = []
            for u in range(2):
                hidx = g * heads_per_group + 2 * j + u
                seg = cs[:, hidx:hidx + 1] - cs_t[hidx:hidx + 1, :]
                dec = jnp.exp(jnp.where(tril, seg, -jnp.inf))
                halves.append(_dot((cb * dec).astype(BF16), xp))
            pairs.append(jnp.where(lane < SSD_HEAD_DIM, halves[0], halves[1]))
        y_diag = jnp.concatenate(pairs, axis=1)
        y_groups.append(y_diag + y_off * ecs_e[:, g * gw:(g + 1) * gw])
        st_scr[g] = (state * ecs_e[blk - 1:blk, g * gw:(g + 1) * gw]
                     + _dot_tn(bg, xw_b[:, g * gw:(g + 1) * gw]))
    y = jnp.concatenate(y_groups, axis=1) + xs * dsk_ref[...]
    y = y * _silu(z_ref[0].astype(F32))
    y = jnp.concatenate([_rms(y[:, g * gw:(g + 1) * gw]) for g in range(SSD_GROUPS)], axis=1)
    y_ref[0] = (y * ng_ref[...]).astype(BF16)


def _ssd(xbc, z, dt, conv_w, conv_b, dt_bias, a_log, d_skip_e, norm_gain):
    bsz, seq, _ = xbc.shape
    blk = min(SSD_L, seq)
    halo_per_blk = blk // CONV_HALO
    body = functools.partial(_ssd_body, blk=blk)
    return pl.pallas_call(
        body,
        grid=(bsz, seq // blk),
        in_specs=[pl.BlockSpec((1, blk, XBC_DIM), lambda b, i: (b, i, 0)),
                  pl.BlockSpec((1, CONV_HALO, XBC_DIM), lambda b, i: (b, jnp.maximum(i * halo_per_blk - 1, 0), 0)),
                  pl.BlockSpec((1, blk, SSD_INNER), lambda b, i: (b, i, 0)),
                  pl.BlockSpec((1, blk, LANES), lambda b, i: (b, i, 0)),
                  _const_spec((SSD_CONV, XBC_DIM)), _const_spec((1, XBC_DIM)),
                  _const_spec((1, LANES)), _const_spec((1, LANES)),
                  _const_spec((1, SSD_INNER)), _const_spec((1, SSD_INNER))],
        out_specs=pl.BlockSpec((1, blk, SSD_INNER), lambda b, i: (b, i, 0)),
        out_shape=jax.ShapeDtypeStruct((bsz, seq, SSD_INNER), BF16),
        scratch_shapes=[pltpu.VMEM((CONV_HALO + blk, XBC_DIM), F32),
                        pltpu.VMEM((SSD_GROUPS, SSD_STATE, SSD_INNER // SSD_GROUPS), F32)],
        compiler_params=pltpu.CompilerParams(dimension_semantics=("parallel", "arbitrary"),
                                             vmem_limit_bytes=VMEM_LIMIT),
        name="ssd",
    )(xbc, xbc, z, dt, conv_w, conv_b, dt_bias, a_log, d_skip_e, norm_gain)


def _attn_body(q_ref, k_ref, v_ref, o_ref, *, tq, tk):
    qi = pl.program_id(2)
    n_sub = tq // tk
    n_full = qi * n_sub
    krow = lax.broadcasted_iota(jnp.int32, (tk, tk), 0)
    qcol = lax.broadcasted_iota(jnp.int32, (tk, tk), 1)
    diag_ok = krow // CHUNK <= qcol // CHUNK
    vrow = lax.broadcasted_iota(jnp.int32, (LANES, tk), 0)
    streams = [(u, r) for u in range(2) for r in range(n_sub)]
    qs = [q_ref[0, r * tk:(r + 1) * tk, u * LANES:(u + 1) * LANES] for u, r in streams]

    def scores(ki, which):
        start = pl.multiple_of(ki * tk, tk)
        k2 = k_ref[0, pl.ds(start, tk), :]
        return [_dot_nt(k2[:, streams[si][0] * LANES:(streams[si][0] + 1) * LANES], qs[si]) for si in which]

    def values_t(ki):
        return v_ref[0, :, pl.ds(pl.multiple_of(ki * tk, tk), tk)]

    def softmax_pv(carry, s, vt, masked):
        m, l, acc = carry
        if masked:
            s = jnp.where(diag_ok, s, -jnp.inf)
        m_new = jnp.maximum(m, jnp.max(s, axis=0, keepdims=True))
        alpha = jnp.exp2(m - m_new)
        p = jnp.exp2(s - m_new)
        l = alpha * l + jnp.sum(p, axis=0, keepdims=True)
        acc = alpha * acc + _dot(vt, p.astype(BF16))
        return m_new, l, acc

    every = list(range(len(streams)))

    def step(ki, state):
        carries, s_cur = state
        s_next = scores(ki + 1, every)
        vt = values_t(ki)
        carries = tuple(softmax_pv(c, s, vt, False) for c, s in zip(carries, s_cur))
        return carries, tuple(s_next)

    init = (jnp.full((1, tk), -jnp.inf, F32), jnp.zeros((1, tk), F32), jnp.zeros((LANES, tk), F32))
    carries, s_cur = lax.fori_loop(0, n_full, step, ((init,) * len(streams), tuple(scores(0, every))))
    carries = list(carries)
    s_cur = dict(zip(every, s_cur))
    for dd in range(n_sub):
        live = [si for si in every if dd <= streams[si][1]]
        later = [si for si in every if dd + 1 <= streams[si][1]]
        s_next = dict(zip(later, scores(n_full + dd + 1, later))) if later else {}
        vt = values_t(n_full + dd)
        for si in live:
            carries[si] = softmax_pv(carries[si], s_cur[si], vt, dd == streams[si][1])
        s_cur = s_next
    for r in range(n_sub):
        o0 = carries[streams.index((0, r))]
        o1 = carries[streams.index((1, r))]
        out_t = jnp.where(vrow < MLA_V, o0[2] / o0[1], o1[2] / o1[1])
        o_ref[0, r * tk:(r + 1) * tk, :] = out_t.T.astype(BF16)


def _attn(q, k, v):
    bsz, seq, _ = q.shape
    tk = min(ATT_TK, seq)
    tq = min(ATT_TQ, seq)
    body = functools.partial(_attn_body, tq=tq, tk=tk)
    return pl.pallas_call(
        body,
        grid=(bsz, MLA_HEADS // 2, seq // tq),
        in_specs=[pl.BlockSpec((1, tq, 2 * LANES), lambda b, hp, i: (b, i, hp)),
                  pl.BlockSpec((1, seq, 2 * LANES), lambda b, hp, i: (b, 0, hp)),
                  pl.BlockSpec((1, LANES, seq), lambda b, hp, i: (b, hp, 0))],
        out_specs=pl.BlockSpec((1, tq, LANES), lambda b, hp, i: (b, i, hp)),
        out_shape=jax.ShapeDtypeStruct((bsz, seq, MLA_HEADS * MLA_V), BF16),
        compiler_params=pltpu.CompilerParams(dimension_semantics=("parallel", "parallel", "arbitrary"),
                                             vmem_limit_bytes=VMEM_LIMIT),
        name="attn",
    )(q, k, v)


def _outproj_body(ys_ref, ya_ref, x_ref, g1_ref, sc2_ref, sh2_ref, mn_ref, wo1_ref, wo2_ref, pmn_ref, pfn_ref,
                  wr_ref, br_ref, x1_ref, h2_ref, gate_ref, idx_ref, cnt_ref, cnt_scr):
    first = (pl.program_id(0) == 0) & (pl.program_id(1) == 0)

    @pl.when(first)
    def _():
        cnt_scr[...] = jnp.zeros_like(cnt_scr)

    yan =(_rms(ya_ref[0].astype(F32)) * mn_ref[...]).astype(BF16)
    mix = _dot(ys_ref[0], wo1_ref[...]) + _dot(yan, wo2_ref[...])
    x1 = x_ref[0] + g1_ref[0] * (_rms(mix) * pmn_ref[...])
    x1_ref[0] = x1
    h2 = _rms(x1) * (pfn_ref[...] * (1.0 + sc2_ref[0])) + sh2_ref[0]
    h2_ref[0] = h2.astype(BF16)
    logits = _dot3(h2, wr_ref[...]) + br_ref[...]
    lane = lax.broadcasted_iota(jnp.int32, logits.shape, 1)
    cur = jnp.where(lane < N_EXPERTS, logits, -jnp.inf)
    vals, idxs = [], []
    for _ in range(TOP_K):
        m = jnp.max(cur, axis=-1, keepdims=True)
        ix = jnp.min(jnp.where(cur == m, lane, LANES), axis=-1, keepdims=True)
        vals.append(m)
        idxs.append(ix)
        cur = jnp.where(lane == ix, -jnp.inf, cur)
    es = [jnp.exp(v - vals[0]) for v in vals]
    denom = es[0]
    for e in es[1:]:
        denom = denom + e
    onehot = jnp.zeros(logits.shape, F32)
    for kk in range(TOP_K):
        onehot = onehot + jnp.where(lane == idxs[kk], 1.0, 0.0)
    ts = logits.shape[0]
    row = lax.broadcasted_iota(jnp.int32, (ts, ts), 0)
    col = lax.broadcasted_iota(jnp.int32, (ts, ts), 1)
    before = jnp.where(row > col, 1.0, 0.0).astype(BF16)
    prior = _dot(before, onehot.astype(BF16)) + cnt_scr[...]
    gate_out = jnp.zeros(logits.shape, F32)
    idx_out = jnp.zeros(logits.shape, jnp.int32)
    for kk in range(TOP_K):
        rank = jnp.sum(jnp.where(lane == idxs[kk], prior, 0.0), axis=-1, keepdims=True)
        gate_out = jnp.where(lane == kk, es[kk] / denom, gate_out)
        idx_out = jnp.where(lane == kk, idxs[kk], idx_out)
        idx_out = jnp.where(lane == TOP_K + kk, rank.astype(jnp.int32), idx_out)
    gate_ref[0] = gate_out
    idx_ref[0] = idx_out
    cnt_scr[...] = cnt_scr[...] + jnp.sum(onehot, axis=0, keepdims=True)
    cnt_ref[...] = cnt_scr[...]


def _outproj(y_ssd, y_att, x, g1, sc2, sh2, mla_norm, wo1, wo2, post_mix_norm, pre_ffn_norm, wr, br):
    bsz, seq, d = x.shape
    ts = min(TS_PROJ, seq)

    def tok(width):
        return pl.BlockSpec((1, ts, width), lambda b, i: (b, i, 0))

    def per_batch(width):
        return pl.BlockSpec((1, 1, width), lambda b, i: (b, 0, 0))

    consts = (mla_norm, wo1, wo2, post_mix_norm, pre_ffn_norm, wr, br)
    return pl.pallas_call(
        _outproj_body,
        grid=(bsz, seq // ts),
        in_specs=[tok(SSD_INNER), tok(MLA_HEADS * MLA_V), tok(d), per_batch(d), per_batch(d), per_batch(d)]
                 + [_const_spec(w.shape) for w in consts],
        out_specs=[tok(d), tok(d), tok(LANES), tok(LANES), _const_spec((1, LANES))],
        out_shape=[jax.ShapeDtypeStruct((bsz, seq, d), F32), jax.ShapeDtypeStruct((bsz, seq, d), BF16),
                   jax.ShapeDtypeStruct((bsz, seq, LANES), F32), jax.ShapeDtypeStruct((bsz, seq, LANES), jnp.int32),
                   jax.ShapeDtypeStruct((1, LANES), F32)],
        scratch_shapes=[pltpu.VMEM((1, LANES), F32)],
        compiler_params=pltpu.CompilerParams(dimension_semantics=("arbitrary", "arbitrary"),
                                             vmem_limit_bytes=VMEM_LIMIT),
        name="outproj",
    )(y_ssd, y_att, x, g1, sc2, sh2, *consts)


def _moe_body(be_ref, na_ref, x_ref, wgu_ref, bgu_ref, wd_ref, bd_ref, y_ref, wgu_b, wd_b):
    i = pl.program_id(0)

    @pl.when((i == 0) | (be_ref[i] != be_ref[jnp.maximum(i - 1, 0)]))
    def _():
        wgu_b[...] = wgu_ref[0].astype(BF16)
        wd_b[...] = wd_ref[0].astype(BF16)

    @pl.when(i < na_ref[0])
    def _():
        gu = _dot(x_ref[...], wgu_b[...]) + bgu_ref[0]
        glu = jnp.minimum(gu[:, :D_FF_EXPERT], SWIGLU_LIMIT)
        lin = jnp.clip(gu[:, D_FF_EXPERT:], -SWIGLU_LIMIT, SWIGLU_LIMIT)
        act = glu * jax.nn.sigmoid(SWIGLU_ALPHA * glu) * (lin + 1.0)
        y_ref[...] = (_dot(act.astype(BF16), wd_b[...]) + bd_ref[0]).astype(BF16)

    @pl.when(i >= na_ref[0])
    def _():
        y_ref[...] = jnp.zeros_like(y_ref)


def _moe(block_expert, n_active, xg, wgu, bgu, wd, bd):
    n_slots, d = xg.shape
    n_blocks = n_slots // MOE_TB
    f2 = wgu.shape[2]
    return pl.pallas_call(
        _moe_body,
        grid_spec=pltpu.PrefetchScalarGridSpec(
            num_scalar_prefetch=2,
            grid=(n_blocks,),
            in_specs=[pl.BlockSpec((MOE_TB, d), lambda i, be, na: (i, 0)),
                      pl.BlockSpec((1, d, f2), lambda i, be, na: (be[i], 0, 0)),
                      pl.BlockSpec((1, 1, f2), lambda i, be, na: (be[i], 0, 0)),
                      pl.BlockSpec((1, f2 // 2, d), lambda i, be, na: (be[i], 0, 0)),
                      pl.BlockSpec((1, 1, d), lambda i, be, na: (be[i], 0, 0))],
            out_specs=pl.BlockSpec((MOE_TB, d), lambda i, be, na: (i, 0)),
            scratch_shapes=[pltpu.VMEM((d, f2), BF16), pltpu.VMEM((f2 // 2, d), BF16)],
        ),
        out_shape=jax.ShapeDtypeStruct((n_slots, d), BF16),
        compiler_params=pltpu.CompilerParams(dimension_semantics=("arbitrary",),
                                             vmem_limit_bytes=VMEM_LIMIT),
        name="moe",
    )(block_expert, n_active, xg, wgu, bgu, wd, bd)


def _final_body(x1_ref, y_ref, gate_ref, g2_ref, gain_ref, o_ref):
    d = x1_ref.shape[-1]
    gates = gate_ref[0]
    f = gates[:, 0:1] * y_ref[0, :, 0:d].astype(F32)
    for kk in range(1, TOP_K):
        f = f + gates[:, kk:kk + 1] * y_ref[0, :, kk * d:(kk + 1) * d].astype(F32)
    o_ref[0] = x1_ref[0] + g2_ref[0] * (_rms(f) * gain_ref[...])


def _final(x1, y4, gates, g2, gain):
    bsz, seq, d = x1.shape
    ts = min(TS_PROJ, seq)

    def tok(width):
        return pl.BlockSpec((1, ts, width), lambda b, i: (b, i, 0))

    return pl.pallas_call(
        _final_body,
        grid=(bsz, seq // ts),
        in_specs=[tok(d), tok(TOP_K * d), tok(LANES), pl.BlockSpec((1, 1, d), lambda b, i: (b, 0, 0)),
                  _const_spec((1, d))],
        out_specs=tok(d),
        out_shape=jax.ShapeDtypeStruct((bsz, seq, d), F32),
        compiler_params=pltpu.CompilerParams(dimension_semantics=("parallel", "parallel"),
                                             vmem_limit_bytes=VMEM_LIMIT),
        name="final",
    )(x1, y4, gates, g2, gain)


def _head_blocks(cols):
    out = []
    for c in cols:
        pad = LANES - c.shape[1]
        out.append(jnp.pad(c, ((0, 0), (0, pad))) if pad else c)
    return jnp.concatenate(out, axis=1)


def _prep_mixer_weights(w_in, w_q_up, w_kv_up):
    d = w_in.shape[0]
    wz = w_in[:, OFF_Z:OFF_XBC]
    wxbc = w_in[:, OFF_XBC:OFF_DT]
    wdt = w_in[:, OFF_DT:OFF_QA]
    wqa = w_in[:, OFF_QA:OFF_KVA]
    wkva = w_in[:, OFF_KVA:OFF_KR]
    wkr = w_in[:, OFF_KR:IN_COLS]
    kr_blk = jnp.concatenate([jnp.zeros((d, ROPE_LO), F32), wkr, jnp.zeros((d, LANES - ROPE_LO - MLA_ROPE), F32)], axis=1)
    dt_blk = jnp.pad(wdt, ((0, 0), (0, LANES - SSD_HEADS)))
    wsm = jnp.concatenate([kr_blk, dt_blk], axis=1)
    qh = MLA_NOPE + MLA_ROPE
    scale = math.log2(math.e) / math.sqrt(qh)
    wqup = _head_blocks([w_q_up[:, h * qh:(h + 1) * qh] for h in range(MLA_HEADS)]) * scale
    kvh = MLA_NOPE + MLA_V
    wkup = _head_blocks([w_kv_up[:, h * kvh:h * kvh + MLA_NOPE] for h in range(MLA_HEADS)])
    wvup = jnp.concatenate([w_kv_up[:, h * kvh + MLA_NOPE:(h + 1) * kvh] for h in range(MLA_HEADS)], axis=1).T
    return tuple(w.astype(BF16) for w in (wz, wxbc, wsm, wqa, wkva)) + tuple(w.astype(BF16) for w in (wqup, wkup, wvup))


def _rope_tables(positions):
    inv_freq = ROPE_BASE ** (-(jnp.arange(HALF_ROPE, dtype=F32) * 2.0 / MLA_ROPE))
    angles = positions.astype(F32)[..., None] * inv_freq
    cos = jnp.cos(angles)
    sin = jnp.sin(angles)
    shp = angles.shape[:-1]
    ct = jnp.concatenate([jnp.ones(shp + (ROPE_LO,), F32), cos, cos,
                          jnp.zeros(shp + (LANES - ROPE_LO - MLA_ROPE,), F32)], axis=-1)
    st = jnp.concatenate([jnp.zeros(shp + (ROPE_LO,), F32), -sin, sin,
                          jnp.zeros(shp + (LANES - ROPE_LO - MLA_ROPE,), F32)], axis=-1)
    return ct, st


def _route(idx, rank, counts, n_tok):
    n_assign = n_tok * TOP_K
    padded = ((counts + MOE_TB - 1) // MOE_TB) * MOE_TB
    padded_end = jnp.cumsum(padded)
    padded_start = padded_end - padded
    experts = jnp.arange(N_EXPERTS, dtype=jnp.int32)
    start_of = jnp.sum(jnp.where(idx[..., None] == experts, padded_start, 0), axis=-1)
    dest = (start_of + rank).reshape(-1)
    n_blocks = n_assign // MOE_TB + N_EXPERTS
    n_slots = n_blocks * MOE_TB
    flat_tok = jnp.arange(n_assign, dtype=jnp.int32) // TOP_K
    slot_tok = jnp.zeros((n_slots,), jnp.int32).at[dest].set(flat_tok)
    block_start = jnp.arange(n_blocks, dtype=jnp.int32) * MOE_TB
    block_expert = jnp.minimum(jnp.sum((padded_end[None, :] <= block_start[:, None]).astype(jnp.int32), axis=1),
                               N_EXPERTS - 1)
    n_active = (padded_end[-1] // MOE_TB).astype(jnp.int32).reshape(1)
    return dest, slot_tok, block_expert, n_active


def kernel(x, c, positions, w_ada, b_ada, pre_mix_norm, w_in, conv_w, conv_b, dt_bias, a_log, d_skip, ssd_norm, q_a_norm, w_q_up, kv_a_norm, w_kv_up, mla_norm, w_out, post_mix_norm, pre_ffn_norm, w_router, b_router, w_gate_up, b_gate_up, w_down, b_down, post_ffn_norm):
    bsz, seq, d = x.shape
    n_tok = bsz * seq
    ct, st = _rope_tables(positions)
    for l in range(w_ada.shape[0]):
        mod = _adaln(c, w_ada[l], b_ada[l])
        sh1, sc1, g1, sh2, sc2, g2 = [m.reshape(bsz, 1, d) for m in jnp.split(mod, 6, axis=-1)]

        mixer_w = _prep_mixer_weights(w_in[l], w_q_up[l], w_kv_up[l])
        z, xbc, dt, q, k, v = _inproj(x, sc1, sh1, pre_mix_norm[l].reshape(1, d), ct, st, *mixer_w[:5],
                                      q_a_norm[l].reshape(1, -1), kv_a_norm[l].reshape(1, -1), *mixer_w[5:])
        pad_h = LANES - SSD_HEADS
        y_ssd = _ssd(xbc, z, dt, conv_w[l], conv_b[l].reshape(1, -1),
                     jnp.pad(dt_bias[l], (0, pad_h)).reshape(1, LANES), jnp.pad(a_log[l], (0, pad_h)).reshape(1, LANES),
                     jnp.repeat(d_skip[l], SSD_HEAD_DIM).reshape(1, -1), ssd_norm[l].reshape(1, -1))
        y_att = _attn(q, k, v)

        wo = w_out[l].astype(BF16)
        wr = jnp.pad(w_router[l], ((0, 0), (0, LANES - N_EXPERTS)))
        br = jnp.pad(b_router[l], (0, LANES - N_EXPERTS)).reshape(1, LANES)
        x1, h2, gates, route, cnt = _outproj(y_ssd, y_att, x, g1, sc2, sh2, mla_norm[l].reshape(1, -1),
                                             wo[:SSD_INNER], wo[SSD_INNER:], post_mix_norm[l].reshape(1, d),
                                             pre_ffn_norm[l].reshape(1, d), wr, br)

        route = route.reshape(n_tok, LANES)
        counts = cnt[0, :N_EXPERTS].astype(jnp.int32)
        dest, slot_tok, block_expert, n_active = _route(route[:, :TOP_K], route[:, TOP_K:2 * TOP_K], counts, n_tok)
        xg = h2.reshape(n_tok, d)[slot_tok]
        y = _moe(block_expert, n_active, xg, w_gate_up[l], b_gate_up[l][:, None, :], w_down[l], b_down[l][:, None, :])
        y4 = y[dest].reshape(bsz, seq, TOP_K * d)
        x = _final(x1, y4, gates, g2, post_ffn_norm[l].reshape(1, d))
    return x
```

```python
import functools
import math

import jax
import jax.numpy as jnp
from jax import lax
from jax.experimental import pallas as pl
from jax.experimental.pallas import tpu as pltpu

F32 = jnp.float32
BF16 = jnp.bfloat16

D_MODEL = 1024
CHUNK = 64
SSD_INNER = 512
SSD_HEAD_DIM = 64
SSD_HEADS = 8
SSD_GROUPS = 2
SSD_STATE = 128
SSD_CONV = 4
SSD_BC = SSD_GROUPS * SSD_STATE
XBC_DIM = SSD_INNER + 2 * SSD_BC
MLA_V = 64
MLA_HEADS = 8
MLA_NOPE = 64
MLA_ROPE = 32
Q_LORA = 384
KV_LORA = 256
ROPE_BASE = 10000.0
OFF_Z = 0
OFF_XBC = OFF_Z + SSD_INNER
OFF_DT = OFF_XBC + XBC_DIM
OFF_QA = OFF_DT + SSD_HEADS
OFF_KVA = OFF_QA + Q_LORA
OFF_KR = OFF_KVA + KV_LORA
IN_COLS = OFF_KR + MLA_ROPE
N_EXPERTS = 32
TOP_K = 4
D_FF_EXPERT = 1024
SWIGLU_LIMIT = 7.0
SWIGLU_ALPHA = 1.702
NORM_EPS = 1e-6

LANES = 128
HALF_ROPE = MLA_ROPE // 2
ROPE_LO = MLA_NOPE
ROPE_HI = MLA_NOPE + HALF_ROPE

TS_PROJ = 512
SSD_L = 256
ATT_TQ = 512
ATT_TK = 256
MOE_TB = 512
BATCH_GROUPS = 2
VMEM_LIMIT = 56 * 1024 * 1024


def _dot(a, b):
    return jnp.dot(a, b, preferred_element_type=F32)


def _dot_nt(a, b):
    return lax.dot_general(a, b, (((1,), (1,)), ((), ())), preferred_element_type=F32)


def _dot_tn(a, b):
    return lax.dot_general(a, b, (((0,), (0,)), ((), ())), preferred_element_type=F32)


def _split2(x):
    hi = x.astype(BF16)
    lo = (x - hi.astype(F32)).astype(BF16)
    return hi, lo


def _split3(x):
    h1 = x.astype(BF16)
    r1 = x - h1.astype(F32)
    h2 = r1.astype(BF16)
    h3 = (r1 - h2.astype(F32)).astype(BF16)
    return h1, h2, h3


def _dot3(a, b):
    ah, al = _split2(a)
    bh, bl = _split2(b)
    return _dot(ah, bh) + _dot(ah, bl) + _dot(al, bh)


def _rms(x):
    return x * lax.rsqrt(jnp.mean(x * x, axis=-1, keepdims=True) + NORM_EPS)


def _silu(x):
    return x * jax.nn.sigmoid(x)


def _const_spec(shape):
    nd = len(shape)
    return pl.BlockSpec(shape, lambda *_: (0,) * nd)


def _adaln_body(c_ref, w_ref, b_ref, o_ref):
    o_ref[...] = _dot3(_silu(c_ref[...]), w_ref[...]) + b_ref[...]


def _adaln(c, w_ada, b_ada):
    bsz, d = c.shape
    n = w_ada.shape[1]
    tn = 1024
    return pl.pallas_call(
        _adaln_body,
        grid=(n // tn,),
        in_specs=[_const_spec((bsz, d)),
                  pl.BlockSpec((d, tn), lambda j: (0, j)),
                  pl.BlockSpec((1, tn), lambda j: (0, j))],
        out_specs=pl.BlockSpec((bsz, tn), lambda j: (0, j)),
        out_shape=jax.ShapeDtypeStruct((bsz, n), F32),
        compiler_params=pltpu.CompilerParams(dimension_semantics=("arbitrary",),
                                             vmem_limit_bytes=VMEM_LIMIT),
        name="adaln",
    )(c, w_ada, b_ada.reshape(1, n))


def _rope_block(xb, ct, st, lane):
    partner = jnp.where(lane < ROPE_HI, pltpu.roll(xb, LANES - HALF_ROPE, 1), pltpu.roll(xb, HALF_ROPE, 1))
    return xb * ct + partner * st


def _inproj_body(x_ref, sc_ref, sh_ref, g_ref, ct_ref, st_ref, wz_ref, wxbc_ref, wsm_ref, wqa_ref, wkva_ref,
                 qn_ref, kvn_ref, wqup_ref, wkup_ref, wvup_ref,
                 z_ref, xbc_ref, dt_ref, q_ref, k_ref, v_ref):
    x = x_ref[0]
    h = _rms(x) * (g_ref[...] * (1.0 + sc_ref[0])) + sh_ref[0]
    hb = h.astype(BF16)
    z_ref[0] = _dot(hb, wz_ref[...]).astype(BF16)
    xbc_ref[0] = _dot(hb, wxbc_ref[...]).astype(BF16)
    sm = _dot(hb, wsm_ref[...])
    dt_ref[0] = sm[:, LANES:]
    ct = ct_ref[0]
    st = st_ref[0]
    lane = lax.broadcasted_iota(jnp.int32, ct.shape, 1)
    kr = _rope_block(sm[:, :LANES], ct, st, lane)
    qan = (_rms(_dot(hb, wqa_ref[...])) * qn_ref[...]).astype(BF16)
    q = _dot(qan, wqup_ref[...])
    for hh in range(MLA_HEADS):
        blk = slice(hh * LANES, (hh + 1) * LANES)
        q_ref[0, :, blk] = _rope_block(q[:, blk], ct, st, lane).astype(BF16)
    kvn = (_rms(_dot(hb, wkva_ref[...])) * kvn_ref[...]).astype(BF16)
    k = _dot(kvn, wkup_ref[...])
    for hh in range(MLA_HEADS):
        blk = slice(hh * LANES, (hh + 1) * LANES)
        k_ref[0, :, blk] = (k[:, blk] + kr).astype(BF16)
    v_ref[0] = _dot_nt(wvup_ref[...], kvn).astype(BF16)


def _inproj(x, sc1, sh1, gain, ct, st, wz, wxbc, wsm, wqa, wkva, qn, kvn, wqup, wkup, wvup):
    bsz, seq, d = x.shape
    ts = min(TS_PROJ, seq)
    hw = MLA_HEADS * LANES

    def tok(width):
        return pl.BlockSpec((1, ts, width), lambda b, i: (b, i, 0))

    def per_batch(width):
        return pl.BlockSpec((1, 1, width), lambda b, i: (b, 0, 0))

    weights = (wz, wxbc, wsm, wqa, wkva, qn, kvn, wqup, wkup, wvup)
    out_widths = (SSD_INNER, XBC_DIM, LANES, hw, hw)
    out_dtypes = (BF16, BF16, F32, BF16, BF16)
    vdim = MLA_HEADS * MLA_V
    return pl.pallas_call(
        _inproj_body,
        grid=(bsz, seq // ts),
        in_specs=[tok(d), per_batch(d), per_batch(d), _const_spec((1, d)), tok(LANES), tok(LANES)]
                 + [_const_spec(w.shape) for w in weights],
        out_specs=[tok(w) for w in out_widths] + [pl.BlockSpec((1, vdim, ts), lambda b, i: (b, 0, i))],
        out_shape=[jax.ShapeDtypeStruct((bsz, seq, w), dt) for w, dt in zip(out_widths, out_dtypes)]
                  + [jax.ShapeDtypeStruct((bsz, vdim, seq), BF16)],
        compiler_params=pltpu.CompilerParams(dimension_semantics=("parallel", "parallel"),
                                             vmem_limit_bytes=VMEM_LIMIT),
        name="inproj",
    )(x, sc1, sh1, gain, ct, st, *weights)


CONV_HALO = 16


def _ssd_body(xc_ref, xp_ref, z_ref, dt_ref, cw_ref, cb_ref, dtb_ref, alog_ref, dsk_ref, ng_ref,
              y_ref, xs_scr, st_scr, *, blk):
    i = pl.program_id(1)

    @pl.when(i == 0)
    def _():
        st_scr[...] = jnp.zeros_like(st_scr)

    xs_scr[0:CONV_HALO, :] = jnp.where(i > 0, xp_ref[0].astype(F32), 0.0)
    xs_scr[CONV_HALO:CONV_HALO + blk, :] = xc_ref[0].astype(F32)
    conv = cb_ref[...]
    for kk in range(SSD_CONV):
        off = CONV_HALO - (SSD_CONV - 1) + kk
        conv = conv + cw_ref[kk:kk + 1, :] * xs_scr[off:off + blk, :]
    xa = _silu(conv)
    xs = xa[:, :SSD_INNER]
    bm = xa[:, SSD_INNER:SSD_INNER + SSD_BC]
    cm = xa[:, SSD_INNER + SSD_BC:]

    hl = lax.broadcasted_iota(jnp.int32, (1, LANES), 1)
    dtr = dt_ref[0] + dtb_ref[...]
    dt = jnp.maximum(dtr, 0.0) + jnp.log(1.0 + jnp.exp(-jnp.abs(dtr)))
    a = jnp.where(hl < SSD_HEADS, -jnp.exp(alog_ref[...]), 0.0)
    dta = dt * a
    row = lax.broadcasted_iota(jnp.int32, (blk, blk), 0)
    col = lax.broadcasted_iota(jnp.int32, (blk, blk), 1)
    tril = row >= col
    trilb = jnp.where(tril, 1.0, 0.0).astype(BF16)
    d1, d2, d3 = _split3(dta)
    cs = _dot(trilb, d1) + _dot(trilb, d2) + _dot(trilb, d3)
    cs_last = cs[blk - 1:blk, :]
    ecs = jnp.exp(cs)
    dte = jnp.exp(cs_last - cs)
    cs_t = cs.T

    er = lax.broadcasted_iota(jnp.int32, (LANES, SSD_INNER), 0)
    ec = lax.broadcasted_iota(jnp.int32, (LANES, SSD_INNER), 1)
    expand = jnp.where(ec // SSD_HEAD_DIM == er, 1.0, 0.0).astype(BF16)

    def per_channel(v):
        vh, vl = _split2(v)
        return _dot(vh, expand) + _dot(vl, expand)

    dt_e = per_channel(dt)
    ecs_e = per_channel(ecs)
    dte_e = per_channel(dte)
    xdt = xs * dt_e
    xdt_b = xdt.astype(BF16)
    xw_b = (xdt * dte_e).astype(BF16)

    gw = SSD_INNER // SSD_GROUPS
    heads_per_group = SSD_HEADS // SSD_GROUPS
    lane = lax.broadcasted_iota(jnp.int32, (blk, LANES), 1)
    y_groups = []
    for g in range(SSD_GROUPS):
        bg = bm[:, g * SSD_STATE:(g + 1) * SSD_STATE].astype(BF16)
        cg = cm[:, g * SSD_STATE:(g + 1) * SSD_STATE].astype(BF16)
        cb = _dot_nt(cg, bg)
        state = st_scr[g]
        y_off = _dot(cg, state.astype(BF16))
        pairs = []
        for j in range(heads_per_group // 2):
            xp = xdt_b[:, g * gw + j * LANES:g * gw + (j + 1) * LANES]
            halves = []
            for u in range(2):
                hidx = g * heads_per_group + 2 * j + u
                seg = cs[:, hidx:hidx + 1] - cs_t[hidx:hidx + 1, :]
                dec = jnp.exp(jnp.where(tril, seg, -jnp.inf))
                halves.append(_dot((cb * dec).astype(BF16), xp))
            pairs.append(jnp.where(lane < SSD_HEAD_DIM, halves[0], halves[1]))
        y_diag = jnp.concatenate(pairs, axis=1)
        y_groups.append(y_diag + y_off * ecs_e[:, g * gw:(g + 1) * gw])
        st_scr[g] = (state * ecs_e[blk - 1:blk, g * gw:(g + 1) * gw]
                     + _dot_tn(bg, xw_b[:, g * gw:(g + 1) * gw]))
    y = jnp.concatenate(y_groups, axis=1) + xs * dsk_ref[...]
    y = y * _silu(z_ref[0].astype(F32))
    y = jnp.concatenate([_rms(y[:, g * gw:(g + 1) * gw]) for g in range(SSD_GROUPS)], axis=1)
    y_ref[0] = (y * ng_ref[...]).astype(BF16)


def _ssd(xbc, z, dt, conv_w, conv_b, dt_bias, a_log, d_skip_e, norm_gain):
    bsz, seq, _ = xbc.shape
    blk = min(SSD_L, seq)
    halo_per_blk = blk // CONV_HALO
    body = functools.partial(_ssd_body, blk=blk)
    return pl.pallas_call(
        body,
        grid=(bsz, seq // blk),
        in_specs=[pl.BlockSpec((1, blk, XBC_DIM), lambda b, i: (b, i, 0)),
                  pl.BlockSpec((1, CONV_HALO, XBC_DIM), lambda b, i: (b, jnp.maximum(i * halo_per_blk - 1, 0), 0)),
                  pl.BlockSpec((1, blk, SSD_INNER), lambda b, i: (b, i, 0)),
                  pl.BlockSpec((1, blk, LANES), lambda b, i: (b, i, 0)),
                  _const_spec((SSD_CONV, XBC_DIM)), _const_spec((1, XBC_DIM)),
                  _const_spec((1, LANES)), _const_spec((1, LANES)),
                  _const_spec((1, SSD_INNER)), _const_spec((1, SSD_INNER))],
        out_specs=pl.BlockSpec((1, blk, SSD_INNER), lambda b, i: (b, i, 0)),
        out_shape=jax.ShapeDtypeStruct((bsz, seq, SSD_INNER), BF16),
        scratch_shapes=[pltpu.VMEM((CONV_HALO + blk, XBC_DIM), F32),
                        pltpu.VMEM((SSD_GROUPS, SSD_STATE, SSD_INNER // SSD_GROUPS), F32)],
        compiler_params=pltpu.CompilerParams(dimension_semantics=("parallel", "arbitrary"),
                                             vmem_limit_bytes=VMEM_LIMIT),
        name="ssd",
    )(xbc, xbc, z, dt, conv_w, conv_b, dt_bias, a_log, d_skip_e, norm_gain)


def _attn_body(q_ref, k_ref, v_ref, o_ref, *, tq, tk):
    qi = pl.program_id(2)
    n_sub = tq // tk
    n_full = qi * n_sub
    krow = lax.broadcasted_iota(jnp.int32, (tk, tk), 0)
    qcol = lax.broadcasted_iota(jnp.int32, (tk, tk), 1)
    diag_ok = krow // CHUNK <= qcol // CHUNK
    vrow = lax.broadcasted_iota(jnp.int32, (LANES, tk), 0)
    streams = [(u, r) for u in range(2) for r in range(n_sub)]
    qs = [q_ref[0, r * tk:(r + 1) * tk, u * LANES:(u + 1) * LANES] for u, r in streams]

    def scores(ki, which):
        start = pl.multiple_of(ki * tk, tk)
        k2 = k_ref[0, pl.ds(start, tk), :]
        return [_dot_nt(k2[:, streams[si][0] * LANES:(streams[si][0] + 1) * LANES], qs[si]) for si in which]

    def values_t(ki):
        return v_ref[0, :, pl.ds(pl.multiple_of(ki * tk, tk), tk)]

    def softmax_pv(carry, s, vt, masked):
        m, l, acc = carry
        if masked:
            s = jnp.where(diag_ok, s, -jnp.inf)
        m_new = jnp.maximum(m, jnp.max(s, axis=0, keepdims=True))
        alpha = jnp.exp2(m - m_new)
        p = jnp.exp2(s - m_new)
        l = alpha * l + jnp.sum(p, axis=0, keepdims=True)
        acc = alpha * acc + _dot(vt, p.astype(BF16))
        return m_new, l, acc

    every = list(range(len(streams)))

    def step(ki, state):
        carries, s_cur = state
        s_next = scores(ki + 1, every)
        vt = values_t(ki)
        carries = tuple(softmax_pv(c, s, vt, False) for c, s in zip(carries, s_cur))
        return carries, tuple(s_next)

    init = (jnp.full((1, tk), -jnp.inf, F32), jnp.zeros((1, tk), F32), jnp.zeros((LANES, tk), F32))
    carries, s_cur = lax.fori_loop(0, n_full, step, ((init,) * len(streams), tuple(scores(0, every))))
    carries = list(carries)
    s_cur = dict(zip(every, s_cur))
    for dd in range(n_sub):
        live = [si for si in every if dd <= streams[si][1]]
        later = [si for si in every if dd + 1 <= streams[si][1]]
        s_next = dict(zip(later, scores(n_full + dd + 1, later))) if later else {}
        vt = values_t(n_full + dd)
        for si in live:
            carries[si] = softmax_pv(carries[si], s_cur[si], vt, dd == streams[si][1])
        s_cur = s_next
    for r in range(n_sub):
        o0 = carries[streams.index((0, r))]
        o1 = carries[streams.index((1, r))]
        out_t = jnp.where(vrow < MLA_V, o0[2] / o0[1], o1[2] / o1[1])
        o_ref[0, r * tk:(r + 1) * tk, :] = out_t.T.astype(BF16)


def _attn(q, k, v):
    bsz, seq, _ = q.shape
    tk = min(ATT_TK, seq)
    tq = min(ATT_TQ, seq)
    body = functools.partial(_attn_body, tq=tq, tk=tk)
    return pl.pallas_call(
        body,
        grid=(bsz, MLA_HEADS // 2, seq // tq),
        in_specs=[pl.BlockSpec((1, tq, 2 * LANES), lambda b, hp, i: (b, i, hp)),
                  pl.BlockSpec((1, seq, 2 * LANES), lambda b, hp, i: (b, 0, hp)),
                  pl.BlockSpec((1, LANES, seq), lambda b, hp, i: (b, hp, 0))],
        out_specs=pl.BlockSpec((1, tq, LANES), lambda b, hp, i: (b, i, hp)),
        out_shape=jax.ShapeDtypeStruct((bsz, seq, MLA_HEADS * MLA_V), BF16),
        compiler_params=pltpu.CompilerParams(dimension_semantics=("parallel", "parallel", "arbitrary"),
                                             vmem_limit_bytes=VMEM_LIMIT),
        name="attn",
    )(q, k, v)


def _outproj_body(ys_ref, ya_ref, x_ref, g1_ref, sc2_ref, sh2_ref, mn_ref, wo1_ref, wo2_ref, pmn_ref, pfn_ref,
                  wr_ref, br_ref, x1_ref, h2_ref, gate_ref, idx_ref, cnt_ref, cnt_scr):
    first = (pl.program_id(0) == 0) & (pl.program_id(1) == 0)

    @pl.when(first)
    def _():
        cnt_scr[...] = jnp.zeros_like(cnt_scr)

    yan =(_rms(ya_ref[0].astype(F32)) * mn_ref[...]).astype(BF16)
    mix = _dot(ys_ref[0], wo1_ref[...]) + _dot(yan, wo2_ref[...])
    x1 = x_ref[0] + g1_ref[0] * (_rms(mix) * pmn_ref[...])
    x1_ref[0] = x1
    h2 = _rms(x1) * (pfn_ref[...] * (1.0 + sc2_ref[0])) + sh2_ref[0]
    h2_ref[0] = h2.astype(BF16)
    logits = _dot3(h2, wr_ref[...]) + br_ref[...]
    lane = lax.broadcasted_iota(jnp.int32, logits.shape, 1)
    cur = jnp.where(lane < N_EXPERTS, logits, -jnp.inf)
    vals, idxs = [], []
    for _ in range(TOP_K):
        m = jnp.max(cur, axis=-1, keepdims=True)
        ix = jnp.min(jnp.where(cur == m, lane, LANES), axis=-1, keepdims=True)
        vals.append(m)
        idxs.append(ix)
        cur = jnp.where(lane == ix, -jnp.inf, cur)
    es = [jnp.exp(v - vals[0]) for v in vals]
    denom = es[0]
    for e in es[1:]:
        denom = denom + e
    onehot = jnp.zeros(logits.shape, F32)
    for kk in range(TOP_K):
        onehot = onehot + jnp.where(lane == idxs[kk], 1.0, 0.0)
    ts = logits.shape[0]
    row = lax.broadcasted_iota(jnp.int32, (ts, ts), 0)
    col = lax.broadcasted_iota(jnp.int32, (ts, ts), 1)
    before = jnp.where(row > col, 1.0, 0.0).astype(BF16)
    prior = _dot(before, onehot.astype(BF16)) + cnt_scr[...]
    gate_out = jnp.zeros(logits.shape, F32)
    idx_out = jnp.zeros(logits.shape, jnp.int32)
    for kk in range(TOP_K):
        rank = jnp.sum(jnp.where(lane == idxs[kk], prior, 0.0), axis=-1, keepdims=True)
        gate_out = jnp.where(lane == kk, es[kk] / denom, gate_out)
        idx_out = jnp.where(lane == kk, idxs[kk], idx_out)
        idx_out = jnp.where(lane == TOP_K + kk, rank.astype(jnp.int32), idx_out)
    gate_ref[0] = gate_out
    idx_ref[0] = idx_out
    cnt_scr[...] = cnt_scr[...] + jnp.sum(onehot, axis=0, keepdims=True)
    cnt_ref[...] = cnt_scr[...]


def _outproj(y_ssd, y_att, x, g1, sc2, sh2, mla_norm, wo1, wo2, post_mix_norm, pre_ffn_norm, wr, br):
    bsz, seq, d = x.shape
    ts = min(TS_PROJ, seq)

    def tok(width):
        return pl.BlockSpec((1, ts, width), lambda b, i: (b, i, 0))

    def per_batch(width):
        return pl.BlockSpec((1, 1, width), lambda b, i: (b, 0, 0))

    consts = (mla_norm, wo1, wo2, post_mix_norm, pre_ffn_norm, wr, br)
    return pl.pallas_call(
        _outproj_body,
        grid=(bsz, seq // ts),
        in_specs=[tok(SSD_INNER), tok(MLA_HEADS * MLA_V), tok(d), per_batch(d), per_batch(d), per_batch(d)]
                 + [_const_spec(w.shape) for w in consts],
        out_specs=[tok(d), tok(d), tok(LANES), tok(LANES), _const_spec((1, LANES))],
        out_shape=[jax.ShapeDtypeStruct((bsz, seq, d), F32), jax.ShapeDtypeStruct((bsz, seq, d), BF16),
                   jax.ShapeDtypeStruct((bsz, seq, LANES), F32), jax.ShapeDtypeStruct((bsz, seq, LANES), jnp.int32),
                   jax.ShapeDtypeStruct((1, LANES), F32)],
        scratch_shapes=[pltpu.VMEM((1, LANES), F32)],
        compiler_params=pltpu.CompilerParams(dimension_semantics=("arbitrary", "arbitrary"),
                                             vmem_limit_bytes=VMEM_LIMIT),
        name="outproj",
    )(y_ssd, y_att, x, g1, sc2, sh2, *consts)


def _moe_body(be_ref, na_ref, x_ref, wgu_ref, bgu_ref, wd_ref, bd_ref, y_ref, wgu_b, wd_b):
    i = pl.program_id(0)

    @pl.when((i == 0) | (be_ref[i] != be_ref[jnp.maximum(i - 1, 0)]))
    def _():
        wgu_b[...] = wgu_ref[0].astype(BF16)
        wd_b[...] = wd_ref[0].astype(BF16)

    @pl.when(i < na_ref[0])
    def _():
        gu = _dot(x_ref[...], wgu_b[...]) + bgu_ref[0]
        glu = jnp.minimum(gu[:, :D_FF_EXPERT], SWIGLU_LIMIT)
        lin = jnp.clip(gu[:, D_FF_EXPERT:], -SWIGLU_LIMIT, SWIGLU_LIMIT)
        act = glu * jax.nn.sigmoid(SWIGLU_ALPHA * glu) * (lin + 1.0)
        y_ref[...] = (_dot(act.astype(BF16), wd_b[...]) + bd_ref[0]).astype(BF16)

    @pl.when(i >= na_ref[0])
    def _():
        y_ref[...] = jnp.zeros_like(y_ref)


def _moe(block_expert, n_active, xg, wgu, bgu, wd, bd):
    n_slots, d = xg.shape
    n_blocks = n_slots // MOE_TB
    f2 = wgu.shape[2]
    return pl.pallas_call(
        _moe_body,
        grid_spec=pltpu.PrefetchScalarGridSpec(
            num_scalar_prefetch=2,
            grid=(n_blocks,),
            in_specs=[pl.BlockSpec((MOE_TB, d), lambda i, be, na: (i, 0)),
                      pl.BlockSpec((1, d, f2), lambda i, be, na: (be[i], 0, 0)),
                      pl.BlockSpec((1, 1, f2), lambda i, be, na: (be[i], 0, 0)),
                      pl.BlockSpec((1, f2 // 2, d), lambda i, be, na: (be[i], 0, 0)),
                      pl.BlockSpec((1, 1, d), lambda i, be, na: (be[i], 0, 0))],
            out_specs=pl.BlockSpec((MOE_TB, d), lambda i, be, na: (i, 0)),
            scratch_shapes=[pltpu.VMEM((d, f2), BF16), pltpu.VMEM((f2 // 2, d), BF16)],
        ),
        out_shape=jax.ShapeDtypeStruct((n_slots, d), BF16),
        compiler_params=pltpu.CompilerParams(dimension_semantics=("arbitrary",),
                                             vmem_limit_bytes=VMEM_LIMIT),
        name="moe",
    )(block_expert, n_active, xg, wgu, bgu, wd, bd)


def _final_body(x1_ref, y_ref, gate_ref, g2_ref, gain_ref, o_ref):
    gates = gate_ref[0]
    f = gates[:, 0:1] * y_ref[0, 0].astype(F32)
    for kk in range(1, TOP_K):
        f = f + gates[:, kk:kk + 1] * y_ref[kk, 0].astype(F32)
    o_ref[0] = x1_ref[0] + g2_ref[0] * (_rms(f) * gain_ref[...])


def _final(x1, y4, gates, g2, gain):
    bsz, seq, d = x1.shape
    ts = min(TS_PROJ, seq)

    def tok(width):
        return pl.BlockSpec((1, ts, width), lambda b, i: (b, i, 0))

    return pl.pallas_call(
        _final_body,
        grid=(bsz, seq // ts),
        in_specs=[tok(d), pl.BlockSpec((TOP_K, 1, ts, d), lambda b, i: (0, b, i, 0)), tok(LANES),
                  pl.BlockSpec((1, 1, d), lambda b, i: (b, 0, 0)), _const_spec((1, d))],
        out_specs=tok(d),
        out_shape=jax.ShapeDtypeStruct((bsz, seq, d), F32),
        compiler_params=pltpu.CompilerParams(dimension_semantics=("parallel", "parallel"),
                                             vmem_limit_bytes=VMEM_LIMIT),
        name="final",
    )(x1, y4, gates, g2, gain)


def _head_blocks(cols):
    out = []
    for c in cols:
        pad = LANES - c.shape[1]
        out.append(jnp.pad(c, ((0, 0), (0, pad))) if pad else c)
    return jnp.concatenate(out, axis=1)


def _prep_mixer_weights(w_in, w_q_up, w_kv_up):
    d = w_in.shape[0]
    wz = w_in[:, OFF_Z:OFF_XBC]
    wxbc = w_in[:, OFF_XBC:OFF_DT]
    wdt = w_in[:, OFF_DT:OFF_QA]
    wqa = w_in[:, OFF_QA:OFF_KVA]
    wkva = w_in[:, OFF_KVA:OFF_KR]
    wkr = w_in[:, OFF_KR:IN_COLS]
    kr_blk = jnp.concatenate([jnp.zeros((d, ROPE_LO), F32), wkr, jnp.zeros((d, LANES - ROPE_LO - MLA_ROPE), F32)], axis=1)
    dt_blk = jnp.pad(wdt, ((0, 0), (0, LANES - SSD_HEADS)))
    wsm = jnp.concatenate([kr_blk, dt_blk], axis=1)
    qh = MLA_NOPE + MLA_ROPE
    scale = math.log2(math.e) / math.sqrt(qh)
    wqup = _head_blocks([w_q_up[:, h * qh:(h + 1) * qh] for h in range(MLA_HEADS)]) * scale
    kvh = MLA_NOPE + MLA_V
    wkup = _head_blocks([w_kv_up[:, h * kvh:h * kvh + MLA_NOPE] for h in range(MLA_HEADS)])
    wvup = jnp.concatenate([w_kv_up[:, h * kvh + MLA_NOPE:(h + 1) * kvh] for h in range(MLA_HEADS)], axis=1).T
    return tuple(w.astype(BF16) for w in (wz, wxbc, wsm, wqa, wkva)) + tuple(w.astype(BF16) for w in (wqup, wkup, wvup))


def _rope_tables(positions):
    inv_freq = ROPE_BASE ** (-(jnp.arange(HALF_ROPE, dtype=F32) * 2.0 / MLA_ROPE))
    angles = positions.astype(F32)[..., None] * inv_freq
    cos = jnp.cos(angles)
    sin = jnp.sin(angles)
    shp = angles.shape[:-1]
    ct = jnp.concatenate([jnp.ones(shp + (ROPE_LO,), F32), cos, cos,
                          jnp.zeros(shp + (LANES - ROPE_LO - MLA_ROPE,), F32)], axis=-1)
    st = jnp.concatenate([jnp.zeros(shp + (ROPE_LO,), F32), -sin, sin,
                          jnp.zeros(shp + (LANES - ROPE_LO - MLA_ROPE,), F32)], axis=-1)
    return ct, st


def _route(idx, rank, counts, n_tok):
    n_assign = n_tok * TOP_K
    padded = ((counts + MOE_TB - 1) // MOE_TB) * MOE_TB
    padded_end = jnp.cumsum(padded)
    padded_start = padded_end - padded
    experts = jnp.arange(N_EXPERTS, dtype=jnp.int32)
    start_of = jnp.sum(jnp.where(idx[..., None] == experts, padded_start, 0), axis=-1)
    dest = (start_of + rank).reshape(-1)
    n_blocks = n_assign // MOE_TB + N_EXPERTS
    n_slots = n_blocks * MOE_TB
    flat_tok = jnp.arange(n_assign, dtype=jnp.int32) // TOP_K
    slot_tok = jnp.zeros((n_slots,), jnp.int32).at[dest].set(flat_tok)
    block_start = jnp.arange(n_blocks, dtype=jnp.int32) * MOE_TB
    block_expert = jnp.minimum(jnp.sum((padded_end[None, :] <= block_start[:, None]).astype(jnp.int32), axis=1),
                               N_EXPERTS - 1)
    n_active = (padded_end[-1] // MOE_TB).astype(jnp.int32).reshape(1)
    return dest, slot_tok, block_expert, n_active


def kernel(x, c, positions, w_ada, b_ada, pre_mix_norm, w_in, conv_w, conv_b, dt_bias, a_log, d_skip, ssd_norm, q_a_norm, w_q_up, kv_a_norm, w_kv_up, mla_norm, w_out, post_mix_norm, pre_ffn_norm, w_router, b_router, w_gate_up, b_gate_up, w_down, b_down, post_ffn_norm):
    bsz, seq, d = x.shape
    ct, st = _rope_tables(positions)
    n_groups = BATCH_GROUPS if bsz % BATCH_GROUPS == 0 else 1
    gb = bsz // n_groups
    n_tok = gb * seq
    pad_h = LANES - SSD_HEADS
    for l in range(w_ada.shape[0]):
        mod = _adaln(c, w_ada[l], b_ada[l])
        mods = [m.reshape(bsz, 1, d) for m in jnp.split(mod, 6, axis=-1)]
        mixer_w = _prep_mixer_weights(w_in[l], w_q_up[l], w_kv_up[l])
        wo = w_out[l].astype(BF16)
        wr = jnp.pad(w_router[l], ((0, 0), (0, LANES - N_EXPERTS)))
        br = jnp.pad(b_router[l], (0, LANES - N_EXPERTS)).reshape(1, LANES)
        dtb = jnp.pad(dt_bias[l], (0, pad_h)).reshape(1, LANES)
        alog = jnp.pad(a_log[l], (0, pad_h)).reshape(1, LANES)
        dsk = jnp.repeat(d_skip[l], SSD_HEAD_DIM).reshape(1, -1)
        outs = []
        for gi in range(n_groups):
            grp = slice(gi * gb, (gi + 1) * gb)
            sh1, sc1, g1, sh2, sc2, g2 = [m[grp] for m in mods]
            xg_ = x[grp]
            z, xbc, dt, q, k, v = _inproj(xg_, sc1, sh1, pre_mix_norm[l].reshape(1, d), ct[grp], st[grp], *mixer_w[:5],
                                          q_a_norm[l].reshape(1, -1), kv_a_norm[l].reshape(1, -1), *mixer_w[5:])
            y_ssd = _ssd(xbc, z, dt, conv_w[l], conv_b[l].reshape(1, -1), dtb, alog, dsk, ssd_norm[l].reshape(1, -1))
            y_att = _attn(q, k, v)
            x1, h2, gates, route, cnt = _outproj(y_ssd, y_att, xg_, g1, sc2, sh2, mla_norm[l].reshape(1, -1),
                                                 wo[:SSD_INNER], wo[SSD_INNER:], post_mix_norm[l].reshape(1, d),
                                                 pre_ffn_norm[l].reshape(1, d), wr, br)
            route = route.reshape(n_tok, LANES)
            counts = cnt[0, :N_EXPERTS].astype(jnp.int32)
            dest, slot_tok, block_expert, n_active = _route(route[:, :TOP_K], route[:, TOP_K:2 * TOP_K], counts, n_tok)
            xs = h2.reshape(n_tok, d)[slot_tok]
            y = _moe(block_expert, n_active, xs, w_gate_up[l], b_gate_up[l][:, None, :], w_down[l], b_down[l][:, None, :])
            y4 = y[dest.reshape(n_tok, TOP_K).T.reshape(-1)].reshape(TOP_K, gb, seq, d)
            outs.append(_final(x1, y4, gates, g2, post_ffn_norm[l].reshape(1, d)))
        x = outs[0] if n_groups == 1 else jnp.concatenate(outs, axis=0)
    return x
```

```python
import functools
import math

import jax
import jax.numpy as jnp
from jax import lax
from jax.experimental import pallas as pl
from jax.experimental.pallas import tpu as pltpu

F32 = jnp.float32
BF16 = jnp.bfloat16

D_MODEL = 1024
CHUNK = 64
SSD_INNER = 512
SSD_HEAD_DIM = 64
SSD_HEADS = 8
SSD_GROUPS = 2
SSD_STATE = 128
SSD_CONV = 4
SSD_BC = SSD_GROUPS * SSD_STATE
XBC_DIM = SSD_INNER + 2 * SSD_BC
MLA_V = 64
MLA_HEADS = 8
MLA_NOPE = 64
MLA_ROPE = 32
Q_LORA = 384
KV_LORA = 256
ROPE_BASE = 10000.0
OFF_Z = 0
OFF_XBC = OFF_Z + SSD_INNER
OFF_DT = OFF_XBC + XBC_DIM
OFF_QA = OFF_DT + SSD_HEADS
OFF_KVA = OFF_QA + Q_LORA
OFF_KR = OFF_KVA + KV_LORA
IN_COLS = OFF_KR + MLA_ROPE
N_EXPERTS = 32
TOP_K = 4
D_FF_EXPERT = 1024
SWIGLU_LIMIT = 7.0
SWIGLU_ALPHA = 1.702
NORM_EPS = 1e-6

LANES = 128
HALF_ROPE = MLA_ROPE // 2
ROPE_LO = MLA_NOPE
ROPE_HI = MLA_NOPE + HALF_ROPE
VT_ROWS = 144

TS_PROJ = 512
SSD_L = 256
ATT_TQ = 512
ATT_TK = 256
MOE_TB = 512
BATCH_GROUPS = 2
VMEM_LIMIT = 56 * 1024 * 1024


def _dot(a, b):
    return jnp.dot(a, b, preferred_element_type=F32)


def _dot_nt(a, b):
    return lax.dot_general(a, b, (((1,), (1,)), ((), ())), preferred_element_type=F32)


def _dot_tn(a, b):
    return lax.dot_general(a, b, (((0,), (0,)), ((), ())), preferred_element_type=F32)


def _split2(x):
    hi = x.astype(BF16)
    lo = (x - hi.astype(F32)).astype(BF16)
    return hi, lo


def _split3(x):
    h1 = x.astype(BF16)
    r1 = x - h1.astype(F32)
    h2 = r1.astype(BF16)
    h3 = (r1 - h2.astype(F32)).astype(BF16)
    return h1, h2, h3


def _dot3(a, b):
    ah, al = _split2(a)
    bh, bl = _split2(b)
    return _dot(ah, bh) + _dot(ah, bl) + _dot(al, bh)


def _rms(x):
    return x * lax.rsqrt(jnp.mean(x * x, axis=-1, keepdims=True) + NORM_EPS)


def _silu(x):
    return x * jax.nn.sigmoid(x)


def _const_spec(shape):
    nd = len(shape)
    return pl.BlockSpec(shape, lambda *_: (0,) * nd)


def _adaln_body(c_ref, w_ref, b_ref, o_ref):
    o_ref[...] = _dot3(_silu(c_ref[...]), w_ref[...]) + b_ref[...]


def _adaln(c, w_ada, b_ada):
    bsz, d = c.shape
    n = w_ada.shape[1]
    tn = 1024
    return pl.pallas_call(
        _adaln_body,
        grid=(n // tn,),
        in_specs=[_const_spec((bsz, d)),
                  pl.BlockSpec((d, tn), lambda j: (0, j)),
                  pl.BlockSpec((1, tn), lambda j: (0, j))],
        out_specs=pl.BlockSpec((bsz, tn), lambda j: (0, j)),
        out_shape=jax.ShapeDtypeStruct((bsz, n), F32),
        compiler_params=pltpu.CompilerParams(dimension_semantics=("arbitrary",),
                                             vmem_limit_bytes=VMEM_LIMIT),
        name="adaln",
    )(c, w_ada, b_ada.reshape(1, n))


def _rope_block(xb, ct, st, lane):
    partner = jnp.where(lane < ROPE_HI, pltpu.roll(xb, LANES - HALF_ROPE, 1), pltpu.roll(xb, HALF_ROPE, 1))
    return xb * ct + partner * st


def _inproj_body(x_ref, sc_ref, sh_ref, g_ref, ct_ref, st_ref, wz_ref, wxbc_ref, wsm_ref, wqa_ref, wkva_ref,
                 qn_ref, kvn_ref, wqup_ref, wkup_ref, wvup_ref,
                 z_ref, xbc_ref, dt_ref, q_ref, k_ref, v_ref):
    x = x_ref[0]
    h = _rms(x) * (g_ref[...] * (1.0 + sc_ref[0])) + sh_ref[0]
    hb = h.astype(BF16)
    z_ref[0] = _dot(hb, wz_ref[...]).astype(BF16)
    xbc_ref[0] = _dot(hb, wxbc_ref[...]).astype(BF16)
    sm = _dot(hb, wsm_ref[...])
    dt_ref[0] = sm[:, LANES:]
    ct = ct_ref[0]
    st = st_ref[0]
    lane = lax.broadcasted_iota(jnp.int32, ct.shape, 1)
    kr = _rope_block(sm[:, :LANES], ct, st, lane)
    qan = (_rms(_dot(hb, wqa_ref[...])) * qn_ref[...]).astype(BF16)
    q = _dot(qan, wqup_ref[...])
    for hh in range(MLA_HEADS):
        blk = slice(hh * LANES, (hh + 1) * LANES)
        q_ref[0, :, blk] = _rope_block(q[:, blk], ct, st, lane).astype(BF16)
    kvn = (_rms(_dot(hb, wkva_ref[...])) * kvn_ref[...]).astype(BF16)
    k = _dot(kvn, wkup_ref[...])
    for hh in range(MLA_HEADS):
        blk = slice(hh * LANES, (hh + 1) * LANES)
        k_ref[0, :, blk] = (k[:, blk] + kr).astype(BF16)
    vt = _dot_nt(wvup_ref[...], kvn)
    vrow = lax.broadcasted_iota(jnp.int32, vt.shape, 0)
    v_ref[0] = jnp.where(vrow % VT_ROWS == LANES, 1.0, vt).astype(BF16)


def _inproj(x, sc1, sh1, gain, ct, st, wz, wxbc, wsm, wqa, wkva, qn, kvn, wqup, wkup, wvup):
    bsz, seq, d = x.shape
    ts = min(TS_PROJ, seq)
    hw = MLA_HEADS * LANES

    def tok(width):
        return pl.BlockSpec((1, ts, width), lambda b, i: (b, i, 0))

    def per_batch(width):
        return pl.BlockSpec((1, 1, width), lambda b, i: (b, 0, 0))

    weights = (wz, wxbc, wsm, wqa, wkva, qn, kvn, wqup, wkup, wvup)
    out_widths = (SSD_INNER, XBC_DIM, LANES, hw, hw)
    out_dtypes = (BF16, BF16, F32, BF16, BF16)
    vdim = (MLA_HEADS // 2) * VT_ROWS
    return pl.pallas_call(
        _inproj_body,
        grid=(bsz, seq // ts),
        in_specs=[tok(d), per_batch(d), per_batch(d), _const_spec((1, d)), tok(LANES), tok(LANES)]
                 + [_const_spec(w.shape) for w in weights],
        out_specs=[tok(w) for w in out_widths] + [pl.BlockSpec((1, vdim, ts), lambda b, i: (b, 0, i))],
        out_shape=[jax.ShapeDtypeStruct((bsz, seq, w), dt) for w, dt in zip(out_widths, out_dtypes)]
                  + [jax.ShapeDtypeStruct((bsz, vdim, seq), BF16)],
        compiler_params=pltpu.CompilerParams(dimension_semantics=("parallel", "parallel"),
                                             vmem_limit_bytes=VMEM_LIMIT),
        name="inproj",
    )(x, sc1, sh1, gain, ct, st, *weights)


CONV_HALO = 16


def _ssd_body(xc_ref, xp_ref, z_ref, dt_ref, cw_ref, cb_ref, dtb_ref, alog_ref, dsk_ref, ng_ref,
              y_ref, xs_scr, st_scr, *, blk):
    i = pl.program_id(1)

    @pl.when(i == 0)
    def _():
        st_scr[...] = jnp.zeros_like(st_scr)

    xs_scr[0:CONV_HALO, :] = jnp.where(i > 0, xp_ref[0].astype(F32), 0.0)
    xs_scr[CONV_HALO:CONV_HALO + blk, :] = xc_ref[0].astype(F32)
    conv = cb_ref[...]
    for kk in range(SSD_CONV):
        off = CONV_HALO - (SSD_CONV - 1) + kk
        conv = conv + cw_ref[kk:kk + 1, :] * xs_scr[off:off + blk, :]
    xa = _silu(conv)
    xs = xa[:, :SSD_INNER]
    bm = xa[:, SSD_INNER:SSD_INNER + SSD_BC]
    cm = xa[:, SSD_INNER + SSD_BC:]

    hl = lax.broadcasted_iota(jnp.int32, (1, LANES), 1)
    dtr = dt_ref[0] + dtb_ref[...]
    dt = jnp.maximum(dtr, 0.0) + jnp.log(1.0 + jnp.exp(-jnp.abs(dtr)))
    a = jnp.where(hl < SSD_HEADS, -jnp.exp(alog_ref[...]), 0.0)
    dta = dt * a
    row = lax.broadcasted_iota(jnp.int32, (blk, blk), 0)
    col = lax.broadcasted_iota(jnp.int32, (blk, blk), 1)
    tril = row >= col
    trilb = jnp.where(tril, 1.0, 0.0).astype(BF16)
    d1, d2, d3 = _split3(dta)
    cs = _dot(trilb, d1) + _dot(trilb, d2) + _dot(trilb, d3)
    cs_last = cs[blk - 1:blk, :]
    ecs = jnp.exp(cs)
    dte = jnp.exp(cs_last - cs)
    cs_t = cs.T

    er = lax.broadcasted_iota(jnp.int32, (LANES, SSD_INNER), 0)
    ec = lax.broadcasted_iota(jnp.int32, (LANES, SSD_INNER), 1)
    expand = jnp.where(ec // SSD_HEAD_DIM == er, 1.0, 0.0).astype(BF16)

    def per_channel(v):
        vh, vl = _split2(v)
        return _dot(vh, expand) + _dot(vl, expand)

    dt_e = per_channel(dt)
    ecs_e = per_channel(ecs)
    dte_e = per_channel(dte)
    xdt = xs * dt_e
    xdt_b = xdt.astype(BF16)
    xw_b = (xdt * dte_e).astype(BF16)

    gw = SSD_INNER // SSD_GROUPS
    heads_per_group = SSD_HEADS // SSD_GROUPS
    lane = lax.broadcasted_iota(jnp.int32, (blk, LANES), 1)
    y_groups = []
    for g in range(SSD_GROUPS):
        bg = bm[:, g * SSD_STATE:(g + 1) * SSD_STATE].astype(BF16)
        cg = cm[:, g * SSD_STATE:(g + 1) * SSD_STATE].astype(BF16)
        cb = _dot_nt(cg, bg)
        state = st_scr[g]
        y_off = _dot(cg, state.astype(BF16))
        pairs = []
        for j in range(heads_per_group // 2):
            xp = xdt_b[:, g * gw + j * LANES:g * gw + (j + 1) * LANES]
            halves = []
            for u in range(2):
                hidx = g * heads_per_group + 2 * j + u
                seg = cs[:, hidx:hidx + 1] - cs_t[hidx:hidx + 1, :]
                dec = jnp.exp(jnp.where(tril, seg, -jnp.inf))
                halves.append(_dot((cb * dec).astype(BF16), xp))
            pairs.append(jnp.where(lane < SSD_HEAD_DIM, halves[0], halves[1]))
        y_diag = jnp.concatenate(pairs, axis=1)
        y_groups.append(y_diag + y_off * ecs_e[:, g * gw:(g + 1) * gw])
        st_scr[g] = (state * ecs_e[blk - 1:blk, g * gw:(g + 1) * gw]
                     + _dot_tn(bg, xw_b[:, g * gw:(g + 1) * gw]))
    y = jnp.concatenate(y_groups, axis=1) + xs * dsk_ref[...]
    y = y * _silu(z_ref[0].astype(F32))
    y = jnp.concatenate([_rms(y[:, g * gw:(g + 1) * gw]) for g in range(SSD_GROUPS)], axis=1)
    y_ref[0] = (y * ng_ref[...]).astype(BF16)


def _ssd(xbc, z, dt, conv_w, conv_b, dt_bias, a_log, d_skip_e, norm_gain):
    bsz, seq, _ = xbc.shape
    blk = min(SSD_L, seq)
    halo_per_blk = blk // CONV_HALO
    body = functools.partial(_ssd_body, blk=blk)
    return pl.pallas_call(
        body,
        grid=(bsz, seq // blk),
        in_specs=[pl.BlockSpec((1, blk, XBC_DIM), lambda b, i: (b, i, 0)),
                  pl.BlockSpec((1, CONV_HALO, XBC_DIM), lambda b, i: (b, jnp.maximum(i * halo_per_blk - 1, 0), 0)),
                  pl.BlockSpec((1, blk, SSD_INNER), lambda b, i: (b, i, 0)),
                  pl.BlockSpec((1, blk, LANES), lambda b, i: (b, i, 0)),
                  _const_spec((SSD_CONV, XBC_DIM)), _const_spec((1, XBC_DIM)),
                  _const_spec((1, LANES)), _const_spec((1, LANES)),
                  _const_spec((1, SSD_INNER)), _const_spec((1, SSD_INNER))],
        out_specs=pl.BlockSpec((1, blk, SSD_INNER), lambda b, i: (b, i, 0)),
        out_shape=jax.ShapeDtypeStruct((bsz, seq, SSD_INNER), BF16),
        scratch_shapes=[pltpu.VMEM((CONV_HALO + blk, XBC_DIM), F32),
                        pltpu.VMEM((SSD_GROUPS, SSD_STATE, SSD_INNER // SSD_GROUPS), F32)],
        compiler_params=pltpu.CompilerParams(dimension_semantics=("parallel", "arbitrary"),
                                             vmem_limit_bytes=VMEM_LIMIT),
        name="ssd",
    )(xbc, xbc, z, dt, conv_w, conv_b, dt_bias, a_log, d_skip_e, norm_gain)


def _attn_body(q_ref, k_ref, v_ref, o_ref, *scratch, tq, tk):
    n_streams = 2 * (tq // tk)
    s_scr = (scratch[:n_streams], scratch[n_streams:2 * n_streams])
    acc_scr = scratch[2 * n_streams:]
    qi = pl.program_id(2)
    n_sub = tq // tk
    n_full = qi * n_sub
    krow = lax.broadcasted_iota(jnp.int32, (tk, tk), 0)
    qcol = lax.broadcasted_iota(jnp.int32, (tk, tk), 1)
    diag_ok = krow // CHUNK <= qcol // CHUNK
    vrow = lax.broadcasted_iota(jnp.int32, (LANES, tk), 0)
    streams = [(u, r) for u in range(2) for r in range(n_sub)]
    qs = [q_ref[0, r * tk:(r + 1) * tk, u * LANES:(u + 1) * LANES] for u, r in streams]

    def put_scores(ki, which, slot):
        start = pl.multiple_of(ki * tk, tk)
        k2 = k_ref[0, pl.ds(start, tk), :]
        out = {}
        for si in which:
            u = streams[si][0]
            s = _dot_nt(k2[:, u * LANES:(u + 1) * LANES], qs[si])
            s_scr[slot][si][...] = s
            out[si] = jnp.max(s, axis=0, keepdims=True)
        return out

    def values_t(ki):
        return v_ref[0, :, pl.ds(pl.multiple_of(ki * tk, tk), tk)]

    def softmax_pv(si, m, s_max, slot, vt, masked):
        s = s_scr[slot][si][...]
        if masked:
            s = jnp.where(diag_ok, s, -jnp.inf)
            s_max = jnp.max(s, axis=0, keepdims=True)
        m_new = jnp.maximum(m, s_max)
        alpha = jnp.exp2(m - m_new)
        p = jnp.exp2((s - m_new).astype(BF16))
        acc_scr[si][...] = alpha * acc_scr[si][...] + _dot(vt, p)
        return m_new

    every = list(range(len(streams)))
    for ref in acc_scr:
        ref[...] = jnp.zeros_like(ref)

    def step(j, state):
        ms, s_maxes = state
        for slot in range(2):
            ki = 2 * j + slot
            nxt = put_scores(ki + 1, every, 1 - slot)
            vt = values_t(ki)
            ms = tuple(softmax_pv(si, ms[si], s_maxes[si], slot, vt, False) for si in every)
            s_maxes = tuple(nxt[si] for si in every)
        return ms, s_maxes

    first = put_scores(0, every, 0)
    m_init = jnp.full((1, tk), -jnp.inf, F32)
    ms, s_maxes = lax.fori_loop(0, n_full // 2, step, ((m_init,) * len(streams), tuple(first[si] for si in every)))
    ms = list(ms)
    s_maxes = dict(zip(every, s_maxes))
    for dd in range(n_sub):
        slot = dd % 2
        live = [si for si in every if dd <= streams[si][1]]
        later = [si for si in every if dd + 1 <= streams[si][1]]
        nxt = put_scores(n_full + dd + 1, later, 1 - slot) if later else {}
        vt = values_t(n_full + dd)
        for si in live:
            ms[si] = softmax_pv(si, ms[si], s_maxes[si], slot, vt, dd == streams[si][1])
        s_maxes = nxt
    for r in range(n_sub):
        a0 = acc_scr[streams.index((0, r))][...]
        a1 = acc_scr[streams.index((1, r))][...]
        out_t = jnp.where(vrow < MLA_V, a0[:LANES] / a0[LANES:LANES + 1], a1[:LANES] / a1[LANES:LANES + 1])
        o_ref[0, r * tk:(r + 1) * tk, :] = out_t.T.astype(BF16)


def _attn(q, k, v):
    bsz, seq, _ = q.shape
    tk = min(ATT_TK, seq)
    tq = min(ATT_TQ, seq)
    assert (tq // tk) % 2 == 0, "the two-slot score pipeline needs an even number of query sub-tiles"
    n_streams = 2 * (tq // tk)
    body = functools.partial(_attn_body, tq=tq, tk=tk)
    return pl.pallas_call(
        body,
        grid=(bsz, MLA_HEADS // 2, seq // tq),
        in_specs=[pl.BlockSpec((1, tq, 2 * LANES), lambda b, hp, i: (b, i, hp)),
                  pl.BlockSpec((1, seq, 2 * LANES), lambda b, hp, i: (b, 0, hp)),
                  pl.BlockSpec((1, VT_ROWS, seq), lambda b, hp, i: (b, hp, 0))],
        out_specs=pl.BlockSpec((1, tq, LANES), lambda b, hp, i: (b, i, hp)),
        out_shape=jax.ShapeDtypeStruct((bsz, seq, MLA_HEADS * MLA_V), BF16),
        scratch_shapes=[pltpu.VMEM((tk, tk), F32)] * (2 * n_streams) + [pltpu.VMEM((VT_ROWS, tk), F32)] * n_streams,
        compiler_params=pltpu.CompilerParams(dimension_semantics=("parallel", "parallel", "arbitrary"),
                                             vmem_limit_bytes=VMEM_LIMIT),
        name="attn",
    )(q, k, v)


def _outproj_body(ys_ref, ya_ref, x_ref, g1_ref, sc2_ref, sh2_ref, mn_ref, wo1_ref, wo2_ref, pmn_ref, pfn_ref,
                  wr_ref, br_ref, x1_ref, h2_ref, gate_ref, idx_ref, cnt_ref, cnt_scr):
    first = (pl.program_id(0) == 0) & (pl.program_id(1) == 0)

    @pl.when(first)
    def _():
        cnt_scr[...] = jnp.zeros_like(cnt_scr)

    yan =(_rms(ya_ref[0].astype(F32)) * mn_ref[...]).astype(BF16)
    mix = _dot(ys_ref[0], wo1_ref[...]) + _dot(yan, wo2_ref[...])
    x1 = x_ref[0] + g1_ref[0] * (_rms(mix) * pmn_ref[...])
    x1_ref[0] = x1
    h2 = _rms(x1) * (pfn_ref[...] * (1.0 + sc2_ref[0])) + sh2_ref[0]
    h2_ref[0] = h2.astype(BF16)
    logits = _dot3(h2, wr_ref[...]) + br_ref[...]
    lane = lax.broadcasted_iota(jnp.int32, logits.shape, 1)
    cur = jnp.where(lane < N_EXPERTS, logits, -jnp.inf)
    vals, idxs = [], []
    for _ in range(TOP_K):
        m = jnp.max(cur, axis=-1, keepdims=True)
        ix = jnp.min(jnp.where(cur == m, lane, LANES), axis=-1, keepdims=True)
        vals.append(m)
        idxs.append(ix)
        cur = jnp.where(lane == ix, -jnp.inf, cur)
    es = [jnp.exp(v - vals[0]) for v in vals]
    denom = es[0]
    for e in es[1:]:
        denom = denom + e
    onehot = jnp.zeros(logits.shape, F32)
    for kk in range(TOP_K):
        onehot = onehot + jnp.where(lane == idxs[kk], 1.0, 0.0)
    ts = logits.shape[0]
    row = lax.broadcasted_iota(jnp.int32, (ts, ts), 0)
    col = lax.broadcasted_iota(jnp.int32, (ts, ts), 1)
    before = jnp.where(row > col, 1.0, 0.0).astype(BF16)
    prior = _dot(before, onehot.astype(BF16)) + cnt_scr[...]
    gate_out = jnp.zeros(logits.shape, F32)
    idx_out = jnp.zeros(logits.shape, jnp.int32)
    for kk in range(TOP_K):
        rank = jnp.sum(jnp.where(lane == idxs[kk], prior, 0.0), axis=-1, keepdims=True)
        gate_out = jnp.where(lane == kk, es[kk] / denom, gate_out)
        idx_out = jnp.where(lane == kk, idxs[kk], idx_out)
        idx_out = jnp.where(lane == TOP_K + kk, rank.astype(jnp.int32), idx_out)
    gate_ref[0] = gate_out
    idx_ref[0] = idx_out
    cnt_scr[...] = cnt_scr[...] + jnp.sum(onehot, axis=0, keepdims=True)
    cnt_ref[...] = cnt_scr[...]


def _outproj(y_ssd, y_att, x, g1, sc2, sh2, mla_norm, wo1, wo2, post_mix_norm, pre_ffn_norm, wr, br):
    bsz, seq, d = x.shape
    ts = min(TS_PROJ, seq)

    def tok(width):
        return pl.BlockSpec((1, ts, width), lambda b, i: (b, i, 0))

    def per_batch(width):
        return pl.BlockSpec((1, 1, width), lambda b, i: (b, 0, 0))

    consts = (mla_norm, wo1, wo2, post_mix_norm, pre_ffn_norm, wr, br)
    return pl.pallas_call(
        _outproj_body,
        grid=(bsz, seq // ts),
        in_specs=[tok(SSD_INNER), tok(MLA_HEADS * MLA_V), tok(d), per_batch(d), per_batch(d), per_batch(d)]
                 + [_const_spec(w.shape) for w in consts],
        out_specs=[tok(d), tok(d), tok(LANES), tok(LANES), _const_spec((1, LANES))],
        out_shape=[jax.ShapeDtypeStruct((bsz, seq, d), F32), jax.ShapeDtypeStruct((bsz, seq, d), BF16),
                   jax.ShapeDtypeStruct((bsz, seq, LANES), F32), jax.ShapeDtypeStruct((bsz, seq, LANES), jnp.int32),
                   jax.ShapeDtypeStruct((1, LANES), F32)],
        scratch_shapes=[pltpu.VMEM((1, LANES), F32)],
        compiler_params=pltpu.CompilerParams(dimension_semantics=("arbitrary", "arbitrary"),
                                             vmem_limit_bytes=VMEM_LIMIT),
        name="outproj",
    )(y_ssd, y_att, x, g1, sc2, sh2, *consts)


def _moe_body(be_ref, na_ref, x_ref, wgu_ref, bgu_ref, wd_ref, bd_ref, y_ref, wgu_b, wd_b):
    i = pl.program_id(0)

    @pl.when((i == 0) | (be_ref[i] != be_ref[jnp.maximum(i - 1, 0)]))
    def _():
        wgu_b[...] = wgu_ref[0].astype(BF16)
        wd_b[...] = wd_ref[0].astype(BF16)

    @pl.when(i < na_ref[0])
    def _():
        gu = _dot(x_ref[...], wgu_b[...]) + bgu_ref[0]
        glu = jnp.minimum(gu[:, :D_FF_EXPERT], SWIGLU_LIMIT)
        lin = jnp.clip(gu[:, D_FF_EXPERT:], -SWIGLU_LIMIT, SWIGLU_LIMIT)
        act = glu * jax.nn.sigmoid(SWIGLU_ALPHA * glu) * (lin + 1.0)
        y_ref[...] = (_dot(act.astype(BF16), wd_b[...]) + bd_ref[0]).astype(BF16)

    @pl.when(i >= na_ref[0])
    def _():
        y_ref[...] = jnp.zeros_like(y_ref)


def _moe(block_expert, n_active, xg, wgu, bgu, wd, bd):
    n_slots, d = xg.shape
    n_blocks = n_slots // MOE_TB
    f2 = wgu.shape[2]
    return pl.pallas_call(
        _moe_body,
        grid_spec=pltpu.PrefetchScalarGridSpec(
            num_scalar_prefetch=2,
            grid=(n_blocks,),
            in_specs=[pl.BlockSpec((MOE_TB, d), lambda i, be, na: (i, 0)),
                      pl.BlockSpec((1, d, f2), lambda i, be, na: (be[i], 0, 0)),
                      pl.BlockSpec((1, 1, f2), lambda i, be, na: (be[i], 0, 0)),
                      pl.BlockSpec((1, f2 // 2, d), lambda i, be, na: (be[i], 0, 0)),
                      pl.BlockSpec((1, 1, d), lambda i, be, na: (be[i], 0, 0))],
            out_specs=pl.BlockSpec((MOE_TB, d), lambda i, be, na: (i, 0)),
            scratch_shapes=[pltpu.VMEM((d, f2), BF16), pltpu.VMEM((f2 // 2, d), BF16)],
        ),
        out_shape=jax.ShapeDtypeStruct((n_slots, d), BF16),
        compiler_params=pltpu.CompilerParams(dimension_semantics=("arbitrary",),
                                             vmem_limit_bytes=VMEM_LIMIT),
        name="moe",
    )(block_expert, n_active, xg, wgu, bgu, wd, bd)


def _final_body(x1_ref, y_ref, gate_ref, g2_ref, gain_ref, o_ref):
    gates = gate_ref[0]
    f = gates[:, 0:1] * y_ref[0, 0].astype(F32)
    for kk in range(1, TOP_K):
        f = f + gates[:, kk:kk + 1] * y_ref[kk, 0].astype(F32)
    o_ref[0] = x1_ref[0] + g2_ref[0] * (_rms(f) * gain_ref[...])


def _final(x1, y4, gates, g2, gain):
    bsz, seq, d = x1.shape
    ts = min(TS_PROJ, seq)

    def tok(width):
        return pl.BlockSpec((1, ts, width), lambda b, i: (b, i, 0))

    return pl.pallas_call(
        _final_body,
        grid=(bsz, seq // ts),
        in_specs=[tok(d), pl.BlockSpec((TOP_K, 1, ts, d), lambda b, i: (0, b, i, 0)), tok(LANES),
                  pl.BlockSpec((1, 1, d), lambda b, i: (b, 0, 0)), _const_spec((1, d))],
        out_specs=tok(d),
        out_shape=jax.ShapeDtypeStruct((bsz, seq, d), F32),
        compiler_params=pltpu.CompilerParams(dimension_semantics=("parallel", "parallel"),
                                             vmem_limit_bytes=VMEM_LIMIT),
        name="final",
    )(x1, y4, gates, g2, gain)


def _head_blocks(cols):
    out = []
    for c in cols:
        pad = LANES - c.shape[1]
        out.append(jnp.pad(c, ((0, 0), (0, pad))) if pad else c)
    return jnp.concatenate(out, axis=1)


def _prep_mixer_weights(w_in, w_q_up, w_kv_up):
    d = w_in.shape[0]
    wz = w_in[:, OFF_Z:OFF_XBC]
    wxbc = w_in[:, OFF_XBC:OFF_DT]
    wdt = w_in[:, OFF_DT:OFF_QA]
    wqa = w_in[:, OFF_QA:OFF_KVA]
    wkva = w_in[:, OFF_KVA:OFF_KR]
    wkr = w_in[:, OFF_KR:IN_COLS]
    kr_blk = jnp.concatenate([jnp.zeros((d, ROPE_LO), F32), wkr, jnp.zeros((d, LANES - ROPE_LO - MLA_ROPE), F32)], axis=1)
    dt_blk = jnp.pad(wdt, ((0, 0), (0, LANES - SSD_HEADS)))
    wsm = jnp.concatenate([kr_blk, dt_blk], axis=1)
    qh = MLA_NOPE + MLA_ROPE
    scale = math.log2(math.e) / math.sqrt(qh)
    wqup = _head_blocks([w_q_up[:, h * qh:(h + 1) * qh] for h in range(MLA_HEADS)]) * scale
    kvh = MLA_NOPE + MLA_V
    wkup = _head_blocks([w_kv_up[:, h * kvh:h * kvh + MLA_NOPE] for h in range(MLA_HEADS)])
    vcols = []
    for h in range(MLA_HEADS):
        vcols.append(w_kv_up[:, h * kvh + MLA_NOPE:(h + 1) * kvh])
        if h % 2 == 1:
            vcols.append(jnp.zeros((w_kv_up.shape[0], VT_ROWS - LANES), F32))
    wvup = jnp.concatenate(vcols, axis=1).T
    return tuple(w.astype(BF16) for w in (wz, wxbc, wsm, wqa, wkva)) + tuple(w.astype(BF16) for w in (wqup, wkup, wvup))


def _rope_tables(positions):
    inv_freq = ROPE_BASE ** (-(jnp.arange(HALF_ROPE, dtype=F32) * 2.0 / MLA_ROPE))
    angles = positions.astype(F32)[..., None] * inv_freq
    cos = jnp.cos(angles)
    sin = jnp.sin(angles)
    shp = angles.shape[:-1]
    ct = jnp.concatenate([jnp.ones(shp + (ROPE_LO,), F32), cos, cos,
                          jnp.zeros(shp + (LANES - ROPE_LO - MLA_ROPE,), F32)], axis=-1)
    st = jnp.concatenate([jnp.zeros(shp + (ROPE_LO,), F32), -sin, sin,
                          jnp.zeros(shp + (LANES - ROPE_LO - MLA_ROPE,), F32)], axis=-1)
    return ct, st


def _route(idx, rank, counts, n_tok):
    n_assign = n_tok * TOP_K
    padded = ((counts + MOE_TB - 1) // MOE_TB) * MOE_TB
    padded_end = jnp.cumsum(padded)
    padded_start = padded_end - padded
    experts = jnp.arange(N_EXPERTS, dtype=jnp.int32)
    start_of = jnp.sum(jnp.where(idx[..., None] == experts, padded_start, 0), axis=-1)
    dest = (start_of + rank).reshape(-1)
    n_blocks = n_assign // MOE_TB + N_EXPERTS
    n_slots = n_blocks * MOE_TB
    flat_tok = jnp.arange(n_assign, dtype=jnp.int32) // TOP_K
    slot_tok = jnp.zeros((n_slots,), jnp.int32).at[dest].set(flat_tok, unique_indices=True, mode="promise_in_bounds")
    block_start = jnp.arange(n_blocks, dtype=jnp.int32) * MOE_TB
    block_expert = jnp.minimum(jnp.sum((padded_end[None, :] <= block_start[:, None]).astype(jnp.int32), axis=1),
                               N_EXPERTS - 1)
    n_active = (padded_end[-1] // MOE_TB).astype(jnp.int32).reshape(1)
    return dest, slot_tok, block_expert, n_active


def kernel(x, c, positions, w_ada, b_ada, pre_mix_norm, w_in, conv_w, conv_b, dt_bias, a_log, d_skip, ssd_norm, q_a_norm, w_q_up, kv_a_norm, w_kv_up, mla_norm, w_out, post_mix_norm, pre_ffn_norm, w_router, b_router, w_gate_up, b_gate_up, w_down, b_down, post_ffn_norm):
    bsz, seq, d = x.shape
    ct, st = _rope_tables(positions)
    n_groups = BATCH_GROUPS if bsz % BATCH_GROUPS == 0 else 1
    gb = bsz // n_groups
    n_tok = gb * seq
    pad_h = LANES - SSD_HEADS
    for l in range(w_ada.shape[0]):
        mod = _adaln(c, w_ada[l], b_ada[l])
        mods = [m.reshape(bsz, 1, d) for m in jnp.split(mod, 6, axis=-1)]
        mixer_w = _prep_mixer_weights(w_in[l], w_q_up[l], w_kv_up[l])
        wo = w_out[l].astype(BF16)
        wr = jnp.pad(w_router[l], ((0, 0), (0, LANES - N_EXPERTS)))
        br = jnp.pad(b_router[l], (0, LANES - N_EXPERTS)).reshape(1, LANES)
        dtb = jnp.pad(dt_bias[l], (0, pad_h)).reshape(1, LANES)
        alog = jnp.pad(a_log[l], (0, pad_h)).reshape(1, LANES)
        dsk = jnp.repeat(d_skip[l], SSD_HEAD_DIM).reshape(1, -1)
        outs = []
        for gi in range(n_groups):
            grp = slice(gi * gb, (gi + 1) * gb)
            sh1, sc1, g1, sh2, sc2, g2 = [m[grp] for m in mods]
            xg_ = x[grp]
            z, xbc, dt, q, k, v = _inproj(xg_, sc1, sh1, pre_mix_norm[l].reshape(1, d), ct[grp], st[grp], *mixer_w[:5],
                                          q_a_norm[l].reshape(1, -1), kv_a_norm[l].reshape(1, -1), *mixer_w[5:])
            y_ssd = _ssd(xbc, z, dt, conv_w[l], conv_b[l].reshape(1, -1), dtb, alog, dsk, ssd_norm[l].reshape(1, -1))
            y_att = _attn(q, k, v)
            x1, h2, gates, route, cnt = _outproj(y_ssd, y_att, xg_, g1, sc2, sh2, mla_norm[l].reshape(1, -1),
                                                 wo[:SSD_INNER], wo[SSD_INNER:], post_mix_norm[l].reshape(1, d),
                                                 pre_ffn_norm[l].reshape(1, d), wr, br)
            route = route.reshape(n_tok, LANES)
            counts = cnt[0, :N_EXPERTS].astype(jnp.int32)
            dest, slot_tok, block_expert, n_active = _route(route[:, :TOP_K], route[:, TOP_K:2 * TOP_K], counts, n_tok)
            xs = h2.reshape(n_tok, d)[slot_tok]
            y = _moe(block_expert, n_active, xs, w_gate_up[l], b_gate_up[l][:, None, :], w_down[l], b_down[l][:, None, :])
            y4 = y[dest.reshape(n_tok, TOP_K).T.reshape(-1)].reshape(TOP_K, gb, seq, d)
            outs.append(_final(x1, y4, gates, g2, post_ffn_norm[l].reshape(1, d)))
        x = outs[0] if n_groups == 1 else jnp.concatenate(outs, axis=0)
    return x
```

```python
import functools
import math

import jax
import jax.numpy as jnp
from jax import lax
from jax.experimental import pallas as pl
from jax.experimental.pallas import tpu as pltpu

F32 = jnp.float32
BF16 = jnp.bfloat16

D_MODEL = 1024
CHUNK = 64
SSD_INNER = 512
SSD_HEAD_DIM = 64
SSD_HEADS = 8
SSD_GROUPS = 2
SSD_STATE = 128
SSD_CONV = 4
SSD_BC = SSD_GROUPS * SSD_STATE
XBC_DIM = SSD_INNER + 2 * SSD_BC
MLA_V = 64
MLA_HEADS = 8
MLA_NOPE = 64
MLA_ROPE = 32
Q_LORA = 384
KV_LORA = 256
ROPE_BASE = 10000.0
OFF_Z = 0
OFF_XBC = OFF_Z + SSD_INNER
OFF_DT = OFF_XBC + XBC_DIM
OFF_QA = OFF_DT + SSD_HEADS
OFF_KVA = OFF_QA + Q_LORA
OFF_KR = OFF_KVA + KV_LORA
IN_COLS = OFF_KR + MLA_ROPE
N_EXPERTS = 32
TOP_K = 4
D_FF_EXPERT = 1024
SWIGLU_LIMIT = 7.0
SWIGLU_ALPHA = 1.702
NORM_EPS = 1e-6

LANES = 128
HALF_ROPE = MLA_ROPE // 2
ROPE_LO = MLA_NOPE
ROPE_HI = MLA_NOPE + HALF_ROPE
VT_ROWS = 144

TS_PROJ = 512
SSD_L = 256
ATT_TQ = 512
ATT_TK = 256
MOE_TB = 512
BATCH_GROUPS = 2
VMEM_LIMIT = 56 * 1024 * 1024


def _dot(a, b):
    return jnp.dot(a, b, preferred_element_type=F32)


def _dot_nt(a, b):
    return lax.dot_general(a, b, (((1,), (1,)), ((), ())), preferred_element_type=F32)


def _dot_tn(a, b):
    return lax.dot_general(a, b, (((0,), (0,)), ((), ())), preferred_element_type=F32)


def _split2(x):
    hi = x.astype(BF16)
    lo = (x - hi.astype(F32)).astype(BF16)
    return hi, lo


def _split3(x):
    h1 = x.astype(BF16)
    r1 = x - h1.astype(F32)
    h2 = r1.astype(BF16)
    h3 = (r1 - h2.astype(F32)).astype(BF16)
    return h1, h2, h3


def _dot3(a, b):
    ah, al = _split2(a)
    bh, bl = _split2(b)
    return _dot(ah, bh) + _dot(ah, bl) + _dot(al, bh)


def _rms(x):
    return x * lax.rsqrt(jnp.mean(x * x, axis=-1, keepdims=True) + NORM_EPS)


def _silu(x):
    return x * jax.nn.sigmoid(x)


def _const_spec(shape):
    nd = len(shape)
    return pl.BlockSpec(shape, lambda *_: (0,) * nd)


def _adaln_body(c_ref, w_ref, b_ref, o_ref):
    o_ref[...] = _dot3(_silu(c_ref[...]), w_ref[...]) + b_ref[...]


def _adaln(c, w_ada, b_ada):
    bsz, d = c.shape
    n = w_ada.shape[1]
    tn = 1024
    return pl.pallas_call(
        _adaln_body,
        grid=(n // tn,),
        in_specs=[_const_spec((bsz, d)),
                  pl.BlockSpec((d, tn), lambda j: (0, j)),
                  pl.BlockSpec((1, tn), lambda j: (0, j))],
        out_specs=pl.BlockSpec((bsz, tn), lambda j: (0, j)),
        out_shape=jax.ShapeDtypeStruct((bsz, n), F32),
        compiler_params=pltpu.CompilerParams(dimension_semantics=("arbitrary",),
                                             vmem_limit_bytes=VMEM_LIMIT),
        name="adaln",
    )(c, w_ada, b_ada.reshape(1, n))


def _rope_block(xb, ct, st, lane):
    partner = jnp.where(lane < ROPE_HI, pltpu.roll(xb, LANES - HALF_ROPE, 1), pltpu.roll(xb, HALF_ROPE, 1))
    return xb * ct + partner * st


def _inproj_body(x_ref, sc_ref, sh_ref, g_ref, ct_ref, st_ref, wz_ref, wxbc_ref, wsm_ref, wqa_ref, wkva_ref,
                 qn_ref, kvn_ref, wqup_ref, wkup_ref, wvup_ref,
                 z_ref, xbc_ref, dt_ref, q_ref, k_ref, v_ref):
    x = x_ref[0]
    h = _rms(x) * (g_ref[...] * (1.0 + sc_ref[0])) + sh_ref[0]
    hb = h.astype(BF16)
    z_ref[0] = _dot(hb, wz_ref[...]).astype(BF16)
    xbc_ref[0] = _dot(hb, wxbc_ref[...]).astype(BF16)
    sm = _dot(hb, wsm_ref[...])
    dt_ref[0] = sm[:, LANES:]
    ct = ct_ref[0]
    st = st_ref[0]
    lane = lax.broadcasted_iota(jnp.int32, ct.shape, 1)
    kr = _rope_block(sm[:, :LANES], ct, st, lane)
    qan = (_rms(_dot(hb, wqa_ref[...])) * qn_ref[...]).astype(BF16)
    q = _dot(qan, wqup_ref[...])
    for hh in range(MLA_HEADS):
        blk = slice(hh * LANES, (hh + 1) * LANES)
        q_ref[0, :, blk] = _rope_block(q[:, blk], ct, st, lane).astype(BF16)
    kvn = (_rms(_dot(hb, wkva_ref[...])) * kvn_ref[...]).astype(BF16)
    k = _dot(kvn, wkup_ref[...])
    for hh in range(MLA_HEADS):
        blk = slice(hh * LANES, (hh + 1) * LANES)
        k_ref[0, :, blk] = (k[:, blk] + kr).astype(BF16)
    vt = _dot_nt(wvup_ref[...], kvn)
    vrow = lax.broadcasted_iota(jnp.int32, vt.shape, 0)
    v_ref[0] = jnp.where(vrow % VT_ROWS == LANES, 1.0, vt).astype(BF16)


def _inproj(x, sc1, sh1, gain, ct, st, wz, wxbc, wsm, wqa, wkva, qn, kvn, wqup, wkup, wvup):
    bsz, seq, d = x.shape
    ts = min(TS_PROJ, seq)
    hw = MLA_HEADS * LANES

    def tok(width):
        return pl.BlockSpec((1, ts, width), lambda b, i: (b, i, 0))

    def per_batch(width):
        return pl.BlockSpec((1, 1, width), lambda b, i: (b, 0, 0))

    weights = (wz, wxbc, wsm, wqa, wkva, qn, kvn, wqup, wkup, wvup)
    out_widths = (SSD_INNER, XBC_DIM, LANES, hw, hw)
    out_dtypes = (BF16, BF16, F32, BF16, BF16)
    vdim = (MLA_HEADS // 2) * VT_ROWS
    return pl.pallas_call(
        _inproj_body,
        grid=(bsz, seq // ts),
        in_specs=[tok(d), per_batch(d), per_batch(d), _const_spec((1, d)), tok(LANES), tok(LANES)]
                 + [_const_spec(w.shape) for w in weights],
        out_specs=[tok(w) for w in out_widths] + [pl.BlockSpec((1, vdim, ts), lambda b, i: (b, 0, i))],
        out_shape=[jax.ShapeDtypeStruct((bsz, seq, w), dt) for w, dt in zip(out_widths, out_dtypes)]
                  + [jax.ShapeDtypeStruct((bsz, vdim, seq), BF16)],
        compiler_params=pltpu.CompilerParams(dimension_semantics=("parallel", "parallel"),
                                             vmem_limit_bytes=VMEM_LIMIT),
        name="inproj",
    )(x, sc1, sh1, gain, ct, st, *weights)


CONV_HALO = 16


def _ssd_body(xc_ref, xp_ref, z_ref, dt_ref, cw_ref, cb_ref, dtb_ref, alog_ref, dsk_ref, ng_ref,
              y_ref, xs_scr, st_scr, *, blk):
    i = pl.program_id(1)

    @pl.when(i == 0)
    def _():
        st_scr[...] = jnp.zeros_like(st_scr)

    xs_scr[0:CONV_HALO, :] = jnp.where(i > 0, xp_ref[0].astype(F32), 0.0)
    xs_scr[CONV_HALO:CONV_HALO + blk, :] = xc_ref[0].astype(F32)
    conv = cb_ref[...]
    for kk in range(SSD_CONV):
        off = CONV_HALO - (SSD_CONV - 1) + kk
        conv = conv + cw_ref[kk:kk + 1, :] * xs_scr[off:off + blk, :]
    xa = _silu(conv)
    xs = xa[:, :SSD_INNER]
    bm = xa[:, SSD_INNER:SSD_INNER + SSD_BC]
    cm = xa[:, SSD_INNER + SSD_BC:]

    hl = lax.broadcasted_iota(jnp.int32, (1, LANES), 1)
    dtr = dt_ref[0] + dtb_ref[...]
    dt = jnp.maximum(dtr, 0.0) + jnp.log(1.0 + jnp.exp(-jnp.abs(dtr)))
    a = jnp.where(hl < SSD_HEADS, -jnp.exp(alog_ref[...]), 0.0)
    dta = dt * a
    row = lax.broadcasted_iota(jnp.int32, (blk, blk), 0)
    col = lax.broadcasted_iota(jnp.int32, (blk, blk), 1)
    tril = row >= col
    trilb = jnp.where(tril, 1.0, 0.0).astype(BF16)
    d1, d2, d3 = _split3(dta)
    cs = _dot(trilb, d1) + _dot(trilb, d2) + _dot(trilb, d3)
    cs_last = cs[blk - 1:blk, :]
    ecs = jnp.exp(cs)
    dte = jnp.exp(cs_last - cs)
    cs_t = cs.T

    er = lax.broadcasted_iota(jnp.int32, (LANES, SSD_INNER), 0)
    ec = lax.broadcasted_iota(jnp.int32, (LANES, SSD_INNER), 1)
    expand = jnp.where(ec // SSD_HEAD_DIM == er, 1.0, 0.0).astype(BF16)

    def per_channel(v):
        vh, vl = _split2(v)
        return _dot(vh, expand) + _dot(vl, expand)

    dt_e = per_channel(dt)
    ecs_e = per_channel(ecs)
    dte_e = per_channel(dte)
    xdt = xs * dt_e
    xdt_b = xdt.astype(BF16)
    xw_b = (xdt * dte_e).astype(BF16)

    gw = SSD_INNER // SSD_GROUPS
    heads_per_group = SSD_HEADS // SSD_GROUPS
    lane = lax.broadcasted_iota(jnp.int32, (blk, LANES), 1)
    y_groups = []
    for g in range(SSD_GROUPS):
        bg = bm[:, g * SSD_STATE:(g + 1) * SSD_STATE].astype(BF16)
        cg = cm[:, g * SSD_STATE:(g + 1) * SSD_STATE].astype(BF16)
        cb = _dot_nt(cg, bg)
        state = st_scr[g]
        y_off = _dot(cg, state.astype(BF16))
        pairs = []
        for j in range(heads_per_group // 2):
            xp = xdt_b[:, g * gw + j * LANES:g * gw + (j + 1) * LANES]
            halves = []
            for u in range(2):
                hidx = g * heads_per_group + 2 * j + u
                seg = cs[:, hidx:hidx + 1] - cs_t[hidx:hidx + 1, :]
                dec = jnp.exp(jnp.where(tril, seg, -jnp.inf))
                halves.append(_dot((cb * dec).astype(BF16), xp))
            pairs.append(jnp.where(lane < SSD_HEAD_DIM, halves[0], halves[1]))
        y_diag = jnp.concatenate(pairs, axis=1)
        y_groups.append(y_diag + y_off * ecs_e[:, g * gw:(g + 1) * gw])
        st_scr[g] = (state * ecs_e[blk - 1:blk, g * gw:(g + 1) * gw]
                     + _dot_tn(bg, xw_b[:, g * gw:(g + 1) * gw]))
    y = jnp.concatenate(y_groups, axis=1) + xs * dsk_ref[...]
    y = y * _silu(z_ref[0].astype(F32))
    y = jnp.concatenate([_rms(y[:, g * gw:(g + 1) * gw]) for g in range(SSD_GROUPS)], axis=1)
    y_ref[0] = (y * ng_ref[...]).astype(BF16)


def _ssd(xbc, z, dt, conv_w, conv_b, dt_bias, a_log, d_skip_e, norm_gain):
    bsz, seq, _ = xbc.shape
    blk = min(SSD_L, seq)
    halo_per_blk = blk // CONV_HALO
    body = functools.partial(_ssd_body, blk=blk)
    return pl.pallas_call(
        body,
        grid=(bsz, seq // blk),
        in_specs=[pl.BlockSpec((1, blk, XBC_DIM), lambda b, i: (b, i, 0)),
                  pl.BlockSpec((1, CONV_HALO, XBC_DIM), lambda b, i: (b, jnp.maximum(i * halo_per_blk - 1, 0), 0)),
                  pl.BlockSpec((1, blk, SSD_INNER), lambda b, i: (b, i, 0)),
                  pl.BlockSpec((1, blk, LANES), lambda b, i: (b, i, 0)),
                  _const_spec((SSD_CONV, XBC_DIM)), _const_spec((1, XBC_DIM)),
                  _const_spec((1, LANES)), _const_spec((1, LANES)),
                  _const_spec((1, SSD_INNER)), _const_spec((1, SSD_INNER))],
        out_specs=pl.BlockSpec((1, blk, SSD_INNER), lambda b, i: (b, i, 0)),
        out_shape=jax.ShapeDtypeStruct((bsz, seq, SSD_INNER), BF16),
        scratch_shapes=[pltpu.VMEM((CONV_HALO + blk, XBC_DIM), F32),
                        pltpu.VMEM((SSD_GROUPS, SSD_STATE, SSD_INNER // SSD_GROUPS), F32)],
        compiler_params=pltpu.CompilerParams(dimension_semantics=("parallel", "arbitrary"),
                                             vmem_limit_bytes=VMEM_LIMIT),
        name="ssd",
    )(xbc, xbc, z, dt, conv_w, conv_b, dt_bias, a_log, d_skip_e, norm_gain)


def _attn_body(q_ref, k_ref, v_ref, o_ref, *scratch, tq, tk):
    n_streams = 2 * (tq // tk)
    s_scr = (scratch[:n_streams], scratch[n_streams:2 * n_streams])
    acc_scr = scratch[2 * n_streams:]
    qi = pl.program_id(2)
    n_sub = tq // tk
    n_full = qi * n_sub
    krow = lax.broadcasted_iota(jnp.int32, (tk, tk), 0)
    qcol = lax.broadcasted_iota(jnp.int32, (tk, tk), 1)
    diag_ok = krow // CHUNK <= qcol // CHUNK
    vrow = lax.broadcasted_iota(jnp.int32, (LANES, tk), 0)
    streams = [(u, r) for u in range(2) for r in range(n_sub)]
    qs = [q_ref[0, r * tk:(r + 1) * tk, u * LANES:(u + 1) * LANES] for u, r in streams]

    def put_scores(ki, which, slot):
        start = pl.multiple_of(ki * tk, tk)
        k2 = k_ref[0, pl.ds(start, tk), :]
        out = {}
        for si in which:
            u = streams[si][0]
            s = _dot_nt(k2[:, u * LANES:(u + 1) * LANES], qs[si])
            s_scr[slot][si][...] = s
            out[si] = jnp.max(s, axis=0, keepdims=True)
        return out

    def values_t(ki):
        return v_ref[0, :, pl.ds(pl.multiple_of(ki * tk, tk), tk)]

    def softmax_pv(si, m, s_max, slot, vt, masked):
        s = s_scr[slot][si][...]
        if masked:
            s = jnp.where(diag_ok, s, -jnp.inf)
            s_max = jnp.max(s, axis=0, keepdims=True)
        m_new = jnp.maximum(m, s_max)
        alpha = jnp.exp2(m - m_new)
        p = jnp.exp2((s - m_new).astype(BF16))
        acc_scr[si][...] = alpha * acc_scr[si][...] + _dot(vt, p)
        return m_new

    every = list(range(len(streams)))
    for ref in acc_scr:
        ref[...] = jnp.zeros_like(ref)

    def step(j, state):
        ms, s_maxes = state
        for slot in range(2):
            ki = 2 * j + slot
            nxt = put_scores(ki + 1, every, 1 - slot)
            vt = values_t(ki)
            ms = tuple(softmax_pv(si, ms[si], s_maxes[si], slot, vt, False) for si in every)
            s_maxes = tuple(nxt[si] for si in every)
        return ms, s_maxes

    first = put_scores(0, every, 0)
    m_init = jnp.full((1, tk), -jnp.inf, F32)
    ms, s_maxes = lax.fori_loop(0, n_full // 2, step, ((m_init,) * len(streams), tuple(first[si] for si in every)))
    ms = list(ms)
    s_maxes = dict(zip(every, s_maxes))
    for dd in range(n_sub):
        slot = dd % 2
        live = [si for si in every if dd <= streams[si][1]]
        later = [si for si in every if dd + 1 <= streams[si][1]]
        nxt = put_scores(n_full + dd + 1, later, 1 - slot) if later else {}
        vt = values_t(n_full + dd)
        for si in live:
            ms[si] = softmax_pv(si, ms[si], s_maxes[si], slot, vt, dd == streams[si][1])
        s_maxes = nxt
    for r in range(n_sub):
        a0 = acc_scr[streams.index((0, r))][...]
        a1 = acc_scr[streams.index((1, r))][...]
        out_t = jnp.where(vrow < MLA_V, a0[:LANES] / a0[LANES:LANES + 1], a1[:LANES] / a1[LANES:LANES + 1])
        o_ref[0, r * tk:(r + 1) * tk, :] = out_t.T.astype(BF16)


def _attn(q, k, v):
    bsz, seq, _ = q.shape
    tk = min(ATT_TK, seq)
    tq = min(ATT_TQ, seq)
    assert (tq // tk) % 2 == 0, "the two-slot score pipeline needs an even number of query sub-tiles"
    n_streams = 2 * (tq // tk)
    body = functools.partial(_attn_body, tq=tq, tk=tk)
    return pl.pallas_call(
        body,
        grid=(bsz, MLA_HEADS // 2, seq // tq),
        in_specs=[pl.BlockSpec((1, tq, 2 * LANES), lambda b, hp, i: (b, i, hp)),
                  pl.BlockSpec((1, seq, 2 * LANES), lambda b, hp, i: (b, 0, hp)),
                  pl.BlockSpec((1, VT_ROWS, seq), lambda b, hp, i: (b, hp, 0))],
        out_specs=pl.BlockSpec((1, tq, LANES), lambda b, hp, i: (b, i, hp)),
        out_shape=jax.ShapeDtypeStruct((bsz, seq, MLA_HEADS * MLA_V), BF16),
        scratch_shapes=[pltpu.VMEM((tk, tk), F32)] * (2 * n_streams) + [pltpu.VMEM((VT_ROWS, tk), F32)] * n_streams,
        compiler_params=pltpu.CompilerParams(dimension_semantics=("parallel", "parallel", "arbitrary"),
                                             vmem_limit_bytes=VMEM_LIMIT),
        name="attn",
    )(q, k, v)


def _outproj_body(ys_ref, ya_ref, x_ref, g1_ref, sc2_ref, sh2_ref, mn_ref, wo1_ref, wo2_ref, pmn_ref, pfn_ref,
                  wr_ref, br_ref, x1_ref, h2_ref, gate_ref, idx_ref, cnt_ref, pref_ref, ltab_ref, cnt_scr):
    first = (pl.program_id(0) == 0) & (pl.program_id(1) == 0)

    @pl.when(first)
    def _():
        cnt_scr[...] = jnp.zeros_like(cnt_scr)

    yan =(_rms(ya_ref[0].astype(F32)) * mn_ref[...]).astype(BF16)
    mix = _dot(ys_ref[0], wo1_ref[...]) + _dot(yan, wo2_ref[...])
    x1 = x_ref[0] + g1_ref[0] * (_rms(mix) * pmn_ref[...])
    x1_ref[0] = x1
    h2 = _rms(x1) * (pfn_ref[...] * (1.0 + sc2_ref[0])) + sh2_ref[0]
    h2_ref[0] = h2.astype(BF16)
    logits = _dot3(h2, wr_ref[...]) + br_ref[...]
    lane = lax.broadcasted_iota(jnp.int32, logits.shape, 1)
    cur = jnp.where(lane < N_EXPERTS, logits, -jnp.inf)
    vals, idxs = [], []
    for _ in range(TOP_K):
        m = jnp.max(cur, axis=-1, keepdims=True)
        ix = jnp.min(jnp.where(cur == m, lane, LANES), axis=-1, keepdims=True)
        vals.append(m)
        idxs.append(ix)
        cur = jnp.where(lane == ix, -jnp.inf, cur)
    es = [jnp.exp(v - vals[0]) for v in vals]
    denom = es[0]
    for e in es[1:]:
        denom = denom + e
    onehot = jnp.zeros(logits.shape, F32)
    for kk in range(TOP_K):
        onehot = onehot + jnp.where(lane == idxs[kk], 1.0, 0.0)
    ts = logits.shape[0]
    row = lax.broadcasted_iota(jnp.int32, (ts, ts), 0)
    col = lax.broadcasted_iota(jnp.int32, (ts, ts), 1)
    before = jnp.where(row > col, 1.0, 0.0).astype(BF16)
    prior = _dot(before, onehot.astype(BF16))
    seen = cnt_scr[...]
    pref_ref[0] = seen
    tok = lax.broadcasted_iota(jnp.int32, logits.shape, 0)
    tok_hi = (tok // 16).astype(F32)
    tok_lo = (tok % 16).astype(F32)
    ltab = jnp.zeros((2 * LANES, ts), F32)
    gate_out = jnp.zeros(logits.shape, F32)
    idx_out = jnp.zeros(logits.shape, jnp.int32)
    for kk in range(TOP_K):
        mine = lane == idxs[kk]
        local = jnp.sum(jnp.where(mine, prior, 0.0), axis=-1, keepdims=True)
        rank = local + jnp.sum(jnp.where(mine, seen, 0.0), axis=-1, keepdims=True)
        at_rank = jnp.where(col == local.astype(jnp.int32), 1.0, 0.0).astype(BF16)
        tagged = jnp.concatenate([jnp.where(mine, tok_hi, 0.0), jnp.where(mine, tok_lo, 0.0)], axis=1)
        ltab = ltab + _dot_tn(tagged.astype(BF16), at_rank)
        gate_out = jnp.where(lane == kk, es[kk] / denom, gate_out)
        idx_out = jnp.where(lane == kk, idxs[kk], idx_out)
        idx_out = jnp.where(lane == TOP_K + kk, rank.astype(jnp.int32), idx_out)
    gate_ref[0] = gate_out
    idx_ref[0] = idx_out
    ltab_ref[0] = (16.0 * ltab[:N_EXPERTS] + ltab[LANES:LANES + N_EXPERTS]).astype(jnp.int32)
    cnt_scr[...] = seen + jnp.sum(onehot, axis=0, keepdims=True)
    cnt_ref[...] = cnt_scr[...]


def _outproj(y_ssd, y_att, x, g1, sc2, sh2, mla_norm, wo1, wo2, post_mix_norm, pre_ffn_norm, wr, br):
    bsz, seq, d = x.shape
    ts = min(TS_PROJ, seq)
    tiles = seq // ts

    def tok(width):
        return pl.BlockSpec((1, ts, width), lambda b, i: (b, i, 0))

    def per_batch(width):
        return pl.BlockSpec((1, 1, width), lambda b, i: (b, 0, 0))

    consts = (mla_norm, wo1, wo2, post_mix_norm, pre_ffn_norm, wr, br)
    return pl.pallas_call(
        _outproj_body,
        grid=(bsz, seq // ts),
        in_specs=[tok(SSD_INNER), tok(MLA_HEADS * MLA_V), tok(d), per_batch(d), per_batch(d), per_batch(d)]
                 + [_const_spec(w.shape) for w in consts],
        out_specs=[tok(d), tok(d), tok(LANES), tok(LANES), _const_spec((1, LANES)),
                   pl.BlockSpec((1, 1, LANES), lambda b, i: (b * tiles + i, 0, 0)),
                   pl.BlockSpec((1, N_EXPERTS, ts), lambda b, i: (b * tiles + i, 0, 0))],
        out_shape=[jax.ShapeDtypeStruct((bsz, seq, d), F32), jax.ShapeDtypeStruct((bsz, seq, d), BF16),
                   jax.ShapeDtypeStruct((bsz, seq, LANES), F32), jax.ShapeDtypeStruct((bsz, seq, LANES), jnp.int32),
                   jax.ShapeDtypeStruct((1, LANES), F32),
                   jax.ShapeDtypeStruct((bsz * tiles, 1, LANES), F32),
                   jax.ShapeDtypeStruct((bsz * tiles, N_EXPERTS, ts), jnp.int32)],
        scratch_shapes=[pltpu.VMEM((1, LANES), F32)],
        compiler_params=pltpu.CompilerParams(dimension_semantics=("arbitrary", "arbitrary"),
                                             vmem_limit_bytes=VMEM_LIMIT),
        name="outproj",
    )(y_ssd, y_att, x, g1, sc2, sh2, *consts)


def _moe_body(be_ref, na_ref, x_ref, wgu_ref, bgu_ref, wd_ref, bd_ref, y_ref, wgu_b, wd_b):
    i = pl.program_id(0)

    @pl.when((i == 0) | (be_ref[i] != be_ref[jnp.maximum(i - 1, 0)]))
    def _():
        wgu_b[...] = wgu_ref[0].astype(BF16)
        wd_b[...] = wd_ref[0].astype(BF16)

    @pl.when(i < na_ref[0])
    def _():
        gu = _dot(x_ref[...], wgu_b[...]) + bgu_ref[0]
        glu = jnp.minimum(gu[:, :D_FF_EXPERT], SWIGLU_LIMIT)
        lin = jnp.clip(gu[:, D_FF_EXPERT:], -SWIGLU_LIMIT, SWIGLU_LIMIT)
        act = glu * jax.nn.sigmoid(SWIGLU_ALPHA * glu) * (lin + 1.0)
        y_ref[...] = (_dot(act.astype(BF16), wd_b[...]) + bd_ref[0]).astype(BF16)

    @pl.when(i >= na_ref[0])
    def _():
        y_ref[...] = jnp.zeros_like(y_ref)


def _moe(block_expert, n_active, xg, wgu, bgu, wd, bd):
    n_slots, d = xg.shape
    n_blocks = n_slots // MOE_TB
    f2 = wgu.shape[2]
    return pl.pallas_call(
        _moe_body,
        grid_spec=pltpu.PrefetchScalarGridSpec(
            num_scalar_prefetch=2,
            grid=(n_blocks,),
            in_specs=[pl.BlockSpec((MOE_TB, d), lambda i, be, na: (i, 0)),
                      pl.BlockSpec((1, d, f2), lambda i, be, na: (be[i], 0, 0)),
                      pl.BlockSpec((1, 1, f2), lambda i, be, na: (be[i], 0, 0)),
                      pl.BlockSpec((1, f2 // 2, d), lambda i, be, na: (be[i], 0, 0)),
                      pl.BlockSpec((1, 1, d), lambda i, be, na: (be[i], 0, 0))],
            out_specs=pl.BlockSpec((MOE_TB, d), lambda i, be, na: (i, 0)),
            scratch_shapes=[pltpu.VMEM((d, f2), BF16), pltpu.VMEM((f2 // 2, d), BF16)],
        ),
        out_shape=jax.ShapeDtypeStruct((n_slots, d), BF16),
        compiler_params=pltpu.CompilerParams(dimension_semantics=("arbitrary",),
                                             vmem_limit_bytes=VMEM_LIMIT),
        name="moe",
    )(block_expert, n_active, xg, wgu, bgu, wd, bd)


def _final_body(x1_ref, y_ref, gate_ref, g2_ref, gain_ref, *rest):
    o_ref = rest[-1]
    gates = gate_ref[0]
    f = gates[:, 0:1] * y_ref[0, 0].astype(F32)
    for kk in range(1, TOP_K):
        f = f + gates[:, kk:kk + 1] * y_ref[kk, 0].astype(F32)
    o_ref[0] = x1_ref[0] + g2_ref[0] * (_rms(f) * gain_ref[...])


def _final(x1, y4, gates, g2, gain, out_prev, batch_offset, total_batch):
    bsz, seq, d = x1.shape
    ts = min(TS_PROJ, seq)

    def tok(width):
        return pl.BlockSpec((1, ts, width), lambda b, i: (b, i, 0))

    in_specs = [tok(d), pl.BlockSpec((TOP_K, 1, ts, d), lambda b, i: (0, b, i, 0)), tok(LANES),
                pl.BlockSpec((1, 1, d), lambda b, i: (b, 0, 0)), _const_spec((1, d))]
    args = [x1, y4, gates, g2, gain]
    aliases = {}
    if out_prev is not None:
        in_specs.append(pl.BlockSpec(memory_space=pl.ANY))
        args.append(out_prev)
        aliases = {len(args) - 1: 0}
    return pl.pallas_call(
        _final_body,
        grid=(bsz, seq // ts),
        in_specs=in_specs,
        out_specs=pl.BlockSpec((1, ts, d), lambda b, i: (batch_offset + b, i, 0)),
        out_shape=jax.ShapeDtypeStruct((total_batch, seq, d), F32),
        input_output_aliases=aliases,
        compiler_params=pltpu.CompilerParams(dimension_semantics=("parallel", "parallel"),
                                             vmem_limit_bytes=VMEM_LIMIT),
        name="final",
    )(*args)


def _head_blocks(cols):
    out = []
    for c in cols:
        pad = LANES - c.shape[1]
        out.append(jnp.pad(c, ((0, 0), (0, pad))) if pad else c)
    return jnp.concatenate(out, axis=1)


def _prep_mixer_weights(w_in, w_q_up, w_kv_up):
    d = w_in.shape[0]
    wz = w_in[:, OFF_Z:OFF_XBC]
    wxbc = w_in[:, OFF_XBC:OFF_DT]
    wdt = w_in[:, OFF_DT:OFF_QA]
    wqa = w_in[:, OFF_QA:OFF_KVA]
    wkva = w_in[:, OFF_KVA:OFF_KR]
    wkr = w_in[:, OFF_KR:IN_COLS]
    kr_blk = jnp.concatenate([jnp.zeros((d, ROPE_LO), F32), wkr, jnp.zeros((d, LANES - ROPE_LO - MLA_ROPE), F32)], axis=1)
    dt_blk = jnp.pad(wdt, ((0, 0), (0, LANES - SSD_HEADS)))
    wsm = jnp.concatenate([kr_blk, dt_blk], axis=1)
    qh = MLA_NOPE + MLA_ROPE
    scale = math.log2(math.e) / math.sqrt(qh)
    wqup = _head_blocks([w_q_up[:, h * qh:(h + 1) * qh] for h in range(MLA_HEADS)]) * scale
    kvh = MLA_NOPE + MLA_V
    wkup = _head_blocks([w_kv_up[:, h * kvh:h * kvh + MLA_NOPE] for h in range(MLA_HEADS)])
    vcols = []
    for h in range(MLA_HEADS):
        vcols.append(w_kv_up[:, h * kvh + MLA_NOPE:(h + 1) * kvh])
        if h % 2 == 1:
            vcols.append(jnp.zeros((w_kv_up.shape[0], VT_ROWS - LANES), F32))
    wvup = jnp.concatenate(vcols, axis=1).T
    return tuple(w.astype(BF16) for w in (wz, wxbc, wsm, wqa, wkva)) + tuple(w.astype(BF16) for w in (wqup, wkup, wvup))


def _rope_tables(positions):
    inv_freq = ROPE_BASE ** (-(jnp.arange(HALF_ROPE, dtype=F32) * 2.0 / MLA_ROPE))
    angles = positions.astype(F32)[..., None] * inv_freq
    cos = jnp.cos(angles)
    sin = jnp.sin(angles)
    shp = angles.shape[:-1]
    ct = jnp.concatenate([jnp.ones(shp + (ROPE_LO,), F32), cos, cos,
                          jnp.zeros(shp + (LANES - ROPE_LO - MLA_ROPE,), F32)], axis=-1)
    st = jnp.concatenate([jnp.zeros(shp + (ROPE_LO,), F32), -sin, sin,
                          jnp.zeros(shp + (LANES - ROPE_LO - MLA_ROPE,), F32)], axis=-1)
    return ct, st


def _route(idx, rank, counts, tile_seen, ltab, n_tok):
    n_assign = n_tok * TOP_K
    n_tiles, _, tile = ltab.shape
    padded = ((counts + MOE_TB - 1) // MOE_TB) * MOE_TB
    padded_end = jnp.cumsum(padded)
    padded_start = padded_end - padded
    experts = jnp.arange(N_EXPERTS, dtype=jnp.int32)
    start_of = jnp.sum(jnp.where(idx[..., None] == experts, padded_start, 0), axis=-1)
    dest = (start_of + rank).reshape(-1)
    n_blocks = n_assign // MOE_TB + N_EXPERTS
    block_start = jnp.arange(n_blocks, dtype=jnp.int32) * MOE_TB
    block_expert = jnp.minimum(jnp.sum((padded_end[None, :] <= block_start[:, None]).astype(jnp.int32), axis=1),
                               N_EXPERTS - 1)
    n_active = (padded_end[-1] // MOE_TB).astype(jnp.int32).reshape(1)
    j = (block_start - padded_start[block_expert])[:, None] + jnp.arange(MOE_TB, dtype=jnp.int32)[None, :]
    seen_blk = tile_seen.T[block_expert]
    reached = seen_blk[:, None, :] <= j[:, :, None]
    tau = jnp.sum(reached.astype(jnp.int32), axis=-1) - 1
    j_local = j - jnp.max(jnp.where(reached, seen_blk[:, None, :], 0), axis=-1)
    valid = j < counts[block_expert][:, None]
    flat = (tau * N_EXPERTS + block_expert[:, None]) * tile + j_local
    tok_local = ltab.reshape(-1)[jnp.clip(flat, 0, n_tiles * N_EXPERTS * tile - 1)]
    slot_tok = jnp.where(valid, tau * tile + tok_local, 0).reshape(-1)
    return dest, slot_tok, block_expert, n_active


def kernel(x, c, positions, w_ada, b_ada, pre_mix_norm, w_in, conv_w, conv_b, dt_bias, a_log, d_skip, ssd_norm, q_a_norm, w_q_up, kv_a_norm, w_kv_up, mla_norm, w_out, post_mix_norm, pre_ffn_norm, w_router, b_router, w_gate_up, b_gate_up, w_down, b_down, post_ffn_norm):
    bsz, seq, d = x.shape
    ct, st = _rope_tables(positions)
    n_groups = BATCH_GROUPS if bsz % BATCH_GROUPS == 0 else 1
    gb = bsz // n_groups
    n_tok = gb * seq
    pad_h = LANES - SSD_HEADS
    for l in range(w_ada.shape[0]):
        mod = _adaln(c, w_ada[l], b_ada[l])
        mods = [m.reshape(bsz, 1, d) for m in jnp.split(mod, 6, axis=-1)]
        mixer_w = _prep_mixer_weights(w_in[l], w_q_up[l], w_kv_up[l])
        wo = w_out[l].astype(BF16)
        wr = jnp.pad(w_router[l], ((0, 0), (0, LANES - N_EXPERTS)))
        br = jnp.pad(b_router[l], (0, LANES - N_EXPERTS)).reshape(1, LANES)
        dtb = jnp.pad(dt_bias[l], (0, pad_h)).reshape(1, LANES)
        alog = jnp.pad(a_log[l], (0, pad_h)).reshape(1, LANES)
        dsk = jnp.repeat(d_skip[l], SSD_HEAD_DIM).reshape(1, -1)
        out = None
        for gi in range(n_groups):
            grp = slice(gi * gb, (gi + 1) * gb)
            sh1, sc1, g1, sh2, sc2, g2 = [m[grp] for m in mods]
            xg_ = x[grp]
            z, xbc, dt, q, k, v = _inproj(xg_, sc1, sh1, pre_mix_norm[l].reshape(1, d), ct[grp], st[grp], *mixer_w[:5],
                                          q_a_norm[l].reshape(1, -1), kv_a_norm[l].reshape(1, -1), *mixer_w[5:])
            y_ssd = _ssd(xbc, z, dt, conv_w[l], conv_b[l].reshape(1, -1), dtb, alog, dsk, ssd_norm[l].reshape(1, -1))
            y_att = _attn(q, k, v)
            x1, h2, gates, route, cnt, seen, ltab = _outproj(
                y_ssd, y_att, xg_, g1, sc2, sh2, mla_norm[l].reshape(1, -1), wo[:SSD_INNER], wo[SSD_INNER:],
                post_mix_norm[l].reshape(1, d), pre_ffn_norm[l].reshape(1, d), wr, br)
            route = route.reshape(n_tok, LANES)
            counts = cnt[0, :N_EXPERTS].astype(jnp.int32)
            tile_seen = seen[:, 0, :N_EXPERTS].astype(jnp.int32)
            dest, slot_tok, block_expert, n_active = _route(route[:, :TOP_K], route[:, TOP_K:2 * TOP_K], counts,
                                                            tile_seen, ltab, n_tok)
            xs = h2.reshape(n_tok, d)[slot_tok]
            y = _moe(block_expert, n_active, xs, w_gate_up[l], b_gate_up[l][:, None, :], w_down[l], b_down[l][:, None, :])
            y4 = y[dest.reshape(n_tok, TOP_K).T.reshape(-1)].reshape(TOP_K, gb, seq, d)
            out = _final(x1, y4, gates, g2, post_ffn_norm[l].reshape(1, d), out, gi * gb, bsz)
        x = out
    return x
```

```python
import functools
import math

import jax
import jax.numpy as jnp
from jax import lax
from jax.experimental import pallas as pl
from jax.experimental.pallas import tpu as pltpu

F32 = jnp.float32
BF16 = jnp.bfloat16

D_MODEL = 1024
CHUNK = 64
SSD_INNER = 512
SSD_HEAD_DIM = 64
SSD_HEADS = 8
SSD_GROUPS = 2
SSD_STATE = 128
SSD_CONV = 4
SSD_BC = SSD_GROUPS * SSD_STATE
XBC_DIM = SSD_INNER + 2 * SSD_BC
MLA_V = 64
MLA_HEADS = 8
MLA_NOPE = 64
MLA_ROPE = 32
Q_LORA = 384
KV_LORA = 256
ROPE_BASE = 10000.0
OFF_Z = 0
OFF_XBC = OFF_Z + SSD_INNER
OFF_DT = OFF_XBC + XBC_DIM
OFF_QA = OFF_DT + SSD_HEADS
OFF_KVA = OFF_QA + Q_LORA
OFF_KR = OFF_KVA + KV_LORA
IN_COLS = OFF_KR + MLA_ROPE
N_EXPERTS = 32
TOP_K = 4
D_FF_EXPERT = 1024
SWIGLU_LIMIT = 7.0
SWIGLU_ALPHA = 1.702
NORM_EPS = 1e-6

LANES = 128
HALF_ROPE = MLA_ROPE // 2
ROPE_LO = MLA_NOPE
ROPE_HI = MLA_NOPE + HALF_ROPE
VT_ROWS = 144

TS_PROJ = 512
SSD_L = 256
ATT_TQ = 512
ATT_TK = 256
MOE_TB = 512
BATCH_GROUPS = 2
VMEM_LIMIT = 56 * 1024 * 1024


def _dot(a, b):
    return jnp.dot(a, b, preferred_element_type=F32)


def _dot_nt(a, b):
    return lax.dot_general(a, b, (((1,), (1,)), ((), ())), preferred_element_type=F32)


def _dot_tn(a, b):
    return lax.dot_general(a, b, (((0,), (0,)), ((), ())), preferred_element_type=F32)


def _split2(x):
    hi = x.astype(BF16)
    lo = (x - hi.astype(F32)).astype(BF16)
    return hi, lo


def _split3(x):
    h1 = x.astype(BF16)
    r1 = x - h1.astype(F32)
    h2 = r1.astype(BF16)
    h3 = (r1 - h2.astype(F32)).astype(BF16)
    return h1, h2, h3


def _dot3(a, b):
    ah, al = _split2(a)
    bh, bl = _split2(b)
    return _dot(ah, bh) + _dot(ah, bl) + _dot(al, bh)


def _rms(x):
    return x * lax.rsqrt(jnp.mean(x * x, axis=-1, keepdims=True) + NORM_EPS)


def _silu(x):
    return x * jax.nn.sigmoid(x)


def _const_spec(shape):
    nd = len(shape)
    return pl.BlockSpec(shape, lambda *_: (0,) * nd)


def _adaln_body(c_ref, w_ref, b_ref, o_ref):
    o_ref[...] = _dot3(_silu(c_ref[...]), w_ref[...]) + b_ref[...]


def _adaln(c, w_ada, b_ada):
    bsz, d = c.shape
    n = w_ada.shape[1]
    tn = 1024
    return pl.pallas_call(
        _adaln_body,
        grid=(n // tn,),
        in_specs=[_const_spec((bsz, d)),
                  pl.BlockSpec((d, tn), lambda j: (0, j)),
                  pl.BlockSpec((1, tn), lambda j: (0, j))],
        out_specs=pl.BlockSpec((bsz, tn), lambda j: (0, j)),
        out_shape=jax.ShapeDtypeStruct((bsz, n), F32),
        compiler_params=pltpu.CompilerParams(dimension_semantics=("arbitrary",),
                                             vmem_limit_bytes=VMEM_LIMIT),
        name="adaln",
    )(c, w_ada, b_ada.reshape(1, n))


def _rope_block(xb, ct, st, lane):
    partner = jnp.where(lane < ROPE_HI, pltpu.roll(xb, LANES - HALF_ROPE, 1), pltpu.roll(xb, HALF_ROPE, 1))
    return xb * ct + partner * st


def _inproj_body(x_ref, sc_ref, sh_ref, g_ref, ct_ref, st_ref, wz_ref, wxbc_ref, wsm_ref, wqa_ref, wkva_ref,
                 qn_ref, kvn_ref, wqup_ref, wkup_ref, wvup_ref,
                 z_ref, xbc_ref, dt_ref, q_ref, k_ref, v_ref):
    x = x_ref[0]
    h = _rms(x) * (g_ref[...] * (1.0 + sc_ref[0])) + sh_ref[0]
    hb = h.astype(BF16)
    z_ref[0] = _dot(hb, wz_ref[...]).astype(BF16)
    xbc_ref[0] = _dot(hb, wxbc_ref[...]).astype(BF16)
    sm = _dot(hb, wsm_ref[...])
    dt_ref[0] = sm[:, LANES:]
    ct = ct_ref[0]
    st = st_ref[0]
    lane = lax.broadcasted_iota(jnp.int32, ct.shape, 1)
    kr = _rope_block(sm[:, :LANES], ct, st, lane)
    qan = (_rms(_dot(hb, wqa_ref[...])) * qn_ref[...]).astype(BF16)
    q = _dot(qan, wqup_ref[...])
    for hh in range(MLA_HEADS):
        blk = slice(hh * LANES, (hh + 1) * LANES)
        q_ref[0, :, blk] = _rope_block(q[:, blk], ct, st, lane).astype(BF16)
    kvn = (_rms(_dot(hb, wkva_ref[...])) * kvn_ref[...]).astype(BF16)
    k = _dot(kvn, wkup_ref[...])
    for hh in range(MLA_HEADS):
        blk = slice(hh * LANES, (hh + 1) * LANES)
        k_ref[0, :, blk] = (k[:, blk] + kr).astype(BF16)
    vt = _dot_nt(wvup_ref[...], kvn)
    vrow = lax.broadcasted_iota(jnp.int32, vt.shape, 0)
    v_ref[0] = jnp.where(vrow % VT_ROWS == LANES, 1.0, vt).astype(BF16)


def _inproj(x, sc1, sh1, gain, ct, st, wz, wxbc, wsm, wqa, wkva, qn, kvn, wqup, wkup, wvup, *, batch_offset):
    _, seq, d = x.shape
    bsz = sc1.shape[0]
    ts = min(TS_PROJ, seq)
    hw = MLA_HEADS * LANES

    def tok(width):
        return pl.BlockSpec((1, ts, width), lambda b, i: (b, i, 0))

    def tok_full(width):
        return pl.BlockSpec((1, ts, width), lambda b, i: (batch_offset + b, i, 0))

    def per_batch(width):
        return pl.BlockSpec((1, 1, width), lambda b, i: (b, 0, 0))

    weights = (wz, wxbc, wsm, wqa, wkva, qn, kvn, wqup, wkup, wvup)
    out_widths = (SSD_INNER, XBC_DIM, LANES, hw, hw)
    out_dtypes = (BF16, BF16, F32, BF16, BF16)
    vdim = (MLA_HEADS // 2) * VT_ROWS
    return pl.pallas_call(
        _inproj_body,
        grid=(bsz, seq // ts),
        in_specs=[tok_full(d), per_batch(d), per_batch(d), _const_spec((1, d)), tok_full(LANES), tok_full(LANES)]
                 + [_const_spec(w.shape) for w in weights],
        out_specs=[tok(w) for w in out_widths] + [pl.BlockSpec((1, vdim, ts), lambda b, i: (b, 0, i))],
        out_shape=[jax.ShapeDtypeStruct((bsz, seq, w), dt) for w, dt in zip(out_widths, out_dtypes)]
                  + [jax.ShapeDtypeStruct((bsz, vdim, seq), BF16)],
        compiler_params=pltpu.CompilerParams(dimension_semantics=("parallel", "parallel"),
                                             vmem_limit_bytes=VMEM_LIMIT),
        name="inproj",
    )(x, sc1, sh1, gain, ct, st, *weights)


CONV_HALO = 16


def _ssd_body(xc_ref, xp_ref, z_ref, dt_ref, cw_ref, cb_ref, dtb_ref, alog_ref, dsk_ref, ng_ref,
              y_ref, xs_scr, st_scr, *, blk):
    i = pl.program_id(1)

    @pl.when(i == 0)
    def _():
        st_scr[...] = jnp.zeros_like(st_scr)

    xs_scr[0:CONV_HALO, :] = jnp.where(i > 0, xp_ref[0].astype(F32), 0.0)
    xs_scr[CONV_HALO:CONV_HALO + blk, :] = xc_ref[0].astype(F32)
    conv = cb_ref[...]
    for kk in range(SSD_CONV):
        off = CONV_HALO - (SSD_CONV - 1) + kk
        conv = conv + cw_ref[kk:kk + 1, :] * xs_scr[off:off + blk, :]
    xa = _silu(conv)
    xs = xa[:, :SSD_INNER]
    bm = xa[:, SSD_INNER:SSD_INNER + SSD_BC]
    cm = xa[:, SSD_INNER + SSD_BC:]

    hl = lax.broadcasted_iota(jnp.int32, (1, LANES), 1)
    dtr = dt_ref[0] + dtb_ref[...]
    dt = jnp.maximum(dtr, 0.0) + jnp.log(1.0 + jnp.exp(-jnp.abs(dtr)))
    a = jnp.where(hl < SSD_HEADS, -jnp.exp(alog_ref[...]), 0.0)
    dta = dt * a
    row = lax.broadcasted_iota(jnp.int32, (blk, blk), 0)
    col = lax.broadcasted_iota(jnp.int32, (blk, blk), 1)
    tril = row >= col
    trilb = jnp.where(tril, 1.0, 0.0).astype(BF16)
    d1, d2, d3 = _split3(dta)
    cs = _dot(trilb, d1) + _dot(trilb, d2) + _dot(trilb, d3)
    cs_last = cs[blk - 1:blk, :]
    ecs = jnp.exp(cs)
    dte = jnp.exp(cs_last - cs)
    cs_t = cs.T

    er = lax.broadcasted_iota(jnp.int32, (LANES, SSD_INNER), 0)
    ec = lax.broadcasted_iota(jnp.int32, (LANES, SSD_INNER), 1)
    expand = jnp.where(ec // SSD_HEAD_DIM == er, 1.0, 0.0).astype(BF16)

    def per_channel(v):
        vh, vl = _split2(v)
        return _dot(vh, expand) + _dot(vl, expand)

    dt_e = per_channel(dt)
    ecs_e = per_channel(ecs)
    dte_e = per_channel(dte)
    xdt = xs * dt_e
    xdt_b = xdt.astype(BF16)
    xw_b = (xdt * dte_e).astype(BF16)

    gw = SSD_INNER // SSD_GROUPS
    heads_per_group = SSD_HEADS // SSD_GROUPS
    lane = lax.broadcasted_iota(jnp.int32, (blk, LANES), 1)
    y_groups = []
    for g in range(SSD_GROUPS):
        bg = bm[:, g * SSD_STATE:(g + 1) * SSD_STATE].astype(BF16)
        cg = cm[:, g * SSD_STATE:(g + 1) * SSD_STATE].astype(BF16)
        cb = _dot_nt(cg, bg)
        state = st_scr[g]
        y_off = _dot(cg, state.astype(BF16))
        pairs = []
        for j in range(heads_per_group // 2):
            xp = xdt_b[:, g * gw + j * LANES:g * gw + (j + 1) * LANES]
            halves = []
            for u in range(2):
                hidx = g * heads_per_group + 2 * j + u
                seg = cs[:, hidx:hidx + 1] - cs_t[hidx:hidx + 1, :]
                dec = jnp.exp(jnp.where(tril, seg, -jnp.inf))
                halves.append(_dot((cb * dec).astype(BF16), xp))
            pairs.append(jnp.where(lane < SSD_HEAD_DIM, halves[0], halves[1]))
        y_diag = jnp.concatenate(pairs, axis=1)
        y_groups.append(y_diag + y_off * ecs_e[:, g * gw:(g + 1) * gw])
        st_scr[g] = (state * ecs_e[blk - 1:blk, g * gw:(g + 1) * gw]
                     + _dot_tn(bg, xw_b[:, g * gw:(g + 1) * gw]))
    y = jnp.concatenate(y_groups, axis=1) + xs * dsk_ref[...]
    y = y * _silu(z_ref[0].astype(F32))
    y = jnp.concatenate([_rms(y[:, g * gw:(g + 1) * gw]) for g in range(SSD_GROUPS)], axis=1)
    y_ref[0] = (y * ng_ref[...]).astype(BF16)


def _ssd(xbc, z, dt, conv_w, conv_b, dt_bias, a_log, d_skip_e, norm_gain):
    bsz, seq, _ = xbc.shape
    blk = min(SSD_L, seq)
    halo_per_blk = blk // CONV_HALO
    body = functools.partial(_ssd_body, blk=blk)
    return pl.pallas_call(
        body,
        grid=(bsz, seq // blk),
        in_specs=[pl.BlockSpec((1, blk, XBC_DIM), lambda b, i: (b, i, 0)),
                  pl.BlockSpec((1, CONV_HALO, XBC_DIM), lambda b, i: (b, jnp.maximum(i * halo_per_blk - 1, 0), 0)),
                  pl.BlockSpec((1, blk, SSD_INNER), lambda b, i: (b, i, 0)),
                  pl.BlockSpec((1, blk, LANES), lambda b, i: (b, i, 0)),
                  _const_spec((SSD_CONV, XBC_DIM)), _const_spec((1, XBC_DIM)),
                  _const_spec((1, LANES)), _const_spec((1, LANES)),
                  _const_spec((1, SSD_INNER)), _const_spec((1, SSD_INNER))],
        out_specs=pl.BlockSpec((1, blk, SSD_INNER), lambda b, i: (b, i, 0)),
        out_shape=jax.ShapeDtypeStruct((bsz, seq, SSD_INNER), BF16),
        scratch_shapes=[pltpu.VMEM((CONV_HALO + blk, XBC_DIM), F32),
                        pltpu.VMEM((SSD_GROUPS, SSD_STATE, SSD_INNER // SSD_GROUPS), F32)],
        compiler_params=pltpu.CompilerParams(dimension_semantics=("parallel", "arbitrary"),
                                             vmem_limit_bytes=VMEM_LIMIT),
        name="ssd",
    )(xbc, xbc, z, dt, conv_w, conv_b, dt_bias, a_log, d_skip_e, norm_gain)


def _attn_body(q_ref, k_ref, v_ref, o_ref, *scratch, tq, tk):
    n_streams = 2 * (tq // tk)
    s_scr = (scratch[:n_streams], scratch[n_streams:2 * n_streams])
    acc_scr = scratch[2 * n_streams:]
    qi = pl.program_id(2)
    n_sub = tq // tk
    n_full = qi * n_sub
    krow = lax.broadcasted_iota(jnp.int32, (tk, tk), 0)
    qcol = lax.broadcasted_iota(jnp.int32, (tk, tk), 1)
    diag_ok = krow // CHUNK <= qcol // CHUNK
    vrow = lax.broadcasted_iota(jnp.int32, (LANES, tk), 0)
    streams = [(u, r) for u in range(2) for r in range(n_sub)]
    qs = [q_ref[0, r * tk:(r + 1) * tk, u * LANES:(u + 1) * LANES] for u, r in streams]

    def put_scores(ki, which, slot):
        start = pl.multiple_of(ki * tk, tk)
        k2 = k_ref[0, pl.ds(start, tk), :]
        out = {}
        for si in which:
            u = streams[si][0]
            s = _dot_nt(k2[:, u * LANES:(u + 1) * LANES], qs[si])
            s_scr[slot][si][...] = s
            out[si] = jnp.max(s, axis=0, keepdims=True)
        return out

    def values_t(ki):
        return v_ref[0, :, pl.ds(pl.multiple_of(ki * tk, tk), tk)]

    def softmax_pv(si, m, s_max, slot, vt, masked):
        s = s_scr[slot][si][...]
        if masked:
            s = jnp.where(diag_ok, s, -jnp.inf)
            s_max = jnp.max(s, axis=0, keepdims=True)
        m_new = jnp.maximum(m, s_max)
        alpha = jnp.exp2(m - m_new)
        p = jnp.exp2((s - m_new).astype(BF16))
        acc_scr[si][...] = alpha * acc_scr[si][...] + _dot(vt, p)
        return m_new

    every = list(range(len(streams)))
    for ref in acc_scr:
        ref[...] = jnp.zeros_like(ref)

    def step(j, state):
        ms, s_maxes = state
        for slot in range(2):
            ki = 2 * j + slot
            nxt = put_scores(ki + 1, every, 1 - slot)
            vt = values_t(ki)
            ms = tuple(softmax_pv(si, ms[si], s_maxes[si], slot, vt, False) for si in every)
            s_maxes = tuple(nxt[si] for si in every)
        return ms, s_maxes

    first = put_scores(0, every, 0)
    m_init = jnp.full((1, tk), -jnp.inf, F32)
    ms, s_maxes = lax.fori_loop(0, n_full // 2, step, ((m_init,) * len(streams), tuple(first[si] for si in every)))
    ms = list(ms)
    s_maxes = dict(zip(every, s_maxes))
    for dd in range(n_sub):
        slot = dd % 2
        live = [si for si in every if dd <= streams[si][1]]
        later = [si for si in every if dd + 1 <= streams[si][1]]
        nxt = put_scores(n_full + dd + 1, later, 1 - slot) if later else {}
        vt = values_t(n_full + dd)
        for si in live:
            ms[si] = softmax_pv(si, ms[si], s_maxes[si], slot, vt, dd == streams[si][1])
        s_maxes = nxt
    for r in range(n_sub):
        a0 = acc_scr[streams.index((0, r))][...]
        a1 = acc_scr[streams.index((1, r))][...]
        out_t = jnp.where(vrow < MLA_V, a0[:LANES] / a0[LANES:LANES + 1], a1[:LANES] / a1[LANES:LANES + 1])
        o_ref[0, r * tk:(r + 1) * tk, :] = out_t.T.astype(BF16)


def _attn(q, k, v):
    bsz, seq, _ = q.shape
    tk = min(ATT_TK, seq)
    tq = min(ATT_TQ, seq)
    assert (tq // tk) % 2 == 0, "the two-slot score pipeline needs an even number of query sub-tiles"
    n_streams = 2 * (tq // tk)
    body = functools.partial(_attn_body, tq=tq, tk=tk)
    return pl.pallas_call(
        body,
        grid=(bsz, MLA_HEADS // 2, seq // tq),
        in_specs=[pl.BlockSpec((1, tq, 2 * LANES), lambda b, hp, i: (b, i, hp)),
                  pl.BlockSpec((1, seq, 2 * LANES), lambda b, hp, i: (b, 0, hp)),
                  pl.BlockSpec((1, VT_ROWS, seq), lambda b, hp, i: (b, hp, 0))],
        out_specs=pl.BlockSpec((1, tq, LANES), lambda b, hp, i: (b, i, hp)),
        out_shape=jax.ShapeDtypeStruct((bsz, seq, MLA_HEADS * MLA_V), BF16),
        scratch_shapes=[pltpu.VMEM((tk, tk), F32)] * (2 * n_streams) + [pltpu.VMEM((VT_ROWS, tk), F32)] * n_streams,
        compiler_params=pltpu.CompilerParams(dimension_semantics=("parallel", "parallel", "arbitrary"),
                                             vmem_limit_bytes=VMEM_LIMIT),
        name="attn",
    )(q, k, v)


def _outproj_body(ys_ref, ya_ref, x_ref, g1_ref, sc2_ref, sh2_ref, mn_ref, wo1_ref, wo2_ref, pmn_ref, pfn_ref,
                  wr_ref, br_ref, x1_ref, h2_ref, gate_ref, idx_ref, cnt_ref, pref_ref, ltab_ref, cnt_scr):
    first = (pl.program_id(0) == 0) & (pl.program_id(1) == 0)

    @pl.when(first)
    def _():
        cnt_scr[...] = jnp.zeros_like(cnt_scr)

    yan =(_rms(ya_ref[0].astype(F32)) * mn_ref[...]).astype(BF16)
    mix = _dot(ys_ref[0], wo1_ref[...]) + _dot(yan, wo2_ref[...])
    x1 = x_ref[0] + g1_ref[0] * (_rms(mix) * pmn_ref[...])
    x1_ref[0] = x1
    h2 = _rms(x1) * (pfn_ref[...] * (1.0 + sc2_ref[0])) + sh2_ref[0]
    h2_ref[0] = h2.astype(BF16)
    logits = _dot3(h2, wr_ref[...]) + br_ref[...]
    lane = lax.broadcasted_iota(jnp.int32, logits.shape, 1)
    cur = jnp.where(lane < N_EXPERTS, logits, -jnp.inf)
    vals, idxs = [], []
    for _ in range(TOP_K):
        m = jnp.max(cur, axis=-1, keepdims=True)
        ix = jnp.min(jnp.where(cur == m, lane, LANES), axis=-1, keepdims=True)
        vals.append(m)
        idxs.append(ix)
        cur = jnp.where(lane == ix, -jnp.inf, cur)
    es = [jnp.exp(v - vals[0]) for v in vals]
    denom = es[0]
    for e in es[1:]:
        denom = denom + e
    onehot = jnp.zeros(logits.shape, F32)
    for kk in range(TOP_K):
        onehot = onehot + jnp.where(lane == idxs[kk], 1.0, 0.0)
    ts = logits.shape[0]
    row = lax.broadcasted_iota(jnp.int32, (ts, ts), 0)
    col = lax.broadcasted_iota(jnp.int32, (ts, ts), 1)
    before = jnp.where(row > col, 1.0, 0.0).astype(BF16)
    prior = _dot(before, onehot.astype(BF16))
    seen = cnt_scr[...]
    pref_ref[0] = seen
    tok = lax.broadcasted_iota(jnp.int32, logits.shape, 0)
    tok_hi = (tok // 16).astype(F32)
    tok_lo = (tok % 16).astype(F32)
    ltab = jnp.zeros((2 * LANES, ts), F32)
    gate_out = jnp.zeros(logits.shape, F32)
    idx_out = jnp.zeros(logits.shape, jnp.int32)
    for kk in range(TOP_K):
        mine = lane == idxs[kk]
        local = jnp.sum(jnp.where(mine, prior, 0.0), axis=-1, keepdims=True)
        rank = local + jnp.sum(jnp.where(mine, seen, 0.0), axis=-1, keepdims=True)
        at_rank = jnp.where(col == local.astype(jnp.int32), 1.0, 0.0).astype(BF16)
        tagged = jnp.concatenate([jnp.where(mine, tok_hi, 0.0), jnp.where(mine, tok_lo, 0.0)], axis=1)
        ltab = ltab + _dot_tn(tagged.astype(BF16), at_rank)
        gate_out = jnp.where(lane == kk, es[kk] / denom, gate_out)
        idx_out = jnp.where(lane == kk, idxs[kk], idx_out)
        idx_out = jnp.where(lane == TOP_K + kk, rank.astype(jnp.int32), idx_out)
    gate_ref[0] = gate_out
    idx_ref[0] = idx_out
    ltab_ref[0] = (16.0 * ltab[:N_EXPERTS] + ltab[LANES:LANES + N_EXPERTS]).astype(jnp.int32)
    cnt_scr[...] = seen + jnp.sum(onehot, axis=0, keepdims=True)
    cnt_ref[...] = cnt_scr[...]


def _outproj(y_ssd, y_att, x, g1, sc2, sh2, mla_norm, wo1, wo2, post_mix_norm, pre_ffn_norm, wr, br, *, batch_offset):
    bsz, seq, _ = y_ssd.shape
    d = x.shape[-1]
    ts = min(TS_PROJ, seq)
    tiles = seq // ts

    def tok(width):
        return pl.BlockSpec((1, ts, width), lambda b, i: (b, i, 0))

    x_spec = pl.BlockSpec((1, ts, d), lambda b, i: (batch_offset + b, i, 0))

    def per_batch(width):
        return pl.BlockSpec((1, 1, width), lambda b, i: (b, 0, 0))

    consts = (mla_norm, wo1, wo2, post_mix_norm, pre_ffn_norm, wr, br)
    return pl.pallas_call(
        _outproj_body,
        grid=(bsz, seq // ts),
        in_specs=[tok(SSD_INNER), tok(MLA_HEADS * MLA_V), x_spec, per_batch(d), per_batch(d), per_batch(d)]
                 + [_const_spec(w.shape) for w in consts],
        out_specs=[tok(d), tok(d), tok(LANES), tok(LANES), _const_spec((1, LANES)),
                   pl.BlockSpec((1, 1, LANES), lambda b, i: (b * tiles + i, 0, 0)),
                   pl.BlockSpec((1, N_EXPERTS, ts), lambda b, i: (b * tiles + i, 0, 0))],
        out_shape=[jax.ShapeDtypeStruct((bsz, seq, d), F32), jax.ShapeDtypeStruct((bsz, seq, d), BF16),
                   jax.ShapeDtypeStruct((bsz, seq, LANES), F32), jax.ShapeDtypeStruct((bsz, seq, LANES), jnp.int32),
                   jax.ShapeDtypeStruct((1, LANES), F32),
                   jax.ShapeDtypeStruct((bsz * tiles, 1, LANES), F32),
                   jax.ShapeDtypeStruct((bsz * tiles, N_EXPERTS, ts), jnp.int32)],
        scratch_shapes=[pltpu.VMEM((1, LANES), F32)],
        compiler_params=pltpu.CompilerParams(dimension_semantics=("arbitrary", "arbitrary"),
                                             vmem_limit_bytes=VMEM_LIMIT),
        name="outproj",
    )(y_ssd, y_att, x, g1, sc2, sh2, *consts)


def _moe_body(be_ref, na_ref, x_ref, wgu_ref, bgu_ref, wd_ref, bd_ref, y_ref, wgu_b, wd_b):
    i = pl.program_id(0)

    @pl.when((i == 0) | (be_ref[i] != be_ref[jnp.maximum(i - 1, 0)]))
    def _():
        wgu_b[...] = wgu_ref[0].astype(BF16)
        wd_b[...] = wd_ref[0].astype(BF16)

    @pl.when(i < na_ref[0])
    def _():
        gu = _dot(x_ref[...], wgu_b[...]) + bgu_ref[0]
        glu = jnp.minimum(gu[:, :D_FF_EXPERT], SWIGLU_LIMIT)
        lin = jnp.clip(gu[:, D_FF_EXPERT:], -SWIGLU_LIMIT, SWIGLU_LIMIT)
        act = glu * jax.nn.sigmoid(SWIGLU_ALPHA * glu) * (lin + 1.0)
        y_ref[...] = (_dot(act.astype(BF16), wd_b[...]) + bd_ref[0]).astype(BF16)

    @pl.when(i >= na_ref[0])
    def _():
        y_ref[...] = jnp.zeros_like(y_ref)


def _moe(block_expert, n_active, xg, wgu, bgu, wd, bd):
    n_slots, d = xg.shape
    n_blocks = n_slots // MOE_TB
    f2 = wgu.shape[2]
    return pl.pallas_call(
        _moe_body,
        grid_spec=pltpu.PrefetchScalarGridSpec(
            num_scalar_prefetch=2,
            grid=(n_blocks,),
            in_specs=[pl.BlockSpec((MOE_TB, d), lambda i, be, na: (i, 0)),
                      pl.BlockSpec((1, d, f2), lambda i, be, na: (be[i], 0, 0)),
                      pl.BlockSpec((1, 1, f2), lambda i, be, na: (be[i], 0, 0)),
                      pl.BlockSpec((1, f2 // 2, d), lambda i, be, na: (be[i], 0, 0)),
                      pl.BlockSpec((1, 1, d), lambda i, be, na: (be[i], 0, 0))],
            out_specs=pl.BlockSpec((MOE_TB, d), lambda i, be, na: (i, 0)),
            scratch_shapes=[pltpu.VMEM((d, f2), BF16), pltpu.VMEM((f2 // 2, d), BF16)],
        ),
        out_shape=jax.ShapeDtypeStruct((n_slots, d), BF16),
        compiler_params=pltpu.CompilerParams(dimension_semantics=("arbitrary",),
                                             vmem_limit_bytes=VMEM_LIMIT),
        name="moe",
    )(block_expert, n_active, xg, wgu, bgu, wd, bd)


def _final_body(x1_ref, y_ref, gate_ref, g2_ref, gain_ref, *rest):
    o_ref = rest[-1]
    gates = gate_ref[0]
    f = gates[:, 0:1] * y_ref[0, 0].astype(F32)
    for kk in range(1, TOP_K):
        f = f + gates[:, kk:kk + 1] * y_ref[kk, 0].astype(F32)
    o_ref[0] = x1_ref[0] + g2_ref[0] * (_rms(f) * gain_ref[...])


def _final(x1, y4, gates, g2, gain, out_prev, batch_offset, total_batch):
    bsz, seq, d = x1.shape
    ts = min(TS_PROJ, seq)

    def tok(width):
        return pl.BlockSpec((1, ts, width), lambda b, i: (b, i, 0))

    in_specs = [tok(d), pl.BlockSpec((TOP_K, 1, ts, d), lambda b, i: (0, b, i, 0)), tok(LANES),
                pl.BlockSpec((1, 1, d), lambda b, i: (b, 0, 0)), _const_spec((1, d))]
    args = [x1, y4, gates, g2, gain]
    aliases = {}
    if out_prev is not None:
        in_specs.append(pl.BlockSpec(memory_space=pl.ANY))
        args.append(out_prev)
        aliases = {len(args) - 1: 0}
    return pl.pallas_call(
        _final_body,
        grid=(bsz, seq // ts),
        in_specs=in_specs,
        out_specs=pl.BlockSpec((1, ts, d), lambda b, i: (batch_offset + b, i, 0)),
        out_shape=jax.ShapeDtypeStruct((total_batch, seq, d), F32),
        input_output_aliases=aliases,
        compiler_params=pltpu.CompilerParams(dimension_semantics=("parallel", "parallel"),
                                             vmem_limit_bytes=VMEM_LIMIT),
        name="final",
    )(*args)


def _head_blocks(cols):
    out = []
    for c in cols:
        pad = LANES - c.shape[1]
        out.append(jnp.pad(c, ((0, 0), (0, pad))) if pad else c)
    return jnp.concatenate(out, axis=1)


def _prep_mixer_weights(w_in, w_q_up, w_kv_up):
    d = w_in.shape[0]
    wz = w_in[:, OFF_Z:OFF_XBC]
    wxbc = w_in[:, OFF_XBC:OFF_DT]
    wdt = w_in[:, OFF_DT:OFF_QA]
    wqa = w_in[:, OFF_QA:OFF_KVA]
    wkva = w_in[:, OFF_KVA:OFF_KR]
    wkr = w_in[:, OFF_KR:IN_COLS]
    kr_blk = jnp.concatenate([jnp.zeros((d, ROPE_LO), F32), wkr, jnp.zeros((d, LANES - ROPE_LO - MLA_ROPE), F32)], axis=1)
    dt_blk = jnp.pad(wdt, ((0, 0), (0, LANES - SSD_HEADS)))
    wsm = jnp.concatenate([kr_blk, dt_blk], axis=1)
    qh = MLA_NOPE + MLA_ROPE
    scale = math.log2(math.e) / math.sqrt(qh)
    wqup = _head_blocks([w_q_up[:, h * qh:(h + 1) * qh] for h in range(MLA_HEADS)]) * scale
    kvh = MLA_NOPE + MLA_V
    wkup = _head_blocks([w_kv_up[:, h * kvh:h * kvh + MLA_NOPE] for h in range(MLA_HEADS)])
    vcols = []
    for h in range(MLA_HEADS):
        vcols.append(w_kv_up[:, h * kvh + MLA_NOPE:(h + 1) * kvh])
        if h % 2 == 1:
            vcols.append(jnp.zeros((w_kv_up.shape[0], VT_ROWS - LANES), F32))
    wvup = jnp.concatenate(vcols, axis=1).T
    return tuple(w.astype(BF16) for w in (wz, wxbc, wsm, wqa, wkva)) + tuple(w.astype(BF16) for w in (wqup, wkup, wvup))


def _rope_tables(positions):
    inv_freq = ROPE_BASE ** (-(jnp.arange(HALF_ROPE, dtype=F32) * 2.0 / MLA_ROPE))
    angles = positions.astype(F32)[..., None] * inv_freq
    cos = jnp.cos(angles)
    sin = jnp.sin(angles)
    shp = angles.shape[:-1]
    ct = jnp.concatenate([jnp.ones(shp + (ROPE_LO,), F32), cos, cos,
                          jnp.zeros(shp + (LANES - ROPE_LO - MLA_ROPE,), F32)], axis=-1)
    st = jnp.concatenate([jnp.zeros(shp + (ROPE_LO,), F32), -sin, sin,
                          jnp.zeros(shp + (LANES - ROPE_LO - MLA_ROPE,), F32)], axis=-1)
    return ct, st


def _route(idx, rank, counts, tile_seen, ltab, n_tok):
    n_assign = n_tok * TOP_K
    n_tiles, _, tile = ltab.shape
    padded = ((counts + MOE_TB - 1) // MOE_TB) * MOE_TB
    padded_end = jnp.cumsum(padded)
    padded_start = padded_end - padded
    experts = jnp.arange(N_EXPERTS, dtype=jnp.int32)
    start_of = jnp.sum(jnp.where(idx[..., None] == experts, padded_start, 0), axis=-1)
    dest = (start_of + rank).reshape(-1)
    n_blocks = n_assign // MOE_TB + N_EXPERTS
    block_start = jnp.arange(n_blocks, dtype=jnp.int32) * MOE_TB
    block_expert = jnp.minimum(jnp.sum((padded_end[None, :] <= block_start[:, None]).astype(jnp.int32), axis=1),
                               N_EXPERTS - 1)
    n_active = (padded_end[-1] // MOE_TB).astype(jnp.int32).reshape(1)
    j = (block_start - padded_start[block_expert])[:, None] + jnp.arange(MOE_TB, dtype=jnp.int32)[None, :]
    seen_blk = tile_seen.T[block_expert]
    reached = seen_blk[:, None, :] <= j[:, :, None]
    tau = jnp.sum(reached.astype(jnp.int32), axis=-1) - 1
    j_local = j - jnp.max(jnp.where(reached, seen_blk[:, None, :], 0), axis=-1)
    valid = j < counts[block_expert][:, None]
    flat = (tau * N_EXPERTS + block_expert[:, None]) * tile + j_local
    tok_local = ltab.reshape(-1)[jnp.clip(flat, 0, n_tiles * N_EXPERTS * tile - 1)]
    slot_tok = jnp.where(valid, tau * tile + tok_local, 0).reshape(-1)
    return dest, slot_tok, block_expert, n_active


def kernel(x, c, positions, w_ada, b_ada, pre_mix_norm, w_in, conv_w, conv_b, dt_bias, a_log, d_skip, ssd_norm, q_a_norm, w_q_up, kv_a_norm, w_kv_up, mla_norm, w_out, post_mix_norm, pre_ffn_norm, w_router, b_router, w_gate_up, b_gate_up, w_down, b_down, post_ffn_norm):
    bsz, seq, d = x.shape
    ct, st = _rope_tables(positions)
    n_groups = BATCH_GROUPS if bsz % BATCH_GROUPS == 0 else 1
    gb = bsz // n_groups
    n_tok = gb * seq
    pad_h = LANES - SSD_HEADS
    for l in range(w_ada.shape[0]):
        mod = _adaln(c, w_ada[l], b_ada[l])
        mods = [m.reshape(bsz, 1, d) for m in jnp.split(mod, 6, axis=-1)]
        mixer_w = _prep_mixer_weights(w_in[l], w_q_up[l], w_kv_up[l])
        wo = w_out[l].astype(BF16)
        wr = jnp.pad(w_router[l], ((0, 0), (0, LANES - N_EXPERTS)))
        br = jnp.pad(b_router[l], (0, LANES - N_EXPERTS)).reshape(1, LANES)
        dtb = jnp.pad(dt_bias[l], (0, pad_h)).reshape(1, LANES)
        alog = jnp.pad(a_log[l], (0, pad_h)).reshape(1, LANES)
        dsk = jnp.repeat(d_skip[l], SSD_HEAD_DIM).reshape(1, -1)
        out = None
        prev_slot_tok = None
        for gi in range(n_groups):
            grp = slice(gi * gb, (gi + 1) * gb)
            sh1, sc1, g1, sh2, sc2, g2 = [m[grp] for m in mods]
            if prev_slot_tok is not None:
                sc1, prev_slot_tok = lax.optimization_barrier((sc1, prev_slot_tok))
            z, xbc, dt, q, k, v = _inproj(x, sc1, sh1, pre_mix_norm[l].reshape(1, d), ct, st, *mixer_w[:5],
                                          q_a_norm[l].reshape(1, -1), kv_a_norm[l].reshape(1, -1), *mixer_w[5:],
                                          batch_offset=gi * gb)
            y_ssd = _ssd(xbc, z, dt, conv_w[l], conv_b[l].reshape(1, -1), dtb, alog, dsk, ssd_norm[l].reshape(1, -1))
            y_att = _attn(q, k, v)
            x1, h2, gates, route, cnt, seen, ltab = _outproj(
                y_ssd, y_att, x, g1, sc2, sh2, mla_norm[l].reshape(1, -1), wo[:SSD_INNER], wo[SSD_INNER:],
                post_mix_norm[l].reshape(1, d), pre_ffn_norm[l].reshape(1, d), wr, br, batch_offset=gi * gb)
            route = route.reshape(n_tok, LANES)
            counts = cnt[0, :N_EXPERTS].astype(jnp.int32)
            tile_seen = seen[:, 0, :N_EXPERTS].astype(jnp.int32)
            dest, slot_tok, block_expert, n_active = _route(route[:, :TOP_K], route[:, TOP_K:2 * TOP_K], counts,
                                                            tile_seen, ltab, n_tok)
            prev_slot_tok = slot_tok
            xs = h2.reshape(n_tok, d)[slot_tok]
            y = _moe(block_expert, n_active, xs, w_gate_up[l], b_gate_up[l][:, None, :], w_down[l], b_down[l][:, None, :])
            y4 = y[dest.reshape(n_tok, TOP_K).T.reshape(-1)].reshape(TOP_K, gb, seq, d)
            out = _final(x1, y4, gates, g2, post_ffn_norm[l].reshape(1, d), out, gi * gb, bsz)
        x = out
    return x
```

```python
import functools
import math

import jax
import jax.numpy as jnp
from jax import lax
from jax.experimental import pallas as pl
from jax.experimental.pallas import tpu as pltpu

F32 = jnp.float32
BF16 = jnp.bfloat16

D_MODEL = 1024
CHUNK = 64
SSD_INNER = 512
SSD_HEAD_DIM = 64
SSD_HEADS = 8
SSD_GROUPS = 2
SSD_STATE = 128
SSD_CONV = 4
SSD_BC = SSD_GROUPS * SSD_STATE
XBC_DIM = SSD_INNER + 2 * SSD_BC
MLA_V = 64
MLA_HEADS = 8
MLA_NOPE = 64
MLA_ROPE = 32
Q_LORA = 384
KV_LORA = 256
ROPE_BASE = 10000.0
OFF_Z = 0
OFF_XBC = OFF_Z + SSD_INNER
OFF_DT = OFF_XBC + XBC_DIM
OFF_QA = OFF_DT + SSD_HEADS
OFF_KVA = OFF_QA + Q_LORA
OFF_KR = OFF_KVA + KV_LORA
IN_COLS = OFF_KR + MLA_ROPE
N_EXPERTS = 32
TOP_K = 4
D_FF_EXPERT = 1024
SWIGLU_LIMIT = 7.0
SWIGLU_ALPHA = 1.702
NORM_EPS = 1e-6

LANES = 128
HALF_ROPE = MLA_ROPE // 2
ROPE_LO = MLA_NOPE
ROPE_HI = MLA_NOPE + HALF_ROPE
VT_ROWS = 144

TS_PROJ = 512
SSD_L = 256
ATT_TQ = 512
ATT_TK = 256
MOE_TB = 512
BATCH_GROUPS = 2
VMEM_LIMIT = 56 * 1024 * 1024


def _dot(a, b):
    return jnp.dot(a, b, preferred_element_type=F32)


def _dot_nt(a, b):
    return lax.dot_general(a, b, (((1,), (1,)), ((), ())), preferred_element_type=F32)


def _dot_tn(a, b):
    return lax.dot_general(a, b, (((0,), (0,)), ((), ())), preferred_element_type=F32)


def _split2(x):
    hi = x.astype(BF16)
    lo = (x - hi.astype(F32)).astype(BF16)
    return hi, lo


def _split3(x):
    h1 = x.astype(BF16)
    r1 = x - h1.astype(F32)
    h2 = r1.astype(BF16)
    h3 = (r1 - h2.astype(F32)).astype(BF16)
    return h1, h2, h3


def _dot3(a, b):
    ah, al = _split2(a)
    bh, bl = _split2(b)
    return _dot(ah, bh) + _dot(ah, bl) + _dot(al, bh)


def _rms(x):
    return x * lax.rsqrt(jnp.mean(x * x, axis=-1, keepdims=True) + NORM_EPS)


def _silu(x):
    return x * jax.nn.sigmoid(x)


def _const_spec(shape):
    nd = len(shape)
    return pl.BlockSpec(shape, lambda *_: (0,) * nd)


def _adaln_body(c_ref, w_ref, b_ref, o_ref):
    o_ref[...] = _dot3(_silu(c_ref[...]), w_ref[...]) + b_ref[...]


def _adaln(c, w_ada, b_ada):
    bsz, d = c.shape
    n = w_ada.shape[1]
    tn = 1024
    return pl.pallas_call(
        _adaln_body,
        grid=(n // tn,),
        in_specs=[_const_spec((bsz, d)),
                  pl.BlockSpec((d, tn), lambda j: (0, j)),
                  pl.BlockSpec((1, tn), lambda j: (0, j))],
        out_specs=pl.BlockSpec((bsz, tn), lambda j: (0, j)),
        out_shape=jax.ShapeDtypeStruct((bsz, n), F32),
        compiler_params=pltpu.CompilerParams(dimension_semantics=("arbitrary",),
                                             vmem_limit_bytes=VMEM_LIMIT),
        name="adaln",
    )(c, w_ada, b_ada.reshape(1, n))


def _rope_block(xb, ct, st, lane):
    partner = jnp.where(lane < ROPE_HI, pltpu.roll(xb, LANES - HALF_ROPE, 1), pltpu.roll(xb, HALF_ROPE, 1))
    return xb * ct + partner * st


def _inproj_body(x_ref, sc_ref, sh_ref, g_ref, ct_ref, st_ref, wz_ref, wxbc_ref, wsm_ref, wqa_ref, wkva_ref,
                 qn_ref, kvn_ref, wqup_ref, wkup_ref, wvup_ref,
                 z_ref, xbc_ref, dt_ref, q_ref, k_ref, v_ref):
    x = x_ref[0]
    h = _rms(x) * (g_ref[...] * (1.0 + sc_ref[0])) + sh_ref[0]
    hb = h.astype(BF16)
    z_ref[0] = _dot(hb, wz_ref[...]).astype(BF16)
    xbc_ref[0] = _dot(hb, wxbc_ref[...]).astype(BF16)
    sm = _dot(hb, wsm_ref[...])
    dt_ref[0] = sm[:, LANES:]
    ct = ct_ref[0]
    st = st_ref[0]
    lane = lax.broadcasted_iota(jnp.int32, ct.shape, 1)
    kr = _rope_block(sm[:, :LANES], ct, st, lane)
    qan = (_rms(_dot(hb, wqa_ref[...])) * qn_ref[...]).astype(BF16)
    q = _dot(qan, wqup_ref[...])
    for hh in range(MLA_HEADS):
        blk = slice(hh * LANES, (hh + 1) * LANES)
        q_ref[0, :, blk] = _rope_block(q[:, blk], ct, st, lane).astype(BF16)
    kvn = (_rms(_dot(hb, wkva_ref[...])) * kvn_ref[...]).astype(BF16)
    k = _dot(kvn, wkup_ref[...])
    for hh in range(MLA_HEADS):
        blk = slice(hh * LANES, (hh + 1) * LANES)
        k_ref[0, :, blk] = (k[:, blk] + kr).astype(BF16)
    vt = _dot_nt(wvup_ref[...], kvn)
    vrow = lax.broadcasted_iota(jnp.int32, vt.shape, 0)
    v_ref[0] = jnp.where(vrow % VT_ROWS == LANES, 1.0, vt).astype(BF16)


def _inproj(x, sc1, sh1, gain, ct, st, wz, wxbc, wsm, wqa, wkva, qn, kvn, wqup, wkup, wvup, *, batch_offset):
    _, seq, d = x.shape
    bsz = sc1.shape[0]
    ts = min(TS_PROJ, seq)
    hw = MLA_HEADS * LANES

    def tok(width):
        return pl.BlockSpec((1, ts, width), lambda b, i: (b, i, 0))

    def tok_full(width):
        return pl.BlockSpec((1, ts, width), lambda b, i: (batch_offset + b, i, 0))

    def per_batch(width):
        return pl.BlockSpec((1, 1, width), lambda b, i: (b, 0, 0))

    weights = (wz, wxbc, wsm, wqa, wkva, qn, kvn, wqup, wkup, wvup)
    out_widths = (SSD_INNER, XBC_DIM, LANES, hw, hw)
    out_dtypes = (BF16, BF16, F32, BF16, BF16)
    vdim = (MLA_HEADS // 2) * VT_ROWS
    return pl.pallas_call(
        _inproj_body,
        grid=(bsz, seq // ts),
        in_specs=[tok_full(d), per_batch(d), per_batch(d), _const_spec((1, d)), tok_full(LANES), tok_full(LANES)]
                 + [_const_spec(w.shape) for w in weights],
        out_specs=[tok(w) for w in out_widths] + [pl.BlockSpec((1, vdim, ts), lambda b, i: (b, 0, i))],
        out_shape=[jax.ShapeDtypeStruct((bsz, seq, w), dt) for w, dt in zip(out_widths, out_dtypes)]
                  + [jax.ShapeDtypeStruct((bsz, vdim, seq), BF16)],
        compiler_params=pltpu.CompilerParams(dimension_semantics=("parallel", "parallel"),
                                             vmem_limit_bytes=VMEM_LIMIT),
        name="inproj",
    )(x, sc1, sh1, gain, ct, st, *weights)


CONV_HALO = 16


def _ssd_body(xc_ref, xp_ref, z_ref, dt_ref, cw_ref, cb_ref, dtb_ref, alog_ref, dsk_ref, ng_ref,
              y_ref, xs_scr, st_scr, *, blk):
    i = pl.program_id(1)

    @pl.when(i == 0)
    def _():
        st_scr[...] = jnp.zeros_like(st_scr)

    xs_scr[0:CONV_HALO, :] = jnp.where(i > 0, xp_ref[0].astype(F32), 0.0)
    xs_scr[CONV_HALO:CONV_HALO + blk, :] = xc_ref[0].astype(F32)
    conv = cb_ref[...]
    for kk in range(SSD_CONV):
        off = CONV_HALO - (SSD_CONV - 1) + kk
        conv = conv + cw_ref[kk:kk + 1, :] * xs_scr[off:off + blk, :]
    xa = _silu(conv)
    xs = xa[:, :SSD_INNER]
    bm = xa[:, SSD_INNER:SSD_INNER + SSD_BC]
    cm = xa[:, SSD_INNER + SSD_BC:]

    hl = lax.broadcasted_iota(jnp.int32, (1, LANES), 1)
    dtr = dt_ref[0] + dtb_ref[...]
    dt = jnp.maximum(dtr, 0.0) + jnp.log(1.0 + jnp.exp(-jnp.abs(dtr)))
    a = jnp.where(hl < SSD_HEADS, -jnp.exp(alog_ref[...]), 0.0)
    dta = dt * a
    row = lax.broadcasted_iota(jnp.int32, (blk, blk), 0)
    col = lax.broadcasted_iota(jnp.int32, (blk, blk), 1)
    tril = row >= col
    trilb = jnp.where(tril, 1.0, 0.0).astype(BF16)
    d1, d2, d3 = _split3(dta)
    cs = _dot(trilb, d1) + _dot(trilb, d2) + _dot(trilb, d3)
    cs_last = cs[blk - 1:blk, :]
    ecs = jnp.exp(cs)
    dte = jnp.exp(cs_last - cs)
    cs_t = cs.T

    er = lax.broadcasted_iota(jnp.int32, (LANES, SSD_INNER), 0)
    ec = lax.broadcasted_iota(jnp.int32, (LANES, SSD_INNER), 1)
    expand = jnp.where(ec // SSD_HEAD_DIM == er, 1.0, 0.0).astype(BF16)

    def per_channel(v):
        vh, vl = _split2(v)
        return _dot(vh, expand) + _dot(vl, expand)

    dt_e = per_channel(dt)
    ecs_e = per_channel(ecs)
    dte_e = per_channel(dte)
    xdt = xs * dt_e
    xdt_b = xdt.astype(BF16)
    xw_b = (xdt * dte_e).astype(BF16)

    gw = SSD_INNER // SSD_GROUPS
    heads_per_group = SSD_HEADS // SSD_GROUPS
    lane = lax.broadcasted_iota(jnp.int32, (blk, LANES), 1)
    y_groups = []
    for g in range(SSD_GROUPS):
        bg = bm[:, g * SSD_STATE:(g + 1) * SSD_STATE].astype(BF16)
        cg = cm[:, g * SSD_STATE:(g + 1) * SSD_STATE].astype(BF16)
        cb = _dot_nt(cg, bg)
        state = st_scr[g]
        y_off = _dot(cg, state.astype(BF16))
        pairs = []
        for j in range(heads_per_group // 2):
            xp = xdt_b[:, g * gw + j * LANES:g * gw + (j + 1) * LANES]
            halves = []
            for u in range(2):
                hidx = g * heads_per_group + 2 * j + u
                seg = cs[:, hidx:hidx + 1] - cs_t[hidx:hidx + 1, :]
                dec = jnp.exp(jnp.where(tril, seg, -jnp.inf))
                halves.append(_dot((cb * dec).astype(BF16), xp))
            pairs.append(jnp.where(lane < SSD_HEAD_DIM, halves[0], halves[1]))
        y_diag = jnp.concatenate(pairs, axis=1)
        y_groups.append(y_diag + y_off * ecs_e[:, g * gw:(g + 1) * gw])
        st_scr[g] = (state * ecs_e[blk - 1:blk, g * gw:(g + 1) * gw]
                     + _dot_tn(bg, xw_b[:, g * gw:(g + 1) * gw]))
    y = jnp.concatenate(y_groups, axis=1) + xs * dsk_ref[...]
    y = y * _silu(z_ref[0].astype(F32))
    y = jnp.concatenate([_rms(y[:, g * gw:(g + 1) * gw]) for g in range(SSD_GROUPS)], axis=1)
    y_ref[0] = (y * ng_ref[...]).astype(BF16)


def _ssd(xbc, z, dt, conv_w, conv_b, dt_bias, a_log, d_skip_e, norm_gain):
    bsz, seq, _ = xbc.shape
    blk = min(SSD_L, seq)
    halo_per_blk = blk // CONV_HALO
    body = functools.partial(_ssd_body, blk=blk)
    return pl.pallas_call(
        body,
        grid=(bsz, seq // blk),
        in_specs=[pl.BlockSpec((1, blk, XBC_DIM), lambda b, i: (b, i, 0)),
                  pl.BlockSpec((1, CONV_HALO, XBC_DIM), lambda b, i: (b, jnp.maximum(i * halo_per_blk - 1, 0), 0)),
                  pl.BlockSpec((1, blk, SSD_INNER), lambda b, i: (b, i, 0)),
                  pl.BlockSpec((1, blk, LANES), lambda b, i: (b, i, 0)),
                  _const_spec((SSD_CONV, XBC_DIM)), _const_spec((1, XBC_DIM)),
                  _const_spec((1, LANES)), _const_spec((1, LANES)),
                  _const_spec((1, SSD_INNER)), _const_spec((1, SSD_INNER))],
        out_specs=pl.BlockSpec((1, blk, SSD_INNER), lambda b, i: (b, i, 0)),
        out_shape=jax.ShapeDtypeStruct((bsz, seq, SSD_INNER), BF16),
        scratch_shapes=[pltpu.VMEM((CONV_HALO + blk, XBC_DIM), F32),
                        pltpu.VMEM((SSD_GROUPS, SSD_STATE, SSD_INNER // SSD_GROUPS), F32)],
        compiler_params=pltpu.CompilerParams(dimension_semantics=("parallel", "arbitrary"),
                                             vmem_limit_bytes=VMEM_LIMIT),
        name="ssd",
    )(xbc, xbc, z, dt, conv_w, conv_b, dt_bias, a_log, d_skip_e, norm_gain)


def _attn_body(q_ref, k_ref, v_ref, o_ref, *scratch, tq, tk):
    n_streams = 2 * (tq // tk)
    s_scr = (scratch[:n_streams], scratch[n_streams:2 * n_streams])
    acc_scr = scratch[2 * n_streams:]
    qi = pl.program_id(2)
    n_sub = tq // tk
    n_full = qi * n_sub
    krow = lax.broadcasted_iota(jnp.int32, (tk, tk), 0)
    qcol = lax.broadcasted_iota(jnp.int32, (tk, tk), 1)
    diag_ok = krow // CHUNK <= qcol // CHUNK
    vrow = lax.broadcasted_iota(jnp.int32, (LANES, tk), 0)
    streams = [(u, r) for u in range(2) for r in range(n_sub)]
    qs = [q_ref[0, r * tk:(r + 1) * tk, u * LANES:(u + 1) * LANES] for u, r in streams]

    def put_scores(ki, which, slot):
        start = pl.multiple_of(ki * tk, tk)
        k2 = k_ref[0, pl.ds(start, tk), :]
        out = {}
        for si in which:
            u = streams[si][0]
            s = _dot_nt(k2[:, u * LANES:(u + 1) * LANES], qs[si])
            s_scr[slot][si][...] = s
            out[si] = jnp.max(s, axis=0, keepdims=True)
        return out

    def values_t(ki):
        return v_ref[0, :, pl.ds(pl.multiple_of(ki * tk, tk), tk)]

    def softmax_pv(si, m, s_max, slot, vt, masked):
        s = s_scr[slot][si][...]
        if masked:
            s = jnp.where(diag_ok, s, -jnp.inf)
            s_max = jnp.max(s, axis=0, keepdims=True)
        m_new = jnp.maximum(m, s_max)
        alpha = jnp.exp2(m - m_new)
        p = jnp.exp2((s - m_new).astype(BF16))
        acc_scr[si][...] = alpha * acc_scr[si][...] + _dot(vt, p)
        return m_new

    every = list(range(len(streams)))
    for ref in acc_scr:
        ref[...] = jnp.zeros_like(ref)

    def step(j, state):
        ms, s_maxes = state
        for slot in range(2):
            ki = 2 * j + slot
            nxt = put_scores(ki + 1, every, 1 - slot)
            vt = values_t(ki)
            ms = tuple(softmax_pv(si, ms[si], s_maxes[si], slot, vt, False) for si in every)
            s_maxes = tuple(nxt[si] for si in every)
        return ms, s_maxes

    first = put_scores(0, every, 0)
    m_init = jnp.full((1, tk), -jnp.inf, F32)
    ms, s_maxes = lax.fori_loop(0, n_full // 2, step, ((m_init,) * len(streams), tuple(first[si] for si in every)))
    ms = list(ms)
    s_maxes = dict(zip(every, s_maxes))
    for dd in range(n_sub):
        slot = dd % 2
        live = [si for si in every if dd <= streams[si][1]]
        later = [si for si in every if dd + 1 <= streams[si][1]]
        nxt = put_scores(n_full + dd + 1, later, 1 - slot) if later else {}
        vt = values_t(n_full + dd)
        for si in live:
            ms[si] = softmax_pv(si, ms[si], s_maxes[si], slot, vt, dd == streams[si][1])
        s_maxes = nxt
    for r in range(n_sub):
        a0 = acc_scr[streams.index((0, r))][...]
        a1 = acc_scr[streams.index((1, r))][...]
        out_t = jnp.where(vrow < MLA_V, a0[:LANES] / a0[LANES:LANES + 1], a1[:LANES] / a1[LANES:LANES + 1])
        o_ref[0, r * tk:(r + 1) * tk, :] = out_t.T.astype(BF16)


def _attn(q, k, v):
    bsz, seq, _ = q.shape
    tk = min(ATT_TK, seq)
    tq = min(ATT_TQ, seq)
    assert (tq // tk) % 2 == 0, "the two-slot score pipeline needs an even number of query sub-tiles"
    n_streams = 2 * (tq // tk)
    body = functools.partial(_attn_body, tq=tq, tk=tk)
    return pl.pallas_call(
        body,
        grid=(bsz, MLA_HEADS // 2, seq // tq),
        in_specs=[pl.BlockSpec((1, tq, 2 * LANES), lambda b, hp, i: (b, i, hp)),
                  pl.BlockSpec((1, seq, 2 * LANES), lambda b, hp, i: (b, 0, hp)),
                  pl.BlockSpec((1, VT_ROWS, seq), lambda b, hp, i: (b, hp, 0))],
        out_specs=pl.BlockSpec((1, tq, LANES), lambda b, hp, i: (b, i, hp)),
        out_shape=jax.ShapeDtypeStruct((bsz, seq, MLA_HEADS * MLA_V), BF16),
        scratch_shapes=[pltpu.VMEM((tk, tk), F32)] * (2 * n_streams) + [pltpu.VMEM((VT_ROWS, tk), F32)] * n_streams,
        compiler_params=pltpu.CompilerParams(dimension_semantics=("parallel", "parallel", "arbitrary"),
                                             vmem_limit_bytes=VMEM_LIMIT),
        name="attn",
    )(q, k, v)


def _outproj_body(ys_ref, ya_ref, x_ref, g1_ref, sc2_ref, sh2_ref, mn_ref, wo1_ref, wo2_ref, pmn_ref, pfn_ref,
                  wr_ref, br_ref, x1_ref, h2_ref, gate_ref, idx_ref, cnt_ref, pref_ref, ltab_ref, cnt_scr):
    first = (pl.program_id(0) == 0) & (pl.program_id(1) == 0)

    @pl.when(first)
    def _():
        cnt_scr[...] = jnp.zeros_like(cnt_scr)

    yan =(_rms(ya_ref[0].astype(F32)) * mn_ref[...]).astype(BF16)
    mix = _dot(ys_ref[0], wo1_ref[...]) + _dot(yan, wo2_ref[...])
    x1 = x_ref[0] + g1_ref[0] * (_rms(mix) * pmn_ref[...])
    x1_ref[0] = x1
    h2 = _rms(x1) * (pfn_ref[...] * (1.0 + sc2_ref[0])) + sh2_ref[0]
    h2_ref[0] = h2.astype(BF16)
    logits = _dot3(h2, wr_ref[...]) + br_ref[...]
    lane = lax.broadcasted_iota(jnp.int32, logits.shape, 1)
    cur = jnp.where(lane < N_EXPERTS, logits, -jnp.inf)
    vals, idxs = [], []
    for _ in range(TOP_K):
        m = jnp.max(cur, axis=-1, keepdims=True)
        ix = jnp.min(jnp.where(cur == m, lane, LANES), axis=-1, keepdims=True)
        vals.append(m)
        idxs.append(ix)
        cur = jnp.where(lane == ix, -jnp.inf, cur)
    es = [jnp.exp(v - vals[0]) for v in vals]
    denom = es[0]
    for e in es[1:]:
        denom = denom + e
    onehot = jnp.zeros(logits.shape, F32)
    for kk in range(TOP_K):
        onehot = onehot + jnp.where(lane == idxs[kk], 1.0, 0.0)
    ts = logits.shape[0]
    row = lax.broadcasted_iota(jnp.int32, (ts, ts), 0)
    col = lax.broadcasted_iota(jnp.int32, (ts, ts), 1)
    before = jnp.where(row > col, 1.0, 0.0).astype(BF16)
    prior = _dot(before, onehot.astype(BF16))
    seen = cnt_scr[...]
    pref_ref[0] = seen
    tok = lax.broadcasted_iota(jnp.int32, logits.shape, 0)
    tok_hi = (tok // 16).astype(F32)
    tok_lo = (tok % 16).astype(F32)
    ltab = jnp.zeros((2 * LANES, ts), F32)
    gate_out = jnp.zeros(logits.shape, F32)
    idx_out = jnp.zeros(logits.shape, jnp.int32)
    for kk in range(TOP_K):
        mine = lane == idxs[kk]
        local = jnp.sum(jnp.where(mine, prior, 0.0), axis=-1, keepdims=True)
        rank = local + jnp.sum(jnp.where(mine, seen, 0.0), axis=-1, keepdims=True)
        at_rank = jnp.where(col == local.astype(jnp.int32), 1.0, 0.0).astype(BF16)
        tagged = jnp.concatenate([jnp.where(mine, tok_hi, 0.0), jnp.where(mine, tok_lo, 0.0)], axis=1)
        ltab = ltab + _dot_tn(tagged.astype(BF16), at_rank)
        gate_out = jnp.where(lane == kk, es[kk] / denom, gate_out)
        idx_out = jnp.where(lane == kk, idxs[kk], idx_out)
        idx_out = jnp.where(lane == TOP_K + kk, rank.astype(jnp.int32), idx_out)
    gate_ref[0] = gate_out
    idx_ref[0] = idx_out
    ltab_ref[0] = (16.0 * ltab[:N_EXPERTS] + ltab[LANES:LANES + N_EXPERTS]).astype(jnp.int32)
    cnt_scr[...] = seen + jnp.sum(onehot, axis=0, keepdims=True)
    cnt_ref[...] = cnt_scr[...]


def _outproj(y_ssd, y_att, x, g1, sc2, sh2, mla_norm, wo1, wo2, post_mix_norm, pre_ffn_norm, wr, br, *, batch_offset):
    bsz, seq, _ = y_ssd.shape
    d = x.shape[-1]
    ts = min(TS_PROJ, seq)
    tiles = seq // ts

    def tok(width):
        return pl.BlockSpec((1, ts, width), lambda b, i: (b, i, 0))

    x_spec = pl.BlockSpec((1, ts, d), lambda b, i: (batch_offset + b, i, 0))

    def per_batch(width):
        return pl.BlockSpec((1, 1, width), lambda b, i: (b, 0, 0))

    consts = (mla_norm, wo1, wo2, post_mix_norm, pre_ffn_norm, wr, br)
    return pl.pallas_call(
        _outproj_body,
        grid=(bsz, seq // ts),
        in_specs=[tok(SSD_INNER), tok(MLA_HEADS * MLA_V), x_spec, per_batch(d), per_batch(d), per_batch(d)]
                 + [_const_spec(w.shape) for w in consts],
        out_specs=[tok(d), tok(d), tok(LANES), tok(LANES), _const_spec((1, LANES)),
                   pl.BlockSpec((1, 1, LANES), lambda b, i: (b * tiles + i, 0, 0)),
                   pl.BlockSpec((1, N_EXPERTS, ts), lambda b, i: (b * tiles + i, 0, 0))],
        out_shape=[jax.ShapeDtypeStruct((bsz, seq, d), F32), jax.ShapeDtypeStruct((bsz, seq, d), BF16),
                   jax.ShapeDtypeStruct((bsz, seq, LANES), F32), jax.ShapeDtypeStruct((bsz, seq, LANES), jnp.int32),
                   jax.ShapeDtypeStruct((1, LANES), F32),
                   jax.ShapeDtypeStruct((bsz * tiles, 1, LANES), F32),
                   jax.ShapeDtypeStruct((bsz * tiles, N_EXPERTS, ts), jnp.int32)],
        scratch_shapes=[pltpu.VMEM((1, LANES), F32)],
        compiler_params=pltpu.CompilerParams(dimension_semantics=("arbitrary", "arbitrary"),
                                             vmem_limit_bytes=VMEM_LIMIT),
        name="outproj",
    )(y_ssd, y_att, x, g1, sc2, sh2, *consts)


def _moe_body(be_ref, na_ref, x_ref, wgu_ref, bgu_ref, wd_ref, bd_ref, y_ref, wgu_b, wd_b):
    i = pl.program_id(0)

    @pl.when((i == 0) | (be_ref[i] != be_ref[jnp.maximum(i - 1, 0)]))
    def _():
        wgu_b[...] = wgu_ref[0].astype(BF16)
        wd_b[...] = wd_ref[0].astype(BF16)

    @pl.when(i < na_ref[0])
    def _():
        gu = _dot(x_ref[...], wgu_b[...]) + bgu_ref[0]
        glu = jnp.minimum(gu[:, :D_FF_EXPERT], SWIGLU_LIMIT)
        lin = jnp.clip(gu[:, D_FF_EXPERT:], -SWIGLU_LIMIT, SWIGLU_LIMIT)
        act = glu * jax.nn.sigmoid(SWIGLU_ALPHA * glu) * (lin + 1.0)
        y_ref[...] = (_dot(act.astype(BF16), wd_b[...]) + bd_ref[0]).astype(BF16)

    @pl.when(i >= na_ref[0])
    def _():
        y_ref[...] = jnp.zeros_like(y_ref)


def _moe(block_expert, n_active, xg, wgu, bgu, wd, bd):
    n_slots, d = xg.shape
    n_blocks = n_slots // MOE_TB
    f2 = wgu.shape[2]
    return pl.pallas_call(
        _moe_body,
        grid_spec=pltpu.PrefetchScalarGridSpec(
            num_scalar_prefetch=2,
            grid=(n_blocks,),
            in_specs=[pl.BlockSpec((MOE_TB, d), lambda i, be, na: (i, 0)),
                      pl.BlockSpec((1, d, f2), lambda i, be, na: (be[i], 0, 0)),
                      pl.BlockSpec((1, 1, f2), lambda i, be, na: (be[i], 0, 0)),
                      pl.BlockSpec((1, f2 // 2, d), lambda i, be, na: (be[i], 0, 0)),
                      pl.BlockSpec((1, 1, d), lambda i, be, na: (be[i], 0, 0))],
            out_specs=pl.BlockSpec((MOE_TB, d), lambda i, be, na: (i, 0)),
            scratch_shapes=[pltpu.VMEM((d, f2), BF16), pltpu.VMEM((f2 // 2, d), BF16)],
        ),
        out_shape=jax.ShapeDtypeStruct((n_slots, d), BF16),
        compiler_params=pltpu.CompilerParams(dimension_semantics=("arbitrary",),
                                             vmem_limit_bytes=VMEM_LIMIT),
        name="moe",
    )(block_expert, n_active, xg, wgu, bgu, wd, bd)


def _final_body(x1_ref, y_ref, gate_ref, g2_ref, gain_ref, *rest):
    o_ref = rest[-1]
    gates = gate_ref[0]
    f = gates[:, 0:1] * y_ref[0, 0].astype(F32)
    for kk in range(1, TOP_K):
        f = f + gates[:, kk:kk + 1] * y_ref[kk, 0].astype(F32)
    o_ref[0] = x1_ref[0] + g2_ref[0] * (_rms(f) * gain_ref[...])


def _final(x1, y4, gates, g2, gain, out_prev, batch_offset, total_batch):
    bsz, seq, d = x1.shape
    ts = min(TS_PROJ, seq)

    def tok(width):
        return pl.BlockSpec((1, ts, width), lambda b, i: (b, i, 0))

    in_specs = [tok(d), pl.BlockSpec((TOP_K, 1, ts, d), lambda b, i: (0, b, i, 0)), tok(LANES),
                pl.BlockSpec((1, 1, d), lambda b, i: (b, 0, 0)), _const_spec((1, d))]
    args = [x1, y4, gates, g2, gain]
    aliases = {}
    if out_prev is not None:
        in_specs.append(pl.BlockSpec(memory_space=pl.ANY))
        args.append(out_prev)
        aliases = {len(args) - 1: 0}
    return pl.pallas_call(
        _final_body,
        grid=(bsz, seq // ts),
        in_specs=in_specs,
        out_specs=pl.BlockSpec((1, ts, d), lambda b, i: (batch_offset + b, i, 0)),
        out_shape=jax.ShapeDtypeStruct((total_batch, seq, d), F32),
        input_output_aliases=aliases,
        compiler_params=pltpu.CompilerParams(dimension_semantics=("parallel", "parallel"),
                                             vmem_limit_bytes=VMEM_LIMIT),
        name="final",
    )(*args)


def _head_blocks(cols):
    out = []
    for c in cols:
        pad = LANES - c.shape[1]
        out.append(jnp.pad(c, ((0, 0), (0, pad))) if pad else c)
    return jnp.concatenate(out, axis=1)


def _prep_mixer_weights(w_in, w_q_up, w_kv_up):
    d = w_in.shape[0]
    wz = w_in[:, OFF_Z:OFF_XBC]
    wxbc = w_in[:, OFF_XBC:OFF_DT]
    wdt = w_in[:, OFF_DT:OFF_QA]
    wqa = w_in[:, OFF_QA:OFF_KVA]
    wkva = w_in[:, OFF_KVA:OFF_KR]
    wkr = w_in[:, OFF_KR:IN_COLS]
    kr_blk = jnp.concatenate([jnp.zeros((d, ROPE_LO), F32), wkr, jnp.zeros((d, LANES - ROPE_LO - MLA_ROPE), F32)], axis=1)
    dt_blk = jnp.pad(wdt, ((0, 0), (0, LANES - SSD_HEADS)))
    wsm = jnp.concatenate([kr_blk, dt_blk], axis=1)
    qh = MLA_NOPE + MLA_ROPE
    scale = math.log2(math.e) / math.sqrt(qh)
    wqup = _head_blocks([w_q_up[:, h * qh:(h + 1) * qh] for h in range(MLA_HEADS)]) * scale
    kvh = MLA_NOPE + MLA_V
    wkup = _head_blocks([w_kv_up[:, h * kvh:h * kvh + MLA_NOPE] for h in range(MLA_HEADS)])
    vcols = []
    for h in range(MLA_HEADS):
        vcols.append(w_kv_up[:, h * kvh + MLA_NOPE:(h + 1) * kvh])
        if h % 2 == 1:
            vcols.append(jnp.zeros((w_kv_up.shape[0], VT_ROWS - LANES), F32))
    wvup = jnp.concatenate(vcols, axis=1).T
    return tuple(w.astype(BF16) for w in (wz, wxbc, wsm, wqa, wkva)) + tuple(w.astype(BF16) for w in (wqup, wkup, wvup))


def _rope_tables(positions):
    inv_freq = ROPE_BASE ** (-(jnp.arange(HALF_ROPE, dtype=F32) * 2.0 / MLA_ROPE))
    angles = positions.astype(F32)[..., None] * inv_freq
    cos = jnp.cos(angles)
    sin = jnp.sin(angles)
    shp = angles.shape[:-1]
    ct = jnp.concatenate([jnp.ones(shp + (ROPE_LO,), F32), cos, cos,
                          jnp.zeros(shp + (LANES - ROPE_LO - MLA_ROPE,), F32)], axis=-1)
    st = jnp.concatenate([jnp.zeros(shp + (ROPE_LO,), F32), -sin, sin,
                          jnp.zeros(shp + (LANES - ROPE_LO - MLA_ROPE,), F32)], axis=-1)
    return ct, st


def _route(idx, rank, counts, tile_seen, ltab, n_tok):
    n_assign = n_tok * TOP_K
    n_tiles, _, tile = ltab.shape
    padded = ((counts + MOE_TB - 1) // MOE_TB) * MOE_TB
    padded_end = jnp.cumsum(padded)
    padded_start = padded_end - padded
    experts = jnp.arange(N_EXPERTS, dtype=jnp.int32)
    start_of = jnp.sum(jnp.where(idx[..., None] == experts, padded_start, 0), axis=-1)
    dest = (start_of + rank).reshape(-1)
    n_blocks = n_assign // MOE_TB + N_EXPERTS
    block_start = jnp.arange(n_blocks, dtype=jnp.int32) * MOE_TB
    block_expert = jnp.minimum(jnp.sum((padded_end[None, :] <= block_start[:, None]).astype(jnp.int32), axis=1),
                               N_EXPERTS - 1)
    n_active = (padded_end[-1] // MOE_TB).astype(jnp.int32).reshape(1)
    j = (block_start - padded_start[block_expert])[:, None] + jnp.arange(MOE_TB, dtype=jnp.int32)[None, :]
    seen_blk = tile_seen.T[block_expert]
    reached = seen_blk[:, None, :] <= j[:, :, None]
    tau = jnp.sum(reached.astype(jnp.int32), axis=-1) - 1
    j_local = j - jnp.max(jnp.where(reached, seen_blk[:, None, :], 0), axis=-1)
    valid = j < counts[block_expert][:, None]
    flat = (tau * N_EXPERTS + block_expert[:, None]) * tile + j_local
    tok_local = ltab.reshape(-1)[jnp.clip(flat, 0, n_tiles * N_EXPERTS * tile - 1)]
    slot_tok = jnp.where(valid, tau * tile + tok_local, 0).reshape(-1)
    return dest, slot_tok, block_expert, n_active


def kernel(x, c, positions, w_ada, b_ada, pre_mix_norm, w_in, conv_w, conv_b, dt_bias, a_log, d_skip, ssd_norm, q_a_norm, w_q_up, kv_a_norm, w_kv_up, mla_norm, w_out, post_mix_norm, pre_ffn_norm, w_router, b_router, w_gate_up, b_gate_up, w_down, b_down, post_ffn_norm):
    bsz, seq, d = x.shape
    ct, st = _rope_tables(positions)
    n_groups = BATCH_GROUPS if bsz % BATCH_GROUPS == 0 else 1
    gb = bsz // n_groups
    n_tok = gb * seq
    pad_h = LANES - SSD_HEADS
    for l in range(w_ada.shape[0]):
        mod = _adaln(c, w_ada[l], b_ada[l])
        mods = [m.reshape(bsz, 1, d) for m in jnp.split(mod, 6, axis=-1)]
        mixer_w = _prep_mixer_weights(w_in[l], w_q_up[l], w_kv_up[l])
        wo = w_out[l].astype(BF16)
        wr = jnp.pad(w_router[l], ((0, 0), (0, LANES - N_EXPERTS)))
        br = jnp.pad(b_router[l], (0, LANES - N_EXPERTS)).reshape(1, LANES)
        dtb = jnp.pad(dt_bias[l], (0, pad_h)).reshape(1, LANES)
        alog = jnp.pad(a_log[l], (0, pad_h)).reshape(1, LANES)
        dsk = jnp.repeat(d_skip[l], SSD_HEAD_DIM).reshape(1, -1)
        out = None
        prev_slot_tok = None
        for gi in range(n_groups):
            grp = slice(gi * gb, (gi + 1) * gb)
            sh1, sc1, g1, sh2, sc2, g2 = [m[grp] for m in mods]
            if prev_slot_tok is not None:
                sc1 = sc1 + jnp.where(prev_slot_tok[0] < 0, 1.0, 0.0)
            z, xbc, dt, q, k, v = _inproj(x, sc1, sh1, pre_mix_norm[l].reshape(1, d), ct, st, *mixer_w[:5],
                                          q_a_norm[l].reshape(1, -1), kv_a_norm[l].reshape(1, -1), *mixer_w[5:],
                                          batch_offset=gi * gb)
            y_ssd = _ssd(xbc, z, dt, conv_w[l], conv_b[l].reshape(1, -1), dtb, alog, dsk, ssd_norm[l].reshape(1, -1))
            y_att = _attn(q, k, v)
            x1, h2, gates, route, cnt, seen, ltab = _outproj(
                y_ssd, y_att, x, g1, sc2, sh2, mla_norm[l].reshape(1, -1), wo[:SSD_INNER], wo[SSD_INNER:],
                post_mix_norm[l].reshape(1, d), pre_ffn_norm[l].reshape(1, d), wr, br, batch_offset=gi * gb)
            route = route.reshape(n_tok, LANES)
            counts = cnt[0, :N_EXPERTS].astype(jnp.int32)
            tile_seen = seen[:, 0, :N_EXPERTS].astype(jnp.int32)
            dest, slot_tok, block_expert, n_active = _route(route[:, :TOP_K], route[:, TOP_K:2 * TOP_K], counts,
                                                            tile_seen, ltab, n_tok)
            prev_slot_tok = slot_tok
            xs = h2.reshape(n_tok, d)[slot_tok]
            y = _moe(block_expert, n_active, xs, w_gate_up[l], b_gate_up[l][:, None, :], w_down[l], b_down[l][:, None, :])
            y4 = y[dest.reshape(n_tok, TOP_K).T.reshape(-1)].reshape(TOP_K, gb, seq, d)
            out = _final(x1, y4, gates, g2, post_ffn_norm[l].reshape(1, d), out, gi * gb, bsz)
        x = out
    return x
```

```python
import functools
import math

import jax
import jax.numpy as jnp
from jax import lax
from jax.experimental import pallas as pl
from jax.experimental.pallas import tpu as pltpu

F32 = jnp.float32
BF16 = jnp.bfloat16

D_MODEL = 1024
CHUNK = 64
SSD_INNER = 512
SSD_HEAD_DIM = 64
SSD_HEADS = 8
SSD_GROUPS = 2
SSD_STATE = 128
SSD_CONV = 4
SSD_BC = SSD_GROUPS * SSD_STATE
XBC_DIM = SSD_INNER + 2 * SSD_BC
MLA_V = 64
MLA_HEADS = 8
MLA_NOPE = 64
MLA_ROPE = 32
Q_LORA = 384
KV_LORA = 256
ROPE_BASE = 10000.0
OFF_Z = 0
OFF_XBC = OFF_Z + SSD_INNER
OFF_DT = OFF_XBC + XBC_DIM
OFF_QA = OFF_DT + SSD_HEADS
OFF_KVA = OFF_QA + Q_LORA
OFF_KR = OFF_KVA + KV_LORA
IN_COLS = OFF_KR + MLA_ROPE
N_EXPERTS = 32
TOP_K = 4
D_FF_EXPERT = 1024
SWIGLU_LIMIT = 7.0
SWIGLU_ALPHA = 1.702
NORM_EPS = 1e-6

LANES = 128
HALF_ROPE = MLA_ROPE // 2
ROPE_LO = MLA_NOPE
ROPE_HI = MLA_NOPE + HALF_ROPE
VT_ROWS = 144

TS_PROJ = 512
SSD_L = 256
ATT_TQ = 512
ATT_TK = 256
MOE_TB = 512
BATCH_GROUPS = 2
VMEM_LIMIT = 56 * 1024 * 1024


def _dot(a, b):
    return jnp.dot(a, b, preferred_element_type=F32)


def _dot_nt(a, b):
    return lax.dot_general(a, b, (((1,), (1,)), ((), ())), preferred_element_type=F32)


def _dot_tn(a, b):
    return lax.dot_general(a, b, (((0,), (0,)), ((), ())), preferred_element_type=F32)


def _split2(x):
    hi = x.astype(BF16)
    lo = (x - hi.astype(F32)).astype(BF16)
    return hi, lo


def _split3(x):
    h1 = x.astype(BF16)
    r1 = x - h1.astype(F32)
    h2 = r1.astype(BF16)
    h3 = (r1 - h2.astype(F32)).astype(BF16)
    return h1, h2, h3


def _dot3(a, b):
    ah, al = _split2(a)
    bh, bl = _split2(b)
    return _dot(ah, bh) + _dot(ah, bl) + _dot(al, bh)


def _rms(x):
    return x * lax.rsqrt(jnp.mean(x * x, axis=-1, keepdims=True) + NORM_EPS)


def _silu(x):
    return x * jax.nn.sigmoid(x)


def _cost(flops, transcendentals, operands, out_shapes):
    nbytes = sum(math.prod(a.shape) * a.dtype.itemsize for a in operands)
    nbytes += sum(math.prod(o.shape) * jnp.dtype(o.dtype).itemsize for o in out_shapes)
    return pl.CostEstimate(flops=int(flops), transcendentals=int(transcendentals), bytes_accessed=int(nbytes))


def _const_spec(shape):
    nd = len(shape)
    return pl.BlockSpec(shape, lambda *_: (0,) * nd)


def _adaln_body(c_ref, w_ref, b_ref, o_ref):
    o_ref[...] = _dot3(_silu(c_ref[...]), w_ref[...]) + b_ref[...]


def _adaln(c, w_ada, b_ada):
    bsz, d = c.shape
    n = w_ada.shape[1]
    tn = 1024
    return pl.pallas_call(
        _adaln_body,
        grid=(n // tn,),
        in_specs=[_const_spec((bsz, d)),
                  pl.BlockSpec((d, tn), lambda j: (0, j)),
                  pl.BlockSpec((1, tn), lambda j: (0, j))],
        out_specs=pl.BlockSpec((bsz, tn), lambda j: (0, j)),
        out_shape=jax.ShapeDtypeStruct((bsz, n), F32),
        cost_estimate=_cost(6 * bsz * d * n, bsz * d, (c, w_ada, b_ada), (jax.ShapeDtypeStruct((bsz, n), F32),)),
        compiler_params=pltpu.CompilerParams(dimension_semantics=("arbitrary",),
                                             vmem_limit_bytes=VMEM_LIMIT),
        name="adaln",
    )(c, w_ada, b_ada.reshape(1, n))


def _rope_block(xb, ct, st, lane):
    partner = jnp.where(lane < ROPE_HI, pltpu.roll(xb, LANES - HALF_ROPE, 1), pltpu.roll(xb, HALF_ROPE, 1))
    return xb * ct + partner * st


def _inproj_body(x_ref, sc_ref, sh_ref, g_ref, ct_ref, st_ref, wz_ref, wxbc_ref, wsm_ref, wqa_ref, wkva_ref,
                 qn_ref, kvn_ref, wqup_ref, wkup_ref, wvup_ref,
                 z_ref, xbc_ref, dt_ref, q_ref, k_ref, v_ref):
    x = x_ref[0]
    h = _rms(x) * (g_ref[...] * (1.0 + sc_ref[0])) + sh_ref[0]
    hb = h.astype(BF16)
    z_ref[0] = _dot(hb, wz_ref[...]).astype(BF16)
    xbc_ref[0] = _dot(hb, wxbc_ref[...]).astype(BF16)
    sm = _dot(hb, wsm_ref[...])
    dt_ref[0] = sm[:, LANES:]
    ct = ct_ref[0]
    st = st_ref[0]
    lane = lax.broadcasted_iota(jnp.int32, ct.shape, 1)
    kr = _rope_block(sm[:, :LANES], ct, st, lane)
    qan = (_rms(_dot(hb, wqa_ref[...])) * qn_ref[...]).astype(BF16)
    q = _dot(qan, wqup_ref[...])
    for hh in range(MLA_HEADS):
        blk = slice(hh * LANES, (hh + 1) * LANES)
        q_ref[0, :, blk] = _rope_block(q[:, blk], ct, st, lane).astype(BF16)
    kvn = (_rms(_dot(hb, wkva_ref[...])) * kvn_ref[...]).astype(BF16)
    k = _dot(kvn, wkup_ref[...])
    for hh in range(MLA_HEADS):
        blk = slice(hh * LANES, (hh + 1) * LANES)
        k_ref[0, :, blk] = (k[:, blk] + kr).astype(BF16)
    vt = _dot_nt(wvup_ref[...], kvn)
    vrow = lax.broadcasted_iota(jnp.int32, vt.shape, 0)
    v_ref[0] = jnp.where(vrow % VT_ROWS == LANES, 1.0, vt).astype(BF16)


def _inproj(x, sc1, sh1, gain, ct, st, wz, wxbc, wsm, wqa, wkva, qn, kvn, wqup, wkup, wvup, *, batch_offset):
    _, seq, d = x.shape
    bsz = sc1.shape[0]
    ts = min(TS_PROJ, seq)
    hw = MLA_HEADS * LANES

    def tok(width):
        return pl.BlockSpec((1, ts, width), lambda b, i: (b, i, 0))

    def tok_full(width):
        return pl.BlockSpec((1, ts, width), lambda b, i: (batch_offset + b, i, 0))

    def per_batch(width):
        return pl.BlockSpec((1, 1, width), lambda b, i: (b, 0, 0))

    weights = (wz, wxbc, wsm, wqa, wkva, qn, kvn, wqup, wkup, wvup)
    out_widths = (SSD_INNER, XBC_DIM, LANES, hw, hw)
    out_dtypes = (BF16, BF16, F32, BF16, BF16)
    vdim = (MLA_HEADS // 2) * VT_ROWS
    out_shape = ([jax.ShapeDtypeStruct((bsz, seq, w), dt) for w, dt in zip(out_widths, out_dtypes)]
                 + [jax.ShapeDtypeStruct((bsz, vdim, seq), BF16)])
    n_tok = bsz * seq
    mm_cols = sum(w.shape[1] for w in (wz, wxbc, wsm, wqa, wkva))
    flops = 2 * n_tok * (d * mm_cols + wqup.shape[0] * wqup.shape[1] + wkup.shape[0] * wkup.shape[1]
                         + wvup.shape[0] * wvup.shape[1])
    read = [jax.ShapeDtypeStruct((bsz, seq, d + 2 * LANES), F32), *weights]
    return pl.pallas_call(
        _inproj_body,
        grid=(bsz, seq // ts),
        in_specs=[tok_full(d), per_batch(d), per_batch(d), _const_spec((1, d)), tok_full(LANES), tok_full(LANES)]
                 + [_const_spec(w.shape) for w in weights],
        out_specs=[tok(w) for w in out_widths] + [pl.BlockSpec((1, vdim, ts), lambda b, i: (b, 0, i))],
        out_shape=out_shape,
        cost_estimate=_cost(flops, 3 * n_tok, read, out_shape),
        compiler_params=pltpu.CompilerParams(dimension_semantics=("parallel", "parallel"),
                                             vmem_limit_bytes=VMEM_LIMIT),
        name="inproj",
    )(x, sc1, sh1, gain, ct, st, *weights)


CONV_HALO = 16


def _ssd_body(xc_ref, xp_ref, z_ref, dt_ref, cw_ref, cb_ref, dtb_ref, alog_ref, dsk_ref, ng_ref,
              y_ref, xs_scr, st_scr, *, blk):
    i = pl.program_id(1)

    @pl.when(i == 0)
    def _():
        st_scr[...] = jnp.zeros_like(st_scr)

    xs_scr[0:CONV_HALO, :] = jnp.where(i > 0, xp_ref[0].astype(F32), 0.0)
    xs_scr[CONV_HALO:CONV_HALO + blk, :] = xc_ref[0].astype(F32)
    conv = cb_ref[...]
    for kk in range(SSD_CONV):
        off = CONV_HALO - (SSD_CONV - 1) + kk
        conv = conv + cw_ref[kk:kk + 1, :] * xs_scr[off:off + blk, :]
    xa = _silu(conv)
    xs = xa[:, :SSD_INNER]
    bm = xa[:, SSD_INNER:SSD_INNER + SSD_BC]
    cm = xa[:, SSD_INNER + SSD_BC:]

    hl = lax.broadcasted_iota(jnp.int32, (1, LANES), 1)
    dtr = dt_ref[0] + dtb_ref[...]
    dt = jnp.maximum(dtr, 0.0) + jnp.log(1.0 + jnp.exp(-jnp.abs(dtr)))
    a = jnp.where(hl < SSD_HEADS, -jnp.exp(alog_ref[...]), 0.0)
    dta = dt * a
    row = lax.broadcasted_iota(jnp.int32, (blk, blk), 0)
    col = lax.broadcasted_iota(jnp.int32, (blk, blk), 1)
    tril = row >= col
    trilb = jnp.where(tril, 1.0, 0.0).astype(BF16)
    d1, d2, d3 = _split3(dta)
    cs = _dot(trilb, d1) + _dot(trilb, d2) + _dot(trilb, d3)
    cs_last = cs[blk - 1:blk, :]
    ecs = jnp.exp(cs)
    dte = jnp.exp(cs_last - cs)
    cs_t = cs.T

    er = lax.broadcasted_iota(jnp.int32, (LANES, SSD_INNER), 0)
    ec = lax.broadcasted_iota(jnp.int32, (LANES, SSD_INNER), 1)
    expand = jnp.where(ec // SSD_HEAD_DIM == er, 1.0, 0.0).astype(BF16)

    def per_channel(v):
        vh, vl = _split2(v)
        return _dot(vh, expand) + _dot(vl, expand)

    dt_e = per_channel(dt)
    ecs_e = per_channel(ecs)
    dte_e = per_channel(dte)
    xdt = xs * dt_e
    xdt_b = xdt.astype(BF16)
    xw_b = (xdt * dte_e).astype(BF16)

    gw = SSD_INNER // SSD_GROUPS
    heads_per_group = SSD_HEADS // SSD_GROUPS
    lane = lax.broadcasted_iota(jnp.int32, (blk, LANES), 1)
    y_groups = []
    for g in range(SSD_GROUPS):
        bg = bm[:, g * SSD_STATE:(g + 1) * SSD_STATE].astype(BF16)
        cg = cm[:, g * SSD_STATE:(g + 1) * SSD_STATE].astype(BF16)
        cb = _dot_nt(cg, bg)
        state = st_scr[g]
        y_off = _dot(cg, state.astype(BF16))
        pairs = []
        for j in range(heads_per_group // 2):
            xp = xdt_b[:, g * gw + j * LANES:g * gw + (j + 1) * LANES]
            halves = []
            for u in range(2):
                hidx = g * heads_per_group + 2 * j + u
                seg = cs[:, hidx:hidx + 1] - cs_t[hidx:hidx + 1, :]
                dec = jnp.exp(jnp.where(tril, seg, -jnp.inf))
                halves.append(_dot((cb * dec).astype(BF16), xp))
            pairs.append(jnp.where(lane < SSD_HEAD_DIM, halves[0], halves[1]))
        y_diag = jnp.concatenate(pairs, axis=1)
        y_groups.append(y_diag + y_off * ecs_e[:, g * gw:(g + 1) * gw])
        st_scr[g] = (state * ecs_e[blk - 1:blk, g * gw:(g + 1) * gw]
                     + _dot_tn(bg, xw_b[:, g * gw:(g + 1) * gw]))
    y = jnp.concatenate(y_groups, axis=1) + xs * dsk_ref[...]
    y = y * _silu(z_ref[0].astype(F32))
    y = jnp.concatenate([_rms(y[:, g * gw:(g + 1) * gw]) for g in range(SSD_GROUPS)], axis=1)
    y_ref[0] = (y * ng_ref[...]).astype(BF16)


def _ssd(xbc, z, dt, conv_w, conv_b, dt_bias, a_log, d_skip_e, norm_gain):
    bsz, seq, _ = xbc.shape
    blk = min(SSD_L, seq)
    halo_per_blk = blk // CONV_HALO
    body = functools.partial(_ssd_body, blk=blk)
    out_shape = jax.ShapeDtypeStruct((bsz, seq, SSD_INNER), BF16)
    n_tok = bsz * seq
    flops = 2 * n_tok * (3 * blk * LANES + 6 * LANES * SSD_INNER + SSD_GROUPS * blk * SSD_STATE
                         + SSD_HEADS * blk * LANES + 2 * SSD_STATE * SSD_INNER) + 20 * n_tok * XBC_DIM
    trans = n_tok * (XBC_DIM + SSD_INNER + SSD_HEADS * blk + 4 * LANES)
    return pl.pallas_call(
        body,
        grid=(bsz, seq // blk),
        cost_estimate=_cost(flops, trans, (xbc, z, dt), (out_shape,)),
        in_specs=[pl.BlockSpec((1, blk, XBC_DIM), lambda b, i: (b, i, 0)),
                  pl.BlockSpec((1, CONV_HALO, XBC_DIM), lambda b, i: (b, jnp.maximum(i * halo_per_blk - 1, 0), 0)),
                  pl.BlockSpec((1, blk, SSD_INNER), lambda b, i: (b, i, 0)),
                  pl.BlockSpec((1, blk, LANES), lambda b, i: (b, i, 0)),
                  _const_spec((SSD_CONV, XBC_DIM)), _const_spec((1, XBC_DIM)),
                  _const_spec((1, LANES)), _const_spec((1, LANES)),
                  _const_spec((1, SSD_INNER)), _const_spec((1, SSD_INNER))],
        out_specs=pl.BlockSpec((1, blk, SSD_INNER), lambda b, i: (b, i, 0)),
        out_shape=out_shape,
        scratch_shapes=[pltpu.VMEM((CONV_HALO + blk, XBC_DIM), F32),
                        pltpu.VMEM((SSD_GROUPS, SSD_STATE, SSD_INNER // SSD_GROUPS), F32)],
        compiler_params=pltpu.CompilerParams(dimension_semantics=("parallel", "arbitrary"),
                                             vmem_limit_bytes=VMEM_LIMIT),
        name="ssd",
    )(xbc, xbc, z, dt, conv_w, conv_b, dt_bias, a_log, d_skip_e, norm_gain)


def _attn_body(q_ref, k_ref, v_ref, o_ref, *scratch, tq, tk):
    n_streams = 2 * (tq // tk)
    s_scr = (scratch[:n_streams], scratch[n_streams:2 * n_streams])
    acc_scr = scratch[2 * n_streams:]
    qi = pl.program_id(2)
    n_sub = tq // tk
    n_full = qi * n_sub
    krow = lax.broadcasted_iota(jnp.int32, (tk, tk), 0)
    qcol = lax.broadcasted_iota(jnp.int32, (tk, tk), 1)
    diag_ok = krow // CHUNK <= qcol // CHUNK
    vrow = lax.broadcasted_iota(jnp.int32, (LANES, tk), 0)
    streams = [(u, r) for u in range(2) for r in range(n_sub)]
    qs = [q_ref[0, r * tk:(r + 1) * tk, u * LANES:(u + 1) * LANES] for u, r in streams]

    def put_scores(ki, which, slot):
        start = pl.multiple_of(ki * tk, tk)
        k2 = k_ref[0, pl.ds(start, tk), :]
        out = {}
        for si in which:
            u = streams[si][0]
            s = _dot_nt(k2[:, u * LANES:(u + 1) * LANES], qs[si])
            s_scr[slot][si][...] = s
            out[si] = jnp.max(s, axis=0, keepdims=True)
        return out

    def values_t(ki):
        return v_ref[0, :, pl.ds(pl.multiple_of(ki * tk, tk), tk)]

    def softmax_pv(si, m, s_max, slot, vt, masked):
        s = s_scr[slot][si][...]
        if masked:
            s = jnp.where(diag_ok, s, -jnp.inf)
            s_max = jnp.max(s, axis=0, keepdims=True)
        m_new = jnp.maximum(m, s_max)
        alpha = jnp.exp2(m - m_new)
        p = jnp.exp2((s - m_new).astype(BF16))
        acc_scr[si][...] = alpha * acc_scr[si][...] + _dot(vt, p)
        return m_new

    every = list(range(len(streams)))
    for ref in acc_scr:
        ref[...] = jnp.zeros_like(ref)

    def step(j, state):
        ms, s_maxes = state
        for slot in range(2):
            ki = 2 * j + slot
            nxt = put_scores(ki + 1, every, 1 - slot)
            vt = values_t(ki)
            ms = tuple(softmax_pv(si, ms[si], s_maxes[si], slot, vt, False) for si in every)
            s_maxes = tuple(nxt[si] for si in every)
        return ms, s_maxes

    first = put_scores(0, every, 0)
    m_init = jnp.full((1, tk), -jnp.inf, F32)
    ms, s_maxes = lax.fori_loop(0, n_full // 2, step, ((m_init,) * len(streams), tuple(first[si] for si in every)))
    ms = list(ms)
    s_maxes = dict(zip(every, s_maxes))
    for dd in range(n_sub):
        slot = dd % 2
        live = [si for si in every if dd <= streams[si][1]]
        later = [si for si in every if dd + 1 <= streams[si][1]]
        nxt = put_scores(n_full + dd + 1, later, 1 - slot) if later else {}
        vt = values_t(n_full + dd)
        for si in live:
            ms[si] = softmax_pv(si, ms[si], s_maxes[si], slot, vt, dd == streams[si][1])
        s_maxes = nxt
    for r in range(n_sub):
        a0 = acc_scr[streams.index((0, r))][...]
        a1 = acc_scr[streams.index((1, r))][...]
        out_t = jnp.where(vrow < MLA_V, a0[:LANES] / a0[LANES:LANES + 1], a1[:LANES] / a1[LANES:LANES + 1])
        o_ref[0, r * tk:(r + 1) * tk, :] = out_t.T.astype(BF16)


def _attn(q, k, v):
    bsz, seq, _ = q.shape
    tk = min(ATT_TK, seq)
    tq = min(ATT_TQ, seq)
    assert (tq // tk) % 2 == 0, "the two-slot score pipeline needs an even number of query sub-tiles"
    n_streams = 2 * (tq // tk)
    body = functools.partial(_attn_body, tq=tq, tk=tk)
    out_shape = jax.ShapeDtypeStruct((bsz, seq, MLA_HEADS * MLA_V), BF16)
    tile_pairs = bsz * MLA_HEADS * (seq // tk) * (seq // tk + 1) // 2
    flops = 2 * tile_pairs * tk * tk * (LANES + VT_ROWS)
    return pl.pallas_call(
        body,
        grid=(bsz, MLA_HEADS // 2, seq // tq),
        cost_estimate=_cost(flops, tile_pairs * tk * tk, (q, k, v), (out_shape,)),
        in_specs=[pl.BlockSpec((1, tq, 2 * LANES), lambda b, hp, i: (b, i, hp)),
                  pl.BlockSpec((1, seq, 2 * LANES), lambda b, hp, i: (b, 0, hp)),
                  pl.BlockSpec((1, VT_ROWS, seq), lambda b, hp, i: (b, hp, 0))],
        out_specs=pl.BlockSpec((1, tq, LANES), lambda b, hp, i: (b, i, hp)),
        out_shape=out_shape,
        scratch_shapes=[pltpu.VMEM((tk, tk), F32)] * (2 * n_streams) + [pltpu.VMEM((VT_ROWS, tk), F32)] * n_streams,
        compiler_params=pltpu.CompilerParams(dimension_semantics=("parallel", "parallel", "arbitrary"),
                                             vmem_limit_bytes=VMEM_LIMIT),
        name="attn",
    )(q, k, v)


def _outproj_body(ys_ref, ya_ref, x_ref, g1_ref, sc2_ref, sh2_ref, mn_ref, wo1_ref, wo2_ref, pmn_ref, pfn_ref,
                  wr_ref, br_ref, x1_ref, h2_ref, gate_ref, idx_ref, cnt_ref, pref_ref, ltab_ref, cnt_scr):
    first = (pl.program_id(0) == 0) & (pl.program_id(1) == 0)

    @pl.when(first)
    def _():
        cnt_scr[...] = jnp.zeros_like(cnt_scr)

    yan =(_rms(ya_ref[0].astype(F32)) * mn_ref[...]).astype(BF16)
    mix = _dot(ys_ref[0], wo1_ref[...]) + _dot(yan, wo2_ref[...])
    x1 = x_ref[0] + g1_ref[0] * (_rms(mix) * pmn_ref[...])
    x1_ref[0] = x1
    h2 = _rms(x1) * (pfn_ref[...] * (1.0 + sc2_ref[0])) + sh2_ref[0]
    h2_ref[0] = h2.astype(BF16)
    logits = _dot3(h2, wr_ref[...]) + br_ref[...]
    lane = lax.broadcasted_iota(jnp.int32, logits.shape, 1)
    cur = jnp.where(lane < N_EXPERTS, logits, -jnp.inf)
    vals, idxs = [], []
    for _ in range(TOP_K):
        m = jnp.max(cur, axis=-1, keepdims=True)
        ix = jnp.min(jnp.where(cur == m, lane, LANES), axis=-1, keepdims=True)
        vals.append(m)
        idxs.append(ix)
        cur = jnp.where(lane == ix, -jnp.inf, cur)
    es = [jnp.exp(v - vals[0]) for v in vals]
    denom = es[0]
    for e in es[1:]:
        denom = denom + e
    onehot = jnp.zeros(logits.shape, F32)
    for kk in range(TOP_K):
        onehot = onehot + jnp.where(lane == idxs[kk], 1.0, 0.0)
    ts = logits.shape[0]
    row = lax.broadcasted_iota(jnp.int32, (ts, ts), 0)
    col = lax.broadcasted_iota(jnp.int32, (ts, ts), 1)
    before = jnp.where(row > col, 1.0, 0.0).astype(BF16)
    prior = _dot(before, onehot.astype(BF16))
    seen = cnt_scr[...]
    pref_ref[0] = seen
    tok = lax.broadcasted_iota(jnp.int32, logits.shape, 0)
    tok_hi = (tok // 16).astype(F32)
    tok_lo = (tok % 16).astype(F32)
    ltab = jnp.zeros((2 * LANES, ts), F32)
    gate_out = jnp.zeros(logits.shape, F32)
    idx_out = jnp.zeros(logits.shape, jnp.int32)
    for kk in range(TOP_K):
        mine = lane == idxs[kk]
        local = jnp.sum(jnp.where(mine, prior, 0.0), axis=-1, keepdims=True)
        rank = local + jnp.sum(jnp.where(mine, seen, 0.0), axis=-1, keepdims=True)
        at_rank = jnp.where(col == local.astype(jnp.int32), 1.0, 0.0).astype(BF16)
        tagged = jnp.concatenate([jnp.where(mine, tok_hi, 0.0), jnp.where(mine, tok_lo, 0.0)], axis=1)
        ltab = ltab + _dot_tn(tagged.astype(BF16), at_rank)
        gate_out = jnp.where(lane == kk, es[kk] / denom, gate_out)
        idx_out = jnp.where(lane == kk, idxs[kk], idx_out)
        idx_out = jnp.where(lane == TOP_K + kk, rank.astype(jnp.int32), idx_out)
    gate_ref[0] = gate_out
    idx_ref[0] = idx_out
    ltab_ref[0] = (16.0 * ltab[:N_EXPERTS] + ltab[LANES:LANES + N_EXPERTS]).astype(jnp.int32)
    cnt_scr[...] = seen + jnp.sum(onehot, axis=0, keepdims=True)
    cnt_ref[...] = cnt_scr[...]


def _outproj(y_ssd, y_att, x, g1, sc2, sh2, mla_norm, wo1, wo2, post_mix_norm, pre_ffn_norm, wr, br, *, batch_offset):
    bsz, seq, _ = y_ssd.shape
    d = x.shape[-1]
    ts = min(TS_PROJ, seq)
    tiles = seq // ts

    def tok(width):
        return pl.BlockSpec((1, ts, width), lambda b, i: (b, i, 0))

    x_spec = pl.BlockSpec((1, ts, d), lambda b, i: (batch_offset + b, i, 0))

    def per_batch(width):
        return pl.BlockSpec((1, 1, width), lambda b, i: (b, 0, 0))

    consts = (mla_norm, wo1, wo2, post_mix_norm, pre_ffn_norm, wr, br)
    out_shape = [jax.ShapeDtypeStruct((bsz, seq, d), F32), jax.ShapeDtypeStruct((bsz, seq, d), BF16),
                 jax.ShapeDtypeStruct((bsz, seq, LANES), F32), jax.ShapeDtypeStruct((bsz, seq, LANES), jnp.int32),
                 jax.ShapeDtypeStruct((1, LANES), F32),
                 jax.ShapeDtypeStruct((bsz * tiles, 1, LANES), F32),
                 jax.ShapeDtypeStruct((bsz * tiles, N_EXPERTS, ts), jnp.int32)]
    n_tok = bsz * seq
    flops = 2 * n_tok * (d * d + 3 * d * LANES + ts * LANES + TOP_K * 2 * LANES * ts)
    read = [y_ssd, y_att, jax.ShapeDtypeStruct((bsz, seq, d), F32), *consts]
    return pl.pallas_call(
        _outproj_body,
        grid=(bsz, seq // ts),
        in_specs=[tok(SSD_INNER), tok(MLA_HEADS * MLA_V), x_spec, per_batch(d), per_batch(d), per_batch(d)]
                 + [_const_spec(w.shape) for w in consts],
        out_specs=[tok(d), tok(d), tok(LANES), tok(LANES), _const_spec((1, LANES)),
                   pl.BlockSpec((1, 1, LANES), lambda b, i: (b * tiles + i, 0, 0)),
                   pl.BlockSpec((1, N_EXPERTS, ts), lambda b, i: (b * tiles + i, 0, 0))],
        out_shape=out_shape,
        cost_estimate=_cost(flops, 8 * n_tok, read, out_shape),
        scratch_shapes=[pltpu.VMEM((1, LANES), F32)],
        compiler_params=pltpu.CompilerParams(dimension_semantics=("arbitrary", "arbitrary"),
                                             vmem_limit_bytes=VMEM_LIMIT),
        name="outproj",
    )(y_ssd, y_att, x, g1, sc2, sh2, *consts)


def _moe_body(be_ref, na_ref, x_ref, wgu_ref, bgu_ref, wd_ref, bd_ref, y_ref, wgu_b, wd_b):
    i = pl.program_id(0)

    @pl.when((i == 0) | (be_ref[i] != be_ref[jnp.maximum(i - 1, 0)]))
    def _():
        wgu_b[...] = wgu_ref[0].astype(BF16)
        wd_b[...] = wd_ref[0].astype(BF16)

    @pl.when(i < na_ref[0])
    def _():
        gu = _dot(x_ref[...], wgu_b[...]) + bgu_ref[0]
        glu = jnp.minimum(gu[:, :D_FF_EXPERT], SWIGLU_LIMIT)
        lin = jnp.clip(gu[:, D_FF_EXPERT:], -SWIGLU_LIMIT, SWIGLU_LIMIT)
        act = glu * jax.nn.sigmoid(SWIGLU_ALPHA * glu) * (lin + 1.0)
        y_ref[...] = (_dot(act.astype(BF16), wd_b[...]) + bd_ref[0]).astype(BF16)

    @pl.when(i >= na_ref[0])
    def _():
        y_ref[...] = jnp.zeros_like(y_ref)


def _moe(block_expert, n_active, xg, wgu, bgu, wd, bd):
    n_slots, d = xg.shape
    n_blocks = n_slots // MOE_TB
    f2 = wgu.shape[2]
    out_shape = jax.ShapeDtypeStruct((n_slots, d), BF16)
    flops = 2 * n_slots * (d * f2 + (f2 // 2) * d)
    return pl.pallas_call(
        _moe_body,
        cost_estimate=_cost(flops, n_slots * (f2 // 2), (xg, wgu, bgu, wd, bd), (out_shape,)),
        grid_spec=pltpu.PrefetchScalarGridSpec(
            num_scalar_prefetch=2,
            grid=(n_blocks,),
            in_specs=[pl.BlockSpec((MOE_TB, d), lambda i, be, na: (i, 0)),
                      pl.BlockSpec((1, d, f2), lambda i, be, na: (be[i], 0, 0)),
                      pl.BlockSpec((1, 1, f2), lambda i, be, na: (be[i], 0, 0)),
                      pl.BlockSpec((1, f2 // 2, d), lambda i, be, na: (be[i], 0, 0)),
                      pl.BlockSpec((1, 1, d), lambda i, be, na: (be[i], 0, 0))],
            out_specs=pl.BlockSpec((MOE_TB, d), lambda i, be, na: (i, 0)),
            scratch_shapes=[pltpu.VMEM((d, f2), BF16), pltpu.VMEM((f2 // 2, d), BF16)],
        ),
        out_shape=out_shape,
        compiler_params=pltpu.CompilerParams(dimension_semantics=("arbitrary",),
                                             vmem_limit_bytes=VMEM_LIMIT),
        name="moe",
    )(block_expert, n_active, xg, wgu, bgu, wd, bd)


def _final_body(x1_ref, y_ref, gate_ref, g2_ref, gain_ref, *rest):
    o_ref = rest[-1]
    gates = gate_ref[0]
    f = gates[:, 0:1] * y_ref[0, 0].astype(F32)
    for kk in range(1, TOP_K):
        f = f + gates[:, kk:kk + 1] * y_ref[kk, 0].astype(F32)
    o_ref[0] = x1_ref[0] + g2_ref[0] * (_rms(f) * gain_ref[...])


def _final(x1, y4, gates, g2, gain, out_prev, batch_offset, total_batch):
    bsz, seq, d = x1.shape
    ts = min(TS_PROJ, seq)

    def tok(width):
        return pl.BlockSpec((1, ts, width), lambda b, i: (b, i, 0))

    in_specs = [tok(d), pl.BlockSpec((TOP_K, 1, ts, d), lambda b, i: (0, b, i, 0)), tok(LANES),
                pl.BlockSpec((1, 1, d), lambda b, i: (b, 0, 0)), _const_spec((1, d))]
    args = [x1, y4, gates, g2, gain]
    aliases = {}
    if out_prev is not None:
        in_specs.append(pl.BlockSpec(memory_space=pl.ANY))
        args.append(out_prev)
        aliases = {len(args) - 1: 0}
    return pl.pallas_call(
        _final_body,
        grid=(bsz, seq // ts),
        in_specs=in_specs,
        out_specs=pl.BlockSpec((1, ts, d), lambda b, i: (batch_offset + b, i, 0)),
        out_shape=jax.ShapeDtypeStruct((total_batch, seq, d), F32),
        input_output_aliases=aliases,
        cost_estimate=_cost(12 * bsz * seq * d, bsz * seq, (x1, y4, gates), (x1,)),
        compiler_params=pltpu.CompilerParams(dimension_semantics=("parallel", "parallel"),
                                             vmem_limit_bytes=VMEM_LIMIT),
        name="final",
    )(*args)


def _head_blocks(cols):
    out = []
    for c in cols:
        pad = LANES - c.shape[1]
        out.append(jnp.pad(c, ((0, 0), (0, pad))) if pad else c)
    return jnp.concatenate(out, axis=1)


def _prep_mixer_weights(w_in, w_q_up, w_kv_up):
    d = w_in.shape[0]
    wz = w_in[:, OFF_Z:OFF_XBC]
    wxbc = w_in[:, OFF_XBC:OFF_DT]
    wdt = w_in[:, OFF_DT:OFF_QA]
    wqa = w_in[:, OFF_QA:OFF_KVA]
    wkva = w_in[:, OFF_KVA:OFF_KR]
    wkr = w_in[:, OFF_KR:IN_COLS]
    kr_blk = jnp.concatenate([jnp.zeros((d, ROPE_LO), F32), wkr, jnp.zeros((d, LANES - ROPE_LO - MLA_ROPE), F32)], axis=1)
    dt_blk = jnp.pad(wdt, ((0, 0), (0, LANES - SSD_HEADS)))
    wsm = jnp.concatenate([kr_blk, dt_blk], axis=1)
    qh = MLA_NOPE + MLA_ROPE
    scale = math.log2(math.e) / math.sqrt(qh)
    wqup = _head_blocks([w_q_up[:, h * qh:(h + 1) * qh] for h in range(MLA_HEADS)]) * scale
    kvh = MLA_NOPE + MLA_V
    wkup = _head_blocks([w_kv_up[:, h * kvh:h * kvh + MLA_NOPE] for h in range(MLA_HEADS)])
    vcols = []
    for h in range(MLA_HEADS):
        vcols.append(w_kv_up[:, h * kvh + MLA_NOPE:(h + 1) * kvh])
        if h % 2 == 1:
            vcols.append(jnp.zeros((w_kv_up.shape[0], VT_ROWS - LANES), F32))
    wvup = jnp.concatenate(vcols, axis=1).T
    return tuple(w.astype(BF16) for w in (wz, wxbc, wsm, wqa, wkva)) + tuple(w.astype(BF16) for w in (wqup, wkup, wvup))


def _rope_tables(positions):
    inv_freq = ROPE_BASE ** (-(jnp.arange(HALF_ROPE, dtype=F32) * 2.0 / MLA_ROPE))
    angles = positions.astype(F32)[..., None] * inv_freq
    cos = jnp.cos(angles)
    sin = jnp.sin(angles)
    shp = angles.shape[:-1]
    ct = jnp.concatenate([jnp.ones(shp + (ROPE_LO,), F32), cos, cos,
                          jnp.zeros(shp + (LANES - ROPE_LO - MLA_ROPE,), F32)], axis=-1)
    st = jnp.concatenate([jnp.zeros(shp + (ROPE_LO,), F32), -sin, sin,
                          jnp.zeros(shp + (LANES - ROPE_LO - MLA_ROPE,), F32)], axis=-1)
    return ct, st


def _route(idx, rank, counts, tile_seen, ltab, n_tok):
    n_assign = n_tok * TOP_K
    n_tiles, _, tile = ltab.shape
    padded = ((counts + MOE_TB - 1) // MOE_TB) * MOE_TB
    padded_end = jnp.cumsum(padded)
    padded_start = padded_end - padded
    experts = jnp.arange(N_EXPERTS, dtype=jnp.int32)
    start_of = jnp.sum(jnp.where(idx[..., None] == experts, padded_start, 0), axis=-1)
    dest = (start_of + rank).reshape(-1)
    n_blocks = n_assign // MOE_TB + N_EXPERTS
    block_start = jnp.arange(n_blocks, dtype=jnp.int32) * MOE_TB
    block_expert = jnp.minimum(jnp.sum((padded_end[None, :] <= block_start[:, None]).astype(jnp.int32), axis=1),
                               N_EXPERTS - 1)
    n_active = (padded_end[-1] // MOE_TB).astype(jnp.int32).reshape(1)
    j = (block_start - padded_start[block_expert])[:, None] + jnp.arange(MOE_TB, dtype=jnp.int32)[None, :]
    seen_blk = tile_seen.T[block_expert]
    reached = seen_blk[:, None, :] <= j[:, :, None]
    tau = jnp.sum(reached.astype(jnp.int32), axis=-1) - 1
    j_local = j - jnp.max(jnp.where(reached, seen_blk[:, None, :], 0), axis=-1)
    valid = j < counts[block_expert][:, None]
    flat = (tau * N_EXPERTS + block_expert[:, None]) * tile + j_local
    tok_local = ltab.reshape(-1)[jnp.clip(flat, 0, n_tiles * N_EXPERTS * tile - 1)]
    slot_tok = jnp.where(valid, tau * tile + tok_local, 0).reshape(-1)
    return dest, slot_tok, block_expert, n_active


def kernel(x, c, positions, w_ada, b_ada, pre_mix_norm, w_in, conv_w, conv_b, dt_bias, a_log, d_skip, ssd_norm, q_a_norm, w_q_up, kv_a_norm, w_kv_up, mla_norm, w_out, post_mix_norm, pre_ffn_norm, w_router, b_router, w_gate_up, b_gate_up, w_down, b_down, post_ffn_norm):
    bsz, seq, d = x.shape
    ct, st = _rope_tables(positions)
    n_groups = BATCH_GROUPS if bsz % BATCH_GROUPS == 0 else 1
    gb = bsz // n_groups
    n_tok = gb * seq
    pad_h = LANES - SSD_HEADS
    for l in range(w_ada.shape[0]):
        mod = _adaln(c, w_ada[l], b_ada[l])
        mods = [m.reshape(bsz, 1, d) for m in jnp.split(mod, 6, axis=-1)]
        mixer_w = _prep_mixer_weights(w_in[l], w_q_up[l], w_kv_up[l])
        wo = w_out[l].astype(BF16)
        wr = jnp.pad(w_router[l], ((0, 0), (0, LANES - N_EXPERTS)))
        br = jnp.pad(b_router[l], (0, LANES - N_EXPERTS)).reshape(1, LANES)
        dtb = jnp.pad(dt_bias[l], (0, pad_h)).reshape(1, LANES)
        alog = jnp.pad(a_log[l], (0, pad_h)).reshape(1, LANES)
        dsk = jnp.repeat(d_skip[l], SSD_HEAD_DIM).reshape(1, -1)
        out = None
        prev_slot_tok = None
        for gi in range(n_groups):
            grp = slice(gi * gb, (gi + 1) * gb)
            sh1, sc1, g1, sh2, sc2, g2 = [m[grp] for m in mods]
            if prev_slot_tok is not None:
                sc1 = sc1 + jnp.where(prev_slot_tok[0] < 0, 1.0, 0.0)
            z, xbc, dt, q, k, v = _inproj(x, sc1, sh1, pre_mix_norm[l].reshape(1, d), ct, st, *mixer_w[:5],
                                          q_a_norm[l].reshape(1, -1), kv_a_norm[l].reshape(1, -1), *mixer_w[5:],
                                          batch_offset=gi * gb)
            y_ssd = _ssd(xbc, z, dt, conv_w[l], conv_b[l].reshape(1, -1), dtb, alog, dsk, ssd_norm[l].reshape(1, -1))
            y_att = _attn(q, k, v)
            x1, h2, gates, route, cnt, seen, ltab = _outproj(
                y_ssd, y_att, x, g1, sc2, sh2, mla_norm[l].reshape(1, -1), wo[:SSD_INNER], wo[SSD_INNER:],
                post_mix_norm[l].reshape(1, d), pre_ffn_norm[l].reshape(1, d), wr, br, batch_offset=gi * gb)
            route = route.reshape(n_tok, LANES)
            counts = cnt[0, :N_EXPERTS].astype(jnp.int32)
            tile_seen = seen[:, 0, :N_EXPERTS].astype(jnp.int32)
            dest, slot_tok, block_expert, n_active = _route(route[:, :TOP_K], route[:, TOP_K:2 * TOP_K], counts,
                                                            tile_seen, ltab, n_tok)
            prev_slot_tok = slot_tok
            xs = h2.reshape(n_tok, d)[slot_tok]
            y = _moe(block_expert, n_active, xs, w_gate_up[l], b_gate_up[l][:, None, :], w_down[l], b_down[l][:, None, :])
            y4 = y[dest.reshape(n_tok, TOP_K).T.reshape(-1)].reshape(TOP_K, gb, seq, d)
            out = _final(x1, y4, gates, g2, post_ffn_norm[l].reshape(1, d), out, gi * gb, bsz)
        x = out
    return x
```

```python
import functools
import math

import jax
import jax.numpy as jnp
from jax import lax
from jax.experimental import pallas as pl
from jax.experimental.pallas import tpu as pltpu

F32 = jnp.float32
BF16 = jnp.bfloat16

D_MODEL = 1024
CHUNK = 64
SSD_INNER = 512
SSD_HEAD_DIM = 64
SSD_HEADS = 8
SSD_GROUPS = 2
SSD_STATE = 128
SSD_CONV = 4
SSD_BC = SSD_GROUPS * SSD_STATE
XBC_DIM = SSD_INNER + 2 * SSD_BC
MLA_V = 64
MLA_HEADS = 8
MLA_NOPE = 64
MLA_ROPE = 32
Q_LORA = 384
KV_LORA = 256
ROPE_BASE = 10000.0
OFF_Z = 0
OFF_XBC = OFF_Z + SSD_INNER
OFF_DT = OFF_XBC + XBC_DIM
OFF_QA = OFF_DT + SSD_HEADS
OFF_KVA = OFF_QA + Q_LORA
OFF_KR = OFF_KVA + KV_LORA
IN_COLS = OFF_KR + MLA_ROPE
N_EXPERTS = 32
TOP_K = 4
D_FF_EXPERT = 1024
SWIGLU_LIMIT = 7.0
SWIGLU_ALPHA = 1.702
NORM_EPS = 1e-6

LANES = 128
HALF_ROPE = MLA_ROPE // 2
ROPE_LO = MLA_NOPE
ROPE_HI = MLA_NOPE + HALF_ROPE
VT_ROWS = 144

TS_PROJ = 512
SSD_L = 256
ATT_TQ = 512
ATT_TK = 256
MOE_TB = 512
BATCH_GROUPS = 2
VMEM_LIMIT = 56 * 1024 * 1024


def _dot(a, b):
    return jnp.dot(a, b, preferred_element_type=F32)


def _dot_nt(a, b):
    return lax.dot_general(a, b, (((1,), (1,)), ((), ())), preferred_element_type=F32)


def _dot_tn(a, b):
    return lax.dot_general(a, b, (((0,), (0,)), ((), ())), preferred_element_type=F32)


def _split2(x):
    hi = x.astype(BF16)
    lo = (x - hi.astype(F32)).astype(BF16)
    return hi, lo


def _split3(x):
    h1 = x.astype(BF16)
    r1 = x - h1.astype(F32)
    h2 = r1.astype(BF16)
    h3 = (r1 - h2.astype(F32)).astype(BF16)
    return h1, h2, h3


def _dot3(a, b):
    ah, al = _split2(a)
    bh, bl = _split2(b)
    return _dot(ah, bh) + _dot(ah, bl) + _dot(al, bh)


def _rms(x):
    return x * lax.rsqrt(jnp.mean(x * x, axis=-1, keepdims=True) + NORM_EPS)


def _silu(x):
    return x * jax.nn.sigmoid(x)


def _cost(flops, transcendentals, operands, out_shapes):
    nbytes = sum(math.prod(a.shape) * a.dtype.itemsize for a in operands)
    nbytes += sum(math.prod(o.shape) * jnp.dtype(o.dtype).itemsize for o in out_shapes)
    return pl.CostEstimate(flops=int(flops), transcendentals=int(transcendentals), bytes_accessed=int(nbytes))


def _const_spec(shape):
    nd = len(shape)
    return pl.BlockSpec(shape, lambda *_: (0,) * nd)


def _adaln_body(c_ref, w_ref, b_ref, o_ref):
    o_ref[...] = _dot3(_silu(c_ref[...]), w_ref[...]) + b_ref[...]


def _adaln(c, w_ada, b_ada):
    bsz, d = c.shape
    n = w_ada.shape[1]
    tn = 1024
    return pl.pallas_call(
        _adaln_body,
        grid=(n // tn,),
        in_specs=[_const_spec((bsz, d)),
                  pl.BlockSpec((d, tn), lambda j: (0, j)),
                  pl.BlockSpec((1, tn), lambda j: (0, j))],
        out_specs=pl.BlockSpec((bsz, tn), lambda j: (0, j)),
        out_shape=jax.ShapeDtypeStruct((bsz, n), F32),
        cost_estimate=_cost(6 * bsz * d * n, bsz * d, (c, w_ada, b_ada), (jax.ShapeDtypeStruct((bsz, n), F32),)),
        compiler_params=pltpu.CompilerParams(dimension_semantics=("arbitrary",),
                                             vmem_limit_bytes=VMEM_LIMIT),
        name="adaln",
    )(c, w_ada, b_ada.reshape(1, n))


def _rope_block(xb, ct, st, lane):
    partner = jnp.where(lane < ROPE_HI, pltpu.roll(xb, LANES - HALF_ROPE, 1), pltpu.roll(xb, HALF_ROPE, 1))
    return xb * ct + partner * st


def _inproj_body(x_ref, sc_ref, sh_ref, g_ref, ct_ref, st_ref, wz_ref, wxbc_ref, wsm_ref, wqa_ref, wkva_ref,
                 qn_ref, kvn_ref, wqup_ref, wkup_ref, wvup_ref,
                 z_ref, xbc_ref, dt_ref, q_ref, k_ref, v_ref):
    x = x_ref[0]
    h = _rms(x) * (g_ref[...] * (1.0 + sc_ref[0])) + sh_ref[0]
    hb = h.astype(BF16)
    z_ref[0] = _dot(hb, wz_ref[...]).astype(BF16)
    xbc_ref[0] = _dot(hb, wxbc_ref[...]).astype(BF16)
    sm = _dot(hb, wsm_ref[...])
    dt_ref[0] = sm[:, LANES:]
    ct = ct_ref[0]
    st = st_ref[0]
    lane = lax.broadcasted_iota(jnp.int32, ct.shape, 1)
    kr = _rope_block(sm[:, :LANES], ct, st, lane)
    qan = (_rms(_dot(hb, wqa_ref[...])) * qn_ref[...]).astype(BF16)
    q = _dot(qan, wqup_ref[...])
    for hh in range(MLA_HEADS):
        blk = slice(hh * LANES, (hh + 1) * LANES)
        q_ref[0, :, blk] = _rope_block(q[:, blk], ct, st, lane).astype(BF16)
    kvn = (_rms(_dot(hb, wkva_ref[...])) * kvn_ref[...]).astype(BF16)
    k = _dot(kvn, wkup_ref[...])
    for hh in range(MLA_HEADS):
        blk = slice(hh * LANES, (hh + 1) * LANES)
        k_ref[0, :, blk] = (k[:, blk] + kr).astype(BF16)
    vt = _dot_nt(wvup_ref[...], kvn)
    vrow = lax.broadcasted_iota(jnp.int32, vt.shape, 0)
    v_ref[0] = jnp.where(vrow % VT_ROWS == LANES, 1.0, vt).astype(BF16)


def _inproj(x, sc1, sh1, gain, ct, st, wz, wxbc, wsm, wqa, wkva, qn, kvn, wqup, wkup, wvup, *, batch_offset):
    _, seq, d = x.shape
    bsz = sc1.shape[0]
    ts = min(TS_PROJ, seq)
    hw = MLA_HEADS * LANES

    def tok(width):
        return pl.BlockSpec((1, ts, width), lambda b, i: (b, i, 0))

    def tok_full(width):
        return pl.BlockSpec((1, ts, width), lambda b, i: (batch_offset + b, i, 0))

    def per_batch(width):
        return pl.BlockSpec((1, 1, width), lambda b, i: (b, 0, 0))

    weights = (wz, wxbc, wsm, wqa, wkva, qn, kvn, wqup, wkup, wvup)
    out_widths = (SSD_INNER, XBC_DIM, LANES, hw, hw)
    out_dtypes = (BF16, BF16, F32, BF16, BF16)
    vdim = (MLA_HEADS // 2) * VT_ROWS
    out_shape = ([jax.ShapeDtypeStruct((bsz, seq, w), dt) for w, dt in zip(out_widths, out_dtypes)]
                 + [jax.ShapeDtypeStruct((bsz, vdim, seq), BF16)])
    n_tok = bsz * seq
    mm_cols = sum(w.shape[1] for w in (wz, wxbc, wsm, wqa, wkva))
    flops = 2 * n_tok * (d * mm_cols + wqup.shape[0] * wqup.shape[1] + wkup.shape[0] * wkup.shape[1]
                         + wvup.shape[0] * wvup.shape[1])
    read = [jax.ShapeDtypeStruct((bsz, seq, d + 2 * LANES), F32), *weights]
    return pl.pallas_call(
        _inproj_body,
        grid=(bsz, seq // ts),
        in_specs=[tok_full(d), per_batch(d), per_batch(d), _const_spec((1, d)), tok_full(LANES), tok_full(LANES)]
                 + [_const_spec(w.shape) for w in weights],
        out_specs=[tok(w) for w in out_widths] + [pl.BlockSpec((1, vdim, ts), lambda b, i: (b, 0, i))],
        out_shape=out_shape,
        cost_estimate=_cost(flops, 3 * n_tok, read, out_shape),
        compiler_params=pltpu.CompilerParams(dimension_semantics=("parallel", "parallel"),
                                             vmem_limit_bytes=VMEM_LIMIT),
        name="inproj",
    )(x, sc1, sh1, gain, ct, st, *weights)


CONV_HALO = 16


def _ssd_body(xc_ref, xp_ref, z_ref, dt_ref, cw_ref, cb_ref, dtb_ref, alog_ref, dsk_ref, ng_ref,
              y_ref, xs_scr, st_scr, *, blk):
    i = pl.program_id(1)

    @pl.when(i == 0)
    def _():
        st_scr[...] = jnp.zeros_like(st_scr)

    xs_scr[0:CONV_HALO, :] = jnp.where(i > 0, xp_ref[0].astype(F32), 0.0)
    xs_scr[CONV_HALO:CONV_HALO + blk, :] = xc_ref[0].astype(F32)
    conv = cb_ref[...]
    for kk in range(SSD_CONV):
        off = CONV_HALO - (SSD_CONV - 1) + kk
        conv = conv + cw_ref[kk:kk + 1, :] * xs_scr[off:off + blk, :]
    xa = _silu(conv)
    xs = xa[:, :SSD_INNER]
    bm = xa[:, SSD_INNER:SSD_INNER + SSD_BC]
    cm = xa[:, SSD_INNER + SSD_BC:]

    hl = lax.broadcasted_iota(jnp.int32, (1, LANES), 1)
    dtr = dt_ref[0] + dtb_ref[...]
    dt = jnp.maximum(dtr, 0.0) + jnp.log(1.0 + jnp.exp(-jnp.abs(dtr)))
    a = jnp.where(hl < SSD_HEADS, -jnp.exp(alog_ref[...]), 0.0)
    dta = dt * a
    row = lax.broadcasted_iota(jnp.int32, (blk, blk), 0)
    col = lax.broadcasted_iota(jnp.int32, (blk, blk), 1)
    tril = row >= col
    trilb = jnp.where(tril, 1.0, 0.0).astype(BF16)
    d1, d2, d3 = _split3(dta)
    cs = _dot(trilb, d1) + _dot(trilb, d2) + _dot(trilb, d3)
    cs_last = cs[blk - 1:blk, :]
    ecs = jnp.exp(cs)
    dte = jnp.exp(cs_last - cs)
    cs_t = cs.T

    er = lax.broadcasted_iota(jnp.int32, (LANES, SSD_INNER), 0)
    ec = lax.broadcasted_iota(jnp.int32, (LANES, SSD_INNER), 1)
    expand = jnp.where(ec // SSD_HEAD_DIM == er, 1.0, 0.0).astype(BF16)

    def per_channel(v):
        vh, vl = _split2(v)
        return _dot(vh, expand) + _dot(vl, expand)

    dt_e = per_channel(dt)
    ecs_e = per_channel(ecs)
    dte_e = per_channel(dte)
    xdt = xs * dt_e
    xdt_b = xdt.astype(BF16)
    xw_b = (xdt * dte_e).astype(BF16)

    gw = SSD_INNER // SSD_GROUPS
    heads_per_group = SSD_HEADS // SSD_GROUPS
    lane = lax.broadcasted_iota(jnp.int32, (blk, LANES), 1)
    y_groups = []
    for g in range(SSD_GROUPS):
        bg = bm[:, g * SSD_STATE:(g + 1) * SSD_STATE].astype(BF16)
        cg = cm[:, g * SSD_STATE:(g + 1) * SSD_STATE].astype(BF16)
        cb = _dot_nt(cg, bg)
        state = st_scr[g]
        y_off = _dot(cg, state.astype(BF16))
        pairs = []
        for j in range(heads_per_group // 2):
            xp = xdt_b[:, g * gw + j * LANES:g * gw + (j + 1) * LANES]
            halves = []
            for u in range(2):
                hidx = g * heads_per_group + 2 * j + u
                seg = cs[:, hidx:hidx + 1] - cs_t[hidx:hidx + 1, :]
                dec = jnp.exp(jnp.where(tril, seg, -jnp.inf))
                halves.append(_dot((cb * dec).astype(BF16), xp))
            pairs.append(jnp.where(lane < SSD_HEAD_DIM, halves[0], halves[1]))
        y_diag = jnp.concatenate(pairs, axis=1)
        y_groups.append(y_diag + y_off * ecs_e[:, g * gw:(g + 1) * gw])
        st_scr[g] = (state * ecs_e[blk - 1:blk, g * gw:(g + 1) * gw]
                     + _dot_tn(bg, xw_b[:, g * gw:(g + 1) * gw]))
    y = jnp.concatenate(y_groups, axis=1) + xs * dsk_ref[...]
    y = y * _silu(z_ref[0].astype(F32))
    y = jnp.concatenate([_rms(y[:, g * gw:(g + 1) * gw]) for g in range(SSD_GROUPS)], axis=1)
    y_ref[0] = (y * ng_ref[...]).astype(BF16)


def _ssd(xbc, z, dt, conv_w, conv_b, dt_bias, a_log, d_skip_e, norm_gain):
    bsz, seq, _ = xbc.shape
    blk = min(SSD_L, seq)
    halo_per_blk = blk // CONV_HALO
    body = functools.partial(_ssd_body, blk=blk)
    out_shape = jax.ShapeDtypeStruct((bsz, seq, SSD_INNER), BF16)
    n_tok = bsz * seq
    flops = 2 * n_tok * (3 * blk * LANES + 6 * LANES * SSD_INNER + SSD_GROUPS * blk * SSD_STATE
                         + SSD_HEADS * blk * LANES + 2 * SSD_STATE * SSD_INNER) + 20 * n_tok * XBC_DIM
    trans = n_tok * (XBC_DIM + SSD_INNER + SSD_HEADS * blk + 4 * LANES)
    return pl.pallas_call(
        body,
        grid=(bsz, seq // blk),
        cost_estimate=_cost(flops, trans, (xbc, z, dt), (out_shape,)),
        in_specs=[pl.BlockSpec((1, blk, XBC_DIM), lambda b, i: (b, i, 0)),
                  pl.BlockSpec((1, CONV_HALO, XBC_DIM), lambda b, i: (b, jnp.maximum(i * halo_per_blk - 1, 0), 0)),
                  pl.BlockSpec((1, blk, SSD_INNER), lambda b, i: (b, i, 0)),
                  pl.BlockSpec((1, blk, LANES), lambda b, i: (b, i, 0)),
                  _const_spec((SSD_CONV, XBC_DIM)), _const_spec((1, XBC_DIM)),
                  _const_spec((1, LANES)), _const_spec((1, LANES)),
                  _const_spec((1, SSD_INNER)), _const_spec((1, SSD_INNER))],
        out_specs=pl.BlockSpec((1, blk, SSD_INNER), lambda b, i: (b, i, 0)),
        out_shape=out_shape,
        scratch_shapes=[pltpu.VMEM((CONV_HALO + blk, XBC_DIM), F32),
                        pltpu.VMEM((SSD_GROUPS, SSD_STATE, SSD_INNER // SSD_GROUPS), F32)],
        compiler_params=pltpu.CompilerParams(dimension_semantics=("parallel", "arbitrary"),
                                             vmem_limit_bytes=VMEM_LIMIT),
        name="ssd",
    )(xbc, xbc, z, dt, conv_w, conv_b, dt_bias, a_log, d_skip_e, norm_gain)


def _attn_body(q_ref, k_ref, v_ref, o_ref, *scratch, tq, tk):
    n_streams = 2 * (tq // tk)
    s_scr = (scratch[:n_streams], scratch[n_streams:2 * n_streams])
    acc_scr = scratch[2 * n_streams:]
    qi = pl.program_id(2)
    n_sub = tq // tk
    n_full = qi * n_sub
    krow = lax.broadcasted_iota(jnp.int32, (tk, tk), 0)
    qcol = lax.broadcasted_iota(jnp.int32, (tk, tk), 1)
    diag_ok = krow // CHUNK <= qcol // CHUNK
    vrow = lax.broadcasted_iota(jnp.int32, (LANES, tk), 0)
    streams = [(u, r) for u in range(2) for r in range(n_sub)]
    qs = [q_ref[0, r * tk:(r + 1) * tk, u * LANES:(u + 1) * LANES] for u, r in streams]

    def put_scores(ki, which, slot):
        start = pl.multiple_of(ki * tk, tk)
        k2 = k_ref[0, pl.ds(start, tk), :]
        out = {}
        for si in which:
            u = streams[si][0]
            s = _dot_nt(k2[:, u * LANES:(u + 1) * LANES], qs[si])
            s_scr[slot][si][...] = s
            out[si] = jnp.max(s, axis=0, keepdims=True)
        return out

    def values_t(ki):
        return v_ref[0, :, pl.ds(pl.multiple_of(ki * tk, tk), tk)]

    def softmax_pv(si, m, s_max, slot, vt, masked):
        s = s_scr[slot][si][...]
        if masked:
            s = jnp.where(diag_ok, s, -jnp.inf)
            s_max = jnp.max(s, axis=0, keepdims=True)
        m_new = jnp.maximum(m, s_max)
        alpha = jnp.exp2(m - m_new)
        p = jnp.exp2((s - m_new).astype(BF16))
        acc_scr[si][...] = alpha * acc_scr[si][...] + _dot(vt, p)
        return m_new

    every = list(range(len(streams)))
    for ref in acc_scr:
        ref[...] = jnp.zeros_like(ref)

    def step(j, state):
        ms, s_maxes = state
        for slot in range(2):
            ki = 2 * j + slot
            nxt = put_scores(ki + 1, every, 1 - slot)
            vt = values_t(ki)
            ms = tuple(softmax_pv(si, ms[si], s_maxes[si], slot, vt, False) for si in every)
            s_maxes = tuple(nxt[si] for si in every)
        return ms, s_maxes

    first = put_scores(0, every, 0)
    m_init = jnp.full((1, tk), -jnp.inf, F32)
    ms, s_maxes = lax.fori_loop(0, n_full // 2, step, ((m_init,) * len(streams), tuple(first[si] for si in every)))
    ms = list(ms)
    s_maxes = dict(zip(every, s_maxes))
    for dd in range(n_sub):
        slot = dd % 2
        live = [si for si in every if dd <= streams[si][1]]
        later = [si for si in every if dd + 1 <= streams[si][1]]
        nxt = put_scores(n_full + dd + 1, later, 1 - slot) if later else {}
        vt = values_t(n_full + dd)
        for si in live:
            ms[si] = softmax_pv(si, ms[si], s_maxes[si], slot, vt, dd == streams[si][1])
        s_maxes = nxt
    for r in range(n_sub):
        a0 = acc_scr[streams.index((0, r))][...]
        a1 = acc_scr[streams.index((1, r))][...]
        out_t = jnp.where(vrow < MLA_V, a0[:LANES] / a0[LANES:LANES + 1], a1[:LANES] / a1[LANES:LANES + 1])
        o_ref[0, r * tk:(r + 1) * tk, :] = out_t.T.astype(BF16)


def _attn(q, k, v):
    bsz, seq, _ = q.shape
    tk = min(ATT_TK, seq)
    tq = min(ATT_TQ, seq)
    assert (tq // tk) % 2 == 0, "the two-slot score pipeline needs an even number of query sub-tiles"
    n_streams = 2 * (tq // tk)
    body = functools.partial(_attn_body, tq=tq, tk=tk)
    out_shape = jax.ShapeDtypeStruct((bsz, seq, MLA_HEADS * MLA_V), BF16)
    tile_pairs = bsz * MLA_HEADS * (seq // tk) * (seq // tk + 1) // 2
    flops = 2 * tile_pairs * tk * tk * (LANES + VT_ROWS)
    return pl.pallas_call(
        body,
        grid=(bsz, MLA_HEADS // 2, seq // tq),
        cost_estimate=_cost(flops, tile_pairs * tk * tk, (q, k, v), (out_shape,)),
        in_specs=[pl.BlockSpec((1, tq, 2 * LANES), lambda b, hp, i: (b, i, hp)),
                  pl.BlockSpec((1, seq, 2 * LANES), lambda b, hp, i: (b, 0, hp)),
                  pl.BlockSpec((1, VT_ROWS, seq), lambda b, hp, i: (b, hp, 0))],
        out_specs=pl.BlockSpec((1, tq, LANES), lambda b, hp, i: (b, i, hp)),
        out_shape=out_shape,
        scratch_shapes=[pltpu.VMEM((tk, tk), F32)] * (2 * n_streams) + [pltpu.VMEM((VT_ROWS, tk), F32)] * n_streams,
        compiler_params=pltpu.CompilerParams(dimension_semantics=("parallel", "parallel", "arbitrary"),
                                             vmem_limit_bytes=VMEM_LIMIT),
        name="attn",
    )(q, k, v)


def _outproj_body(ys_ref, ya_ref, x_ref, g1_ref, sc2_ref, sh2_ref, mn_ref, wo1_ref, wo2_ref, pmn_ref, pfn_ref,
                  wr_ref, br_ref, x1_ref, h2_ref, gate_ref, idx_ref, cnt_ref, pref_ref, ltab_ref, cnt_scr):
    first = (pl.program_id(0) == 0) & (pl.program_id(1) == 0)

    @pl.when(first)
    def _():
        cnt_scr[...] = jnp.zeros_like(cnt_scr)

    yan =(_rms(ya_ref[0].astype(F32)) * mn_ref[...]).astype(BF16)
    mix = _dot(ys_ref[0], wo1_ref[...]) + _dot(yan, wo2_ref[...])
    x1 = x_ref[0] + g1_ref[0] * (_rms(mix) * pmn_ref[...])
    x1_ref[0] = x1
    h2 = _rms(x1) * (pfn_ref[...] * (1.0 + sc2_ref[0])) + sh2_ref[0]
    h2_ref[0] = h2.astype(BF16)
    logits = _dot3(h2, wr_ref[...]) + br_ref[...]
    lane = lax.broadcasted_iota(jnp.int32, logits.shape, 1)
    cur = jnp.where(lane < N_EXPERTS, logits, -jnp.inf)
    vals, idxs = [], []
    for _ in range(TOP_K):
        m = jnp.max(cur, axis=-1, keepdims=True)
        ix = jnp.min(jnp.where(cur == m, lane, LANES), axis=-1, keepdims=True)
        vals.append(m)
        idxs.append(ix)
        cur = jnp.where(lane == ix, -jnp.inf, cur)
    es = [jnp.exp(v - vals[0]) for v in vals]
    denom = es[0]
    for e in es[1:]:
        denom = denom + e
    onehot = jnp.zeros(logits.shape, F32)
    for kk in range(TOP_K):
        onehot = onehot + jnp.where(lane == idxs[kk], 1.0, 0.0)
    ts = logits.shape[0]
    row = lax.broadcasted_iota(jnp.int32, (ts, ts), 0)
    col = lax.broadcasted_iota(jnp.int32, (ts, ts), 1)
    before = jnp.where(row > col, 1.0, 0.0).astype(BF16)
    prior = _dot(before, onehot.astype(BF16))
    seen = cnt_scr[...]
    pref_ref[0] = seen
    tok = lax.broadcasted_iota(jnp.int32, logits.shape, 0)
    tok_hi = (tok // 16).astype(F32)
    tok_lo = (tok % 16).astype(F32)
    ltab = jnp.zeros((2 * LANES, ts), F32)
    gate_out = jnp.zeros(logits.shape, F32)
    idx_out = jnp.zeros(logits.shape, jnp.int32)
    for kk in range(TOP_K):
        mine = lane == idxs[kk]
        local = jnp.sum(jnp.where(mine, prior, 0.0), axis=-1, keepdims=True)
        rank = local + jnp.sum(jnp.where(mine, seen, 0.0), axis=-1, keepdims=True)
        at_rank = jnp.where(col == local.astype(jnp.int32), 1.0, 0.0).astype(BF16)
        tagged = jnp.concatenate([jnp.where(mine, tok_hi, 0.0), jnp.where(mine, tok_lo, 0.0)], axis=1)
        ltab = ltab + _dot_tn(tagged.astype(BF16), at_rank)
        gate_out = jnp.where(lane == kk, es[kk] / denom, gate_out)
        idx_out = jnp.where(lane == kk, idxs[kk], idx_out)
        idx_out = jnp.where(lane == TOP_K + kk, rank.astype(jnp.int32), idx_out)
    gate_ref[0] = gate_out
    idx_ref[0] = idx_out
    ltab_ref[0] = (16.0 * ltab[:N_EXPERTS] + ltab[LANES:LANES + N_EXPERTS]).astype(jnp.int32)
    cnt_scr[...] = seen + jnp.sum(onehot, axis=0, keepdims=True)
    cnt_ref[...] = cnt_scr[...]


def _outproj(y_ssd, y_att, x, g1, sc2, sh2, mla_norm, wo1, wo2, post_mix_norm, pre_ffn_norm, wr, br, *, batch_offset):
    bsz, seq, _ = y_ssd.shape
    d = x.shape[-1]
    ts = min(TS_PROJ, seq)
    tiles = seq // ts

    def tok(width):
        return pl.BlockSpec((1, ts, width), lambda b, i: (b, i, 0))

    x_spec = pl.BlockSpec((1, ts, d), lambda b, i: (batch_offset + b, i, 0))

    def per_batch(width):
        return pl.BlockSpec((1, 1, width), lambda b, i: (b, 0, 0))

    consts = (mla_norm, wo1, wo2, post_mix_norm, pre_ffn_norm, wr, br)
    out_shape = [jax.ShapeDtypeStruct((bsz, seq, d), F32), jax.ShapeDtypeStruct((bsz, seq, d), BF16),
                 jax.ShapeDtypeStruct((bsz, seq, LANES), F32), jax.ShapeDtypeStruct((bsz, seq, LANES), jnp.int32),
                 jax.ShapeDtypeStruct((1, LANES), F32),
                 jax.ShapeDtypeStruct((bsz * tiles, 1, LANES), F32),
                 jax.ShapeDtypeStruct((bsz * tiles, N_EXPERTS, ts), jnp.int32)]
    n_tok = bsz * seq
    flops = 2 * n_tok * (d * d + 3 * d * LANES + ts * LANES + TOP_K * 2 * LANES * ts)
    read = [y_ssd, y_att, jax.ShapeDtypeStruct((bsz, seq, d), F32), *consts]
    return pl.pallas_call(
        _outproj_body,
        grid=(bsz, seq // ts),
        in_specs=[tok(SSD_INNER), tok(MLA_HEADS * MLA_V), x_spec, per_batch(d), per_batch(d), per_batch(d)]
                 + [_const_spec(w.shape) for w in consts],
        out_specs=[tok(d), tok(d), tok(LANES), tok(LANES), _const_spec((1, LANES)),
                   pl.BlockSpec((1, 1, LANES), lambda b, i: (b * tiles + i, 0, 0)),
                   pl.BlockSpec((1, N_EXPERTS, ts), lambda b, i: (b * tiles + i, 0, 0))],
        out_shape=out_shape,
        cost_estimate=_cost(flops, 8 * n_tok, read, out_shape),
        scratch_shapes=[pltpu.VMEM((1, LANES), F32)],
        compiler_params=pltpu.CompilerParams(dimension_semantics=("arbitrary", "arbitrary"),
                                             vmem_limit_bytes=VMEM_LIMIT),
        name="outproj",
    )(y_ssd, y_att, x, g1, sc2, sh2, *consts)


def _moe_body(be_ref, na_ref, x_ref, wgu_ref, bgu_ref, wd_ref, bd_ref, y_ref, wgu_b, wd_b):
    i = pl.program_id(0)

    @pl.when((i == 0) | (be_ref[i] != be_ref[jnp.maximum(i - 1, 0)]))
    def _():
        wgu_b[...] = wgu_ref[0].astype(BF16)
        wd_b[...] = wd_ref[0].astype(BF16)

    @pl.when(i < na_ref[0])
    def _():
        gu = _dot(x_ref[...], wgu_b[...]) + bgu_ref[0]
        glu = jnp.minimum(gu[:, :D_FF_EXPERT], SWIGLU_LIMIT)
        lin = jnp.clip(gu[:, D_FF_EXPERT:], -SWIGLU_LIMIT, SWIGLU_LIMIT)
        act = glu * jax.nn.sigmoid(SWIGLU_ALPHA * glu) * (lin + 1.0)
        y_ref[...] = (_dot(act.astype(BF16), wd_b[...]) + bd_ref[0]).astype(BF16)

    @pl.when(i >= na_ref[0])
    def _():
        y_ref[...] = jnp.zeros_like(y_ref)


def _moe(block_expert, n_active, xg, wgu, bgu, wd, bd):
    n_slots, d = xg.shape
    n_blocks = n_slots // MOE_TB
    f2 = wgu.shape[2]
    out_shape = jax.ShapeDtypeStruct((n_slots, d), BF16)
    flops = 2 * n_slots * (d * f2 + (f2 // 2) * d)
    return pl.pallas_call(
        _moe_body,
        cost_estimate=_cost(flops, n_slots * (f2 // 2), (xg, wgu, bgu, wd, bd), (out_shape,)),
        grid_spec=pltpu.PrefetchScalarGridSpec(
            num_scalar_prefetch=2,
            grid=(n_blocks,),
            in_specs=[pl.BlockSpec((MOE_TB, d), lambda i, be, na: (i, 0)),
                      pl.BlockSpec((1, d, f2), lambda i, be, na: (be[i], 0, 0)),
                      pl.BlockSpec((1, 1, f2), lambda i, be, na: (be[i], 0, 0)),
                      pl.BlockSpec((1, f2 // 2, d), lambda i, be, na: (be[i], 0, 0)),
                      pl.BlockSpec((1, 1, d), lambda i, be, na: (be[i], 0, 0))],
            out_specs=pl.BlockSpec((MOE_TB, d), lambda i, be, na: (i, 0)),
            scratch_shapes=[pltpu.VMEM((d, f2), BF16), pltpu.VMEM((f2 // 2, d), BF16)],
        ),
        out_shape=out_shape,
        compiler_params=pltpu.CompilerParams(dimension_semantics=("arbitrary",),
                                             vmem_limit_bytes=VMEM_LIMIT),
        name="moe",
    )(block_expert, n_active, xg, wgu, bgu, wd, bd)


def _final_body(x1_ref, y_ref, gate_ref, g2_ref, gain_ref, *rest):
    o_ref = rest[-1]
    gates = gate_ref[0]
    f = gates[:, 0:1] * y_ref[0, 0].astype(F32)
    for kk in range(1, TOP_K):
        f = f + gates[:, kk:kk + 1] * y_ref[kk, 0].astype(F32)
    o_ref[0] = x1_ref[0] + g2_ref[0] * (_rms(f) * gain_ref[...])


def _final(x1, y4, gates, g2, gain, out_prev, batch_offset, total_batch):
    bsz, seq, d = x1.shape
    ts = min(TS_PROJ, seq)

    def tok(width):
        return pl.BlockSpec((1, ts, width), lambda b, i: (b, i, 0))

    in_specs = [tok(d), pl.BlockSpec((TOP_K, 1, ts, d), lambda b, i: (0, b, i, 0)), tok(LANES),
                pl.BlockSpec((1, 1, d), lambda b, i: (b, 0, 0)), _const_spec((1, d))]
    args = [x1, y4, gates, g2, gain]
    aliases = {}
    if out_prev is not None:
        in_specs.append(pl.BlockSpec(memory_space=pl.ANY))
        args.append(out_prev)
        aliases = {len(args) - 1: 0}
    return pl.pallas_call(
        _final_body,
        grid=(bsz, seq // ts),
        in_specs=in_specs,
        out_specs=pl.BlockSpec((1, ts, d), lambda b, i: (batch_offset + b, i, 0)),
        out_shape=jax.ShapeDtypeStruct((total_batch, seq, d), F32),
        input_output_aliases=aliases,
        cost_estimate=_cost(12 * bsz * seq * d, bsz * seq, (x1, y4, gates), (x1,)),
        compiler_params=pltpu.CompilerParams(dimension_semantics=("parallel", "parallel"),
                                             vmem_limit_bytes=VMEM_LIMIT),
        name="final",
    )(*args)


def _head_blocks(cols):
    out = []
    for c in cols:
        pad = LANES - c.shape[1]
        out.append(jnp.pad(c, ((0, 0), (0, pad))) if pad else c)
    return jnp.concatenate(out, axis=1)


def _prep_mixer_weights(w_in, w_q_up, w_kv_up):
    d = w_in.shape[0]
    wz = w_in[:, OFF_Z:OFF_XBC]
    wxbc = w_in[:, OFF_XBC:OFF_DT]
    wdt = w_in[:, OFF_DT:OFF_QA]
    wqa = w_in[:, OFF_QA:OFF_KVA]
    wkva = w_in[:, OFF_KVA:OFF_KR]
    wkr = w_in[:, OFF_KR:IN_COLS]
    kr_blk = jnp.concatenate([jnp.zeros((d, ROPE_LO), F32), wkr, jnp.zeros((d, LANES - ROPE_LO - MLA_ROPE), F32)], axis=1)
    dt_blk = jnp.pad(wdt, ((0, 0), (0, LANES - SSD_HEADS)))
    wsm = jnp.concatenate([kr_blk, dt_blk], axis=1)
    qh = MLA_NOPE + MLA_ROPE
    scale = math.log2(math.e) / math.sqrt(qh)
    wqup = _head_blocks([w_q_up[:, h * qh:(h + 1) * qh] for h in range(MLA_HEADS)]) * scale
    kvh = MLA_NOPE + MLA_V
    wkup = _head_blocks([w_kv_up[:, h * kvh:h * kvh + MLA_NOPE] for h in range(MLA_HEADS)])
    vcols = []
    for h in range(MLA_HEADS):
        vcols.append(w_kv_up[:, h * kvh + MLA_NOPE:(h + 1) * kvh])
        if h % 2 == 1:
            vcols.append(jnp.zeros((w_kv_up.shape[0], VT_ROWS - LANES), F32))
    wvup = jnp.concatenate(vcols, axis=1).T
    return tuple(w.astype(BF16) for w in (wz, wxbc, wsm, wqa, wkva)) + tuple(w.astype(BF16) for w in (wqup, wkup, wvup))


def _rope_tables(positions):
    inv_freq = ROPE_BASE ** (-(jnp.arange(HALF_ROPE, dtype=F32) * 2.0 / MLA_ROPE))
    angles = positions.astype(F32)[..., None] * inv_freq
    cos = jnp.cos(angles)
    sin = jnp.sin(angles)
    shp = angles.shape[:-1]
    ct = jnp.concatenate([jnp.ones(shp + (ROPE_LO,), F32), cos, cos,
                          jnp.zeros(shp + (LANES - ROPE_LO - MLA_ROPE,), F32)], axis=-1)
    st = jnp.concatenate([jnp.zeros(shp + (ROPE_LO,), F32), -sin, sin,
                          jnp.zeros(shp + (LANES - ROPE_LO - MLA_ROPE,), F32)], axis=-1)
    return ct, st


def _route(idx, rank, counts, tile_seen, ltab, n_tok):
    n_assign = n_tok * TOP_K
    n_tiles, _, tile = ltab.shape
    padded = ((counts + MOE_TB - 1) // MOE_TB) * MOE_TB
    padded_end = jnp.cumsum(padded)
    padded_start = padded_end - padded
    experts = jnp.arange(N_EXPERTS, dtype=jnp.int32)
    start_of = jnp.sum(jnp.where(idx[..., None] == experts, padded_start, 0), axis=-1)
    dest = (start_of + rank).reshape(-1)
    n_blocks = n_assign // MOE_TB + N_EXPERTS
    block_start = jnp.arange(n_blocks, dtype=jnp.int32) * MOE_TB
    block_expert = jnp.minimum(jnp.sum((padded_end[None, :] <= block_start[:, None]).astype(jnp.int32), axis=1),
                               N_EXPERTS - 1)
    n_active = (padded_end[-1] // MOE_TB).astype(jnp.int32).reshape(1)
    j = (block_start - padded_start[block_expert])[:, None] + jnp.arange(MOE_TB, dtype=jnp.int32)[None, :]
    seen_blk = tile_seen.T[block_expert]
    reached = seen_blk[:, None, :] <= j[:, :, None]
    tau = jnp.sum(reached.astype(jnp.int32), axis=-1) - 1
    j_local = j - jnp.max(jnp.where(reached, seen_blk[:, None, :], 0), axis=-1)
    valid = j < counts[block_expert][:, None]
    flat = (tau * N_EXPERTS + block_expert[:, None]) * tile + j_local
    tok_local = ltab.reshape(-1)[jnp.clip(flat, 0, n_tiles * N_EXPERTS * tile - 1)]
    slot_tok = jnp.where(valid, tau * tile + tok_local, 0).reshape(-1)
    return dest, slot_tok, block_expert, n_active


def kernel(x, c, positions, w_ada, b_ada, pre_mix_norm, w_in, conv_w, conv_b, dt_bias, a_log, d_skip, ssd_norm, q_a_norm, w_q_up, kv_a_norm, w_kv_up, mla_norm, w_out, post_mix_norm, pre_ffn_norm, w_router, b_router, w_gate_up, b_gate_up, w_down, b_down, post_ffn_norm):
    bsz, seq, d = x.shape
    ct, st = _rope_tables(positions)
    n_groups = BATCH_GROUPS if bsz % BATCH_GROUPS == 0 else 1
    gb = bsz // n_groups
    n_tok = gb * seq
    pad_h = LANES - SSD_HEADS
    for l in range(w_ada.shape[0]):
        mod = _adaln(c, w_ada[l], b_ada[l])
        mods = [m.reshape(bsz, 1, d) for m in jnp.split(mod, 6, axis=-1)]
        mixer_w = _prep_mixer_weights(w_in[l], w_q_up[l], w_kv_up[l])
        wo = w_out[l].astype(BF16)
        wr = jnp.pad(w_router[l], ((0, 0), (0, LANES - N_EXPERTS)))
        br = jnp.pad(b_router[l], (0, LANES - N_EXPERTS)).reshape(1, LANES)
        dtb = jnp.pad(dt_bias[l], (0, pad_h)).reshape(1, LANES)
        alog = jnp.pad(a_log[l], (0, pad_h)).reshape(1, LANES)
        dsk = jnp.repeat(d_skip[l], SSD_HEAD_DIM).reshape(1, -1)
        out = None
        prev_slot_tok = None
        prev_xs = None
        for gi in range(n_groups):
            grp = slice(gi * gb, (gi + 1) * gb)
            sh1, sc1, g1, sh2, sc2, g2 = [m[grp] for m in mods]
            if prev_slot_tok is not None:
                sc1 = sc1 + jnp.where(prev_slot_tok[0] < 0, 1.0, 0.0)
            z, xbc, dt, q, k, v = _inproj(x, sc1, sh1, pre_mix_norm[l].reshape(1, d), ct, st, *mixer_w[:5],
                                          q_a_norm[l].reshape(1, -1), kv_a_norm[l].reshape(1, -1), *mixer_w[5:],
                                          batch_offset=gi * gb)
            y_ssd = _ssd(xbc, z, dt, conv_w[l], conv_b[l].reshape(1, -1), dtb, alog, dsk, ssd_norm[l].reshape(1, -1))
            y_att = _attn(q, k, v)
            x1, h2, gates, route, cnt, seen, ltab = _outproj(
                y_ssd, y_att, x, g1, sc2, sh2, mla_norm[l].reshape(1, -1), wo[:SSD_INNER], wo[SSD_INNER:],
                post_mix_norm[l].reshape(1, d), pre_ffn_norm[l].reshape(1, d), wr, br, batch_offset=gi * gb)
            route = route.reshape(n_tok, LANES)
            counts = cnt[0, :N_EXPERTS].astype(jnp.int32)
            tile_seen = seen[:, 0, :N_EXPERTS].astype(jnp.int32)
            if prev_xs is not None:
                bits = lax.bitcast_convert_type(prev_xs[0, 0], jnp.uint16).astype(jnp.int32)
                tile_seen = tile_seen + jnp.where(bits < 0, 1, 0)
            dest, slot_tok, block_expert, n_active = _route(route[:, :TOP_K], route[:, TOP_K:2 * TOP_K], counts,
                                                            tile_seen, ltab, n_tok)
            prev_slot_tok = slot_tok
            xs = h2.reshape(n_tok, d)[slot_tok]
            prev_xs = xs
            y = _moe(block_expert, n_active, xs, w_gate_up[l], b_gate_up[l][:, None, :], w_down[l], b_down[l][:, None, :])
            y4 = y[dest.reshape(n_tok, TOP_K).T.reshape(-1)].reshape(TOP_K, gb, seq, d)
            out = _final(x1, y4, gates, g2, post_ffn_norm[l].reshape(1, d), out, gi * gb, bsz)
        x = out
    return x
```

```python
import functools
import math

import jax
import jax.numpy as jnp
from jax import lax
from jax.experimental import pallas as pl
from jax.experimental.pallas import tpu as pltpu

F32 = jnp.float32
BF16 = jnp.bfloat16

D_MODEL = 1024
CHUNK = 64
SSD_INNER = 512
SSD_HEAD_DIM = 64
SSD_HEADS = 8
SSD_GROUPS = 2
SSD_STATE = 128
SSD_CONV = 4
SSD_BC = SSD_GROUPS * SSD_STATE
XBC_DIM = SSD_INNER + 2 * SSD_BC
MLA_V = 64
MLA_HEADS = 8
MLA_NOPE = 64
MLA_ROPE = 32
Q_LORA = 384
KV_LORA = 256
ROPE_BASE = 10000.0
OFF_Z = 0
OFF_XBC = OFF_Z + SSD_INNER
OFF_DT = OFF_XBC + XBC_DIM
OFF_QA = OFF_DT + SSD_HEADS
OFF_KVA = OFF_QA + Q_LORA
OFF_KR = OFF_KVA + KV_LORA
IN_COLS = OFF_KR + MLA_ROPE
N_EXPERTS = 32
TOP_K = 4
D_FF_EXPERT = 1024
SWIGLU_LIMIT = 7.0
SWIGLU_ALPHA = 1.702
NORM_EPS = 1e-6

LANES = 128
HALF_ROPE = MLA_ROPE // 2
ROPE_LO = MLA_NOPE
ROPE_HI = MLA_NOPE + HALF_ROPE
VT_ROWS = 144

TS_PROJ = 512
SSD_L = 256
ATT_TQ = 512
ATT_TK = 256
MOE_TB = 512
BATCH_GROUPS = 2
VMEM_LIMIT = 56 * 1024 * 1024


def _dot(a, b):
    return jnp.dot(a, b, preferred_element_type=F32)


def _dot_nt(a, b):
    return lax.dot_general(a, b, (((1,), (1,)), ((), ())), preferred_element_type=F32)


def _dot_tn(a, b):
    return lax.dot_general(a, b, (((0,), (0,)), ((), ())), preferred_element_type=F32)


def _split2(x):
    hi = x.astype(BF16)
    lo = (x - hi.astype(F32)).astype(BF16)
    return hi, lo


def _split3(x):
    h1 = x.astype(BF16)
    r1 = x - h1.astype(F32)
    h2 = r1.astype(BF16)
    h3 = (r1 - h2.astype(F32)).astype(BF16)
    return h1, h2, h3


def _dot3(a, b):
    ah, al = _split2(a)
    bh, bl = _split2(b)
    return _dot(ah, bh) + _dot(ah, bl) + _dot(al, bh)


def _rms(x):
    return x * lax.rsqrt(jnp.mean(x * x, axis=-1, keepdims=True) + NORM_EPS)


def _silu(x):
    return x * jax.nn.sigmoid(x)


def _cost(flops, transcendentals, operands, out_shapes):
    nbytes = sum(math.prod(a.shape) * a.dtype.itemsize for a in operands)
    nbytes += sum(math.prod(o.shape) * jnp.dtype(o.dtype).itemsize for o in out_shapes)
    return pl.CostEstimate(flops=int(flops), transcendentals=int(transcendentals), bytes_accessed=int(nbytes))


def _const_spec(shape):
    nd = len(shape)
    return pl.BlockSpec(shape, lambda *_: (0,) * nd)


def _adaln_body(c_ref, w_ref, b_ref, o_ref):
    o_ref[...] = _dot3(_silu(c_ref[...]), w_ref[...]) + b_ref[...]


def _adaln(c, w_ada, b_ada):
    bsz, d = c.shape
    n = w_ada.shape[1]
    tn = 1024
    return pl.pallas_call(
        _adaln_body,
        grid=(n // tn,),
        in_specs=[_const_spec((bsz, d)),
                  pl.BlockSpec((d, tn), lambda j: (0, j)),
                  pl.BlockSpec((1, tn), lambda j: (0, j))],
        out_specs=pl.BlockSpec((bsz, tn), lambda j: (0, j)),
        out_shape=jax.ShapeDtypeStruct((bsz, n), F32),
        compiler_params=pltpu.CompilerParams(dimension_semantics=("arbitrary",),
                                             vmem_limit_bytes=VMEM_LIMIT),
        name="adaln",
    )(c, w_ada, b_ada.reshape(1, n))


def _rope_block(xb, ct, st, lane):
    partner = jnp.where(lane < ROPE_HI, pltpu.roll(xb, LANES - HALF_ROPE, 1), pltpu.roll(xb, HALF_ROPE, 1))
    return xb * ct + partner * st


def _inproj_body(x_ref, sc_ref, sh_ref, g_ref, ct_ref, st_ref, wz_ref, wxbc_ref, wsm_ref, wqa_ref, wkva_ref,
                 qn_ref, kvn_ref, wqup_ref, wkup_ref, wvup_ref,
                 z_ref, xbc_ref, dt_ref, q_ref, k_ref, v_ref):
    x = x_ref[0]
    h = _rms(x) * (g_ref[...] * (1.0 + sc_ref[0])) + sh_ref[0]
    hb = h.astype(BF16)
    z_ref[0] = _dot(hb, wz_ref[...]).astype(BF16)
    xbc_ref[0] = _dot(hb, wxbc_ref[...]).astype(BF16)
    sm = _dot(hb, wsm_ref[...])
    dt_ref[0] = sm[:, LANES:]
    ct = ct_ref[0]
    st = st_ref[0]
    lane = lax.broadcasted_iota(jnp.int32, ct.shape, 1)
    kr = _rope_block(sm[:, :LANES], ct, st, lane)
    qan = (_rms(_dot(hb, wqa_ref[...])) * qn_ref[...]).astype(BF16)
    q = _dot(qan, wqup_ref[...])
    for hh in range(MLA_HEADS):
        blk = slice(hh * LANES, (hh + 1) * LANES)
        q_ref[0, :, blk] = _rope_block(q[:, blk], ct, st, lane).astype(BF16)
    kvn = (_rms(_dot(hb, wkva_ref[...])) * kvn_ref[...]).astype(BF16)
    k = _dot(kvn, wkup_ref[...])
    for hh in range(MLA_HEADS):
        blk = slice(hh * LANES, (hh + 1) * LANES)
        k_ref[0, :, blk] = (k[:, blk] + kr).astype(BF16)
    vt = _dot_nt(wvup_ref[...], kvn)
    vrow = lax.broadcasted_iota(jnp.int32, vt.shape, 0)
    v_ref[0] = jnp.where(vrow % VT_ROWS == LANES, 1.0, vt).astype(BF16)


def _inproj(x, sc1, sh1, gain, ct, st, wz, wxbc, wsm, wqa, wkva, qn, kvn, wqup, wkup, wvup, *, batch_offset):
    _, seq, d = x.shape
    bsz = sc1.shape[0]
    ts = min(TS_PROJ, seq)
    hw = MLA_HEADS * LANES

    def tok(width):
        return pl.BlockSpec((1, ts, width), lambda b, i: (b, i, 0))

    def tok_full(width):
        return pl.BlockSpec((1, ts, width), lambda b, i: (batch_offset + b, i, 0))

    def per_batch(width):
        return pl.BlockSpec((1, 1, width), lambda b, i: (b, 0, 0))

    weights = (wz, wxbc, wsm, wqa, wkva, qn, kvn, wqup, wkup, wvup)
    out_widths = (SSD_INNER, XBC_DIM, LANES, hw, hw)
    out_dtypes = (BF16, BF16, F32, BF16, BF16)
    vdim = (MLA_HEADS // 2) * VT_ROWS
    out_shape = ([jax.ShapeDtypeStruct((bsz, seq, w), dt) for w, dt in zip(out_widths, out_dtypes)]
                 + [jax.ShapeDtypeStruct((bsz, vdim, seq), BF16)])
    n_tok = bsz * seq
    mm_cols = sum(w.shape[1] for w in (wz, wxbc, wsm, wqa, wkva))
    flops = 2 * n_tok * (d * mm_cols + wqup.shape[0] * wqup.shape[1] + wkup.shape[0] * wkup.shape[1]
                         + wvup.shape[0] * wvup.shape[1])
    read = [jax.ShapeDtypeStruct((bsz, seq, d + 2 * LANES), F32), *weights]
    return pl.pallas_call(
        _inproj_body,
        grid=(bsz, seq // ts),
        in_specs=[tok_full(d), per_batch(d), per_batch(d), _const_spec((1, d)), tok_full(LANES), tok_full(LANES)]
                 + [_const_spec(w.shape) for w in weights],
        out_specs=[tok(w) for w in out_widths] + [pl.BlockSpec((1, vdim, ts), lambda b, i: (b, 0, i))],
        out_shape=out_shape,
        compiler_params=pltpu.CompilerParams(dimension_semantics=("parallel", "parallel"),
                                             vmem_limit_bytes=VMEM_LIMIT),
        name="inproj",
    )(x, sc1, sh1, gain, ct, st, *weights)


CONV_HALO = 16


def _ssd_body(xc_ref, xp_ref, z_ref, dt_ref, cw_ref, cb_ref, dtb_ref, alog_ref, dsk_ref, ng_ref,
              y_ref, xs_scr, st_scr, *, blk):
    i = pl.program_id(1)

    @pl.when(i == 0)
    def _():
        st_scr[...] = jnp.zeros_like(st_scr)

    xs_scr[0:CONV_HALO, :] = jnp.where(i > 0, xp_ref[0].astype(F32), 0.0)
    xs_scr[CONV_HALO:CONV_HALO + blk, :] = xc_ref[0].astype(F32)
    conv = cb_ref[...]
    for kk in range(SSD_CONV):
        off = CONV_HALO - (SSD_CONV - 1) + kk
        conv = conv + cw_ref[kk:kk + 1, :] * xs_scr[off:off + blk, :]
    xa = _silu(conv)
    xs = xa[:, :SSD_INNER]
    bm = xa[:, SSD_INNER:SSD_INNER + SSD_BC]
    cm = xa[:, SSD_INNER + SSD_BC:]

    hl = lax.broadcasted_iota(jnp.int32, (1, LANES), 1)
    dtr = dt_ref[0] + dtb_ref[...]
    dt = jnp.maximum(dtr, 0.0) + jnp.log(1.0 + jnp.exp(-jnp.abs(dtr)))
    a = jnp.where(hl < SSD_HEADS, -jnp.exp(alog_ref[...]), 0.0)
    dta = dt * a
    row = lax.broadcasted_iota(jnp.int32, (blk, blk), 0)
    col = lax.broadcasted_iota(jnp.int32, (blk, blk), 1)
    tril = row >= col
    trilb = jnp.where(tril, 1.0, 0.0).astype(BF16)
    d1, d2, d3 = _split3(dta)
    cs = _dot(trilb, d1) + _dot(trilb, d2) + _dot(trilb, d3)
    cs_last = cs[blk - 1:blk, :]
    ecs = jnp.exp(cs)
    dte = jnp.exp(cs_last - cs)
    cs_t = cs.T

    er = lax.broadcasted_iota(jnp.int32, (LANES, SSD_INNER), 0)
    ec = lax.broadcasted_iota(jnp.int32, (LANES, SSD_INNER), 1)
    expand = jnp.where(ec // SSD_HEAD_DIM == er, 1.0, 0.0).astype(BF16)

    def per_channel(v):
        vh, vl = _split2(v)
        return _dot(vh, expand) + _dot(vl, expand)

    dt_e = per_channel(dt)
    ecs_e = per_channel(ecs)
    dte_e = per_channel(dte)
    xdt = xs * dt_e
    xdt_b = xdt.astype(BF16)
    xw_b = (xdt * dte_e).astype(BF16)

    gw = SSD_INNER // SSD_GROUPS
    heads_per_group = SSD_HEADS // SSD_GROUPS
    lane = lax.broadcasted_iota(jnp.int32, (blk, LANES), 1)
    y_groups = []
    for g in range(SSD_GROUPS):
        bg = bm[:, g * SSD_STATE:(g + 1) * SSD_STATE].astype(BF16)
        cg = cm[:, g * SSD_STATE:(g + 1) * SSD_STATE].astype(BF16)
        cb = _dot_nt(cg, bg)
        state = st_scr[g]
        y_off = _dot(cg, state.astype(BF16))
        pairs = []
        for j in range(heads_per_group // 2):
            xp = xdt_b[:, g * gw + j * LANES:g * gw + (j + 1) * LANES]
            halves = []
            for u in range(2):
                hidx = g * heads_per_group + 2 * j + u
                seg = cs[:, hidx:hidx + 1] - cs_t[hidx:hidx + 1, :]
                dec = jnp.exp(jnp.where(tril, seg, -jnp.inf))
                halves.append(_dot((cb * dec).astype(BF16), xp))
            pairs.append(jnp.where(lane < SSD_HEAD_DIM, halves[0], halves[1]))
        y_diag = jnp.concatenate(pairs, axis=1)
        y_groups.append(y_diag + y_off * ecs_e[:, g * gw:(g + 1) * gw])
        st_scr[g] = (state * ecs_e[blk - 1:blk, g * gw:(g + 1) * gw]
                     + _dot_tn(bg, xw_b[:, g * gw:(g + 1) * gw]))
    y = jnp.concatenate(y_groups, axis=1) + xs * dsk_ref[...]
    y = y * _silu(z_ref[0].astype(F32))
    y = jnp.concatenate([_rms(y[:, g * gw:(g + 1) * gw]) for g in range(SSD_GROUPS)], axis=1)
    y_ref[0] = (y * ng_ref[...]).astype(BF16)


def _ssd(xbc, z, dt, conv_w, conv_b, dt_bias, a_log, d_skip_e, norm_gain):
    bsz, seq, _ = xbc.shape
    blk = min(SSD_L, seq)
    halo_per_blk = blk // CONV_HALO
    body = functools.partial(_ssd_body, blk=blk)
    out_shape = jax.ShapeDtypeStruct((bsz, seq, SSD_INNER), BF16)
    n_tok = bsz * seq
    flops = 2 * n_tok * (3 * blk * LANES + 6 * LANES * SSD_INNER + SSD_GROUPS * blk * SSD_STATE
                         + SSD_HEADS * blk * LANES + 2 * SSD_STATE * SSD_INNER) + 20 * n_tok * XBC_DIM
    trans = n_tok * (XBC_DIM + SSD_INNER + SSD_HEADS * blk + 4 * LANES)
    return pl.pallas_call(
        body,
        grid=(bsz, seq // blk),
        in_specs=[pl.BlockSpec((1, blk, XBC_DIM), lambda b, i: (b, i, 0)),
                  pl.BlockSpec((1, CONV_HALO, XBC_DIM), lambda b, i: (b, jnp.maximum(i * halo_per_blk - 1, 0), 0)),
                  pl.BlockSpec((1, blk, SSD_INNER), lambda b, i: (b, i, 0)),
                  pl.BlockSpec((1, blk, LANES), lambda b, i: (b, i, 0)),
                  _const_spec((SSD_CONV, XBC_DIM)), _const_spec((1, XBC_DIM)),
                  _const_spec((1, LANES)), _const_spec((1, LANES)),
                  _const_spec((1, SSD_INNER)), _const_spec((1, SSD_INNER))],
        out_specs=pl.BlockSpec((1, blk, SSD_INNER), lambda b, i: (b, i, 0)),
        out_shape=out_shape,
        scratch_shapes=[pltpu.VMEM((CONV_HALO + blk, XBC_DIM), F32),
                        pltpu.VMEM((SSD_GROUPS, SSD_STATE, SSD_INNER // SSD_GROUPS), F32)],
        compiler_params=pltpu.CompilerParams(dimension_semantics=("parallel", "arbitrary"),
                                             vmem_limit_bytes=VMEM_LIMIT),
        name="ssd",
    )(xbc, xbc, z, dt, conv_w, conv_b, dt_bias, a_log, d_skip_e, norm_gain)


def _attn_body(q_ref, k_ref, v_ref, o_ref, *scratch, tq, tk):
    n_streams = 2 * (tq // tk)
    s_scr = (scratch[:n_streams], scratch[n_streams:2 * n_streams])
    acc_scr = scratch[2 * n_streams:]
    qi = pl.program_id(2)
    n_sub = tq // tk
    n_full = qi * n_sub
    krow = lax.broadcasted_iota(jnp.int32, (tk, tk), 0)
    qcol = lax.broadcasted_iota(jnp.int32, (tk, tk), 1)
    diag_ok = krow // CHUNK <= qcol // CHUNK
    vrow = lax.broadcasted_iota(jnp.int32, (LANES, tk), 0)
    streams = [(u, r) for u in range(2) for r in range(n_sub)]
    qs = [q_ref[0, r * tk:(r + 1) * tk, u * LANES:(u + 1) * LANES] for u, r in streams]

    def put_scores(ki, which, slot):
        start = pl.multiple_of(ki * tk, tk)
        k2 = k_ref[0, pl.ds(start, tk), :]
        out = {}
        for si in which:
            u = streams[si][0]
            s = _dot_nt(k2[:, u * LANES:(u + 1) * LANES], qs[si])
            s_scr[slot][si][...] = s
            out[si] = jnp.max(s, axis=0, keepdims=True)
        return out

    def values_t(ki):
        return v_ref[0, :, pl.ds(pl.multiple_of(ki * tk, tk), tk)]

    def softmax_pv(si, m, s_max, slot, vt, masked):
        s = s_scr[slot][si][...]
        if masked:
            s = jnp.where(diag_ok, s, -jnp.inf)
            s_max = jnp.max(s, axis=0, keepdims=True)
        m_new = jnp.maximum(m, s_max)
        alpha = jnp.exp2(m - m_new)
        p = jnp.exp2((s - m_new).astype(BF16))
        acc_scr[si][...] = alpha * acc_scr[si][...] + _dot(vt, p)
        return m_new

    every = list(range(len(streams)))
    for ref in acc_scr:
        ref[...] = jnp.zeros_like(ref)

    def step(j, state):
        ms, s_maxes = state
        for slot in range(2):
            ki = 2 * j + slot
            nxt = put_scores(ki + 1, every, 1 - slot)
            vt = values_t(ki)
            ms = tuple(softmax_pv(si, ms[si], s_maxes[si], slot, vt, False) for si in every)
            s_maxes = tuple(nxt[si] for si in every)
        return ms, s_maxes

    first = put_scores(0, every, 0)
    m_init = jnp.full((1, tk), -jnp.inf, F32)
    ms, s_maxes = lax.fori_loop(0, n_full // 2, step, ((m_init,) * len(streams), tuple(first[si] for si in every)))
    ms = list(ms)
    s_maxes = dict(zip(every, s_maxes))
    for dd in range(n_sub):
        slot = dd % 2
        live = [si for si in every if dd <= streams[si][1]]
        later = [si for si in every if dd + 1 <= streams[si][1]]
        nxt = put_scores(n_full + dd + 1, later, 1 - slot) if later else {}
        vt = values_t(n_full + dd)
        for si in live:
            ms[si] = softmax_pv(si, ms[si], s_maxes[si], slot, vt, dd == streams[si][1])
        s_maxes = nxt
    for r in range(n_sub):
        a0 = acc_scr[streams.index((0, r))][...]
        a1 = acc_scr[streams.index((1, r))][...]
        out_t = jnp.where(vrow < MLA_V, a0[:LANES] / a0[LANES:LANES + 1], a1[:LANES] / a1[LANES:LANES + 1])
        o_ref[0, r * tk:(r + 1) * tk, :] = out_t.T.astype(BF16)


def _attn(q, k, v):
    bsz, seq, _ = q.shape
    tk = min(ATT_TK, seq)
    tq = min(ATT_TQ, seq)
    assert (tq // tk) % 2 == 0, "the two-slot score pipeline needs an even number of query sub-tiles"
    n_streams = 2 * (tq // tk)
    body = functools.partial(_attn_body, tq=tq, tk=tk)
    out_shape = jax.ShapeDtypeStruct((bsz, seq, MLA_HEADS * MLA_V), BF16)
    tile_pairs = bsz * MLA_HEADS * (seq // tk) * (seq // tk + 1) // 2
    flops = 2 * tile_pairs * tk * tk * (LANES + VT_ROWS)
    return pl.pallas_call(
        body,
        grid=(bsz, MLA_HEADS // 2, seq // tq),
        in_specs=[pl.BlockSpec((1, tq, 2 * LANES), lambda b, hp, i: (b, i, hp)),
                  pl.BlockSpec((1, seq, 2 * LANES), lambda b, hp, i: (b, 0, hp)),
                  pl.BlockSpec((1, VT_ROWS, seq), lambda b, hp, i: (b, hp, 0))],
        out_specs=pl.BlockSpec((1, tq, LANES), lambda b, hp, i: (b, i, hp)),
        out_shape=out_shape,
        scratch_shapes=[pltpu.VMEM((tk, tk), F32)] * (2 * n_streams) + [pltpu.VMEM((VT_ROWS, tk), F32)] * n_streams,
        compiler_params=pltpu.CompilerParams(dimension_semantics=("parallel", "parallel", "arbitrary"),
                                             vmem_limit_bytes=VMEM_LIMIT),
        name="attn",
    )(q, k, v)


def _outproj_body(ys_ref, ya_ref, x_ref, g1_ref, sc2_ref, sh2_ref, mn_ref, wo1_ref, wo2_ref, pmn_ref, pfn_ref,
                  wr_ref, br_ref, x1_ref, h2_ref, gate_ref, idx_ref, cnt_ref, pref_ref, ltab_ref, cnt_scr):
    first = (pl.program_id(0) == 0) & (pl.program_id(1) == 0)

    @pl.when(first)
    def _():
        cnt_scr[...] = jnp.zeros_like(cnt_scr)

    yan =(_rms(ya_ref[0].astype(F32)) * mn_ref[...]).astype(BF16)
    mix = _dot(ys_ref[0], wo1_ref[...]) + _dot(yan, wo2_ref[...])
    x1 = x_ref[0] + g1_ref[0] * (_rms(mix) * pmn_ref[...])
    x1_ref[0] = x1
    h2 = _rms(x1) * (pfn_ref[...] * (1.0 + sc2_ref[0])) + sh2_ref[0]
    h2_ref[0] = h2.astype(BF16)
    logits = _dot3(h2, wr_ref[...]) + br_ref[...]
    lane = lax.broadcasted_iota(jnp.int32, logits.shape, 1)
    cur = jnp.where(lane < N_EXPERTS, logits, -jnp.inf)
    vals, idxs = [], []
    for _ in range(TOP_K):
        m = jnp.max(cur, axis=-1, keepdims=True)
        ix = jnp.min(jnp.where(cur == m, lane, LANES), axis=-1, keepdims=True)
        vals.append(m)
        idxs.append(ix)
        cur = jnp.where(lane == ix, -jnp.inf, cur)
    es = [jnp.exp(v - vals[0]) for v in vals]
    denom = es[0]
    for e in es[1:]:
        denom = denom + e
    onehot = jnp.zeros(logits.shape, F32)
    for kk in range(TOP_K):
        onehot = onehot + jnp.where(lane == idxs[kk], 1.0, 0.0)
    ts = logits.shape[0]
    row = lax.broadcasted_iota(jnp.int32, (ts, ts), 0)
    col = lax.broadcasted_iota(jnp.int32, (ts, ts), 1)
    before = jnp.where(row > col, 1.0, 0.0).astype(BF16)
    prior = _dot(before, onehot.astype(BF16))
    seen = cnt_scr[...]
    pref_ref[0] = seen
    tok = lax.broadcasted_iota(jnp.int32, logits.shape, 0)
    tok_hi = (tok // 16).astype(F32)
    tok_lo = (tok % 16).astype(F32)
    ltab = jnp.zeros((2 * LANES, ts), F32)
    gate_out = jnp.zeros(logits.shape, F32)
    idx_out = jnp.zeros(logits.shape, jnp.int32)
    for kk in range(TOP_K):
        mine = lane == idxs[kk]
        local = jnp.sum(jnp.where(mine, prior, 0.0), axis=-1, keepdims=True)
        rank = local + jnp.sum(jnp.where(mine, seen, 0.0), axis=-1, keepdims=True)
        at_rank = jnp.where(col == local.astype(jnp.int32), 1.0, 0.0).astype(BF16)
        tagged = jnp.concatenate([jnp.where(mine, tok_hi, 0.0), jnp.where(mine, tok_lo, 0.0)], axis=1)
        ltab = ltab + _dot_tn(tagged.astype(BF16), at_rank)
        gate_out = jnp.where(lane == kk, es[kk] / denom, gate_out)
        idx_out = jnp.where(lane == kk, idxs[kk], idx_out)
        idx_out = jnp.where(lane == TOP_K + kk, rank.astype(jnp.int32), idx_out)
    gate_ref[0] = gate_out
    idx_ref[0] = idx_out
    ltab_ref[0] = (16.0 * ltab[:N_EXPERTS] + ltab[LANES:LANES + N_EXPERTS]).astype(jnp.int32)
    cnt_scr[...] = seen + jnp.sum(onehot, axis=0, keepdims=True)
    cnt_ref[...] = cnt_scr[...]


def _outproj(y_ssd, y_att, x, g1, sc2, sh2, mla_norm, wo1, wo2, post_mix_norm, pre_ffn_norm, wr, br, *, batch_offset):
    bsz, seq, _ = y_ssd.shape
    d = x.shape[-1]
    ts = min(TS_PROJ, seq)
    tiles = seq // ts

    def tok(width):
        return pl.BlockSpec((1, ts, width), lambda b, i: (b, i, 0))

    x_spec = pl.BlockSpec((1, ts, d), lambda b, i: (batch_offset + b, i, 0))

    def per_batch(width):
        return pl.BlockSpec((1, 1, width), lambda b, i: (b, 0, 0))

    consts = (mla_norm, wo1, wo2, post_mix_norm, pre_ffn_norm, wr, br)
    out_shape = [jax.ShapeDtypeStruct((bsz, seq, d), F32), jax.ShapeDtypeStruct((bsz, seq, d), BF16),
                 jax.ShapeDtypeStruct((bsz, seq, LANES), F32), jax.ShapeDtypeStruct((bsz, seq, LANES), jnp.int32),
                 jax.ShapeDtypeStruct((1, LANES), F32),
                 jax.ShapeDtypeStruct((bsz * tiles, 1, LANES), F32),
                 jax.ShapeDtypeStruct((bsz * tiles, N_EXPERTS, ts), jnp.int32)]
    n_tok = bsz * seq
    flops = 2 * n_tok * (d * d + 3 * d * LANES + ts * LANES + TOP_K * 2 * LANES * ts)
    read = [y_ssd, y_att, jax.ShapeDtypeStruct((bsz, seq, d), F32), *consts]
    return pl.pallas_call(
        _outproj_body,
        grid=(bsz, seq // ts),
        in_specs=[tok(SSD_INNER), tok(MLA_HEADS * MLA_V), x_spec, per_batch(d), per_batch(d), per_batch(d)]
                 + [_const_spec(w.shape) for w in consts],
        out_specs=[tok(d), tok(d), tok(LANES), tok(LANES), _const_spec((1, LANES)),
                   pl.BlockSpec((1, 1, LANES), lambda b, i: (b * tiles + i, 0, 0)),
                   pl.BlockSpec((1, N_EXPERTS, ts), lambda b, i: (b * tiles + i, 0, 0))],
        out_shape=out_shape,
        scratch_shapes=[pltpu.VMEM((1, LANES), F32)],
        compiler_params=pltpu.CompilerParams(dimension_semantics=("arbitrary", "arbitrary"),
                                             vmem_limit_bytes=VMEM_LIMIT),
        name="outproj",
    )(y_ssd, y_att, x, g1, sc2, sh2, *consts)


def _moe_body(be_ref, na_ref, x_ref, wgu_ref, bgu_ref, wd_ref, bd_ref, y_ref, wgu_b, wd_b):
    i = pl.program_id(0)

    @pl.when((i == 0) | (be_ref[i] != be_ref[jnp.maximum(i - 1, 0)]))
    def _():
        wgu_b[...] = wgu_ref[0].astype(BF16)
        wd_b[...] = wd_ref[0].astype(BF16)

    @pl.when(i < na_ref[0])
    def _():
        gu = _dot(x_ref[...], wgu_b[...]) + bgu_ref[0]
        glu = jnp.minimum(gu[:, :D_FF_EXPERT], SWIGLU_LIMIT)
        lin = jnp.clip(gu[:, D_FF_EXPERT:], -SWIGLU_LIMIT, SWIGLU_LIMIT)
        act = glu * jax.nn.sigmoid(SWIGLU_ALPHA * glu) * (lin + 1.0)
        y_ref[...] = (_dot(act.astype(BF16), wd_b[...]) + bd_ref[0]).astype(BF16)

    @pl.when(i >= na_ref[0])
    def _():
        y_ref[...] = jnp.zeros_like(y_ref)


def _moe(block_expert, n_active, xg, wgu, bgu, wd, bd):
    n_slots, d = xg.shape
    n_blocks = n_slots // MOE_TB
    f2 = wgu.shape[2]
    out_shape = jax.ShapeDtypeStruct((n_slots, d), BF16)
    flops = 2 * n_slots * (d * f2 + (f2 // 2) * d)
    return pl.pallas_call(
        _moe_body,
        grid_spec=pltpu.PrefetchScalarGridSpec(
            num_scalar_prefetch=2,
            grid=(n_blocks,),
            in_specs=[pl.BlockSpec((MOE_TB, d), lambda i, be, na: (i, 0)),
                      pl.BlockSpec((1, d, f2), lambda i, be, na: (be[i], 0, 0)),
                      pl.BlockSpec((1, 1, f2), lambda i, be, na: (be[i], 0, 0)),
                      pl.BlockSpec((1, f2 // 2, d), lambda i, be, na: (be[i], 0, 0)),
                      pl.BlockSpec((1, 1, d), lambda i, be, na: (be[i], 0, 0))],
            out_specs=pl.BlockSpec((MOE_TB, d), lambda i, be, na: (i, 0)),
            scratch_shapes=[pltpu.VMEM((d, f2), BF16), pltpu.VMEM((f2 // 2, d), BF16)],
        ),
        out_shape=out_shape,
        compiler_params=pltpu.CompilerParams(dimension_semantics=("arbitrary",),
                                             vmem_limit_bytes=VMEM_LIMIT),
        name="moe",
    )(block_expert, n_active, xg, wgu, bgu, wd, bd)


def _final_body(x1_ref, y_ref, gate_ref, g2_ref, gain_ref, *rest):
    o_ref = rest[-1]
    gates = gate_ref[0]
    f = gates[:, 0:1] * y_ref[0, 0].astype(F32)
    for kk in range(1, TOP_K):
        f = f + gates[:, kk:kk + 1] * y_ref[kk, 0].astype(F32)
    o_ref[0] = x1_ref[0] + g2_ref[0] * (_rms(f) * gain_ref[...])


def _final(x1, y4, gates, g2, gain, out_prev, batch_offset, total_batch):
    bsz, seq, d = x1.shape
    ts = min(TS_PROJ, seq)

    def tok(width):
        return pl.BlockSpec((1, ts, width), lambda b, i: (b, i, 0))

    in_specs = [tok(d), pl.BlockSpec((TOP_K, 1, ts, d), lambda b, i: (0, b, i, 0)), tok(LANES),
                pl.BlockSpec((1, 1, d), lambda b, i: (b, 0, 0)), _const_spec((1, d))]
    args = [x1, y4, gates, g2, gain]
    aliases = {}
    if out_prev is not None:
        in_specs.append(pl.BlockSpec(memory_space=pl.ANY))
        args.append(out_prev)
        aliases = {len(args) - 1: 0}
    return pl.pallas_call(
        _final_body,
        grid=(bsz, seq // ts),
        in_specs=in_specs,
        out_specs=pl.BlockSpec((1, ts, d), lambda b, i: (batch_offset + b, i, 0)),
        out_shape=jax.ShapeDtypeStruct((total_batch, seq, d), F32),
        input_output_aliases=aliases,
        compiler_params=pltpu.CompilerParams(dimension_semantics=("parallel", "parallel"),
                                             vmem_limit_bytes=VMEM_LIMIT),
        name="final",
    )(*args)


def _head_blocks(cols):
    out = []
    for c in cols:
        pad = LANES - c.shape[1]
        out.append(jnp.pad(c, ((0, 0), (0, pad))) if pad else c)
    return jnp.concatenate(out, axis=1)


def _prep_mixer_weights(w_in, w_q_up, w_kv_up):
    d = w_in.shape[0]
    wz = w_in[:, OFF_Z:OFF_XBC]
    wxbc = w_in[:, OFF_XBC:OFF_DT]
    wdt = w_in[:, OFF_DT:OFF_QA]
    wqa = w_in[:, OFF_QA:OFF_KVA]
    wkva = w_in[:, OFF_KVA:OFF_KR]
    wkr = w_in[:, OFF_KR:IN_COLS]
    kr_blk = jnp.concatenate([jnp.zeros((d, ROPE_LO), F32), wkr, jnp.zeros((d, LANES - ROPE_LO - MLA_ROPE), F32)], axis=1)
    dt_blk = jnp.pad(wdt, ((0, 0), (0, LANES - SSD_HEADS)))
    wsm = jnp.concatenate([kr_blk, dt_blk], axis=1)
    qh = MLA_NOPE + MLA_ROPE
    scale = math.log2(math.e) / math.sqrt(qh)
    wqup = _head_blocks([w_q_up[:, h * qh:(h + 1) * qh] for h in range(MLA_HEADS)]) * scale
    kvh = MLA_NOPE + MLA_V
    wkup = _head_blocks([w_kv_up[:, h * kvh:h * kvh + MLA_NOPE] for h in range(MLA_HEADS)])
    vcols = []
    for h in range(MLA_HEADS):
        vcols.append(w_kv_up[:, h * kvh + MLA_NOPE:(h + 1) * kvh])
        if h % 2 == 1:
            vcols.append(jnp.zeros((w_kv_up.shape[0], VT_ROWS - LANES), F32))
    wvup = jnp.concatenate(vcols, axis=1).T
    return tuple(w.astype(BF16) for w in (wz, wxbc, wsm, wqa, wkva)) + tuple(w.astype(BF16) for w in (wqup, wkup, wvup))


def _rope_tables(positions):
    inv_freq = ROPE_BASE ** (-(jnp.arange(HALF_ROPE, dtype=F32) * 2.0 / MLA_ROPE))
    angles = positions.astype(F32)[..., None] * inv_freq
    cos = jnp.cos(angles)
    sin = jnp.sin(angles)
    shp = angles.shape[:-1]
    ct = jnp.concatenate([jnp.ones(shp + (ROPE_LO,), F32), cos, cos,
                          jnp.zeros(shp + (LANES - ROPE_LO - MLA_ROPE,), F32)], axis=-1)
    st = jnp.concatenate([jnp.zeros(shp + (ROPE_LO,), F32), -sin, sin,
                          jnp.zeros(shp + (LANES - ROPE_LO - MLA_ROPE,), F32)], axis=-1)
    return ct, st


def _route(idx, rank, counts, tile_seen, ltab, n_tok):
    n_assign = n_tok * TOP_K
    n_tiles, _, tile = ltab.shape
    padded = ((counts + MOE_TB - 1) // MOE_TB) * MOE_TB
    padded_end = jnp.cumsum(padded)
    padded_start = padded_end - padded
    experts = jnp.arange(N_EXPERTS, dtype=jnp.int32)
    start_of = jnp.sum(jnp.where(idx[..., None] == experts, padded_start, 0), axis=-1)
    dest = (start_of + rank).reshape(-1)
    n_blocks = n_assign // MOE_TB + N_EXPERTS
    block_start = jnp.arange(n_blocks, dtype=jnp.int32) * MOE_TB
    block_expert = jnp.minimum(jnp.sum((padded_end[None, :] <= block_start[:, None]).astype(jnp.int32), axis=1),
                               N_EXPERTS - 1)
    n_active = (padded_end[-1] // MOE_TB).astype(jnp.int32).reshape(1)
    j = (block_start - padded_start[block_expert])[:, None] + jnp.arange(MOE_TB, dtype=jnp.int32)[None, :]
    seen_blk = tile_seen.T[block_expert]
    reached = seen_blk[:, None, :] <= j[:, :, None]
    tau = jnp.sum(reached.astype(jnp.int32), axis=-1) - 1
    j_local = j - jnp.max(jnp.where(reached, seen_blk[:, None, :], 0), axis=-1)
    valid = j < counts[block_expert][:, None]
    flat = (tau * N_EXPERTS + block_expert[:, None]) * tile + j_local
    tok_local = ltab.reshape(-1)[jnp.clip(flat, 0, n_tiles * N_EXPERTS * tile - 1)]
    slot_tok = jnp.where(valid, tau * tile + tok_local, 0).reshape(-1)
    return dest, slot_tok, block_expert, n_active


def kernel(x, c, positions, w_ada, b_ada, pre_mix_norm, w_in, conv_w, conv_b, dt_bias, a_log, d_skip, ssd_norm, q_a_norm, w_q_up, kv_a_norm, w_kv_up, mla_norm, w_out, post_mix_norm, pre_ffn_norm, w_router, b_router, w_gate_up, b_gate_up, w_down, b_down, post_ffn_norm):
    bsz, seq, d = x.shape
    ct, st = _rope_tables(positions)
    n_groups = BATCH_GROUPS if bsz % BATCH_GROUPS == 0 else 1
    gb = bsz // n_groups
    n_tok = gb * seq
    pad_h = LANES - SSD_HEADS
    for l in range(w_ada.shape[0]):
        mod = _adaln(c, w_ada[l], b_ada[l])
        mods = [m.reshape(bsz, 1, d) for m in jnp.split(mod, 6, axis=-1)]
        mixer_w = _prep_mixer_weights(w_in[l], w_q_up[l], w_kv_up[l])
        wo = w_out[l].astype(BF16)
        wr = jnp.pad(w_router[l], ((0, 0), (0, LANES - N_EXPERTS)))
        br = jnp.pad(b_router[l], (0, LANES - N_EXPERTS)).reshape(1, LANES)
        dtb = jnp.pad(dt_bias[l], (0, pad_h)).reshape(1, LANES)
        alog = jnp.pad(a_log[l], (0, pad_h)).reshape(1, LANES)
        dsk = jnp.repeat(d_skip[l], SSD_HEAD_DIM).reshape(1, -1)
        out = None
        prev_slot_tok = None
        prev_xs = None
        for gi in range(n_groups):
            grp = slice(gi * gb, (gi + 1) * gb)
            sh1, sc1, g1, sh2, sc2, g2 = [m[grp] for m in mods]
            if prev_slot_tok is not None:
                sc1 = sc1 + jnp.where(prev_slot_tok[0] < 0, 1.0, 0.0)
            z, xbc, dt, q, k, v = _inproj(x, sc1, sh1, pre_mix_norm[l].reshape(1, d), ct, st, *mixer_w[:5],
                                          q_a_norm[l].reshape(1, -1), kv_a_norm[l].reshape(1, -1), *mixer_w[5:],
                                          batch_offset=gi * gb)
            y_ssd = _ssd(xbc, z, dt, conv_w[l], conv_b[l].reshape(1, -1), dtb, alog, dsk, ssd_norm[l].reshape(1, -1))
            y_att = _attn(q, k, v)
            x1, h2, gates, route, cnt, seen, ltab = _outproj(
                y_ssd, y_att, x, g1, sc2, sh2, mla_norm[l].reshape(1, -1), wo[:SSD_INNER], wo[SSD_INNER:],
                post_mix_norm[l].reshape(1, d), pre_ffn_norm[l].reshape(1, d), wr, br, batch_offset=gi * gb)
            route = route.reshape(n_tok, LANES)
            counts = cnt[0, :N_EXPERTS].astype(jnp.int32)
            tile_seen = seen[:, 0, :N_EXPERTS].astype(jnp.int32)
            if prev_xs is not None:
                bits = lax.bitcast_convert_type(prev_xs[0, 0], jnp.uint16).astype(jnp.int32)
                tile_seen = tile_seen + jnp.where(bits < 0, 1, 0)
            dest, slot_tok, block_expert, n_active = _route(route[:, :TOP_K], route[:, TOP_K:2 * TOP_K], counts,
                                                            tile_seen, ltab, n_tok)
            prev_slot_tok = slot_tok
            xs = h2.reshape(n_tok, d)[slot_tok]
            prev_xs = xs
            y = _moe(block_expert, n_active, xs, w_gate_up[l], b_gate_up[l][:, None, :], w_down[l], b_down[l][:, None, :])
            y4 = y[dest.reshape(n_tok, TOP_K).T.reshape(-1)].reshape(TOP_K, gb, seq, d)
            out = _final(x1, y4, gates, g2, post_ffn_norm[l].reshape(1, d), out, gi * gb, bsz)
        x = out
    return x
```

```python
import functools
import math

import jax
import jax.numpy as jnp
from jax import lax
from jax.experimental import pallas as pl
from jax.experimental.pallas import tpu as pltpu

F32 = jnp.float32
BF16 = jnp.bfloat16

D_MODEL = 1024
CHUNK = 64
SSD_INNER = 512
SSD_HEAD_DIM = 64
SSD_HEADS = 8
SSD_GROUPS = 2
SSD_STATE = 128
SSD_CONV = 4
SSD_BC = SSD_GROUPS * SSD_STATE
XBC_DIM = SSD_INNER + 2 * SSD_BC
MLA_V = 64
MLA_HEADS = 8
MLA_NOPE = 64
MLA_ROPE = 32
Q_LORA = 384
KV_LORA = 256
ROPE_BASE = 10000.0
OFF_Z = 0
OFF_XBC = OFF_Z + SSD_INNER
OFF_DT = OFF_XBC + XBC_DIM
OFF_QA = OFF_DT + SSD_HEADS
OFF_KVA = OFF_QA + Q_LORA
OFF_KR = OFF_KVA + KV_LORA
IN_COLS = OFF_KR + MLA_ROPE
N_EXPERTS = 32
TOP_K = 4
D_FF_EXPERT = 1024
SWIGLU_LIMIT = 7.0
SWIGLU_ALPHA = 1.702
NORM_EPS = 1e-6

LANES = 128
HALF_ROPE = MLA_ROPE // 2
ROPE_LO = MLA_NOPE
ROPE_HI = MLA_NOPE + HALF_ROPE
VT_ROWS = 144

TS_PROJ = 512
SSD_L = 256
ATT_TQ = 512
ATT_TK = 256
MOE_TB = 512
BATCH_GROUPS = 2
VMEM_LIMIT = 56 * 1024 * 1024


def _dot(a, b):
    return jnp.dot(a, b, preferred_element_type=F32)


def _dot_nt(a, b):
    return lax.dot_general(a, b, (((1,), (1,)), ((), ())), preferred_element_type=F32)


def _dot_tn(a, b):
    return lax.dot_general(a, b, (((0,), (0,)), ((), ())), preferred_element_type=F32)


def _split2(x):
    hi = x.astype(BF16)
    lo = (x - hi.astype(F32)).astype(BF16)
    return hi, lo


def _split3(x):
    h1 = x.astype(BF16)
    r1 = x - h1.astype(F32)
    h2 = r1.astype(BF16)
    h3 = (r1 - h2.astype(F32)).astype(BF16)
    return h1, h2, h3


def _dot3(a, b):
    ah, al = _split2(a)
    bh, bl = _split2(b)
    return _dot(ah, bh) + _dot(ah, bl) + _dot(al, bh)


def _rms(x):
    return x * lax.rsqrt(jnp.mean(x * x, axis=-1, keepdims=True) + NORM_EPS)


def _silu(x):
    return x * jax.nn.sigmoid(x)


def _const_spec(shape):
    nd = len(shape)
    return pl.BlockSpec(shape, lambda *_: (0,) * nd)


def _adaln_body(c_ref, w_ref, b_ref, o_ref):
    o_ref[...] = _dot3(_silu(c_ref[...]), w_ref[...]) + b_ref[...]


def _adaln(c, w_ada, b_ada):
    bsz, d = c.shape
    n = w_ada.shape[1]
    tn = 1024
    return pl.pallas_call(
        _adaln_body,
        grid=(n // tn,),
        in_specs=[_const_spec((bsz, d)),
                  pl.BlockSpec((d, tn), lambda j: (0, j)),
                  pl.BlockSpec((1, tn), lambda j: (0, j))],
        out_specs=pl.BlockSpec((bsz, tn), lambda j: (0, j)),
        out_shape=jax.ShapeDtypeStruct((bsz, n), F32),
        compiler_params=pltpu.CompilerParams(dimension_semantics=("arbitrary",),
                                             vmem_limit_bytes=VMEM_LIMIT),
        name="adaln",
    )(c, w_ada, b_ada.reshape(1, n))


def _rope_block(xb, ct, st, lane):
    partner = jnp.where(lane < ROPE_HI, pltpu.roll(xb, LANES - HALF_ROPE, 1), pltpu.roll(xb, HALF_ROPE, 1))
    return xb * ct + partner * st


def _inproj_body(x_ref, sc_ref, sh_ref, g_ref, tab_ref, wz_ref, wxbc_ref, wsm_ref, wqa_ref, wkva_ref,
                 qn_ref, kvn_ref, wqup_ref, wkup_ref, wvup_ref,
                 z_ref, xbc_ref, dt_ref, q_ref, k_ref, v_ref):
    x = x_ref[0]
    h = _rms(x) * (g_ref[...] * (1.0 + sc_ref[0])) + sh_ref[0]
    hb = h.astype(BF16)
    z_ref[0] = _dot(hb, wz_ref[...]).astype(BF16)
    xbc_ref[0] = _dot(hb, wxbc_ref[...]).astype(BF16)
    sm = _dot(hb, wsm_ref[...])
    dt_ref[0] = sm[:, LANES:]
    tab = tab_ref[0]
    lane = lax.broadcasted_iota(jnp.int32, tab.shape, 1)
    in_lo = (lane >= ROPE_LO) & (lane < ROPE_HI)
    in_hi = (lane >= ROPE_HI) & (lane < ROPE_HI + HALF_ROPE)
    ct = jnp.where(lane < ROPE_LO, 1.0, jnp.where(in_lo, tab, jnp.where(in_hi, pltpu.roll(tab, HALF_ROPE, 1), 0.0)))
    st = jnp.where(in_lo, -pltpu.roll(tab, LANES - HALF_ROPE, 1), jnp.where(in_hi, tab, 0.0))
    kr =_rope_block(sm[:, :LANES], ct, st, lane)
    qan = (_rms(_dot(hb, wqa_ref[...])) * qn_ref[...]).astype(BF16)
    q = _dot(qan, wqup_ref[...])
    for hh in range(MLA_HEADS):
        blk = slice(hh * LANES, (hh + 1) * LANES)
        q_ref[0, :, blk] = _rope_block(q[:, blk], ct, st, lane).astype(BF16)
    kvn = (_rms(_dot(hb, wkva_ref[...])) * kvn_ref[...]).astype(BF16)
    k = _dot(kvn, wkup_ref[...])
    for hh in range(MLA_HEADS):
        blk = slice(hh * LANES, (hh + 1) * LANES)
        k_ref[0, :, blk] = (k[:, blk] + kr).astype(BF16)
    vt = _dot_nt(wvup_ref[...], kvn)
    vrow = lax.broadcasted_iota(jnp.int32, vt.shape, 0)
    v_ref[0] = jnp.where(vrow % VT_ROWS == LANES, 1.0, vt).astype(BF16)


def _inproj(x, sc1, sh1, gain, rope_tab, wz, wxbc, wsm, wqa, wkva, qn, kvn, wqup, wkup, wvup, *, batch_offset):
    _, seq, d = x.shape
    bsz = sc1.shape[0]
    ts = min(TS_PROJ, seq)
    hw = MLA_HEADS * LANES

    def tok(width):
        return pl.BlockSpec((1, ts, width), lambda b, i: (b, i, 0))

    def tok_full(width):
        return pl.BlockSpec((1, ts, width), lambda b, i: (batch_offset + b, i, 0))

    def per_batch(width):
        return pl.BlockSpec((1, 1, width), lambda b, i: (b, 0, 0))

    weights = (wz, wxbc, wsm, wqa, wkva, qn, kvn, wqup, wkup, wvup)
    out_widths = (SSD_INNER, XBC_DIM, LANES, hw, hw)
    out_dtypes = (BF16, BF16, F32, BF16, BF16)
    vdim = (MLA_HEADS // 2) * VT_ROWS
    out_shape = ([jax.ShapeDtypeStruct((bsz, seq, w), dt) for w, dt in zip(out_widths, out_dtypes)]
                 + [jax.ShapeDtypeStruct((bsz, vdim, seq), BF16)])
    return pl.pallas_call(
        _inproj_body,
        grid=(bsz, seq // ts),
        in_specs=[tok_full(d), per_batch(d), per_batch(d), _const_spec((1, d)), tok_full(LANES)]
                 + [_const_spec(w.shape) for w in weights],
        out_specs=[tok(w) for w in out_widths] + [pl.BlockSpec((1, vdim, ts), lambda b, i: (b, 0, i))],
        out_shape=out_shape,
        compiler_params=pltpu.CompilerParams(dimension_semantics=("parallel", "parallel"),
                                             vmem_limit_bytes=VMEM_LIMIT),
        name="inproj",
    )(x, sc1, sh1, gain, rope_tab, *weights)


CONV_HALO = 16


def _ssd_body(xc_ref, xp_ref, z_ref, dt_ref, cw_ref, cb_ref, dtb_ref, alog_ref, dsk_ref, ng_ref,
              y_ref, xs_scr, st_scr, *, blk):
    i = pl.program_id(1)

    @pl.when(i == 0)
    def _():
        st_scr[...] = jnp.zeros_like(st_scr)

    xs_scr[0:CONV_HALO, :] = jnp.where(i > 0, xp_ref[0].astype(F32), 0.0)
    xs_scr[CONV_HALO:CONV_HALO + blk, :] = xc_ref[0].astype(F32)
    conv = cb_ref[...]
    for kk in range(SSD_CONV):
        off = CONV_HALO - (SSD_CONV - 1) + kk
        conv = conv + cw_ref[kk:kk + 1, :] * xs_scr[off:off + blk, :]
    xa = _silu(conv)
    xs = xa[:, :SSD_INNER]
    bm = xa[:, SSD_INNER:SSD_INNER + SSD_BC]
    cm = xa[:, SSD_INNER + SSD_BC:]

    hl = lax.broadcasted_iota(jnp.int32, (1, LANES), 1)
    dtr = dt_ref[0] + dtb_ref[...]
    dt = jnp.maximum(dtr, 0.0) + jnp.log(1.0 + jnp.exp(-jnp.abs(dtr)))
    a = jnp.where(hl < SSD_HEADS, -jnp.exp(alog_ref[...]), 0.0)
    dta = dt * a
    row = lax.broadcasted_iota(jnp.int32, (blk, blk), 0)
    col = lax.broadcasted_iota(jnp.int32, (blk, blk), 1)
    tril = row >= col
    trilb = jnp.where(tril, 1.0, 0.0).astype(BF16)
    d1, d2, d3 = _split3(dta)
    cs = _dot(trilb, d1) + _dot(trilb, d2) + _dot(trilb, d3)
    cs_last = cs[blk - 1:blk, :]
    ecs = jnp.exp(cs)
    dte = jnp.exp(cs_last - cs)
    cs_t = cs.T

    er = lax.broadcasted_iota(jnp.int32, (LANES, SSD_INNER), 0)
    ec = lax.broadcasted_iota(jnp.int32, (LANES, SSD_INNER), 1)
    expand = jnp.where(ec // SSD_HEAD_DIM == er, 1.0, 0.0).astype(BF16)

    def per_channel(v):
        vh, vl = _split2(v)
        return _dot(vh, expand) + _dot(vl, expand)

    dt_e = per_channel(dt)
    ecs_e = per_channel(ecs)
    dte_e = per_channel(dte)
    xdt = xs * dt_e
    xdt_b = xdt.astype(BF16)
    xw_b = (xdt * dte_e).astype(BF16)

    gw = SSD_INNER // SSD_GROUPS
    heads_per_group = SSD_HEADS // SSD_GROUPS
    lane = lax.broadcasted_iota(jnp.int32, (blk, LANES), 1)
    y_groups = []
    for g in range(SSD_GROUPS):
        bg = bm[:, g * SSD_STATE:(g + 1) * SSD_STATE].astype(BF16)
        cg = cm[:, g * SSD_STATE:(g + 1) * SSD_STATE].astype(BF16)
        cb = _dot_nt(cg, bg)
        state = st_scr[g]
        y_off = _dot(cg, state.astype(BF16))
        pairs = []
        for j in range(heads_per_group // 2):
            xp = xdt_b[:, g * gw + j * LANES:g * gw + (j + 1) * LANES]
            halves = []
            for u in range(2):
                hidx = g * heads_per_group + 2 * j + u
                seg = cs[:, hidx:hidx + 1] - cs_t[hidx:hidx + 1, :]
                dec = jnp.exp(jnp.where(tril, seg, -jnp.inf))
                halves.append(_dot((cb * dec).astype(BF16), xp))
            pairs.append(jnp.where(lane < SSD_HEAD_DIM, halves[0], halves[1]))
        y_diag = jnp.concatenate(pairs, axis=1)
        y_groups.append(y_diag + y_off * ecs_e[:, g * gw:(g + 1) * gw])
        st_scr[g] = (state * ecs_e[blk - 1:blk, g * gw:(g + 1) * gw]
                     + _dot_tn(bg, xw_b[:, g * gw:(g + 1) * gw]))
    y = jnp.concatenate(y_groups, axis=1) + xs * dsk_ref[...]
    y = y * _silu(z_ref[0].astype(F32))
    y = jnp.concatenate([_rms(y[:, g * gw:(g + 1) * gw]) for g in range(SSD_GROUPS)], axis=1)
    y_ref[0] = (y * ng_ref[...]).astype(BF16)


def _ssd(xbc, z, dt, conv_w, conv_b, dt_bias, a_log, d_skip_e, norm_gain):
    bsz, seq, _ = xbc.shape
    blk = min(SSD_L, seq)
    halo_per_blk = blk // CONV_HALO
    body = functools.partial(_ssd_body, blk=blk)
    out_shape = jax.ShapeDtypeStruct((bsz, seq, SSD_INNER), BF16)
    return pl.pallas_call(
        body,
        grid=(bsz, seq // blk),
        in_specs=[pl.BlockSpec((1, blk, XBC_DIM), lambda b, i: (b, i, 0)),
                  pl.BlockSpec((1, CONV_HALO, XBC_DIM), lambda b, i: (b, jnp.maximum(i * halo_per_blk - 1, 0), 0)),
                  pl.BlockSpec((1, blk, SSD_INNER), lambda b, i: (b, i, 0)),
                  pl.BlockSpec((1, blk, LANES), lambda b, i: (b, i, 0)),
                  _const_spec((SSD_CONV, XBC_DIM)), _const_spec((1, XBC_DIM)),
                  _const_spec((1, LANES)), _const_spec((1, LANES)),
                  _const_spec((1, SSD_INNER)), _const_spec((1, SSD_INNER))],
        out_specs=pl.BlockSpec((1, blk, SSD_INNER), lambda b, i: (b, i, 0)),
        out_shape=out_shape,
        scratch_shapes=[pltpu.VMEM((CONV_HALO + blk, XBC_DIM), F32),
                        pltpu.VMEM((SSD_GROUPS, SSD_STATE, SSD_INNER // SSD_GROUPS), F32)],
        compiler_params=pltpu.CompilerParams(dimension_semantics=("parallel", "arbitrary"),
                                             vmem_limit_bytes=VMEM_LIMIT),
        name="ssd",
    )(xbc, xbc, z, dt, conv_w, conv_b, dt_bias, a_log, d_skip_e, norm_gain)


def _attn_body(q_ref, k_ref, v_ref, o_ref, *scratch, tq, tk):
    n_streams = 2 * (tq // tk)
    s_scr = (scratch[:n_streams], scratch[n_streams:2 * n_streams])
    acc_scr = scratch[2 * n_streams:]
    qi = pl.program_id(2)
    n_sub = tq // tk
    n_full = qi * n_sub
    krow = lax.broadcasted_iota(jnp.int32, (tk, tk), 0)
    qcol = lax.broadcasted_iota(jnp.int32, (tk, tk), 1)
    diag_ok = krow // CHUNK <= qcol // CHUNK
    vrow = lax.broadcasted_iota(jnp.int32, (LANES, tk), 0)
    streams = [(u, r) for u in range(2) for r in range(n_sub)]
    qs = [q_ref[0, r * tk:(r + 1) * tk, u * LANES:(u + 1) * LANES] for u, r in streams]

    def put_scores(ki, which, slot):
        start = pl.multiple_of(ki * tk, tk)
        k2 = k_ref[0, pl.ds(start, tk), :]
        out = {}
        for si in which:
            u = streams[si][0]
            s = _dot_nt(k2[:, u * LANES:(u + 1) * LANES], qs[si])
            s_scr[slot][si][...] = s
            out[si] = jnp.max(s, axis=0, keepdims=True)
        return out

    def values_t(ki):
        return v_ref[0, :, pl.ds(pl.multiple_of(ki * tk, tk), tk)]

    def softmax_pv(si, m, s_max, slot, vt, masked):
        s = s_scr[slot][si][...]
        if masked:
            s = jnp.where(diag_ok, s, -jnp.inf)
            s_max = jnp.max(s, axis=0, keepdims=True)
        m_new = jnp.maximum(m, s_max)
        alpha = jnp.exp2(m - m_new)
        p = jnp.exp2((s - m_new).astype(BF16))
        acc_scr[si][...] = alpha * acc_scr[si][...] + _dot(vt, p)
        return m_new

    every = list(range(len(streams)))
    for ref in acc_scr:
        ref[...] = jnp.zeros_like(ref)

    def step(j, state):
        ms, s_maxes = state
        for slot in range(2):
            ki = 2 * j + slot
            nxt = put_scores(ki + 1, every, 1 - slot)
            vt = values_t(ki)
            ms = tuple(softmax_pv(si, ms[si], s_maxes[si], slot, vt, False) for si in every)
            s_maxes = tuple(nxt[si] for si in every)
        return ms, s_maxes

    first = put_scores(0, every, 0)
    m_init = jnp.full((1, tk), -jnp.inf, F32)
    ms, s_maxes = lax.fori_loop(0, n_full // 2, step, ((m_init,) * len(streams), tuple(first[si] for si in every)))
    ms = list(ms)
    s_maxes = dict(zip(every, s_maxes))
    for dd in range(n_sub):
        slot = dd % 2
        live = [si for si in every if dd <= streams[si][1]]
        later = [si for si in every if dd + 1 <= streams[si][1]]
        nxt = put_scores(n_full + dd + 1, later, 1 - slot) if later else {}
        vt = values_t(n_full + dd)
        for si in live:
            ms[si] = softmax_pv(si, ms[si], s_maxes[si], slot, vt, dd == streams[si][1])
        s_maxes = nxt
    for r in range(n_sub):
        a0 = acc_scr[streams.index((0, r))][...]
        a1 = acc_scr[streams.index((1, r))][...]
        out_t = jnp.where(vrow < MLA_V, a0[:LANES] / a0[LANES:LANES + 1], a1[:LANES] / a1[LANES:LANES + 1])
        o_ref[0, r * tk:(r + 1) * tk, :] = out_t.T.astype(BF16)


def _attn(q, k, v):
    bsz, seq, _ = q.shape
    tk = min(ATT_TK, seq)
    tq = min(ATT_TQ, seq)
    assert (tq // tk) % 2 == 0, "the two-slot score pipeline needs an even number of query sub-tiles"
    n_streams = 2 * (tq // tk)
    body = functools.partial(_attn_body, tq=tq, tk=tk)
    out_shape = jax.ShapeDtypeStruct((bsz, seq, MLA_HEADS * MLA_V), BF16)
    return pl.pallas_call(
        body,
        grid=(bsz, MLA_HEADS // 2, seq // tq),
        in_specs=[pl.BlockSpec((1, tq, 2 * LANES), lambda b, hp, i: (b, i, hp)),
                  pl.BlockSpec((1, seq, 2 * LANES), lambda b, hp, i: (b, 0, hp)),
                  pl.BlockSpec((1, VT_ROWS, seq), lambda b, hp, i: (b, hp, 0))],
        out_specs=pl.BlockSpec((1, tq, LANES), lambda b, hp, i: (b, i, hp)),
        out_shape=out_shape,
        scratch_shapes=[pltpu.VMEM((tk, tk), F32)] * (2 * n_streams) + [pltpu.VMEM((VT_ROWS, tk), F32)] * n_streams,
        compiler_params=pltpu.CompilerParams(dimension_semantics=("parallel", "parallel", "arbitrary"),
                                             vmem_limit_bytes=VMEM_LIMIT),
        name="attn",
    )(q, k, v)


def _outproj_body(ys_ref, ya_ref, x_ref, g1_ref, sc2_ref, sh2_ref, mn_ref, wo1_ref, wo2_ref, pmn_ref, pfn_ref,
                  wr_ref, br_ref, x1_ref, h2_ref, gate_ref, idx_ref, cnt_ref, pref_ref, ltab_ref, cnt_scr):
    first = (pl.program_id(0) == 0) & (pl.program_id(1) == 0)

    @pl.when(first)
    def _():
        cnt_scr[...] = jnp.zeros_like(cnt_scr)

    yan =(_rms(ya_ref[0].astype(F32)) * mn_ref[...]).astype(BF16)
    mix = _dot(ys_ref[0], wo1_ref[...]) + _dot(yan, wo2_ref[...])
    x1 = x_ref[0] + g1_ref[0] * (_rms(mix) * pmn_ref[...])
    x1_ref[0] = x1
    h2 = _rms(x1) * (pfn_ref[...] * (1.0 + sc2_ref[0])) + sh2_ref[0]
    h2_ref[0] = h2.astype(BF16)
    h_hi, h_lo = _split2(h2)
    w_hi, w_lo = _split2(wr_ref[...])
    both = _dot(h_hi, jnp.concatenate([w_hi, w_lo], axis=1))
    logits = both[:, :LANES] + both[:, LANES:] + _dot(h_lo, w_hi) + br_ref[...]
    lane = lax.broadcasted_iota(jnp.int32, logits.shape, 1)
    cur = jnp.where(lane < N_EXPERTS, logits, -jnp.inf)
    vals, idxs = [], []
    for _ in range(TOP_K):
        m = jnp.max(cur, axis=-1, keepdims=True)
        ix = jnp.min(jnp.where(cur == m, lane, LANES), axis=-1, keepdims=True)
        vals.append(m)
        idxs.append(ix)
        cur = jnp.where(lane == ix, -jnp.inf, cur)
    es = [jnp.exp(v - vals[0]) for v in vals]
    denom = es[0]
    for e in es[1:]:
        denom = denom + e
    onehot = jnp.zeros(logits.shape, F32)
    for kk in range(TOP_K):
        onehot = onehot + jnp.where(lane == idxs[kk], 1.0, 0.0)
    ts = logits.shape[0]
    row = lax.broadcasted_iota(jnp.int32, (ts, ts), 0)
    col = lax.broadcasted_iota(jnp.int32, (ts, ts), 1)
    before = jnp.where(row > col, 1.0, 0.0).astype(BF16)
    prior = _dot(before, onehot.astype(BF16))
    seen = cnt_scr[...]
    pref_ref[0] = seen
    gate_out = jnp.zeros(logits.shape, F32)
    idx_out = jnp.zeros(logits.shape, jnp.int32)
    local_out = jnp.zeros(logits.shape, F32)
    for kk in range(TOP_K):
        mine = lane == idxs[kk]
        local = jnp.sum(jnp.where(mine, prior, 0.0), axis=-1, keepdims=True)
        rank = local + jnp.sum(jnp.where(mine, seen, 0.0), axis=-1, keepdims=True)
        gate_out = jnp.where(lane == kk, es[kk] / denom, gate_out)
        idx_out = jnp.where(lane == kk, idxs[kk], idx_out)
        idx_out = jnp.where(lane == TOP_K + kk, rank.astype(jnp.int32), idx_out)
        local_out = jnp.where(lane == kk, local, local_out)
    gate_ref[0] = gate_out
    idx_ref[0] = idx_out
    local_rows = local_out.T.astype(jnp.int32)
    tok = lax.broadcasted_iota(jnp.int32, logits.shape, 0)
    tok_hi = (tok // 16).astype(F32)
    tok_lo = (tok % 16).astype(F32)
    ltab = jnp.zeros((ts, 2 * LANES), F32)
    for kk in range(TOP_K):
        mine = lane == idxs[kk]
        at_rank = jnp.where(row == local_rows[kk:kk + 1, :], 1.0, 0.0).astype(BF16)
        tagged = jnp.concatenate([jnp.where(mine, tok_hi, 0.0), jnp.where(mine, tok_lo, 0.0)], axis=1)
        ltab = ltab + _dot(at_rank, tagged.astype(BF16))
    ltab_ref[0] = (16.0 * ltab[:, :LANES] + ltab[:, LANES:]).astype(jnp.int32)
    cnt_scr[...] = seen + jnp.sum(onehot, axis=0, keepdims=True)
    cnt_ref[...] = cnt_scr[...]


def _outproj(y_ssd, y_att, x, g1, sc2, sh2, mla_norm, wo1, wo2, post_mix_norm, pre_ffn_norm, wr, br, *, batch_offset):
    bsz, seq, _ = y_ssd.shape
    d = x.shape[-1]
    ts = min(TS_PROJ, seq)
    tiles = seq // ts

    def tok(width):
        return pl.BlockSpec((1, ts, width), lambda b, i: (b, i, 0))

    x_spec = pl.BlockSpec((1, ts, d), lambda b, i: (batch_offset + b, i, 0))

    def per_batch(width):
        return pl.BlockSpec((1, 1, width), lambda b, i: (b, 0, 0))

    consts = (mla_norm, wo1, wo2, post_mix_norm, pre_ffn_norm, wr, br)
    out_shape = [jax.ShapeDtypeStruct((bsz, seq, d), F32), jax.ShapeDtypeStruct((bsz, seq, d), BF16),
                 jax.ShapeDtypeStruct((bsz, seq, LANES), F32), jax.ShapeDtypeStruct((bsz, seq, LANES), jnp.int32),
                 jax.ShapeDtypeStruct((1, LANES), F32),
                 jax.ShapeDtypeStruct((bsz * tiles, 1, LANES), F32),
                 jax.ShapeDtypeStruct((bsz * tiles, ts, LANES), jnp.int32)]
    return pl.pallas_call(
        _outproj_body,
        grid=(bsz, seq // ts),
        in_specs=[tok(SSD_INNER), tok(MLA_HEADS * MLA_V), x_spec, per_batch(d), per_batch(d), per_batch(d)]
                 + [_const_spec(w.shape) for w in consts],
        out_specs=[tok(d), tok(d), tok(LANES), tok(LANES), _const_spec((1, LANES)),
                   pl.BlockSpec((1, 1, LANES), lambda b, i: (b * tiles + i, 0, 0)),
                   pl.BlockSpec((1, ts, LANES), lambda b, i: (b * tiles + i, 0, 0))],
        out_shape=out_shape,
        scratch_shapes=[pltpu.VMEM((1, LANES), F32)],
        compiler_params=pltpu.CompilerParams(dimension_semantics=("arbitrary", "arbitrary"),
                                             vmem_limit_bytes=VMEM_LIMIT),
        name="outproj",
    )(y_ssd, y_att, x, g1, sc2, sh2, *consts)


def _moe_body(be_ref, na_ref, x_ref, wgu_ref, bgu_ref, wd_ref, bd_ref, y_ref, wgu_b, wd_b):
    i = pl.program_id(0)

    @pl.when((i == 0) | (be_ref[i] != be_ref[jnp.maximum(i - 1, 0)]))
    def _():
        wgu_b[...] = wgu_ref[0].astype(BF16)
        wd_b[...] = wd_ref[0].astype(BF16)

    @pl.when(i < na_ref[0])
    def _():
        gu = _dot(x_ref[...], wgu_b[...]) + bgu_ref[0]
        glu = jnp.minimum(gu[:, :D_FF_EXPERT], SWIGLU_LIMIT)
        lin = jnp.clip(gu[:, D_FF_EXPERT:], -SWIGLU_LIMIT, SWIGLU_LIMIT)
        act = glu * jax.nn.sigmoid(SWIGLU_ALPHA * glu) * (lin + 1.0)
        y_ref[...] = (_dot(act.astype(BF16), wd_b[...]) + bd_ref[0]).astype(BF16)

    @pl.when(i >= na_ref[0])
    def _():
        y_ref[...] = jnp.zeros_like(y_ref)


def _moe(block_expert, n_active, xg, wgu, bgu, wd, bd):
    n_slots, d = xg.shape
    n_blocks = n_slots // MOE_TB
    f2 = wgu.shape[2]
    out_shape = jax.ShapeDtypeStruct((n_slots, d), BF16)
    return pl.pallas_call(
        _moe_body,
        grid_spec=pltpu.PrefetchScalarGridSpec(
            num_scalar_prefetch=2,
            grid=(n_blocks,),
            in_specs=[pl.BlockSpec((MOE_TB, d), lambda i, be, na: (i, 0)),
                      pl.BlockSpec((1, d, f2), lambda i, be, na: (be[i], 0, 0)),
                      pl.BlockSpec((1, 1, f2), lambda i, be, na: (be[i], 0, 0)),
                      pl.BlockSpec((1, f2 // 2, d), lambda i, be, na: (be[i], 0, 0)),
                      pl.BlockSpec((1, 1, d), lambda i, be, na: (be[i], 0, 0))],
            out_specs=pl.BlockSpec((MOE_TB, d), lambda i, be, na: (i, 0)),
            scratch_shapes=[pltpu.VMEM((d, f2), BF16), pltpu.VMEM((f2 // 2, d), BF16)],
        ),
        out_shape=out_shape,
        compiler_params=pltpu.CompilerParams(dimension_semantics=("arbitrary",),
                                             vmem_limit_bytes=VMEM_LIMIT),
        name="moe",
    )(block_expert, n_active, xg, wgu, bgu, wd, bd)


def _final_body(x1_ref, y_ref, gate_ref, g2_ref, gain_ref, *rest):
    o_ref = rest[-1]
    gates = gate_ref[0]
    f = gates[:, 0:1] * y_ref[0, 0].astype(F32)
    for kk in range(1, TOP_K):
        f = f + gates[:, kk:kk + 1] * y_ref[kk, 0].astype(F32)
    o_ref[0] = x1_ref[0] + g2_ref[0] * (_rms(f) * gain_ref[...])


def _final(x1, y4, gates, g2, gain, out_prev, batch_offset, total_batch):
    bsz, seq, d = x1.shape
    ts = min(TS_PROJ, seq)

    def tok(width):
        return pl.BlockSpec((1, ts, width), lambda b, i: (b, i, 0))

    in_specs = [tok(d), pl.BlockSpec((TOP_K, 1, ts, d), lambda b, i: (0, b, i, 0)), tok(LANES),
                pl.BlockSpec((1, 1, d), lambda b, i: (b, 0, 0)), _const_spec((1, d))]
    args = [x1, y4, gates, g2, gain]
    aliases = {}
    if out_prev is not None:
        in_specs.append(pl.BlockSpec(memory_space=pl.ANY))
        args.append(out_prev)
        aliases = {len(args) - 1: 0}
    return pl.pallas_call(
        _final_body,
        grid=(bsz, seq // ts),
        in_specs=in_specs,
        out_specs=pl.BlockSpec((1, ts, d), lambda b, i: (batch_offset + b, i, 0)),
        out_shape=jax.ShapeDtypeStruct((total_batch, seq, d), F32),
        input_output_aliases=aliases,
        compiler_params=pltpu.CompilerParams(dimension_semantics=("parallel", "parallel"),
                                             vmem_limit_bytes=VMEM_LIMIT),
        name="final",
    )(*args)


def _head_blocks(cols):
    out = []
    for c in cols:
        pad = LANES - c.shape[1]
        out.append(jnp.pad(c, ((0, 0), (0, pad))) if pad else c)
    return jnp.concatenate(out, axis=1)


def _prep_mixer_weights(w_in, w_q_up, w_kv_up):
    d = w_in.shape[0]
    wz = w_in[:, OFF_Z:OFF_XBC]
    wxbc = w_in[:, OFF_XBC:OFF_DT]
    wdt = w_in[:, OFF_DT:OFF_QA]
    wqa = w_in[:, OFF_QA:OFF_KVA]
    wkva = w_in[:, OFF_KVA:OFF_KR]
    wkr = w_in[:, OFF_KR:IN_COLS]
    kr_blk = jnp.concatenate([jnp.zeros((d, ROPE_LO), F32), wkr, jnp.zeros((d, LANES - ROPE_LO - MLA_ROPE), F32)], axis=1)
    dt_blk = jnp.pad(wdt, ((0, 0), (0, LANES - SSD_HEADS)))
    wsm = jnp.concatenate([kr_blk, dt_blk], axis=1)
    qh = MLA_NOPE + MLA_ROPE
    scale = math.log2(math.e) / math.sqrt(qh)
    wqup = _head_blocks([w_q_up[:, h * qh:(h + 1) * qh] for h in range(MLA_HEADS)]) * scale
    kvh = MLA_NOPE + MLA_V
    wkup = _head_blocks([w_kv_up[:, h * kvh:h * kvh + MLA_NOPE] for h in range(MLA_HEADS)])
    vcols = []
    for h in range(MLA_HEADS):
        vcols.append(w_kv_up[:, h * kvh + MLA_NOPE:(h + 1) * kvh])
        if h % 2 == 1:
            vcols.append(jnp.zeros((w_kv_up.shape[0], VT_ROWS - LANES), F32))
    wvup = jnp.concatenate(vcols, axis=1).T
    return tuple(w.astype(BF16) for w in (wz, wxbc, wsm, wqa, wkva)) + tuple(w.astype(BF16) for w in (wqup, wkup, wvup))


def _rope_tables(positions):
    inv_freq = ROPE_BASE ** (-(jnp.arange(HALF_ROPE, dtype=F32) * 2.0 / MLA_ROPE))
    angles = positions.astype(F32)[..., None] * inv_freq
    shp = angles.shape[:-1]
    return jnp.concatenate([jnp.zeros(shp + (ROPE_LO,), F32), jnp.cos(angles), jnp.sin(angles),
                            jnp.zeros(shp + (LANES - ROPE_LO - MLA_ROPE,), F32)], axis=-1)


def _route(idx, rank, counts, tile_seen, ltab, n_tok):
    n_assign = n_tok * TOP_K
    n_tiles, tile, lanes = ltab.shape
    padded = ((counts + MOE_TB - 1) // MOE_TB) * MOE_TB
    padded_end = jnp.cumsum(padded)
    padded_start = padded_end - padded
    experts = jnp.arange(N_EXPERTS, dtype=jnp.int32)
    start_of = jnp.sum(jnp.where(idx[..., None] == experts, padded_start, 0), axis=-1)
    dest = (start_of + rank).reshape(-1)
    n_blocks = n_assign // MOE_TB + N_EXPERTS
    block_start = jnp.arange(n_blocks, dtype=jnp.int32) * MOE_TB
    block_expert = jnp.minimum(jnp.sum((padded_end[None, :] <= block_start[:, None]).astype(jnp.int32), axis=1),
                               N_EXPERTS - 1)
    n_active = (padded_end[-1] // MOE_TB).astype(jnp.int32).reshape(1)
    j = (block_start - padded_start[block_expert])[:, None] + jnp.arange(MOE_TB, dtype=jnp.int32)[None, :]
    seen_blk = tile_seen.T[block_expert]
    reached = seen_blk[:, None, :] <= j[:, :, None]
    tau = jnp.sum(reached.astype(jnp.int32), axis=-1) - 1
    j_local = j - jnp.max(jnp.where(reached, seen_blk[:, None, :], 0), axis=-1)
    valid = j < counts[block_expert][:, None]
    flat = (tau * tile + j_local) * lanes + block_expert[:, None]
    tok_local = ltab.reshape(-1)[jnp.clip(flat, 0, n_tiles * tile * lanes - 1)]
    slot_tok = jnp.where(valid, tau * tile + tok_local, 0).reshape(-1)
    return dest, slot_tok, block_expert, n_active


def kernel(x, c, positions, w_ada, b_ada, pre_mix_norm, w_in, conv_w, conv_b, dt_bias, a_log, d_skip, ssd_norm, q_a_norm, w_q_up, kv_a_norm, w_kv_up, mla_norm, w_out, post_mix_norm, pre_ffn_norm, w_router, b_router, w_gate_up, b_gate_up, w_down, b_down, post_ffn_norm):
    bsz, seq, d = x.shape
    rope_tab = _rope_tables(positions)
    n_groups = BATCH_GROUPS if bsz % BATCH_GROUPS == 0 else 1
    gb = bsz // n_groups
    n_tok = gb * seq
    pad_h = LANES - SSD_HEADS
    for l in range(w_ada.shape[0]):
        mod = _adaln(c, w_ada[l], b_ada[l])
        mods = [m.reshape(bsz, 1, d) for m in jnp.split(mod, 6, axis=-1)]
        mixer_w = _prep_mixer_weights(w_in[l], w_q_up[l], w_kv_up[l])
        wo = w_out[l].astype(BF16)
        wr = jnp.pad(w_router[l], ((0, 0), (0, LANES - N_EXPERTS)))
        br = jnp.pad(b_router[l], (0, LANES - N_EXPERTS)).reshape(1, LANES)
        dtb = jnp.pad(dt_bias[l], (0, pad_h)).reshape(1, LANES)
        alog = jnp.pad(a_log[l], (0, pad_h)).reshape(1, LANES)
        dsk = jnp.repeat(d_skip[l], SSD_HEAD_DIM).reshape(1, -1)
        out = None
        prev_slot_tok = None
        prev_xs = None
        for gi in range(n_groups):
            grp = slice(gi * gb, (gi + 1) * gb)
            sh1, sc1, g1, sh2, sc2, g2 = [m[grp] for m in mods]
            if prev_slot_tok is not None:
                sc1 = sc1 + jnp.where(prev_slot_tok[0] < 0, 1.0, 0.0)
            z, xbc, dt, q, k, v = _inproj(x, sc1, sh1, pre_mix_norm[l].reshape(1, d), rope_tab, *mixer_w[:5],
                                          q_a_norm[l].reshape(1, -1), kv_a_norm[l].reshape(1, -1), *mixer_w[5:],
                                          batch_offset=gi * gb)
            y_ssd = _ssd(xbc, z, dt, conv_w[l], conv_b[l].reshape(1, -1), dtb, alog, dsk, ssd_norm[l].reshape(1, -1))
            y_att = _attn(q, k, v)
            x1, h2, gates, route, cnt, seen, ltab = _outproj(
                y_ssd, y_att, x, g1, sc2, sh2, mla_norm[l].reshape(1, -1), wo[:SSD_INNER], wo[SSD_INNER:],
                post_mix_norm[l].reshape(1, d), pre_ffn_norm[l].reshape(1, d), wr, br, batch_offset=gi * gb)
            route = route.reshape(n_tok, LANES)
            counts = cnt[0, :N_EXPERTS].astype(jnp.int32)
            tile_seen = seen[:, 0, :N_EXPERTS].astype(jnp.int32)
            if prev_xs is not None:
                bits = lax.bitcast_convert_type(prev_xs[0, 0], jnp.uint16).astype(jnp.int32)
                tile_seen = tile_seen + jnp.where(bits < 0, 1, 0)
            dest, slot_tok, block_expert, n_active = _route(route[:, :TOP_K], route[:, TOP_K:2 * TOP_K], counts,
                                                            tile_seen, ltab, n_tok)
            prev_slot_tok = slot_tok
            xs = h2.reshape(n_tok, d)[slot_tok]
            prev_xs = xs
            y = _moe(block_expert, n_active, xs, w_gate_up[l], b_gate_up[l][:, None, :], w_down[l], b_down[l][:, None, :])
            y4 = y[dest.reshape(n_tok, TOP_K).T.reshape(-1)].reshape(TOP_K, gb, seq, d)
            out = _final(x1, y4, gates, g2, post_ffn_norm[l].reshape(1, d), out, gi * gb, bsz)
        x = out
    return x
```

```python
import functools
import math

import jax
import jax.numpy as jnp
from jax import lax
from jax.experimental import pallas as pl
from jax.experimental.pallas import tpu as pltpu

F32 = jnp.float32
BF16 = jnp.bfloat16

D_MODEL = 1024
CHUNK = 64
SSD_INNER = 512
SSD_HEAD_DIM = 64
SSD_HEADS = 8
SSD_GROUPS = 2
SSD_STATE = 128
SSD_CONV = 4
SSD_BC = SSD_GROUPS * SSD_STATE
XBC_DIM = SSD_INNER + 2 * SSD_BC
MLA_V = 64
MLA_HEADS = 8
MLA_NOPE = 64
MLA_ROPE = 32
Q_LORA = 384
KV_LORA = 256
ROPE_BASE = 10000.0
OFF_Z = 0
OFF_XBC = OFF_Z + SSD_INNER
OFF_DT = OFF_XBC + XBC_DIM
OFF_QA = OFF_DT + SSD_HEADS
OFF_KVA = OFF_QA + Q_LORA
OFF_KR = OFF_KVA + KV_LORA
IN_COLS = OFF_KR + MLA_ROPE
N_EXPERTS = 32
TOP_K = 4
D_FF_EXPERT = 1024
SWIGLU_LIMIT = 7.0
SWIGLU_ALPHA = 1.702
NORM_EPS = 1e-6

LANES = 128
HALF_ROPE = MLA_ROPE // 2
ROPE_LO = MLA_NOPE
ROPE_HI = MLA_NOPE + HALF_ROPE
VT_ROWS = 144

TS_PROJ = 512
SSD_L = 256
ATT_TQ = 512
ATT_TK = 256
MOE_TB = 512
BATCH_GROUPS = 2
VMEM_LIMIT = 56 * 1024 * 1024


def _dot(a, b):
    return jnp.dot(a, b, preferred_element_type=F32)


def _dot_nt(a, b):
    return lax.dot_general(a, b, (((1,), (1,)), ((), ())), preferred_element_type=F32)


def _dot_tn(a, b):
    return lax.dot_general(a, b, (((0,), (0,)), ((), ())), preferred_element_type=F32)


def _split2(x):
    hi = x.astype(BF16)
    lo = (x - hi.astype(F32)).astype(BF16)
    return hi, lo


def _split3(x):
    h1 = x.astype(BF16)
    r1 = x - h1.astype(F32)
    h2 = r1.astype(BF16)
    h3 = (r1 - h2.astype(F32)).astype(BF16)
    return h1, h2, h3


def _dot3(a, b):
    ah, al = _split2(a)
    bh, bl = _split2(b)
    return _dot(ah, bh) + _dot(ah, bl) + _dot(al, bh)


def _rms(x):
    return x * lax.rsqrt(jnp.mean(x * x, axis=-1, keepdims=True) + NORM_EPS)


def _silu(x):
    hx = 0.5 * x
    return hx + hx * jnp.tanh(hx)


def _const_spec(shape):
    nd = len(shape)
    return pl.BlockSpec(shape, lambda *_: (0,) * nd)


def _adaln_body(c_ref, w_ref, b_ref, o_ref):
    o_ref[...] = _dot3(_silu(c_ref[...]), w_ref[...]) + b_ref[...]


def _adaln(c, w_ada, b_ada):
    bsz, d = c.shape
    n = w_ada.shape[1]
    tn = 1024
    return pl.pallas_call(
        _adaln_body,
        grid=(n // tn,),
        in_specs=[_const_spec((bsz, d)),
                  pl.BlockSpec((d, tn), lambda j: (0, j)),
                  pl.BlockSpec((1, tn), lambda j: (0, j))],
        out_specs=pl.BlockSpec((bsz, tn), lambda j: (0, j)),
        out_shape=jax.ShapeDtypeStruct((bsz, n), F32),
        compiler_params=pltpu.CompilerParams(dimension_semantics=("arbitrary",),
                                             vmem_limit_bytes=VMEM_LIMIT),
        name="adaln",
    )(c, w_ada, b_ada.reshape(1, n))


def _rope_block(xb, ct, st, lane):
    partner = jnp.where(lane < ROPE_HI, pltpu.roll(xb, LANES - HALF_ROPE, 1), pltpu.roll(xb, HALF_ROPE, 1))
    return xb * ct + partner * st


def _inproj_body(x_ref, sc_ref, sh_ref, g_ref, tab_ref, wz_ref, wxbc_ref, wsm_ref, wqa_ref, wkva_ref,
                 qn_ref, kvn_ref, wqup_ref, wkup_ref, wvup_ref,
                 z_ref, xbc_ref, dt_ref, q_ref, k_ref, v_ref):
    x = x_ref[0]
    h = _rms(x) * (g_ref[...] * (1.0 + sc_ref[0])) + sh_ref[0]
    hb = h.astype(BF16)
    z_ref[0] = _dot(hb, wz_ref[...]).astype(BF16)
    xbc_ref[0] = _dot(hb, wxbc_ref[...]).astype(BF16)
    sm = _dot(hb, wsm_ref[...])
    dt_ref[0] = sm[:, LANES:]
    tab = tab_ref[0]
    lane = lax.broadcasted_iota(jnp.int32, tab.shape, 1)
    in_lo = (lane >= ROPE_LO) & (lane < ROPE_HI)
    in_hi = (lane >= ROPE_HI) & (lane < ROPE_HI + HALF_ROPE)
    ct = jnp.where(lane < ROPE_LO, 1.0, jnp.where(in_lo, tab, jnp.where(in_hi, pltpu.roll(tab, HALF_ROPE, 1), 0.0)))
    st = jnp.where(in_lo, -pltpu.roll(tab, LANES - HALF_ROPE, 1), jnp.where(in_hi, tab, 0.0))
    kr =_rope_block(sm[:, :LANES], ct, st, lane)
    qan = (_rms(_dot(hb, wqa_ref[...])) * qn_ref[...]).astype(BF16)
    q = _dot(qan, wqup_ref[...])
    for hh in range(MLA_HEADS):
        blk = slice(hh * LANES, (hh + 1) * LANES)
        q_ref[0, :, blk] = _rope_block(q[:, blk], ct, st, lane).astype(BF16)
    kvn = (_rms(_dot(hb, wkva_ref[...])) * kvn_ref[...]).astype(BF16)
    k = _dot(kvn, wkup_ref[...])
    for hh in range(MLA_HEADS):
        blk = slice(hh * LANES, (hh + 1) * LANES)
        k_ref[0, :, blk] = (k[:, blk] + kr).astype(BF16)
    vt = _dot_nt(wvup_ref[...], kvn)
    vrow = lax.broadcasted_iota(jnp.int32, vt.shape, 0)
    v_ref[0] = jnp.where(vrow % VT_ROWS == LANES, 1.0, vt).astype(BF16)


def _inproj(x, sc1, sh1, gain, rope_tab, wz, wxbc, wsm, wqa, wkva, qn, kvn, wqup, wkup, wvup, *, batch_offset):
    _, seq, d = x.shape
    bsz = sc1.shape[0]
    ts = min(TS_PROJ, seq)
    hw = MLA_HEADS * LANES

    def tok(width):
        return pl.BlockSpec((1, ts, width), lambda b, i: (b, i, 0))

    def tok_full(width):
        return pl.BlockSpec((1, ts, width), lambda b, i: (batch_offset + b, i, 0))

    def per_batch(width):
        return pl.BlockSpec((1, 1, width), lambda b, i: (b, 0, 0))

    weights = (wz, wxbc, wsm, wqa, wkva, qn, kvn, wqup, wkup, wvup)
    out_widths = (SSD_INNER, XBC_DIM, LANES, hw, hw)
    out_dtypes = (BF16, BF16, F32, BF16, BF16)
    vdim = (MLA_HEADS // 2) * VT_ROWS
    out_shape = ([jax.ShapeDtypeStruct((bsz, seq, w), dt) for w, dt in zip(out_widths, out_dtypes)]
                 + [jax.ShapeDtypeStruct((bsz, vdim, seq), BF16)])
    return pl.pallas_call(
        _inproj_body,
        grid=(bsz, seq // ts),
        in_specs=[tok_full(d), per_batch(d), per_batch(d), _const_spec((1, d)), tok_full(LANES)]
                 + [_const_spec(w.shape) for w in weights],
        out_specs=[tok(w) for w in out_widths] + [pl.BlockSpec((1, vdim, ts), lambda b, i: (b, 0, i))],
        out_shape=out_shape,
        compiler_params=pltpu.CompilerParams(dimension_semantics=("parallel", "parallel"),
                                             vmem_limit_bytes=VMEM_LIMIT),
        name="inproj",
    )(x, sc1, sh1, gain, rope_tab, *weights)


CONV_HALO = 16


def _ssd_body(xc_ref, xp_ref, z_ref, dt_ref, cw_ref, cb_ref, dtb_ref, alog_ref, dsk_ref, ng_ref,
              y_ref, st_scr, *, blk):
    i = pl.program_id(1)

    @pl.when(i == 0)
    def _():
        st_scr[...] = jnp.zeros_like(st_scr)

    xc = xc_ref[0]
    row = lax.broadcasted_iota(jnp.int32, (blk, blk), 0)
    col = lax.broadcasted_iota(jnp.int32, (blk, blk), 1)
    conv = cb_ref[...] + cw_ref[SSD_CONV - 1:SSD_CONV, :] * xc.astype(F32)
    tail = jnp.where(i > 0, xp_ref[0].astype(F32)[CONV_HALO - 8:], 0.0)
    row8 = lax.broadcasted_iota(jnp.int32, tail.shape, 0)
    head_fix = jnp.zeros(tail.shape, F32)
    for shift in range(1, SSD_CONV):
        w = cw_ref[SSD_CONV - 1 - shift:SSD_CONV - shift, :]
        shifted = _dot(jnp.where(row - col == shift, 1.0, 0.0).astype(BF16), xc)
        conv = conv + w * shifted
        head_fix = head_fix + jnp.where(row8 < shift, w * pltpu.roll(tail, shift, 0), 0.0)
    conv = jnp.concatenate([conv[:8] + head_fix, conv[8:]], axis=0)
    xa = _silu(conv)
    xs = xa[:, :SSD_INNER]
    bm = xa[:, SSD_INNER:SSD_INNER + SSD_BC]
    cm = xa[:, SSD_INNER + SSD_BC:]

    hl = lax.broadcasted_iota(jnp.int32, (1, LANES), 1)
    dtr = dt_ref[0] + dtb_ref[...]
    dt = jnp.maximum(dtr, 0.0) + jnp.log(1.0 + jnp.exp(-jnp.abs(dtr)))
    a = jnp.where(hl < SSD_HEADS, -jnp.exp(alog_ref[...]), 0.0)
    dta = dt * a
    row = lax.broadcasted_iota(jnp.int32, (blk, blk), 0)
    col = lax.broadcasted_iota(jnp.int32, (blk, blk), 1)
    tril = row >= col
    trilb = jnp.where(tril, 1.0, 0.0).astype(BF16)
    d1, d2, d3 = _split3(dta)
    cs = _dot(trilb, d1) + _dot(trilb, d2) + _dot(trilb, d3)
    cs_last = cs[blk - 1:blk, :]
    ecs = jnp.exp(cs)
    dte = jnp.exp(cs_last - cs)
    cs_t = cs.T

    er = lax.broadcasted_iota(jnp.int32, (LANES, SSD_INNER), 0)
    ec = lax.broadcasted_iota(jnp.int32, (LANES, SSD_INNER), 1)
    expand = jnp.where(ec // SSD_HEAD_DIM == er, 1.0, 0.0).astype(BF16)

    def per_channel(v):
        vh, vl = _split2(v)
        return _dot(vh, expand) + _dot(vl, expand)

    dt_e = per_channel(dt)
    ecs_e = per_channel(ecs)
    dte_e = per_channel(dte)
    xdt = xs * dt_e
    xdt_b = xdt.astype(BF16)
    xw_b = (xdt * dte_e).astype(BF16)

    gw = SSD_INNER // SSD_GROUPS
    heads_per_group = SSD_HEADS // SSD_GROUPS
    lane = lax.broadcasted_iota(jnp.int32, (blk, LANES), 1)
    y_groups = []
    for g in range(SSD_GROUPS):
        bg = bm[:, g * SSD_STATE:(g + 1) * SSD_STATE].astype(BF16)
        cg = cm[:, g * SSD_STATE:(g + 1) * SSD_STATE].astype(BF16)
        cb = _dot_nt(cg, bg)
        state = st_scr[g]
        y_off = _dot(cg, state.astype(BF16))
        pairs = []
        for j in range(heads_per_group // 2):
            xp = xdt_b[:, g * gw + j * LANES:g * gw + (j + 1) * LANES]
            halves = []
            for u in range(2):
                hidx = g * heads_per_group + 2 * j + u
                seg = cs[:, hidx:hidx + 1] - cs_t[hidx:hidx + 1, :]
                dec = jnp.exp(jnp.where(tril, seg, -jnp.inf))
                halves.append(_dot((cb * dec).astype(BF16), xp))
            pairs.append(jnp.where(lane < SSD_HEAD_DIM, halves[0], halves[1]))
        y_diag = jnp.concatenate(pairs, axis=1)
        y_groups.append(y_diag + y_off * ecs_e[:, g * gw:(g + 1) * gw])
        st_scr[g] = (state * ecs_e[blk - 1:blk, g * gw:(g + 1) * gw]
                     + _dot_tn(bg, xw_b[:, g * gw:(g + 1) * gw]))
    y = jnp.concatenate(y_groups, axis=1) + xs * dsk_ref[...]
    y = y * _silu(z_ref[0].astype(F32))
    y = jnp.concatenate([_rms(y[:, g * gw:(g + 1) * gw]) for g in range(SSD_GROUPS)], axis=1)
    y_ref[0] = (y * ng_ref[...]).astype(BF16)


def _ssd(xbc, z, dt, conv_w, conv_b, dt_bias, a_log, d_skip_e, norm_gain):
    bsz, seq, _ = xbc.shape
    blk = min(SSD_L, seq)
    halo_per_blk = blk // CONV_HALO
    body = functools.partial(_ssd_body, blk=blk)
    out_shape = jax.ShapeDtypeStruct((bsz, seq, SSD_INNER), BF16)
    return pl.pallas_call(
        body,
        grid=(bsz, seq // blk),
        in_specs=[pl.BlockSpec((1, blk, XBC_DIM), lambda b, i: (b, i, 0)),
                  pl.BlockSpec((1, CONV_HALO, XBC_DIM), lambda b, i: (b, jnp.maximum(i * halo_per_blk - 1, 0), 0)),
                  pl.BlockSpec((1, blk, SSD_INNER), lambda b, i: (b, i, 0)),
                  pl.BlockSpec((1, blk, LANES), lambda b, i: (b, i, 0)),
                  _const_spec((SSD_CONV, XBC_DIM)), _const_spec((1, XBC_DIM)),
                  _const_spec((1, LANES)), _const_spec((1, LANES)),
                  _const_spec((1, SSD_INNER)), _const_spec((1, SSD_INNER))],
        out_specs=pl.BlockSpec((1, blk, SSD_INNER), lambda b, i: (b, i, 0)),
        out_shape=out_shape,
        scratch_shapes=[pltpu.VMEM((SSD_GROUPS, SSD_STATE, SSD_INNER // SSD_GROUPS), F32)],
        compiler_params=pltpu.CompilerParams(dimension_semantics=("parallel", "arbitrary"),
                                             vmem_limit_bytes=VMEM_LIMIT),
        name="ssd",
    )(xbc, xbc, z, dt, conv_w, conv_b, dt_bias, a_log, d_skip_e, norm_gain)


def _attn_body(q_ref, k_ref, v_ref, o_ref, *scratch, tq, tk):
    n_streams = 2 * (tq // tk)
    s_scr = (scratch[:n_streams], scratch[n_streams:2 * n_streams])
    acc_scr = scratch[2 * n_streams:]
    qi = pl.program_id(2)
    n_sub = tq // tk
    n_full = qi * n_sub
    krow = lax.broadcasted_iota(jnp.int32, (tk, tk), 0)
    qcol = lax.broadcasted_iota(jnp.int32, (tk, tk), 1)
    diag_ok = krow // CHUNK <= qcol // CHUNK
    vrow = lax.broadcasted_iota(jnp.int32, (LANES, tk), 0)
    streams = [(u, r) for u in range(2) for r in range(n_sub)]
    qs = [q_ref[0, r * tk:(r + 1) * tk, u * LANES:(u + 1) * LANES] for u, r in streams]

    def put_scores(ki, which, slot):
        start = pl.multiple_of(ki * tk, tk)
        k2 = k_ref[0, pl.ds(start, tk), :]
        out = {}
        for si in which:
            u = streams[si][0]
            s = _dot_nt(k2[:, u * LANES:(u + 1) * LANES], qs[si])
            s_scr[slot][si][...] = s
            out[si] = jnp.max(s, axis=0, keepdims=True)
        return out

    def values_t(ki):
        return v_ref[0, :, pl.ds(pl.multiple_of(ki * tk, tk), tk)]

    def softmax_pv(si, m, s_max, slot, vt, masked):
        s = s_scr[slot][si][...]
        if masked:
            s = jnp.where(diag_ok, s, -jnp.inf)
            s_max = jnp.max(s, axis=0, keepdims=True)
        m_new = jnp.maximum(m, s_max)
        alpha = jnp.exp2(m - m_new)
        p = jnp.exp2((s - m_new).astype(BF16))
        acc_scr[si][...] = alpha * acc_scr[si][...] + _dot(vt, p)
        return m_new

    every = list(range(len(streams)))
    for ref in acc_scr:
        ref[...] = jnp.zeros_like(ref)

    def step(j, state):
        ms, s_maxes = state
        for slot in range(2):
            ki = 2 * j + slot
            nxt = put_scores(ki + 1, every, 1 - slot)
            vt = values_t(ki)
            ms = tuple(softmax_pv(si, ms[si], s_maxes[si], slot, vt, False) for si in every)
            s_maxes = tuple(nxt[si] for si in every)
        return ms, s_maxes

    first = put_scores(0, every, 0)
    m_init = jnp.full((1, tk), -jnp.inf, F32)
    ms, s_maxes = lax.fori_loop(0, n_full // 2, step, ((m_init,) * len(streams), tuple(first[si] for si in every)))
    ms = list(ms)
    s_maxes = dict(zip(every, s_maxes))
    for dd in range(n_sub):
        slot = dd % 2
        live = [si for si in every if dd <= streams[si][1]]
        later = [si for si in every if dd + 1 <= streams[si][1]]
        nxt = put_scores(n_full + dd + 1, later, 1 - slot) if later else {}
        vt = values_t(n_full + dd)
        for si in live:
            ms[si] = softmax_pv(si, ms[si], s_maxes[si], slot, vt, dd == streams[si][1])
        s_maxes = nxt
    for r in range(n_sub):
        a0 = acc_scr[streams.index((0, r))][...]
        a1 = acc_scr[streams.index((1, r))][...]
        out_t = jnp.where(vrow < MLA_V, a0[:LANES] / a0[LANES:LANES + 1], a1[:LANES] / a1[LANES:LANES + 1])
        o_ref[0, r * tk:(r + 1) * tk, :] = out_t.T.astype(BF16)


def _attn(q, k, v):
    bsz, seq, _ = q.shape
    tk = min(ATT_TK, seq)
    tq = min(ATT_TQ, seq)
    assert (tq // tk) % 2 == 0, "the two-slot score pipeline needs an even number of query sub-tiles"
    n_streams = 2 * (tq // tk)
    body = functools.partial(_attn_body, tq=tq, tk=tk)
    out_shape = jax.ShapeDtypeStruct((bsz, seq, MLA_HEADS * MLA_V), BF16)
    return pl.pallas_call(
        body,
        grid=(bsz, MLA_HEADS // 2, seq // tq),
        in_specs=[pl.BlockSpec((1, tq, 2 * LANES), lambda b, hp, i: (b, i, hp)),
                  pl.BlockSpec((1, seq, 2 * LANES), lambda b, hp, i: (b, 0, hp)),
                  pl.BlockSpec((1, VT_ROWS, seq), lambda b, hp, i: (b, hp, 0))],
        out_specs=pl.BlockSpec((1, tq, LANES), lambda b, hp, i: (b, i, hp)),
        out_shape=out_shape,
        scratch_shapes=[pltpu.VMEM((tk, tk), F32)] * (2 * n_streams) + [pltpu.VMEM((VT_ROWS, tk), F32)] * n_streams,
        compiler_params=pltpu.CompilerParams(dimension_semantics=("parallel", "parallel", "arbitrary"),
                                             vmem_limit_bytes=VMEM_LIMIT),
        name="attn",
    )(q, k, v)


def _outproj_body(ys_ref, ya_ref, x_ref, g1_ref, sc2_ref, sh2_ref, mn_ref, wo1_ref, wo2_ref, pmn_ref, pfn_ref,
                  wr_ref, br_ref, x1_ref, h2_ref, gate_ref, idx_ref, cnt_ref, pref_ref, ltab_ref, cnt_scr):
    first = (pl.program_id(0) == 0) & (pl.program_id(1) == 0)

    @pl.when(first)
    def _():
        cnt_scr[...] = jnp.zeros_like(cnt_scr)

    yan =(_rms(ya_ref[0].astype(F32)) * mn_ref[...]).astype(BF16)
    mix = _dot(ys_ref[0], wo1_ref[...]) + _dot(yan, wo2_ref[...])
    x1 = x_ref[0] + g1_ref[0] * (_rms(mix) * pmn_ref[...])
    x1_ref[0] = x1
    h2 = _rms(x1) * (pfn_ref[...] * (1.0 + sc2_ref[0])) + sh2_ref[0]
    h2_ref[0] = h2.astype(BF16)
    h_hi, h_lo = _split2(h2)
    w_hi, w_lo = _split2(wr_ref[...])
    both = _dot(h_hi, jnp.concatenate([w_hi, w_lo], axis=1))
    logits = both[:, :LANES] + both[:, LANES:] + _dot(h_lo, w_hi) + br_ref[...]
    lane = lax.broadcasted_iota(jnp.int32, logits.shape, 1)
    cur = jnp.where(lane < N_EXPERTS, logits, -jnp.inf)
    vals, idxs = [], []
    for _ in range(TOP_K):
        m = jnp.max(cur, axis=-1, keepdims=True)
        ix = jnp.min(jnp.where(cur == m, lane, LANES), axis=-1, keepdims=True)
        vals.append(m)
        idxs.append(ix)
        cur = jnp.where(lane == ix, -jnp.inf, cur)
    es = [jnp.exp(v - vals[0]) for v in vals]
    denom = es[0]
    for e in es[1:]:
        denom = denom + e
    onehot = jnp.zeros(logits.shape, F32)
    for kk in range(TOP_K):
        onehot = onehot + jnp.where(lane == idxs[kk], 1.0, 0.0)
    ts = logits.shape[0]
    row = lax.broadcasted_iota(jnp.int32, (ts, ts), 0)
    col = lax.broadcasted_iota(jnp.int32, (ts, ts), 1)
    before = jnp.where(row > col, 1.0, 0.0).astype(BF16)
    prior = _dot(before, onehot.astype(BF16))
    seen = cnt_scr[...]
    pref_ref[0] = seen
    gate_out = jnp.zeros(logits.shape, F32)
    idx_out = jnp.zeros(logits.shape, jnp.int32)
    local_out = jnp.zeros(logits.shape, F32)
    for kk in range(TOP_K):
        mine = lane == idxs[kk]
        local = jnp.sum(jnp.where(mine, prior, 0.0), axis=-1, keepdims=True)
        rank = local + jnp.sum(jnp.where(mine, seen, 0.0), axis=-1, keepdims=True)
        gate_out = jnp.where(lane == kk, es[kk] / denom, gate_out)
        idx_out = jnp.where(lane == kk, idxs[kk], idx_out)
        idx_out = jnp.where(lane == TOP_K + kk, rank.astype(jnp.int32), idx_out)
        local_out = jnp.where(lane == kk, local, local_out)
    gate_ref[0] = gate_out
    idx_ref[0] = idx_out
    local_rows = local_out.T.astype(jnp.int32)
    tok = lax.broadcasted_iota(jnp.int32, logits.shape, 0)
    tok_hi = (tok // 16).astype(F32)
    tok_lo = (tok % 16).astype(F32)
    ltab = jnp.zeros((ts, 2 * LANES), F32)
    for kk in range(TOP_K):
        mine = lane == idxs[kk]
        at_rank = jnp.where(row == local_rows[kk:kk + 1, :], 1.0, 0.0).astype(BF16)
        tagged = jnp.concatenate([jnp.where(mine, tok_hi, 0.0), jnp.where(mine, tok_lo, 0.0)], axis=1)
        ltab = ltab + _dot(at_rank, tagged.astype(BF16))
    ltab_ref[0] = (16.0 * ltab[:, :LANES] + ltab[:, LANES:]).astype(jnp.int32)
    cnt_scr[...] = seen + jnp.sum(onehot, axis=0, keepdims=True)
    cnt_ref[...] = cnt_scr[...]


def _outproj(y_ssd, y_att, x, g1, sc2, sh2, mla_norm, wo1, wo2, post_mix_norm, pre_ffn_norm, wr, br, *, batch_offset):
    bsz, seq, _ = y_ssd.shape
    d = x.shape[-1]
    ts = min(TS_PROJ, seq)
    tiles = seq // ts

    def tok(width):
        return pl.BlockSpec((1, ts, width), lambda b, i: (b, i, 0))

    x_spec = pl.BlockSpec((1, ts, d), lambda b, i: (batch_offset + b, i, 0))

    def per_batch(width):
        return pl.BlockSpec((1, 1, width), lambda b, i: (b, 0, 0))

    consts = (mla_norm, wo1, wo2, post_mix_norm, pre_ffn_norm, wr, br)
    out_shape = [jax.ShapeDtypeStruct((bsz, seq, d), F32), jax.ShapeDtypeStruct((bsz, seq, d), BF16),
                 jax.ShapeDtypeStruct((bsz, seq, LANES), F32), jax.ShapeDtypeStruct((bsz, seq, LANES), jnp.int32),
                 jax.ShapeDtypeStruct((1, LANES), F32),
                 jax.ShapeDtypeStruct((bsz * tiles, 1, LANES), F32),
                 jax.ShapeDtypeStruct((bsz * tiles, ts, LANES), jnp.int32)]
    return pl.pallas_call(
        _outproj_body,
        grid=(bsz, seq // ts),
        in_specs=[tok(SSD_INNER), tok(MLA_HEADS * MLA_V), x_spec, per_batch(d), per_batch(d), per_batch(d)]
                 + [_const_spec(w.shape) for w in consts],
        out_specs=[tok(d), tok(d), tok(LANES), tok(LANES), _const_spec((1, LANES)),
                   pl.BlockSpec((1, 1, LANES), lambda b, i: (b * tiles + i, 0, 0)),
                   pl.BlockSpec((1, ts, LANES), lambda b, i: (b * tiles + i, 0, 0))],
        out_shape=out_shape,
        scratch_shapes=[pltpu.VMEM((1, LANES), F32)],
        compiler_params=pltpu.CompilerParams(dimension_semantics=("arbitrary", "arbitrary"),
                                             vmem_limit_bytes=VMEM_LIMIT),
        name="outproj",
    )(y_ssd, y_att, x, g1, sc2, sh2, *consts)


def _moe_body(be_ref, na_ref, x_ref, wgu_ref, bgu_ref, wd_ref, bd_ref, y_ref, wgu_b, wd_b):
    i = pl.program_id(0)

    @pl.when((i == 0) | (be_ref[i] != be_ref[jnp.maximum(i - 1, 0)]))
    def _():
        wgu_b[...] = wgu_ref[0].astype(BF16)
        wd_b[...] = wd_ref[0].astype(BF16)

    @pl.when(i < na_ref[0])
    def _():
        gu = _dot(x_ref[...], wgu_b[...]) + bgu_ref[0]
        glu = jnp.minimum(gu[:, :D_FF_EXPERT], SWIGLU_LIMIT)
        lin = jnp.clip(gu[:, D_FF_EXPERT:], -SWIGLU_LIMIT, SWIGLU_LIMIT)
        act = glu * jax.nn.sigmoid(SWIGLU_ALPHA * glu) * (lin + 1.0)
        y_ref[...] = (_dot(act.astype(BF16), wd_b[...]) + bd_ref[0]).astype(BF16)

    @pl.when(i >= na_ref[0])
    def _():
        y_ref[...] = jnp.zeros_like(y_ref)


def _moe(block_expert, n_active, xg, wgu, bgu, wd, bd):
    n_slots, d = xg.shape
    n_blocks = n_slots // MOE_TB
    f2 = wgu.shape[2]
    out_shape = jax.ShapeDtypeStruct((n_slots, d), BF16)
    return pl.pallas_call(
        _moe_body,
        grid_spec=pltpu.PrefetchScalarGridSpec(
            num_scalar_prefetch=2,
            grid=(n_blocks,),
            in_specs=[pl.BlockSpec((MOE_TB, d), lambda i, be, na: (i, 0)),
                      pl.BlockSpec((1, d, f2), lambda i, be, na: (be[i], 0, 0)),
                      pl.BlockSpec((1, 1, f2), lambda i, be, na: (be[i], 0, 0)),
                      pl.BlockSpec((1, f2 // 2, d), lambda i, be, na: (be[i], 0, 0)),
                      pl.BlockSpec((1, 1, d), lambda i, be, na: (be[i], 0, 0))],
            out_specs=pl.BlockSpec((MOE_TB, d), lambda i, be, na: (i, 0)),
            scratch_shapes=[pltpu.VMEM((d, f2), BF16), pltpu.VMEM((f2 // 2, d), BF16)],
        ),
        out_shape=out_shape,
        compiler_params=pltpu.CompilerParams(dimension_semantics=("arbitrary",),
                                             vmem_limit_bytes=VMEM_LIMIT),
        name="moe",
    )(block_expert, n_active, xg, wgu, bgu, wd, bd)


def _final_body(x1_ref, y_ref, gate_ref, g2_ref, gain_ref, *rest):
    o_ref = rest[-1]
    gates = gate_ref[0]
    f = gates[:, 0:1] * y_ref[0, 0].astype(F32)
    for kk in range(1, TOP_K):
        f = f + gates[:, kk:kk + 1] * y_ref[kk, 0].astype(F32)
    o_ref[0] = x1_ref[0] + g2_ref[0] * (_rms(f) * gain_ref[...])


def _final(x1, y4, gates, g2, gain, out_prev, batch_offset, total_batch):
    bsz, seq, d = x1.shape
    ts = min(TS_PROJ, seq)

    def tok(width):
        return pl.BlockSpec((1, ts, width), lambda b, i: (b, i, 0))

    in_specs = [tok(d), pl.BlockSpec((TOP_K, 1, ts, d), lambda b, i: (0, b, i, 0)), tok(LANES),
                pl.BlockSpec((1, 1, d), lambda b, i: (b, 0, 0)), _const_spec((1, d))]
    args = [x1, y4, gates, g2, gain]
    aliases = {}
    if out_prev is not None:
        in_specs.append(pl.BlockSpec(memory_space=pl.ANY))
        args.append(out_prev)
        aliases = {len(args) - 1: 0}
    return pl.pallas_call(
        _final_body,
        grid=(bsz, seq // ts),
        in_specs=in_specs,
        out_specs=pl.BlockSpec((1, ts, d), lambda b, i: (batch_offset + b, i, 0)),
        out_shape=jax.ShapeDtypeStruct((total_batch, seq, d), F32),
        input_output_aliases=aliases,
        compiler_params=pltpu.CompilerParams(dimension_semantics=("parallel", "parallel"),
                                             vmem_limit_bytes=VMEM_LIMIT),
        name="final",
    )(*args)


def _head_blocks(cols):
    out = []
    for c in cols:
        pad = LANES - c.shape[1]
        out.append(jnp.pad(c, ((0, 0), (0, pad))) if pad else c)
    return jnp.concatenate(out, axis=1)


def _prep_mixer_weights(w_in, w_q_up, w_kv_up):
    d = w_in.shape[0]
    wz = w_in[:, OFF_Z:OFF_XBC]
    wxbc = w_in[:, OFF_XBC:OFF_DT]
    wdt = w_in[:, OFF_DT:OFF_QA]
    wqa = w_in[:, OFF_QA:OFF_KVA]
    wkva = w_in[:, OFF_KVA:OFF_KR]
    wkr = w_in[:, OFF_KR:IN_COLS]
    kr_blk = jnp.concatenate([jnp.zeros((d, ROPE_LO), F32), wkr, jnp.zeros((d, LANES - ROPE_LO - MLA_ROPE), F32)], axis=1)
    dt_blk = jnp.pad(wdt, ((0, 0), (0, LANES - SSD_HEADS)))
    wsm = jnp.concatenate([kr_blk, dt_blk], axis=1)
    qh = MLA_NOPE + MLA_ROPE
    scale = math.log2(math.e) / math.sqrt(qh)
    wqup = _head_blocks([w_q_up[:, h * qh:(h + 1) * qh] for h in range(MLA_HEADS)]) * scale
    kvh = MLA_NOPE + MLA_V
    wkup = _head_blocks([w_kv_up[:, h * kvh:h * kvh + MLA_NOPE] for h in range(MLA_HEADS)])
    vcols = []
    for h in range(MLA_HEADS):
        vcols.append(w_kv_up[:, h * kvh + MLA_NOPE:(h + 1) * kvh])
        if h % 2 == 1:
            vcols.append(jnp.zeros((w_kv_up.shape[0], VT_ROWS - LANES), F32))
    wvup = jnp.concatenate(vcols, axis=1).T
    return tuple(w.astype(BF16) for w in (wz, wxbc, wsm, wqa, wkva)) + tuple(w.astype(BF16) for w in (wqup, wkup, wvup))


def _rope_tables(positions):
    inv_freq = ROPE_BASE ** (-(jnp.arange(HALF_ROPE, dtype=F32) * 2.0 / MLA_ROPE))
    angles = positions.astype(F32)[..., None] * inv_freq
    shp = angles.shape[:-1]
    return jnp.concatenate([jnp.zeros(shp + (ROPE_LO,), F32), jnp.cos(angles), jnp.sin(angles),
                            jnp.zeros(shp + (LANES - ROPE_LO - MLA_ROPE,), F32)], axis=-1)


def _route(idx, rank, counts, tile_seen, ltab, n_tok):
    n_assign = n_tok * TOP_K
    n_tiles, tile, lanes = ltab.shape
    padded = ((counts + MOE_TB - 1) // MOE_TB) * MOE_TB
    padded_end = jnp.cumsum(padded)
    padded_start = padded_end - padded
    experts = jnp.arange(N_EXPERTS, dtype=jnp.int32)
    start_of = jnp.sum(jnp.where(idx[..., None] == experts, padded_start, 0), axis=-1)
    dest = (start_of + rank).reshape(-1)
    n_blocks = n_assign // MOE_TB + N_EXPERTS
    block_start = jnp.arange(n_blocks, dtype=jnp.int32) * MOE_TB
    block_expert = jnp.minimum(jnp.sum((padded_end[None, :] <= block_start[:, None]).astype(jnp.int32), axis=1),
                               N_EXPERTS - 1)
    n_active = (padded_end[-1] // MOE_TB).astype(jnp.int32).reshape(1)
    j = (block_start - padded_start[block_expert])[:, None] + jnp.arange(MOE_TB, dtype=jnp.int32)[None, :]
    seen_blk = tile_seen.T[block_expert]
    reached = seen_blk[:, None, :] <= j[:, :, None]
    tau = jnp.sum(reached.astype(jnp.int32), axis=-1) - 1
    j_local = j - jnp.max(jnp.where(reached, seen_blk[:, None, :], 0), axis=-1)
    valid = j < counts[block_expert][:, None]
    flat = (tau * tile + j_local) * lanes + block_expert[:, None]
    tok_local = ltab.reshape(-1)[jnp.clip(flat, 0, n_tiles * tile * lanes - 1)]
    slot_tok = jnp.where(valid, tau * tile + tok_local, 0).reshape(-1)
    return dest, slot_tok, block_expert, n_active


def kernel(x, c, positions, w_ada, b_ada, pre_mix_norm, w_in, conv_w, conv_b, dt_bias, a_log, d_skip, ssd_norm, q_a_norm, w_q_up, kv_a_norm, w_kv_up, mla_norm, w_out, post_mix_norm, pre_ffn_norm, w_router, b_router, w_gate_up, b_gate_up, w_down, b_down, post_ffn_norm):
    bsz, seq, d = x.shape
    rope_tab = _rope_tables(positions)
    n_groups = BATCH_GROUPS if bsz % BATCH_GROUPS == 0 else 1
    gb = bsz // n_groups
    n_tok = gb * seq
    pad_h = LANES - SSD_HEADS
    for l in range(w_ada.shape[0]):
        mod = _adaln(c, w_ada[l], b_ada[l])
        mods = [m.reshape(bsz, 1, d) for m in jnp.split(mod, 6, axis=-1)]
        mixer_w = _prep_mixer_weights(w_in[l], w_q_up[l], w_kv_up[l])
        wo = w_out[l].astype(BF16)
        wr = jnp.pad(w_router[l], ((0, 0), (0, LANES - N_EXPERTS)))
        br = jnp.pad(b_router[l], (0, LANES - N_EXPERTS)).reshape(1, LANES)
        dtb = jnp.pad(dt_bias[l], (0, pad_h)).reshape(1, LANES)
        alog = jnp.pad(a_log[l], (0, pad_h)).reshape(1, LANES)
        dsk = jnp.repeat(d_skip[l], SSD_HEAD_DIM).reshape(1, -1)
        out = None
        pending = None

        def experts_and_final(p, xs, out):
            y = _moe(p["block_expert"], p["n_active"], xs, w_gate_up[l], b_gate_up[l][:, None, :],
                     w_down[l], b_down[l][:, None, :])
            y4 = y[p["dest"].reshape(n_tok, TOP_K).T.reshape(-1)].reshape(TOP_K, gb, seq, d)
            return _final(p["x1"], y4, p["gates"], p["g2"], post_ffn_norm[l].reshape(1, d), out, p["offset"], bsz)

        for gi in range(n_groups):
            grp = slice(gi * gb, (gi + 1) * gb)
            sh1, sc1, g1, sh2, sc2, g2 = [m[grp] for m in mods]
            if pending is not None:
                sc1 = sc1 + jnp.where(pending["slot_tok"][0] < 0, 1.0, 0.0)
            z, xbc, dt, q, k, v = _inproj(x, sc1, sh1, pre_mix_norm[l].reshape(1, d), rope_tab, *mixer_w[:5],
                                          q_a_norm[l].reshape(1, -1), kv_a_norm[l].reshape(1, -1), *mixer_w[5:],
                                          batch_offset=gi * gb)
            y_ssd = _ssd(xbc, z, dt, conv_w[l], conv_b[l].reshape(1, -1), dtb, alog, dsk, ssd_norm[l].reshape(1, -1))
            prev_xs = None
            if pending is not None:
                ssd_bits = lax.bitcast_convert_type(y_ssd[0, 0, 0], jnp.uint16).astype(jnp.int32)
                prev_xs = pending["h2"].reshape(n_tok, d)[pending["slot_tok"] + jnp.where(ssd_bits < 0, 1, 0)]
            y_att = _attn(q, k, v)
            x1, h2, gates, route, cnt, seen, ltab = _outproj(
                y_ssd, y_att, x, g1, sc2, sh2, mla_norm[l].reshape(1, -1), wo[:SSD_INNER], wo[SSD_INNER:],
                post_mix_norm[l].reshape(1, d), pre_ffn_norm[l].reshape(1, d), wr, br, batch_offset=gi * gb)
            route = route.reshape(n_tok, LANES)
            counts = cnt[0, :N_EXPERTS].astype(jnp.int32)
            tile_seen = seen[:, 0, :N_EXPERTS].astype(jnp.int32)
            if prev_xs is not None:
                bits = lax.bitcast_convert_type(prev_xs[0, 0], jnp.uint16).astype(jnp.int32)
                tile_seen = tile_seen + jnp.where(bits < 0, 1, 0)
            dest, slot_tok, block_expert, n_active = _route(route[:, :TOP_K], route[:, TOP_K:2 * TOP_K], counts,
                                                            tile_seen, ltab, n_tok)
            if pending is not None:
                out = experts_and_final(pending, prev_xs, out)
            pending = dict(h2=h2, slot_tok=slot_tok, dest=dest, block_expert=block_expert, n_active=n_active,
                           x1=x1, gates=gates, g2=g2, offset=gi * gb)
        out = experts_and_final(pending, pending["h2"].reshape(n_tok, d)[pending["slot_tok"]], out)
        x = out
    return x
```

```python
import functools
import math

import jax
import jax.numpy as jnp
from jax import lax
from jax.experimental import pallas as pl
from jax.experimental.pallas import tpu as pltpu

F32 = jnp.float32
BF16 = jnp.bfloat16

D_MODEL = 1024
CHUNK = 64
SSD_INNER = 512
SSD_HEAD_DIM = 64
SSD_HEADS = 8
SSD_GROUPS = 2
SSD_STATE = 128
SSD_CONV = 4
SSD_BC = SSD_GROUPS * SSD_STATE
XBC_DIM = SSD_INNER + 2 * SSD_BC
MLA_V = 64
MLA_HEADS = 8
MLA_NOPE = 64
MLA_ROPE = 32
Q_LORA = 384
KV_LORA = 256
ROPE_BASE = 10000.0
OFF_Z = 0
OFF_XBC = OFF_Z + SSD_INNER
OFF_DT = OFF_XBC + XBC_DIM
OFF_QA = OFF_DT + SSD_HEADS
OFF_KVA = OFF_QA + Q_LORA
OFF_KR = OFF_KVA + KV_LORA
IN_COLS = OFF_KR + MLA_ROPE
N_EXPERTS = 32
TOP_K = 4
D_FF_EXPERT = 1024
SWIGLU_LIMIT = 7.0
SWIGLU_ALPHA = 1.702
NORM_EPS = 1e-6

LANES = 128
HALF_ROPE = MLA_ROPE // 2
ROPE_LO = MLA_NOPE
ROPE_HI = MLA_NOPE + HALF_ROPE
VT_ROWS = 144

TS_PROJ = 512
SSD_L = 256
ATT_TQ = 1024
ATT_TK = 256
MOE_TB = 512
BATCH_GROUPS = 2
VMEM_LIMIT = 56 * 1024 * 1024


def _dot(a, b):
    return jnp.dot(a, b, preferred_element_type=F32)


def _dot_nt(a, b):
    return lax.dot_general(a, b, (((1,), (1,)), ((), ())), preferred_element_type=F32)


def _dot_tn(a, b):
    return lax.dot_general(a, b, (((0,), (0,)), ((), ())), preferred_element_type=F32)


def _split2(x):
    hi = x.astype(BF16)
    lo = (x - hi.astype(F32)).astype(BF16)
    return hi, lo


def _split3(x):
    h1 = x.astype(BF16)
    r1 = x - h1.astype(F32)
    h2 = r1.astype(BF16)
    h3 = (r1 - h2.astype(F32)).astype(BF16)
    return h1, h2, h3


def _dot3(a, b):
    ah, al = _split2(a)
    bh, bl = _split2(b)
    return _dot(ah, bh) + _dot(ah, bl) + _dot(al, bh)


def _rms(x):
    return x * lax.rsqrt(jnp.mean(x * x, axis=-1, keepdims=True) + NORM_EPS)


def _silu(x):
    hx = 0.5 * x
    return hx + hx * jnp.tanh(hx)


def _const_spec(shape):
    nd = len(shape)
    return pl.BlockSpec(shape, lambda *_: (0,) * nd)


def _adaln_body(c_ref, w_ref, b_ref, o_ref):
    o_ref[...] = _dot3(_silu(c_ref[...]), w_ref[...]) + b_ref[...]


def _adaln(c, w_ada, b_ada):
    bsz, d = c.shape
    n = w_ada.shape[1]
    tn = 1024
    return pl.pallas_call(
        _adaln_body,
        grid=(n // tn,),
        in_specs=[_const_spec((bsz, d)),
                  pl.BlockSpec((d, tn), lambda j: (0, j)),
                  pl.BlockSpec((1, tn), lambda j: (0, j))],
        out_specs=pl.BlockSpec((bsz, tn), lambda j: (0, j)),
        out_shape=jax.ShapeDtypeStruct((bsz, n), F32),
        compiler_params=pltpu.CompilerParams(dimension_semantics=("arbitrary",),
                                             vmem_limit_bytes=VMEM_LIMIT),
        name="adaln",
    )(c, w_ada, b_ada.reshape(1, n))


def _rope_block(xb, ct, st, lane):
    partner = jnp.where(lane < ROPE_HI, pltpu.roll(xb, LANES - HALF_ROPE, 1), pltpu.roll(xb, HALF_ROPE, 1))
    return xb * ct + partner * st


def _inproj_body(x_ref, sc_ref, sh_ref, g_ref, tab_ref, wz_ref, wxbc_ref, wsm_ref, wqa_ref, wkva_ref,
                 qn_ref, kvn_ref, wqup_ref, wkup_ref, wvup_ref,
                 z_ref, xbc_ref, dt_ref, q_ref, k_ref, v_ref):
    x = x_ref[0]
    h = _rms(x) * (g_ref[...] * (1.0 + sc_ref[0])) + sh_ref[0]
    hb = h.astype(BF16)
    z_ref[0] = _dot(hb, wz_ref[...]).astype(BF16)
    xbc_ref[0] = _dot(hb, wxbc_ref[...]).astype(BF16)
    sm = _dot(hb, wsm_ref[...])
    dt_ref[0] = sm[:, LANES:]
    tab = tab_ref[0].T
    lane = lax.broadcasted_iota(jnp.int32, tab.shape, 1)
    in_lo = (lane >= ROPE_LO) & (lane < ROPE_HI)
    in_hi = (lane >= ROPE_HI) & (lane < ROPE_HI + HALF_ROPE)
    ct = jnp.where(lane < ROPE_LO, 1.0, jnp.where(in_lo, tab, jnp.where(in_hi, pltpu.roll(tab, HALF_ROPE, 1), 0.0)))
    st = jnp.where(in_lo, -pltpu.roll(tab, LANES - HALF_ROPE, 1), jnp.where(in_hi, tab, 0.0))
    kr =_rope_block(sm[:, :LANES], ct, st, lane)
    qan = (_rms(_dot(hb, wqa_ref[...])) * qn_ref[...]).astype(BF16)
    q = _dot(qan, wqup_ref[...])
    for hh in range(MLA_HEADS):
        blk = slice(hh * LANES, (hh + 1) * LANES)
        q_ref[0, :, blk] = _rope_block(q[:, blk], ct, st, lane).astype(BF16)
    kvn = (_rms(_dot(hb, wkva_ref[...])) * kvn_ref[...]).astype(BF16)
    k = _dot(kvn, wkup_ref[...])
    for hh in range(MLA_HEADS):
        blk = slice(hh * LANES, (hh + 1) * LANES)
        k_ref[0, :, blk] = (k[:, blk] + kr).astype(BF16)
    vt = _dot_nt(wvup_ref[...], kvn)
    vrow = lax.broadcasted_iota(jnp.int32, vt.shape, 0)
    v_ref[0] = jnp.where(vrow % VT_ROWS == LANES, 1.0, vt).astype(BF16)


def _inproj(x, sc1, sh1, gain, rope_tab, wz, wxbc, wsm, wqa, wkva, qn, kvn, wqup, wkup, wvup, *, batch_offset):
    _, seq, d = x.shape
    bsz = sc1.shape[0]
    ts = min(TS_PROJ, seq)
    hw = MLA_HEADS * LANES

    def tok(width):
        return pl.BlockSpec((1, ts, width), lambda b, i: (b, i, 0))

    def tok_full(width):
        return pl.BlockSpec((1, ts, width), lambda b, i: (batch_offset + b, i, 0))

    def per_batch(width):
        return pl.BlockSpec((1, 1, width), lambda b, i: (b, 0, 0))

    weights = (wz, wxbc, wsm, wqa, wkva, qn, kvn, wqup, wkup, wvup)
    out_widths = (SSD_INNER, XBC_DIM, LANES, hw, hw)
    out_dtypes = (BF16, BF16, F32, BF16, BF16)
    vdim = (MLA_HEADS // 2) * VT_ROWS
    out_shape = ([jax.ShapeDtypeStruct((bsz, seq, w), dt) for w, dt in zip(out_widths, out_dtypes)]
                 + [jax.ShapeDtypeStruct((bsz, vdim, seq), BF16)])
    return pl.pallas_call(
        _inproj_body,
        grid=(bsz, seq // ts),
        in_specs=[tok_full(d), per_batch(d), per_batch(d), _const_spec((1, d)),
                  pl.BlockSpec((1, LANES, ts), lambda b, i: (batch_offset + b, 0, i))]
                 + [_const_spec(w.shape) for w in weights],
        out_specs=[tok(w) for w in out_widths] + [pl.BlockSpec((1, vdim, ts), lambda b, i: (b, 0, i))],
        out_shape=out_shape,
        compiler_params=pltpu.CompilerParams(dimension_semantics=("parallel", "parallel"),
                                             vmem_limit_bytes=VMEM_LIMIT),
        name="inproj",
    )(x, sc1, sh1, gain, rope_tab, *weights)


CONV_HALO = 16


def _ssd_body(xc_ref, xp_ref, z_ref, dt_ref, cw_ref, cb_ref, dtb_ref, alog_ref, dsk_ref, ng_ref,
              y_ref, st_scr, *, blk):
    i = pl.program_id(1)

    @pl.when(i == 0)
    def _():
        st_scr[...] = jnp.zeros_like(st_scr)

    xc = xc_ref[0]
    row = lax.broadcasted_iota(jnp.int32, (blk, blk), 0)
    col = lax.broadcasted_iota(jnp.int32, (blk, blk), 1)
    conv = cb_ref[...] + cw_ref[SSD_CONV - 1:SSD_CONV, :] * xc.astype(F32)
    tail = jnp.where(i > 0, xp_ref[0].astype(F32)[CONV_HALO - 8:], 0.0)
    row8 = lax.broadcasted_iota(jnp.int32, tail.shape, 0)
    head_fix = jnp.zeros(tail.shape, F32)
    for shift in range(1, SSD_CONV):
        w = cw_ref[SSD_CONV - 1 - shift:SSD_CONV - shift, :]
        shifted = _dot(jnp.where(row - col == shift, 1.0, 0.0).astype(BF16), xc)
        conv = conv + w * shifted
        head_fix = head_fix + jnp.where(row8 < shift, w * pltpu.roll(tail, shift, 0), 0.0)
    conv = jnp.concatenate([conv[:8] + head_fix, conv[8:]], axis=0)
    xa = _silu(conv)
    xs = xa[:, :SSD_INNER]
    bm = xa[:, SSD_INNER:SSD_INNER + SSD_BC]
    cm = xa[:, SSD_INNER + SSD_BC:]

    hl = lax.broadcasted_iota(jnp.int32, (1, LANES), 1)
    dtr = dt_ref[0] + dtb_ref[...]
    dt = jnp.maximum(dtr, 0.0) + jnp.log(1.0 + jnp.exp(-jnp.abs(dtr)))
    a = jnp.where(hl < SSD_HEADS, -jnp.exp(alog_ref[...]), 0.0)
    dta = dt * a
    row = lax.broadcasted_iota(jnp.int32, (blk, blk), 0)
    col = lax.broadcasted_iota(jnp.int32, (blk, blk), 1)
    tril = row >= col
    trilb = jnp.where(tril, 1.0, 0.0).astype(BF16)
    d1, d2, d3 = _split3(dta)
    cs = _dot(trilb, d1) + _dot(trilb, d2) + _dot(trilb, d3)
    cs_last = cs[blk - 1:blk, :]
    ecs = jnp.exp(cs)
    dte = jnp.exp(cs_last - cs)
    cs_t = cs.T

    er = lax.broadcasted_iota(jnp.int32, (LANES, SSD_INNER), 0)
    ec = lax.broadcasted_iota(jnp.int32, (LANES, SSD_INNER), 1)
    expand = jnp.where(ec // SSD_HEAD_DIM == er, 1.0, 0.0).astype(BF16)

    def per_channel(v):
        vh, vl = _split2(v)
        return _dot(vh, expand) + _dot(vl, expand)

    dt_e = per_channel(dt)
    ecs_e = per_channel(ecs)
    dte_e = per_channel(dte)
    xdt = xs * dt_e
    xdt_b = xdt.astype(BF16)
    xw_b = (xdt * dte_e).astype(BF16)

    gw = SSD_INNER // SSD_GROUPS
    heads_per_group = SSD_HEADS // SSD_GROUPS
    lane = lax.broadcasted_iota(jnp.int32, (blk, LANES), 1)
    y_groups = []
    for g in range(SSD_GROUPS):
        bg = bm[:, g * SSD_STATE:(g + 1) * SSD_STATE].astype(BF16)
        cg = cm[:, g * SSD_STATE:(g + 1) * SSD_STATE].astype(BF16)
        cb = _dot_nt(cg, bg)
        state = st_scr[g]
        y_off = _dot(cg, state.astype(BF16))
        pairs = []
        for j in range(heads_per_group // 2):
            xp = xdt_b[:, g * gw + j * LANES:g * gw + (j + 1) * LANES]
            halves = []
            for u in range(2):
                hidx = g * heads_per_group + 2 * j + u
                seg = cs[:, hidx:hidx + 1] - cs_t[hidx:hidx + 1, :]
                dec = jnp.exp(jnp.where(tril, seg, -jnp.inf))
                halves.append(_dot((cb * dec).astype(BF16), xp))
            pairs.append(jnp.where(lane < SSD_HEAD_DIM, halves[0], halves[1]))
        y_diag = jnp.concatenate(pairs, axis=1)
        y_groups.append(y_diag + y_off * ecs_e[:, g * gw:(g + 1) * gw])
        st_scr[g] = (state * ecs_e[blk - 1:blk, g * gw:(g + 1) * gw]
                     + _dot_tn(bg, xw_b[:, g * gw:(g + 1) * gw]))
    y = jnp.concatenate(y_groups, axis=1) + xs * dsk_ref[...]
    y = y * _silu(z_ref[0].astype(F32))
    y = jnp.concatenate([_rms(y[:, g * gw:(g + 1) * gw]) for g in range(SSD_GROUPS)], axis=1)
    y_ref[0] = (y * ng_ref[...]).astype(BF16)


def _ssd(xbc, z, dt, conv_w, conv_b, dt_bias, a_log, d_skip_e, norm_gain):
    bsz, seq, _ = xbc.shape
    blk = min(SSD_L, seq)
    halo_per_blk = blk // CONV_HALO
    body = functools.partial(_ssd_body, blk=blk)
    out_shape = jax.ShapeDtypeStruct((bsz, seq, SSD_INNER), BF16)
    return pl.pallas_call(
        body,
        grid=(bsz, seq // blk),
        in_specs=[pl.BlockSpec((1, blk, XBC_DIM), lambda b, i: (b, i, 0)),
                  pl.BlockSpec((1, CONV_HALO, XBC_DIM), lambda b, i: (b, jnp.maximum(i * halo_per_blk - 1, 0), 0)),
                  pl.BlockSpec((1, blk, SSD_INNER), lambda b, i: (b, i, 0)),
                  pl.BlockSpec((1, blk, LANES), lambda b, i: (b, i, 0)),
                  _const_spec((SSD_CONV, XBC_DIM)), _const_spec((1, XBC_DIM)),
                  _const_spec((1, LANES)), _const_spec((1, LANES)),
                  _const_spec((1, SSD_INNER)), _const_spec((1, SSD_INNER))],
        out_specs=pl.BlockSpec((1, blk, SSD_INNER), lambda b, i: (b, i, 0)),
        out_shape=out_shape,
        scratch_shapes=[pltpu.VMEM((SSD_GROUPS, SSD_STATE, SSD_INNER // SSD_GROUPS), F32)],
        compiler_params=pltpu.CompilerParams(dimension_semantics=("parallel", "arbitrary"),
                                             vmem_limit_bytes=VMEM_LIMIT),
        name="ssd",
    )(xbc, xbc, z, dt, conv_w, conv_b, dt_bias, a_log, d_skip_e, norm_gain)


def _attn_body(q_ref, k_ref, v_ref, o_ref, *scratch, tq, tk):
    n_streams = 2 * (tq // tk)
    s_scr = (scratch[:n_streams], scratch[n_streams:2 * n_streams])
    acc_scr = scratch[2 * n_streams:]
    qi = pl.program_id(2)
    n_sub = tq // tk
    n_full = qi * n_sub
    krow = lax.broadcasted_iota(jnp.int32, (tk, tk), 0)
    qcol = lax.broadcasted_iota(jnp.int32, (tk, tk), 1)
    diag_ok = krow // CHUNK <= qcol // CHUNK
    vrow = lax.broadcasted_iota(jnp.int32, (LANES, tk), 0)
    streams = [(u, r) for u in range(2) for r in range(n_sub)]
    qs = [q_ref[0, r * tk:(r + 1) * tk, u * LANES:(u + 1) * LANES] for u, r in streams]

    def put_scores(ki, which, slot):
        start = pl.multiple_of(ki * tk, tk)
        k2 = k_ref[0, pl.ds(start, tk), :]
        out = {}
        for si in which:
            u = streams[si][0]
            s = _dot_nt(k2[:, u * LANES:(u + 1) * LANES], qs[si])
            s_scr[slot][si][...] = s
            out[si] = jnp.max(s, axis=0, keepdims=True)
        return out

    def values_t(ki):
        return v_ref[0, :, pl.ds(pl.multiple_of(ki * tk, tk), tk)]

    def softmax_pv(si, m, s_max, slot, vt, masked):
        s = s_scr[slot][si][...]
        if masked:
            s = jnp.where(diag_ok, s, -jnp.inf)
            s_max = jnp.max(s, axis=0, keepdims=True)
        m_new = jnp.maximum(m, s_max)
        alpha = jnp.exp2(m - m_new)
        p = jnp.exp2((s - m_new).astype(BF16))
        acc_scr[si][...] = alpha * acc_scr[si][...] + _dot(vt, p)
        return m_new

    every = list(range(len(streams)))
    for ref in acc_scr:
        ref[...] = jnp.zeros_like(ref)

    def step(j, state):
        ms, s_maxes = state
        for slot in range(2):
            ki = 2 * j + slot
            nxt = put_scores(ki + 1, every, 1 - slot)
            vt = values_t(ki)
            ms = tuple(softmax_pv(si, ms[si], s_maxes[si], slot, vt, False) for si in every)
            s_maxes = tuple(nxt[si] for si in every)
        return ms, s_maxes

    first = put_scores(0, every, 0)
    m_init = jnp.full((1, tk), -jnp.inf, F32)
    ms, s_maxes = lax.fori_loop(0, n_full // 2, step, ((m_init,) * len(streams), tuple(first[si] for si in every)))
    ms = list(ms)
    s_maxes = dict(zip(every, s_maxes))
    for dd in range(n_sub):
        slot = dd % 2
        live = [si for si in every if dd <= streams[si][1]]
        later = [si for si in every if dd + 1 <= streams[si][1]]
        nxt = put_scores(n_full + dd + 1, later, 1 - slot) if later else {}
        vt = values_t(n_full + dd)
        for si in live:
            ms[si] = softmax_pv(si, ms[si], s_maxes[si], slot, vt, dd == streams[si][1])
        s_maxes = nxt
    for r in range(n_sub):
        a0 = acc_scr[streams.index((0, r))][...]
        a1 = acc_scr[streams.index((1, r))][...]
        out_t = jnp.where(vrow < MLA_V, a0[:LANES] / a0[LANES:LANES + 1], a1[:LANES] / a1[LANES:LANES + 1])
        o_ref[0, r * tk:(r + 1) * tk, :] = out_t.T.astype(BF16)


def _attn(q, k, v):
    bsz, seq, _ = q.shape
    tk = min(ATT_TK, seq)
    tq = min(ATT_TQ, seq)
    assert (tq // tk) % 2 == 0, "the two-slot score pipeline needs an even number of query sub-tiles"
    n_streams = 2 * (tq // tk)
    body = functools.partial(_attn_body, tq=tq, tk=tk)
    out_shape = jax.ShapeDtypeStruct((bsz, seq, MLA_HEADS * MLA_V), BF16)
    return pl.pallas_call(
        body,
        grid=(bsz, MLA_HEADS // 2, seq // tq),
        in_specs=[pl.BlockSpec((1, tq, 2 * LANES), lambda b, hp, i: (b, i, hp)),
                  pl.BlockSpec((1, seq, 2 * LANES), lambda b, hp, i: (b, 0, hp)),
                  pl.BlockSpec((1, VT_ROWS, seq), lambda b, hp, i: (b, hp, 0))],
        out_specs=pl.BlockSpec((1, tq, LANES), lambda b, hp, i: (b, i, hp)),
        out_shape=out_shape,
        scratch_shapes=[pltpu.VMEM((tk, tk), F32)] * (2 * n_streams) + [pltpu.VMEM((VT_ROWS, tk), F32)] * n_streams,
        compiler_params=pltpu.CompilerParams(dimension_semantics=("parallel", "parallel", "arbitrary"),
                                             vmem_limit_bytes=VMEM_LIMIT),
        name="attn",
    )(q, k, v)


def _outproj_body(ys_ref, ya_ref, x_ref, g1_ref, sc2_ref, sh2_ref, mn_ref, wo1_ref, wo2_ref, pmn_ref, pfn_ref,
                  wr_ref, br_ref, x1_ref, h2_ref, gate_ref, idx_ref, cnt_ref, pref_ref, ltab_ref, cnt_scr):
    first = (pl.program_id(0) == 0) & (pl.program_id(1) == 0)

    @pl.when(first)
    def _():
        cnt_scr[...] = jnp.zeros_like(cnt_scr)

    yan =(_rms(ya_ref[0].astype(F32)) * mn_ref[...]).astype(BF16)
    mix = _dot(ys_ref[0], wo1_ref[...]) + _dot(yan, wo2_ref[...])
    x1 = x_ref[0] + g1_ref[0] * (_rms(mix) * pmn_ref[...])
    x1_ref[0] = x1
    h2 = _rms(x1) * (pfn_ref[...] * (1.0 + sc2_ref[0])) + sh2_ref[0]
    h2_ref[0] = h2.astype(BF16)
    h_hi, h_lo = _split2(h2)
    w_hi, w_lo = _split2(wr_ref[...])
    both = _dot(h_hi, jnp.concatenate([w_hi, w_lo], axis=1))
    logits = both[:, :LANES] + both[:, LANES:] + _dot(h_lo, w_hi) + br_ref[...]
    lane = lax.broadcasted_iota(jnp.int32, logits.shape, 1)
    cur = jnp.where(lane < N_EXPERTS, logits, -jnp.inf)
    vals, idxs = [], []
    for _ in range(TOP_K):
        m = jnp.max(cur, axis=-1, keepdims=True)
        ix = jnp.min(jnp.where(cur == m, lane, LANES), axis=-1, keepdims=True)
        vals.append(m)
        idxs.append(ix)
        cur = jnp.where(lane == ix, -jnp.inf, cur)
    es = [jnp.exp(v - vals[0]) for v in vals]
    denom = es[0]
    for e in es[1:]:
        denom = denom + e
    onehot = jnp.zeros(logits.shape, F32)
    for kk in range(TOP_K):
        onehot = onehot + jnp.where(lane == idxs[kk], 1.0, 0.0)
    ts = logits.shape[0]
    row = lax.broadcasted_iota(jnp.int32, (ts, ts), 0)
    col = lax.broadcasted_iota(jnp.int32, (ts, ts), 1)
    before = jnp.where(row > col, 1.0, 0.0).astype(BF16)
    prior = _dot(before, onehot.astype(BF16))
    seen = cnt_scr[...]
    pref_ref[0] = seen
    gate_out = jnp.zeros(logits.shape, F32)
    idx_out = jnp.zeros(logits.shape, jnp.int32)
    local_out = jnp.zeros(logits.shape, F32)
    for kk in range(TOP_K):
        mine = lane == idxs[kk]
        local = jnp.sum(jnp.where(mine, prior, 0.0), axis=-1, keepdims=True)
        rank = local + jnp.sum(jnp.where(mine, seen, 0.0), axis=-1, keepdims=True)
        gate_out = jnp.where(lane == kk, es[kk] / denom, gate_out)
        idx_out = jnp.where(lane == kk, idxs[kk], idx_out)
        idx_out = jnp.where(lane == TOP_K + kk, rank.astype(jnp.int32), idx_out)
        local_out = jnp.where(lane == kk, local, local_out)
    gate_ref[0] = gate_out
    idx_ref[0] = idx_out
    local_rows = local_out.T.astype(jnp.int32)
    tok = lax.broadcasted_iota(jnp.int32, logits.shape, 0)
    tok_hi = (tok // 16).astype(F32)
    tok_lo = (tok % 16).astype(F32)
    ltab = jnp.zeros((ts, 2 * LANES), F32)
    for kk in range(TOP_K):
        mine = lane == idxs[kk]
        at_rank = jnp.where(row == local_rows[kk:kk + 1, :], 1.0, 0.0).astype(BF16)
        tagged = jnp.concatenate([jnp.where(mine, tok_hi, 0.0), jnp.where(mine, tok_lo, 0.0)], axis=1)
        ltab = ltab + _dot(at_rank, tagged.astype(BF16))
    ltab_ref[0] = (16.0 * ltab[:, :LANES] + ltab[:, LANES:]).astype(jnp.int32)
    cnt_scr[...] = seen + jnp.sum(onehot, axis=0, keepdims=True)
    cnt_ref[...] = cnt_scr[...]


def _outproj(y_ssd, y_att, x, g1, sc2, sh2, mla_norm, wo1, wo2, post_mix_norm, pre_ffn_norm, wr, br, *, batch_offset):
    bsz, seq, _ = y_ssd.shape
    d = x.shape[-1]
    ts = min(TS_PROJ, seq)
    tiles = seq // ts

    def tok(width):
        return pl.BlockSpec((1, ts, width), lambda b, i: (b, i, 0))

    x_spec = pl.BlockSpec((1, ts, d), lambda b, i: (batch_offset + b, i, 0))

    def per_batch(width):
        return pl.BlockSpec((1, 1, width), lambda b, i: (b, 0, 0))

    consts = (mla_norm, wo1, wo2, post_mix_norm, pre_ffn_norm, wr, br)
    out_shape = [jax.ShapeDtypeStruct((bsz, seq, d), F32), jax.ShapeDtypeStruct((bsz, seq, d), BF16),
                 jax.ShapeDtypeStruct((bsz, seq, LANES), F32), jax.ShapeDtypeStruct((bsz, seq, LANES), jnp.int32),
                 jax.ShapeDtypeStruct((1, LANES), F32),
                 jax.ShapeDtypeStruct((bsz * tiles, 1, LANES), F32),
                 jax.ShapeDtypeStruct((bsz * tiles, ts, LANES), jnp.int32)]
    return pl.pallas_call(
        _outproj_body,
        grid=(bsz, seq // ts),
        in_specs=[tok(SSD_INNER), tok(MLA_HEADS * MLA_V), x_spec, per_batch(d), per_batch(d), per_batch(d)]
                 + [_const_spec(w.shape) for w in consts],
        out_specs=[tok(d), tok(d), tok(LANES), tok(LANES), _const_spec((1, LANES)),
                   pl.BlockSpec((1, 1, LANES), lambda b, i: (b * tiles + i, 0, 0)),
                   pl.BlockSpec((1, ts, LANES), lambda b, i: (b * tiles + i, 0, 0))],
        out_shape=out_shape,
        scratch_shapes=[pltpu.VMEM((1, LANES), F32)],
        compiler_params=pltpu.CompilerParams(dimension_semantics=("arbitrary", "arbitrary"),
                                             vmem_limit_bytes=VMEM_LIMIT),
        name="outproj",
    )(y_ssd, y_att, x, g1, sc2, sh2, *consts)


def _moe_body(be_ref, na_ref, x_ref, wgu_ref, bgu_ref, wd_ref, bd_ref, y_ref, wgu_b, wd_b):
    i = pl.program_id(0)

    @pl.when((i == 0) | (be_ref[i] != be_ref[jnp.maximum(i - 1, 0)]))
    def _():
        wgu_b[...] = wgu_ref[0].astype(BF16)
        wd_b[...] = wd_ref[0].astype(BF16)

    @pl.when(i < na_ref[0])
    def _():
        gu = _dot(x_ref[...], wgu_b[...]) + bgu_ref[0]
        glu = jnp.minimum(gu[:, :D_FF_EXPERT], SWIGLU_LIMIT)
        lin = jnp.clip(gu[:, D_FF_EXPERT:], -SWIGLU_LIMIT, SWIGLU_LIMIT)
        act = glu * jax.nn.sigmoid(SWIGLU_ALPHA * glu) * (lin + 1.0)
        y_ref[...] = (_dot(act.astype(BF16), wd_b[...]) + bd_ref[0]).astype(BF16)

    @pl.when(i >= na_ref[0])
    def _():
        y_ref[...] = jnp.zeros_like(y_ref)


def _moe(block_expert, n_active, xg, wgu, bgu, wd, bd):
    n_slots, d = xg.shape
    n_blocks = n_slots // MOE_TB
    f2 = wgu.shape[2]
    out_shape = jax.ShapeDtypeStruct((n_slots, d), BF16)
    return pl.pallas_call(
        _moe_body,
        grid_spec=pltpu.PrefetchScalarGridSpec(
            num_scalar_prefetch=2,
            grid=(n_blocks,),
            in_specs=[pl.BlockSpec((MOE_TB, d), lambda i, be, na: (i, 0)),
                      pl.BlockSpec((1, d, f2), lambda i, be, na: (be[i], 0, 0)),
                      pl.BlockSpec((1, 1, f2), lambda i, be, na: (be[i], 0, 0)),
                      pl.BlockSpec((1, f2 // 2, d), lambda i, be, na: (be[i], 0, 0)),
                      pl.BlockSpec((1, 1, d), lambda i, be, na: (be[i], 0, 0))],
            out_specs=pl.BlockSpec((MOE_TB, d), lambda i, be, na: (i, 0)),
            scratch_shapes=[pltpu.VMEM((d, f2), BF16), pltpu.VMEM((f2 // 2, d), BF16)],
        ),
        out_shape=out_shape,
        compiler_params=pltpu.CompilerParams(dimension_semantics=("arbitrary",),
                                             vmem_limit_bytes=VMEM_LIMIT),
        name="moe",
    )(block_expert, n_active, xg, wgu, bgu, wd, bd)


def _final_body(x1_ref, y_ref, gate_ref, g2_ref, gain_ref, *rest):
    o_ref = rest[-1]
    gates = gate_ref[0]
    f = gates[:, 0:1] * y_ref[0, 0].astype(F32)
    for kk in range(1, TOP_K):
        f = f + gates[:, kk:kk + 1] * y_ref[kk, 0].astype(F32)
    o_ref[0] = x1_ref[0] + g2_ref[0] * (_rms(f) * gain_ref[...])


def _final(x1, y4, gates, g2, gain, out_prev, batch_offset, total_batch):
    bsz, seq, d = x1.shape
    ts = min(TS_PROJ, seq)

    def tok(width):
        return pl.BlockSpec((1, ts, width), lambda b, i: (b, i, 0))

    in_specs = [tok(d), pl.BlockSpec((TOP_K, 1, ts, d), lambda b, i: (0, b, i, 0)), tok(LANES),
                pl.BlockSpec((1, 1, d), lambda b, i: (b, 0, 0)), _const_spec((1, d))]
    args = [x1, y4, gates, g2, gain]
    aliases = {}
    if out_prev is not None:
        in_specs.append(pl.BlockSpec(memory_space=pl.ANY))
        args.append(out_prev)
        aliases = {len(args) - 1: 0}
    return pl.pallas_call(
        _final_body,
        grid=(bsz, seq // ts),
        in_specs=in_specs,
        out_specs=pl.BlockSpec((1, ts, d), lambda b, i: (batch_offset + b, i, 0)),
        out_shape=jax.ShapeDtypeStruct((total_batch, seq, d), F32),
        input_output_aliases=aliases,
        compiler_params=pltpu.CompilerParams(dimension_semantics=("parallel", "parallel"),
                                             vmem_limit_bytes=VMEM_LIMIT),
        name="final",
    )(*args)


def _head_blocks(cols):
    out = []
    for c in cols:
        pad = LANES - c.shape[1]
        out.append(jnp.pad(c, ((0, 0), (0, pad))) if pad else c)
    return jnp.concatenate(out, axis=1)


def _prep_mixer_weights(w_in, w_q_up, w_kv_up):
    d = w_in.shape[0]
    wz = w_in[:, OFF_Z:OFF_XBC]
    wxbc = w_in[:, OFF_XBC:OFF_DT]
    wdt = w_in[:, OFF_DT:OFF_QA]
    wqa = w_in[:, OFF_QA:OFF_KVA]
    wkva = w_in[:, OFF_KVA:OFF_KR]
    wkr = w_in[:, OFF_KR:IN_COLS]
    kr_blk = jnp.concatenate([jnp.zeros((d, ROPE_LO), F32), wkr, jnp.zeros((d, LANES - ROPE_LO - MLA_ROPE), F32)], axis=1)
    dt_blk = jnp.pad(wdt, ((0, 0), (0, LANES - SSD_HEADS)))
    wsm = jnp.concatenate([kr_blk, dt_blk], axis=1)
    qh = MLA_NOPE + MLA_ROPE
    scale = math.log2(math.e) / math.sqrt(qh)
    wqup = _head_blocks([w_q_up[:, h * qh:(h + 1) * qh] for h in range(MLA_HEADS)]) * scale
    kvh = MLA_NOPE + MLA_V
    wkup = _head_blocks([w_kv_up[:, h * kvh:h * kvh + MLA_NOPE] for h in range(MLA_HEADS)])
    vcols = []
    for h in range(MLA_HEADS):
        vcols.append(w_kv_up[:, h * kvh + MLA_NOPE:(h + 1) * kvh])
        if h % 2 == 1:
            vcols.append(jnp.zeros((w_kv_up.shape[0], VT_ROWS - LANES), F32))
    wvup = jnp.concatenate(vcols, axis=1).T
    return tuple(w.astype(BF16) for w in (wz, wxbc, wsm, wqa, wkva)) + tuple(w.astype(BF16) for w in (wqup, wkup, wvup))


def _rope_tables(positions):
    inv_freq = ROPE_BASE ** (-(jnp.arange(HALF_ROPE, dtype=F32) * 2.0 / MLA_ROPE))
    angles = positions.astype(F32)[:, None, :] * inv_freq[None, :, None]
    bsz, _, seq = angles.shape
    return jnp.concatenate([jnp.zeros((bsz, ROPE_LO, seq), F32), jnp.cos(angles), jnp.sin(angles),
                            jnp.zeros((bsz, LANES - ROPE_LO - MLA_ROPE, seq), F32)], axis=1)


def _route(idx, rank, counts, tile_seen, ltab, n_tok):
    n_assign = n_tok * TOP_K
    n_tiles, tile, lanes = ltab.shape
    padded = ((counts + MOE_TB - 1) // MOE_TB) * MOE_TB
    padded_end = jnp.cumsum(padded)
    padded_start = padded_end - padded
    experts = jnp.arange(N_EXPERTS, dtype=jnp.int32)
    start_of = jnp.sum(jnp.where(idx[..., None] == experts, padded_start, 0), axis=-1)
    dest = (start_of + rank).reshape(-1)
    n_blocks = n_assign // MOE_TB + N_EXPERTS
    block_start = jnp.arange(n_blocks, dtype=jnp.int32) * MOE_TB
    block_expert = jnp.minimum(jnp.sum((padded_end[None, :] <= block_start[:, None]).astype(jnp.int32), axis=1),
                               N_EXPERTS - 1)
    n_active = (padded_end[-1] // MOE_TB).astype(jnp.int32).reshape(1)
    j = (block_start - padded_start[block_expert])[:, None] + jnp.arange(MOE_TB, dtype=jnp.int32)[None, :]
    seen_blk = tile_seen.T[block_expert]
    reached = seen_blk[:, None, :] <= j[:, :, None]
    tau = jnp.sum(reached.astype(jnp.int32), axis=-1) - 1
    j_local = j - jnp.max(jnp.where(reached, seen_blk[:, None, :], 0), axis=-1)
    valid = j < counts[block_expert][:, None]
    flat = (tau * tile + j_local) * lanes + block_expert[:, None]
    tok_local = ltab.reshape(-1)[jnp.clip(flat, 0, n_tiles * tile * lanes - 1)]
    slot_tok = jnp.where(valid, tau * tile + tok_local, 0).reshape(-1)
    return dest, slot_tok, block_expert, n_active


def kernel(x, c, positions, w_ada, b_ada, pre_mix_norm, w_in, conv_w, conv_b, dt_bias, a_log, d_skip, ssd_norm, q_a_norm, w_q_up, kv_a_norm, w_kv_up, mla_norm, w_out, post_mix_norm, pre_ffn_norm, w_router, b_router, w_gate_up, b_gate_up, w_down, b_down, post_ffn_norm):
    bsz, seq, d = x.shape
    rope_tab = _rope_tables(positions)
    n_groups = BATCH_GROUPS if bsz % BATCH_GROUPS == 0 else 1
    gb = bsz // n_groups
    n_tok = gb * seq
    pad_h = LANES - SSD_HEADS
    for l in range(w_ada.shape[0]):
        mod = _adaln(c, w_ada[l], b_ada[l])
        mods = [m.reshape(bsz, 1, d) for m in jnp.split(mod, 6, axis=-1)]
        mixer_w = _prep_mixer_weights(w_in[l], w_q_up[l], w_kv_up[l])
        wo = w_out[l].astype(BF16)
        wr = jnp.pad(w_router[l], ((0, 0), (0, LANES - N_EXPERTS)))
        br = jnp.pad(b_router[l], (0, LANES - N_EXPERTS)).reshape(1, LANES)
        dtb = jnp.pad(dt_bias[l], (0, pad_h)).reshape(1, LANES)
        alog = jnp.pad(a_log[l], (0, pad_h)).reshape(1, LANES)
        dsk = jnp.repeat(d_skip[l], SSD_HEAD_DIM).reshape(1, -1)
        out = None
        pending = None

        def experts_and_final(p, xs, out):
            y = _moe(p["block_expert"], p["n_active"], xs, w_gate_up[l], b_gate_up[l][:, None, :],
                     w_down[l], b_down[l][:, None, :])
            y4 = y[p["dest"].reshape(n_tok, TOP_K).T.reshape(-1)].reshape(TOP_K, gb, seq, d)
            return _final(p["x1"], y4, p["gates"], p["g2"], post_ffn_norm[l].reshape(1, d), out, p["offset"], bsz)

        for gi in range(n_groups):
            grp = slice(gi * gb, (gi + 1) * gb)
            sh1, sc1, g1, sh2, sc2, g2 = [m[grp] for m in mods]
            if pending is not None:
                sc1 = sc1 + jnp.where(pending["slot_tok"][0] < 0, 1.0, 0.0)
            z, xbc, dt, q, k, v = _inproj(x, sc1, sh1, pre_mix_norm[l].reshape(1, d), rope_tab, *mixer_w[:5],
                                          q_a_norm[l].reshape(1, -1), kv_a_norm[l].reshape(1, -1), *mixer_w[5:],
                                          batch_offset=gi * gb)
            y_ssd = _ssd(xbc, z, dt, conv_w[l], conv_b[l].reshape(1, -1), dtb, alog, dsk, ssd_norm[l].reshape(1, -1))
            prev_xs = None
            if pending is not None:
                ssd_bits = lax.bitcast_convert_type(y_ssd[0, 0, 0], jnp.uint16).astype(jnp.int32)
                prev_xs = pending["h2"].reshape(n_tok, d)[pending["slot_tok"] + jnp.where(ssd_bits < 0, 1, 0)]
            y_att = _attn(q, k, v)
            x1, h2, gates, route, cnt, seen, ltab = _outproj(
                y_ssd, y_att, x, g1, sc2, sh2, mla_norm[l].reshape(1, -1), wo[:SSD_INNER], wo[SSD_INNER:],
                post_mix_norm[l].reshape(1, d), pre_ffn_norm[l].reshape(1, d), wr, br, batch_offset=gi * gb)
            route = route.reshape(n_tok, LANES)
            counts = cnt[0, :N_EXPERTS].astype(jnp.int32)
            tile_seen = seen[:, 0, :N_EXPERTS].astype(jnp.int32)
            if prev_xs is not None:
                bits = lax.bitcast_convert_type(prev_xs[0, 0], jnp.uint16).astype(jnp.int32)
                tile_seen = tile_seen + jnp.where(bits < 0, 1, 0)
            dest, slot_tok, block_expert, n_active = _route(route[:, :TOP_K], route[:, TOP_K:2 * TOP_K], counts,
                                                            tile_seen, ltab, n_tok)
            if pending is not None:
                out = experts_and_final(pending, prev_xs, out)
            pending = dict(h2=h2, slot_tok=slot_tok, dest=dest, block_expert=block_expert, n_active=n_active,
                           x1=x1, gates=gates, g2=g2, offset=gi * gb)
        out = experts_and_final(pending, pending["h2"].reshape(n_tok, d)[pending["slot_tok"]], out)
        x = out
    return x
```

```python
import functools
import math

import jax
import jax.numpy as jnp
from jax import lax
from jax.experimental import pallas as pl
from jax.experimental.pallas import tpu as pltpu

F32 = jnp.float32
BF16 = jnp.bfloat16

D_MODEL = 1024
CHUNK = 64
SSD_INNER = 512
SSD_HEAD_DIM = 64
SSD_HEADS = 8
SSD_GROUPS = 2
SSD_STATE = 128
SSD_CONV = 4
SSD_BC = SSD_GROUPS * SSD_STATE
XBC_DIM = SSD_INNER + 2 * SSD_BC
MLA_V = 64
MLA_HEADS = 8
MLA_NOPE = 64
MLA_ROPE = 32
Q_LORA = 384
KV_LORA = 256
ROPE_BASE = 10000.0
OFF_Z = 0
OFF_XBC = OFF_Z + SSD_INNER
OFF_DT = OFF_XBC + XBC_DIM
OFF_QA = OFF_DT + SSD_HEADS
OFF_KVA = OFF_QA + Q_LORA
OFF_KR = OFF_KVA + KV_LORA
IN_COLS = OFF_KR + MLA_ROPE
N_EXPERTS = 32
TOP_K = 4
D_FF_EXPERT = 1024
SWIGLU_LIMIT = 7.0
SWIGLU_ALPHA = 1.702
NORM_EPS = 1e-6

LANES = 128
HALF_ROPE = MLA_ROPE // 2
ROPE_LO = MLA_NOPE
ROPE_HI = MLA_NOPE + HALF_ROPE
VT_ROWS = 144

TS_PROJ = 512
SSD_L = 256
ATT_TQ = 1024
ATT_TK = 256
MOE_TB = 512
BATCH_GROUPS = 2
VMEM_LIMIT = 56 * 1024 * 1024


def _dot(a, b):
    return jnp.dot(a, b, preferred_element_type=F32)


def _dot_nt(a, b):
    return lax.dot_general(a, b, (((1,), (1,)), ((), ())), preferred_element_type=F32)


def _dot_tn(a, b):
    return lax.dot_general(a, b, (((0,), (0,)), ((), ())), preferred_element_type=F32)


def _split2(x):
    hi = x.astype(BF16)
    lo = (x - hi.astype(F32)).astype(BF16)
    return hi, lo


def _split3(x):
    h1 = x.astype(BF16)
    r1 = x - h1.astype(F32)
    h2 = r1.astype(BF16)
    h3 = (r1 - h2.astype(F32)).astype(BF16)
    return h1, h2, h3


def _dot3(a, b):
    ah, al = _split2(a)
    bh, bl = _split2(b)
    return _dot(ah, bh) + _dot(ah, bl) + _dot(al, bh)


def _rms(x):
    return x * lax.rsqrt(jnp.mean(x * x, axis=-1, keepdims=True) + NORM_EPS)


def _silu(x):
    hx = 0.5 * x
    return hx + hx * jnp.tanh(hx)


def _const_spec(shape):
    nd = len(shape)
    return pl.BlockSpec(shape, lambda *_: (0,) * nd)


def _adaln_body(c_ref, w_ref, b_ref, o_ref):
    o_ref[...] = _dot3(_silu(c_ref[...]), w_ref[...]) + b_ref[...]


def _adaln(c, w_ada, b_ada):
    bsz, d = c.shape
    n = w_ada.shape[1]
    tn = 1024
    return pl.pallas_call(
        _adaln_body,
        grid=(n // tn,),
        in_specs=[_const_spec((bsz, d)),
                  pl.BlockSpec((d, tn), lambda j: (0, j)),
                  pl.BlockSpec((1, tn), lambda j: (0, j))],
        out_specs=pl.BlockSpec((bsz, tn), lambda j: (0, j)),
        out_shape=jax.ShapeDtypeStruct((bsz, n), F32),
        compiler_params=pltpu.CompilerParams(dimension_semantics=("arbitrary",),
                                             vmem_limit_bytes=VMEM_LIMIT),
        name="adaln",
    )(c, w_ada, b_ada.reshape(1, n))


def _rope_block(xb, ct, st, lane):
    partner = jnp.where(lane < ROPE_HI, pltpu.roll(xb, LANES - HALF_ROPE, 1), pltpu.roll(xb, HALF_ROPE, 1))
    return xb * ct + partner * st


def _inproj_body(x_ref, sc_ref, sh_ref, g_ref, tab_ref, wz_ref, wxbc_ref, wsm_ref, wqa_ref, wkva_ref,
                 qn_ref, kvn_ref, wqup_ref, wkup_ref, wvup_ref,
                 z_ref, xbc_ref, dt_ref, q_ref, k_ref, v_ref):
    x = x_ref[0]
    h = _rms(x) * (g_ref[...] * (1.0 + sc_ref[0])) + sh_ref[0]
    hb = h.astype(BF16)
    z_ref[0] = _dot(hb, wz_ref[...]).astype(BF16)
    xbc_ref[0] = _dot(hb, wxbc_ref[...]).astype(BF16)
    sm = _dot(hb, wsm_ref[...])
    dt_ref[0] = sm[:, LANES:]
    tab = tab_ref[0].T
    lane = lax.broadcasted_iota(jnp.int32, tab.shape, 1)
    in_lo = (lane >= ROPE_LO) & (lane < ROPE_HI)
    in_hi = (lane >= ROPE_HI) & (lane < ROPE_HI + HALF_ROPE)
    ct = jnp.where(lane < ROPE_LO, 1.0, jnp.where(in_lo, tab, jnp.where(in_hi, pltpu.roll(tab, HALF_ROPE, 1), 0.0)))
    st = jnp.where(in_lo, -pltpu.roll(tab, LANES - HALF_ROPE, 1), jnp.where(in_hi, tab, 0.0))
    kr =_rope_block(sm[:, :LANES], ct, st, lane)
    qan = (_rms(_dot(hb, wqa_ref[...])) * qn_ref[...]).astype(BF16)
    q = _dot(qan, wqup_ref[...])
    for hh in range(MLA_HEADS):
        blk = slice(hh * LANES, (hh + 1) * LANES)
        q_ref[0, :, blk] = _rope_block(q[:, blk], ct, st, lane).astype(BF16)
    kvn = (_rms(_dot(hb, wkva_ref[...])) * kvn_ref[...]).astype(BF16)
    k = _dot(kvn, wkup_ref[...])
    for hh in range(MLA_HEADS):
        blk = slice(hh * LANES, (hh + 1) * LANES)
        k_ref[0, :, blk] = (k[:, blk] + kr).astype(BF16)
    vt = _dot_nt(wvup_ref[...], kvn)
    vrow = lax.broadcasted_iota(jnp.int32, vt.shape, 0)
    v_ref[0] = jnp.where(vrow % VT_ROWS == LANES, 1.0, vt).astype(BF16)


def _inproj(x, sc1, sh1, gain, rope_tab, wz, wxbc, wsm, wqa, wkva, qn, kvn, wqup, wkup, wvup, *, batch_offset):
    _, seq, d = x.shape
    bsz = sc1.shape[0]
    ts = min(TS_PROJ, seq)
    hw = MLA_HEADS * LANES

    def tok(width):
        return pl.BlockSpec((1, ts, width), lambda b, i: (b, i, 0))

    def tok_full(width):
        return pl.BlockSpec((1, ts, width), lambda b, i: (batch_offset + b, i, 0))

    def per_batch(width):
        return pl.BlockSpec((1, 1, width), lambda b, i: (b, 0, 0))

    weights = (wz, wxbc, wsm, wqa, wkva, qn, kvn, wqup, wkup, wvup)
    out_widths = (SSD_INNER, XBC_DIM, LANES, hw, hw)
    out_dtypes = (BF16, BF16, F32, BF16, BF16)
    vdim = (MLA_HEADS // 2) * VT_ROWS
    out_shape = ([jax.ShapeDtypeStruct((bsz, seq, w), dt) for w, dt in zip(out_widths, out_dtypes)]
                 + [jax.ShapeDtypeStruct((bsz, vdim, seq), BF16)])
    return pl.pallas_call(
        _inproj_body,
        grid=(bsz, seq // ts),
        in_specs=[tok_full(d), per_batch(d), per_batch(d), _const_spec((1, d)),
                  pl.BlockSpec((1, LANES, ts), lambda b, i: (batch_offset + b, 0, i))]
                 + [_const_spec(w.shape) for w in weights],
        out_specs=[tok(w) for w in out_widths] + [pl.BlockSpec((1, vdim, ts), lambda b, i: (b, 0, i))],
        out_shape=out_shape,
        compiler_params=pltpu.CompilerParams(dimension_semantics=("parallel", "parallel"),
                                             vmem_limit_bytes=VMEM_LIMIT),
        name="inproj",
    )(x, sc1, sh1, gain, rope_tab, *weights)


CONV_HALO = 16


def _ssd_body(xc_ref, xp_ref, z_ref, dt_ref, cw_ref, cb_ref, dtb_ref, alog_ref, dsk_ref, ng_ref,
              y_ref, st_scr, *, blk):
    i = pl.program_id(1)

    @pl.when(i == 0)
    def _():
        st_scr[...] = jnp.zeros_like(st_scr)

    xc = xc_ref[0]
    row = lax.broadcasted_iota(jnp.int32, (blk, blk), 0)
    col = lax.broadcasted_iota(jnp.int32, (blk, blk), 1)
    conv = cb_ref[...] + cw_ref[SSD_CONV - 1:SSD_CONV, :] * xc.astype(F32)
    tail = jnp.where(i > 0, xp_ref[0].astype(F32)[CONV_HALO - 8:], 0.0)
    row8 = lax.broadcasted_iota(jnp.int32, tail.shape, 0)
    head_fix = jnp.zeros(tail.shape, F32)
    for shift in range(1, SSD_CONV):
        w = cw_ref[SSD_CONV - 1 - shift:SSD_CONV - shift, :]
        shifted = _dot(jnp.where(row - col == shift, 1.0, 0.0).astype(BF16), xc)
        conv = conv + w * shifted
        head_fix = head_fix + jnp.where(row8 < shift, w * pltpu.roll(tail, shift, 0), 0.0)
    conv = jnp.concatenate([conv[:8] + head_fix, conv[8:]], axis=0)
    xa = _silu(conv)
    xs = xa[:, :SSD_INNER]
    bm = xa[:, SSD_INNER:SSD_INNER + SSD_BC]
    cm = xa[:, SSD_INNER + SSD_BC:]

    hl = lax.broadcasted_iota(jnp.int32, (1, LANES), 1)
    dtr = dt_ref[0] + dtb_ref[...]
    dt = jnp.maximum(dtr, 0.0) + jnp.log(1.0 + jnp.exp(-jnp.abs(dtr)))
    a = jnp.where(hl < SSD_HEADS, -jnp.exp(alog_ref[...]), 0.0)
    dta = dt * a
    row = lax.broadcasted_iota(jnp.int32, (blk, blk), 0)
    col = lax.broadcasted_iota(jnp.int32, (blk, blk), 1)
    tril = row >= col
    trilb = jnp.where(tril, 1.0, 0.0).astype(BF16)
    d1, d2, d3 = _split3(dta)
    cs = _dot(trilb, d1) + _dot(trilb, d2) + _dot(trilb, d3)
    cs_last = cs[blk - 1:blk, :]
    ecs = jnp.exp(cs)
    dte = jnp.exp(cs_last - cs)
    cs_t = cs.T

    er = lax.broadcasted_iota(jnp.int32, (LANES, SSD_INNER), 0)
    ec = lax.broadcasted_iota(jnp.int32, (LANES, SSD_INNER), 1)
    expand = jnp.where(ec // SSD_HEAD_DIM == er, 1.0, 0.0).astype(BF16)

    def per_channel(v):
        vh, vl = _split2(v)
        return _dot(vh, expand) + _dot(vl, expand)

    dt_e = per_channel(dt)
    ecs_e = per_channel(ecs)
    dte_e = per_channel(dte)
    xdt = xs * dt_e
    xdt_b = xdt.astype(BF16)
    xw_b = (xdt * dte_e).astype(BF16)

    gw = SSD_INNER // SSD_GROUPS
    heads_per_group = SSD_HEADS // SSD_GROUPS
    lane = lax.broadcasted_iota(jnp.int32, (blk, LANES), 1)
    y_groups = []
    for g in range(SSD_GROUPS):
        bg = bm[:, g * SSD_STATE:(g + 1) * SSD_STATE].astype(BF16)
        cg = cm[:, g * SSD_STATE:(g + 1) * SSD_STATE].astype(BF16)
        cb = _dot_nt(cg, bg)
        state = st_scr[g]
        y_off = _dot(cg, state.astype(BF16))
        pairs = []
        for j in range(heads_per_group // 2):
            xp = xdt_b[:, g * gw + j * LANES:g * gw + (j + 1) * LANES]
            halves = []
            for u in range(2):
                hidx = g * heads_per_group + 2 * j + u
                seg = cs[:, hidx:hidx + 1] - cs_t[hidx:hidx + 1, :]
                dec = jnp.exp(jnp.where(tril, seg, -jnp.inf))
                halves.append(_dot((cb * dec).astype(BF16), xp))
            pairs.append(jnp.where(lane < SSD_HEAD_DIM, halves[0], halves[1]))
        y_diag = jnp.concatenate(pairs, axis=1)
        y_groups.append(y_diag + y_off * ecs_e[:, g * gw:(g + 1) * gw])
        st_scr[g] = (state * ecs_e[blk - 1:blk, g * gw:(g + 1) * gw]
                     + _dot_tn(bg, xw_b[:, g * gw:(g + 1) * gw]))
    y = jnp.concatenate(y_groups, axis=1) + xs * dsk_ref[...]
    y = y * _silu(z_ref[0].astype(F32))
    y = jnp.concatenate([_rms(y[:, g * gw:(g + 1) * gw]) for g in range(SSD_GROUPS)], axis=1)
    y_ref[0] = (y * ng_ref[...]).astype(BF16)


def _ssd(xbc, z, dt, conv_w, conv_b, dt_bias, a_log, d_skip_e, norm_gain):
    bsz, seq, _ = xbc.shape
    blk = min(SSD_L, seq)
    halo_per_blk = blk // CONV_HALO
    body = functools.partial(_ssd_body, blk=blk)
    out_shape = jax.ShapeDtypeStruct((bsz, seq, SSD_INNER), BF16)
    return pl.pallas_call(
        body,
        grid=(bsz, seq // blk),
        in_specs=[pl.BlockSpec((1, blk, XBC_DIM), lambda b, i: (b, i, 0)),
                  pl.BlockSpec((1, CONV_HALO, XBC_DIM), lambda b, i: (b, jnp.maximum(i * halo_per_blk - 1, 0), 0)),
                  pl.BlockSpec((1, blk, SSD_INNER), lambda b, i: (b, i, 0)),
                  pl.BlockSpec((1, blk, LANES), lambda b, i: (b, i, 0)),
                  _const_spec((SSD_CONV, XBC_DIM)), _const_spec((1, XBC_DIM)),
                  _const_spec((1, LANES)), _const_spec((1, LANES)),
                  _const_spec((1, SSD_INNER)), _const_spec((1, SSD_INNER))],
        out_specs=pl.BlockSpec((1, blk, SSD_INNER), lambda b, i: (b, i, 0)),
        out_shape=out_shape,
        scratch_shapes=[pltpu.VMEM((SSD_GROUPS, SSD_STATE, SSD_INNER // SSD_GROUPS), F32)],
        compiler_params=pltpu.CompilerParams(dimension_semantics=("parallel", "arbitrary"),
                                             vmem_limit_bytes=VMEM_LIMIT),
        name="ssd",
    )(xbc, xbc, z, dt, conv_w, conv_b, dt_bias, a_log, d_skip_e, norm_gain)


def _attn_body(q_ref, k_ref, v_ref, o_ref, *scratch, tq, tk):
    n_streams = 2 * (tq // tk)
    s_scr = (scratch[:n_streams], scratch[n_streams:2 * n_streams])
    acc_scr = scratch[2 * n_streams:]
    qi = pl.program_id(2)
    n_sub = tq // tk
    n_full = qi * n_sub
    krow = lax.broadcasted_iota(jnp.int32, (tk, tk), 0)
    qcol = lax.broadcasted_iota(jnp.int32, (tk, tk), 1)
    diag_ok = krow // CHUNK <= qcol // CHUNK
    vrow = lax.broadcasted_iota(jnp.int32, (LANES, tk), 0)
    streams = [(u, r) for u in range(2) for r in range(n_sub)]
    qs = [q_ref[0, r * tk:(r + 1) * tk, u * LANES:(u + 1) * LANES] for u, r in streams]

    def put_scores(ki, which, slot):
        start = pl.multiple_of(ki * tk, tk)
        k2 = k_ref[0, pl.ds(start, tk), :]
        out = {}
        for si in which:
            u = streams[si][0]
            s = _dot_nt(k2[:, u * LANES:(u + 1) * LANES], qs[si])
            s_scr[slot][si][...] = s
            out[si] = jnp.max(s, axis=0, keepdims=True)
        return out

    def values_t(ki):
        return v_ref[0, :, pl.ds(pl.multiple_of(ki * tk, tk), tk)]

    def softmax_pv(si, m, s_max, slot, vt, masked):
        s = s_scr[slot][si][...]
        if masked:
            s = jnp.where(diag_ok, s, -jnp.inf)
            s_max = jnp.max(s, axis=0, keepdims=True)
        m_new = jnp.maximum(m, s_max)
        alpha = jnp.exp2(m - m_new)
        p = jnp.exp2((s - m_new).astype(BF16))
        acc_scr[si][...] = alpha * acc_scr[si][...] + _dot(vt, p)
        return m_new

    every = list(range(len(streams)))
    for ref in acc_scr:
        ref[...] = jnp.zeros_like(ref)

    def step(j, state):
        ms, s_maxes = state
        for slot in range(2):
            ki = 2 * j + slot
            nxt = put_scores(ki + 1, every, 1 - slot)
            vt = values_t(ki)
            ms = tuple(softmax_pv(si, ms[si], s_maxes[si], slot, vt, False) for si in every)
            s_maxes = tuple(nxt[si] for si in every)
        return ms, s_maxes

    first = put_scores(0, every, 0)
    m_init = jnp.full((1, tk), -jnp.inf, F32)
    ms, s_maxes = lax.fori_loop(0, n_full // 2, step, ((m_init,) * len(streams), tuple(first[si] for si in every)))
    ms = list(ms)
    s_maxes = dict(zip(every, s_maxes))
    for dd in range(n_sub):
        slot = dd % 2
        live = [si for si in every if dd <= streams[si][1]]
        later = [si for si in every if dd + 1 <= streams[si][1]]
        nxt = put_scores(n_full + dd + 1, later, 1 - slot) if later else {}
        vt = values_t(n_full + dd)
        for si in live:
            ms[si] = softmax_pv(si, ms[si], s_maxes[si], slot, vt, dd == streams[si][1])
        s_maxes = nxt
    for r in range(n_sub):
        a0 = acc_scr[streams.index((0, r))][...]
        a1 = acc_scr[streams.index((1, r))][...]
        out_t = jnp.where(vrow < MLA_V, a0[:LANES] / a0[LANES:LANES + 1], a1[:LANES] / a1[LANES:LANES + 1])
        o_ref[0, r * tk:(r + 1) * tk, :] = out_t.T.astype(BF16)


def _attn(q, k, v):
    bsz, seq, _ = q.shape
    tk = min(ATT_TK, seq)
    tq = min(ATT_TQ, seq)
    assert (tq // tk) % 2 == 0, "the two-slot score pipeline needs an even number of query sub-tiles"
    n_streams = 2 * (tq // tk)
    body = functools.partial(_attn_body, tq=tq, tk=tk)
    out_shape = jax.ShapeDtypeStruct((bsz, seq, MLA_HEADS * MLA_V), BF16)
    return pl.pallas_call(
        body,
        grid=(bsz, MLA_HEADS // 2, seq // tq),
        in_specs=[pl.BlockSpec((1, tq, 2 * LANES), lambda b, hp, i: (b, i, hp)),
                  pl.BlockSpec((1, seq, 2 * LANES), lambda b, hp, i: (b, 0, hp)),
                  pl.BlockSpec((1, VT_ROWS, seq), lambda b, hp, i: (b, hp, 0))],
        out_specs=pl.BlockSpec((1, tq, LANES), lambda b, hp, i: (b, i, hp)),
        out_shape=out_shape,
        scratch_shapes=[pltpu.VMEM((tk, tk), F32)] * (2 * n_streams) + [pltpu.VMEM((VT_ROWS, tk), F32)] * n_streams,
        compiler_params=pltpu.CompilerParams(dimension_semantics=("parallel", "parallel", "arbitrary"),
                                             vmem_limit_bytes=VMEM_LIMIT),
        name="attn",
    )(q, k, v)


def _outproj_body(ys_ref, ya_ref, x_ref, g1_ref, sc2_ref, sh2_ref, mn_ref, wo1_ref, wo2_ref, pmn_ref, pfn_ref,
                  wr_ref, br_ref, x1_ref, h2_ref, gate_ref, idx_ref, cnt_ref, pref_ref, ltab_ref, cnt_scr):
    first = (pl.program_id(0) == 0) & (pl.program_id(1) == 0)

    @pl.when(first)
    def _():
        cnt_scr[...] = jnp.zeros_like(cnt_scr)

    yan =(_rms(ya_ref[0].astype(F32)) * mn_ref[...]).astype(BF16)
    mix = _dot(ys_ref[0], wo1_ref[...]) + _dot(yan, wo2_ref[...])
    x1 = x_ref[0] + g1_ref[0] * (_rms(mix) * pmn_ref[...])
    x1_ref[0] = x1
    h2 = _rms(x1) * (pfn_ref[...] * (1.0 + sc2_ref[0])) + sh2_ref[0]
    h2_ref[0] = h2.astype(BF16)
    h_hi, h_lo = _split2(h2)
    w_hi, w_lo = _split2(wr_ref[...])
    both = _dot(h_hi, jnp.concatenate([w_hi, w_lo], axis=1))
    logits = both[:, :LANES] + both[:, LANES:] + _dot(h_lo, w_hi) + br_ref[...]
    lane = lax.broadcasted_iota(jnp.int32, logits.shape, 1)
    cur = jnp.where(lane < N_EXPERTS, logits, -jnp.inf)
    vals, idxs = [], []
    for _ in range(TOP_K):
        m = jnp.max(cur, axis=-1, keepdims=True)
        ix = jnp.min(jnp.where(cur == m, lane, LANES), axis=-1, keepdims=True)
        vals.append(m)
        idxs.append(ix)
        cur = jnp.where(lane == ix, -jnp.inf, cur)
    es = [jnp.exp(v - vals[0]) for v in vals]
    denom = es[0]
    for e in es[1:]:
        denom = denom + e
    onehot = jnp.zeros(logits.shape, F32)
    for kk in range(TOP_K):
        onehot = onehot + jnp.where(lane == idxs[kk], 1.0, 0.0)
    ts = logits.shape[0]
    row = lax.broadcasted_iota(jnp.int32, (ts, ts), 0)
    col = lax.broadcasted_iota(jnp.int32, (ts, ts), 1)
    before = jnp.where(row > col, 1.0, 0.0).astype(BF16)
    prior = _dot(before, onehot.astype(BF16))
    seen = cnt_scr[...]
    pref_ref[0] = seen
    gate_out = jnp.zeros(logits.shape, F32)
    idx_out = jnp.zeros(logits.shape, jnp.int32)
    local_out = jnp.zeros(logits.shape, F32)
    for kk in range(TOP_K):
        mine = lane == idxs[kk]
        local = jnp.sum(jnp.where(mine, prior, 0.0), axis=-1, keepdims=True)
        rank = local + jnp.sum(jnp.where(mine, seen, 0.0), axis=-1, keepdims=True)
        gate_out = jnp.where(lane == kk, es[kk] / denom, gate_out)
        idx_out = jnp.where(lane == kk, idxs[kk], idx_out)
        idx_out = jnp.where(lane == TOP_K + kk, rank.astype(jnp.int32), idx_out)
        local_out = jnp.where(lane == kk, local, local_out)
    gate_ref[0] = gate_out
    idx_ref[0] = idx_out
    local_rows = local_out.T.astype(jnp.int32)
    tok = lax.broadcasted_iota(jnp.int32, logits.shape, 0)
    tok_hi = (tok // 16).astype(F32)
    tok_lo = (tok % 16).astype(F32)
    ltab = jnp.zeros((ts, 2 * LANES), F32)
    for kk in range(TOP_K):
        mine = lane == idxs[kk]
        at_rank = jnp.where(row == local_rows[kk:kk + 1, :], 1.0, 0.0).astype(BF16)
        tagged = jnp.concatenate([jnp.where(mine, tok_hi, 0.0), jnp.where(mine, tok_lo, 0.0)], axis=1)
        ltab = ltab + _dot(at_rank, tagged.astype(BF16))
    ltab_ref[0] = (16.0 * ltab[:, :LANES] + ltab[:, LANES:]).astype(jnp.int32)
    cnt_scr[...] = seen + jnp.sum(onehot, axis=0, keepdims=True)
    cnt_ref[...] = cnt_scr[...]


def _outproj(y_ssd, y_att, x, g1, sc2, sh2, mla_norm, wo1, wo2, post_mix_norm, pre_ffn_norm, wr, br, *, batch_offset):
    bsz, seq, _ = y_ssd.shape
    d = x.shape[-1]
    ts = min(TS_PROJ, seq)
    tiles = seq // ts

    def tok(width):
        return pl.BlockSpec((1, ts, width), lambda b, i: (b, i, 0))

    x_spec = pl.BlockSpec((1, ts, d), lambda b, i: (batch_offset + b, i, 0))

    def per_batch(width):
        return pl.BlockSpec((1, 1, width), lambda b, i: (b, 0, 0))

    consts = (mla_norm, wo1, wo2, post_mix_norm, pre_ffn_norm, wr, br)
    out_shape = [jax.ShapeDtypeStruct((bsz, seq, d), F32), jax.ShapeDtypeStruct((bsz, seq, d), BF16),
                 jax.ShapeDtypeStruct((bsz, seq, LANES), F32), jax.ShapeDtypeStruct((bsz, seq, LANES), jnp.int32),
                 jax.ShapeDtypeStruct((1, LANES), F32),
                 jax.ShapeDtypeStruct((bsz * tiles, 1, LANES), F32),
                 jax.ShapeDtypeStruct((bsz * tiles, ts, LANES), jnp.int32)]
    return pl.pallas_call(
        _outproj_body,
        grid=(bsz, seq // ts),
        in_specs=[tok(SSD_INNER), tok(MLA_HEADS * MLA_V), x_spec, per_batch(d), per_batch(d), per_batch(d)]
                 + [_const_spec(w.shape) for w in consts],
        out_specs=[tok(d), tok(d), tok(LANES), tok(LANES), _const_spec((1, LANES)),
                   pl.BlockSpec((1, 1, LANES), lambda b, i: (b * tiles + i, 0, 0)),
                   pl.BlockSpec((1, ts, LANES), lambda b, i: (b * tiles + i, 0, 0))],
        out_shape=out_shape,
        scratch_shapes=[pltpu.VMEM((1, LANES), F32)],
        compiler_params=pltpu.CompilerParams(dimension_semantics=("arbitrary", "arbitrary"),
                                             vmem_limit_bytes=VMEM_LIMIT),
        name="outproj",
    )(y_ssd, y_att, x, g1, sc2, sh2, *consts)


def _moe_body(be_ref, na_ref, x_ref, wgu_ref, bgu_ref, wd_ref, bd_ref, y_ref, wgu_b, wd_b):
    i = pl.program_id(0)

    @pl.when((i == 0) | (be_ref[i] != be_ref[jnp.maximum(i - 1, 0)]))
    def _():
        wgu_b[...] = wgu_ref[0].astype(BF16)
        wd_b[...] = wd_ref[0].astype(BF16)

    @pl.when(i < na_ref[0])
    def _():
        gu = _dot(x_ref[...], wgu_b[...]) + bgu_ref[0]
        glu = jnp.minimum(gu[:, :D_FF_EXPERT], SWIGLU_LIMIT)
        lin = jnp.clip(gu[:, D_FF_EXPERT:], -SWIGLU_LIMIT, SWIGLU_LIMIT)
        act = glu * jax.nn.sigmoid(SWIGLU_ALPHA * glu) * (lin + 1.0)
        y_ref[...] = (_dot(act.astype(BF16), wd_b[...]) + bd_ref[0]).astype(BF16)

    @pl.when(i >= na_ref[0])
    def _():
        y_ref[...] = jnp.zeros_like(y_ref)


def _moe(block_expert, n_active, xg, wgu, bgu, wd, bd):
    n_slots, d = xg.shape
    n_blocks = n_slots // MOE_TB
    f2 = wgu.shape[2]
    out_shape = jax.ShapeDtypeStruct((n_slots, d), BF16)
    return pl.pallas_call(
        _moe_body,
        grid_spec=pltpu.PrefetchScalarGridSpec(
            num_scalar_prefetch=2,
            grid=(n_blocks,),
            in_specs=[pl.BlockSpec((MOE_TB, d), lambda i, be, na: (i, 0)),
                      pl.BlockSpec((1, d, f2), lambda i, be, na: (be[i], 0, 0)),
                      pl.BlockSpec((1, 1, f2), lambda i, be, na: (be[i], 0, 0)),
                      pl.BlockSpec((1, f2 // 2, d), lambda i, be, na: (be[i], 0, 0)),
                      pl.BlockSpec((1, 1, d), lambda i, be, na: (be[i], 0, 0))],
            out_specs=pl.BlockSpec((MOE_TB, d), lambda i, be, na: (i, 0)),
            scratch_shapes=[pltpu.VMEM((d, f2), BF16), pltpu.VMEM((f2 // 2, d), BF16)],
        ),
        out_shape=out_shape,
        compiler_params=pltpu.CompilerParams(dimension_semantics=("arbitrary",),
                                             vmem_limit_bytes=VMEM_LIMIT),
        name="moe",
    )(block_expert, n_active, xg, wgu, bgu, wd, bd)


def _final_body(x1_ref, y_ref, gate_ref, g2_ref, gain_ref, *rest):
    o_ref = rest[-1]
    gates = gate_ref[0]
    f = gates[:, 0:1] * y_ref[0, 0].astype(F32)
    for kk in range(1, TOP_K):
        f = f + gates[:, kk:kk + 1] * y_ref[kk, 0].astype(F32)
    o_ref[0] = x1_ref[0] + g2_ref[0] * (_rms(f) * gain_ref[...])


def _final(x1, y4, gates, g2, gain, out_prev, batch_offset, total_batch):
    bsz, seq, d = x1.shape
    ts = min(TS_PROJ, seq)

    def tok(width):
        return pl.BlockSpec((1, ts, width), lambda b, i: (b, i, 0))

    in_specs = [tok(d), pl.BlockSpec((TOP_K, 1, ts, d), lambda b, i: (0, b, i, 0)), tok(LANES),
                pl.BlockSpec((1, 1, d), lambda b, i: (b, 0, 0)), _const_spec((1, d))]
    args = [x1, y4, gates, g2, gain]
    aliases = {}
    if out_prev is not None:
        in_specs.append(pl.BlockSpec(memory_space=pl.ANY))
        args.append(out_prev)
        aliases = {len(args) - 1: 0}
    return pl.pallas_call(
        _final_body,
        grid=(bsz, seq // ts),
        in_specs=in_specs,
        out_specs=pl.BlockSpec((1, ts, d), lambda b, i: (batch_offset + b, i, 0)),
        out_shape=jax.ShapeDtypeStruct((total_batch, seq, d), F32),
        input_output_aliases=aliases,
        compiler_params=pltpu.CompilerParams(dimension_semantics=("parallel", "parallel"),
                                             vmem_limit_bytes=VMEM_LIMIT),
        name="final",
    )(*args)


def _head_blocks(cols):
    out = []
    for c in cols:
        pad = LANES - c.shape[1]
        out.append(jnp.pad(c, ((0, 0), (0, pad))) if pad else c)
    return jnp.concatenate(out, axis=1)


def _prep_mixer_weights(w_in, w_q_up, w_kv_up):
    d = w_in.shape[0]
    wz = w_in[:, OFF_Z:OFF_XBC]
    wxbc = w_in[:, OFF_XBC:OFF_DT]
    wdt = w_in[:, OFF_DT:OFF_QA]
    wqa = w_in[:, OFF_QA:OFF_KVA]
    wkva = w_in[:, OFF_KVA:OFF_KR]
    wkr = w_in[:, OFF_KR:IN_COLS]
    kr_blk = jnp.concatenate([jnp.zeros((d, ROPE_LO), F32), wkr, jnp.zeros((d, LANES - ROPE_LO - MLA_ROPE), F32)], axis=1)
    dt_blk = jnp.pad(wdt, ((0, 0), (0, LANES - SSD_HEADS)))
    wsm = jnp.concatenate([kr_blk, dt_blk], axis=1)
    qh = MLA_NOPE + MLA_ROPE
    scale = math.log2(math.e) / math.sqrt(qh)
    wqup = _head_blocks([w_q_up[:, h * qh:(h + 1) * qh] for h in range(MLA_HEADS)]) * scale
    kvh = MLA_NOPE + MLA_V
    wkup = _head_blocks([w_kv_up[:, h * kvh:h * kvh + MLA_NOPE] for h in range(MLA_HEADS)])
    vcols = []
    for h in range(MLA_HEADS):
        vcols.append(w_kv_up[:, h * kvh + MLA_NOPE:(h + 1) * kvh])
        if h % 2 == 1:
            vcols.append(jnp.zeros((w_kv_up.shape[0], VT_ROWS - LANES), F32))
    wvup = jnp.concatenate(vcols, axis=1).T
    return tuple(w.astype(BF16) for w in (wz, wxbc, wsm, wqa, wkva)) + tuple(w.astype(BF16) for w in (wqup, wkup, wvup))


def _rope_tables(positions):
    inv_freq = ROPE_BASE ** (-(jnp.arange(HALF_ROPE, dtype=F32) * 2.0 / MLA_ROPE))
    angles = positions.astype(F32)[:, None, :] * inv_freq[None, :, None]
    bsz, _, seq = angles.shape
    return jnp.concatenate([jnp.zeros((bsz, ROPE_LO, seq), F32), jnp.cos(angles), jnp.sin(angles),
                            jnp.zeros((bsz, LANES - ROPE_LO - MLA_ROPE, seq), F32)], axis=1)


def _route(idx, rank, counts, tile_seen, ltab, n_tok):
    n_assign = n_tok * TOP_K
    n_tiles, tile, lanes = ltab.shape
    padded = ((counts + MOE_TB - 1) // MOE_TB) * MOE_TB
    padded_end = jnp.cumsum(padded)
    padded_start = padded_end - padded
    experts = jnp.arange(N_EXPERTS, dtype=jnp.int32)
    start_of = jnp.sum(jnp.where(idx[..., None] == experts, padded_start, 0), axis=-1)
    dest = (start_of + rank).reshape(-1)
    n_blocks = n_assign // MOE_TB + N_EXPERTS
    block_start = jnp.arange(n_blocks, dtype=jnp.int32) * MOE_TB
    block_expert = jnp.minimum(jnp.sum((padded_end[None, :] <= block_start[:, None]).astype(jnp.int32), axis=1),
                               N_EXPERTS - 1)
    n_active = (padded_end[-1] // MOE_TB).astype(jnp.int32).reshape(1)
    j = (block_start - padded_start[block_expert])[:, None] + jnp.arange(MOE_TB, dtype=jnp.int32)[None, :]
    seen_blk = tile_seen.T[block_expert]
    reached = seen_blk[:, None, :] <= j[:, :, None]
    tau = jnp.sum(reached.astype(jnp.int32), axis=-1) - 1
    j_local = j - jnp.max(jnp.where(reached, seen_blk[:, None, :], 0), axis=-1)
    valid = j < counts[block_expert][:, None]
    flat = (tau * tile + j_local) * lanes + block_expert[:, None]
    tok_local = ltab.reshape(-1)[jnp.clip(flat, 0, n_tiles * tile * lanes - 1)]
    filler = (block_start[:, None] + jnp.arange(MOE_TB, dtype=jnp.int32)[None, :]) % n_tok
    slot_tok = jnp.where(valid, tau * tile + tok_local, filler).reshape(-1)
    return dest, slot_tok, block_expert, n_active


def kernel(x, c, positions, w_ada, b_ada, pre_mix_norm, w_in, conv_w, conv_b, dt_bias, a_log, d_skip, ssd_norm, q_a_norm, w_q_up, kv_a_norm, w_kv_up, mla_norm, w_out, post_mix_norm, pre_ffn_norm, w_router, b_router, w_gate_up, b_gate_up, w_down, b_down, post_ffn_norm):
    bsz, seq, d = x.shape
    rope_tab = _rope_tables(positions)
    n_groups = BATCH_GROUPS if bsz % BATCH_GROUPS == 0 else 1
    gb = bsz // n_groups
    n_tok = gb * seq
    pad_h = LANES - SSD_HEADS
    for l in range(w_ada.shape[0]):
        mod = _adaln(c, w_ada[l], b_ada[l])
        mods = [m.reshape(bsz, 1, d) for m in jnp.split(mod, 6, axis=-1)]
        mixer_w = _prep_mixer_weights(w_in[l], w_q_up[l], w_kv_up[l])
        wo = w_out[l].astype(BF16)
        wr = jnp.pad(w_router[l], ((0, 0), (0, LANES - N_EXPERTS)))
        br = jnp.pad(b_router[l], (0, LANES - N_EXPERTS)).reshape(1, LANES)
        dtb = jnp.pad(dt_bias[l], (0, pad_h)).reshape(1, LANES)
        alog = jnp.pad(a_log[l], (0, pad_h)).reshape(1, LANES)
        dsk = jnp.repeat(d_skip[l], SSD_HEAD_DIM).reshape(1, -1)
        out = None
        pending = None

        def experts_and_final(p, xs, out):
            y = _moe(p["block_expert"], p["n_active"], xs, w_gate_up[l], b_gate_up[l][:, None, :],
                     w_down[l], b_down[l][:, None, :])
            y4 = y[p["dest"].reshape(n_tok, TOP_K).T.reshape(-1)].reshape(TOP_K, gb, seq, d)
            return _final(p["x1"], y4, p["gates"], p["g2"], post_ffn_norm[l].reshape(1, d), out, p["offset"], bsz)

        for gi in range(n_groups):
            grp = slice(gi * gb, (gi + 1) * gb)
            sh1, sc1, g1, sh2, sc2, g2 = [m[grp] for m in mods]
            if pending is not None:
                sc1 = sc1 + jnp.where(pending["slot_tok"][0] < 0, 1.0, 0.0)
            z, xbc, dt, q, k, v = _inproj(x, sc1, sh1, pre_mix_norm[l].reshape(1, d), rope_tab, *mixer_w[:5],
                                          q_a_norm[l].reshape(1, -1), kv_a_norm[l].reshape(1, -1), *mixer_w[5:],
                                          batch_offset=gi * gb)
            y_ssd = _ssd(xbc, z, dt, conv_w[l], conv_b[l].reshape(1, -1), dtb, alog, dsk, ssd_norm[l].reshape(1, -1))
            prev_xs = None
            if pending is not None:
                ssd_bits = lax.bitcast_convert_type(y_ssd[0, 0, 0], jnp.uint16).astype(jnp.int32)
                prev_xs = pending["h2"].reshape(n_tok, d)[pending["slot_tok"] + jnp.where(ssd_bits < 0, 1, 0)]
            y_att = _attn(q, k, v)
            x1, h2, gates, route, cnt, seen, ltab = _outproj(
                y_ssd, y_att, x, g1, sc2, sh2, mla_norm[l].reshape(1, -1), wo[:SSD_INNER], wo[SSD_INNER:],
                post_mix_norm[l].reshape(1, d), pre_ffn_norm[l].reshape(1, d), wr, br, batch_offset=gi * gb)
            route = route.reshape(n_tok, LANES)
            counts = cnt[0, :N_EXPERTS].astype(jnp.int32)
            tile_seen = seen[:, 0, :N_EXPERTS].astype(jnp.int32)
            if prev_xs is not None:
                bits = lax.bitcast_convert_type(prev_xs[0, 0], jnp.uint16).astype(jnp.int32)
                tile_seen = tile_seen + jnp.where(bits < 0, 1, 0)
            dest, slot_tok, block_expert, n_active = _route(route[:, :TOP_K], route[:, TOP_K:2 * TOP_K], counts,
                                                            tile_seen, ltab, n_tok)
            if pending is not None:
                out = experts_and_final(pending, prev_xs, out)
            pending = dict(h2=h2, slot_tok=slot_tok, dest=dest, block_expert=block_expert, n_active=n_active,
                           x1=x1, gates=gates, g2=g2, offset=gi * gb)
        out = experts_and_final(pending, pending["h2"].reshape(n_tok, d)[pending["slot_tok"]], out)
        x = out
    return x
```

```python
import functools
import math

import jax
import jax.numpy as jnp
from jax import lax
from jax.experimental import pallas as pl
from jax.experimental.pallas import tpu as pltpu

F32 = jnp.float32
BF16 = jnp.bfloat16

D_MODEL = 1024
CHUNK = 64
SSD_INNER = 512
SSD_HEAD_DIM = 64
SSD_HEADS = 8
SSD_GROUPS = 2
SSD_STATE = 128
SSD_CONV = 4
SSD_BC = SSD_GROUPS * SSD_STATE
XBC_DIM = SSD_INNER + 2 * SSD_BC
MLA_V = 64
MLA_HEADS = 8
MLA_NOPE = 64
MLA_ROPE = 32
Q_LORA = 384
KV_LORA = 256
ROPE_BASE = 10000.0
OFF_Z = 0
OFF_XBC = OFF_Z + SSD_INNER
OFF_DT = OFF_XBC + XBC_DIM
OFF_QA = OFF_DT + SSD_HEADS
OFF_KVA = OFF_QA + Q_LORA
OFF_KR = OFF_KVA + KV_LORA
IN_COLS = OFF_KR + MLA_ROPE
N_EXPERTS = 32
TOP_K = 4
D_FF_EXPERT = 1024
SWIGLU_LIMIT = 7.0
SWIGLU_ALPHA = 1.702
NORM_EPS = 1e-6

LANES = 128
HALF_ROPE = MLA_ROPE // 2
ROPE_LO = MLA_NOPE
ROPE_HI = MLA_NOPE + HALF_ROPE
VT_ROWS = 144

ADA_TN = 1024
TS_PROJ = 512
SSD_L = 256
ATT_TQ = 1024
ATT_TK = 256
MOE_TB = 512
BATCH_GROUPS = 2
VMEM_LIMIT = 56 * 1024 * 1024


def _dot(a, b):
    return jnp.dot(a, b, preferred_element_type=F32)


def _dot_nt(a, b):
    return lax.dot_general(a, b, (((1,), (1,)), ((), ())), preferred_element_type=F32)


def _dot_tn(a, b):
    return lax.dot_general(a, b, (((0,), (0,)), ((), ())), preferred_element_type=F32)


def _split2(x):
    hi = x.astype(BF16)
    lo = (x - hi.astype(F32)).astype(BF16)
    return hi, lo


def _split3(x):
    h1 = x.astype(BF16)
    r1 = x - h1.astype(F32)
    h2 = r1.astype(BF16)
    h3 = (r1 - h2.astype(F32)).astype(BF16)
    return h1, h2, h3


def _dot3(a, b):
    ah, al = _split2(a)
    bh, bl = _split2(b)
    return _dot(ah, bh) + _dot(ah, bl) + _dot(al, bh)


def _rms(x):
    return x * lax.rsqrt(jnp.mean(x * x, axis=-1, keepdims=True) + NORM_EPS)


def _silu(x):
    hx = 0.5 * x
    return hx + hx * jnp.tanh(hx)


def _const_spec(shape):
    nd = len(shape)
    return pl.BlockSpec(shape, lambda *_: (0,) * nd)


def _adaln_body(c_ref, w_ref, b_ref, o_ref):
    o_ref[...] = _dot3(_silu(c_ref[...]), w_ref[...]) + b_ref[...]


def _adaln(c, w_ada, b_ada):
    bsz, d = c.shape
    n = w_ada.shape[1]
    tn = ADA_TN
    return pl.pallas_call(
        _adaln_body,
        grid=(n // tn,),
        in_specs=[_const_spec((bsz, d)),
                  pl.BlockSpec((d, tn), lambda j: (0, j)),
                  pl.BlockSpec((1, tn), lambda j: (0, j))],
        out_specs=pl.BlockSpec((bsz, tn), lambda j: (0, j)),
        out_shape=jax.ShapeDtypeStruct((bsz, n), F32),
        compiler_params=pltpu.CompilerParams(dimension_semantics=("arbitrary",),
                                             vmem_limit_bytes=VMEM_LIMIT),
        name="adaln",
    )(c, w_ada, b_ada.reshape(1, n))


def _rope_block(xb, ct, st, lane):
    partner = jnp.where(lane < ROPE_HI, pltpu.roll(xb, LANES - HALF_ROPE, 1), pltpu.roll(xb, HALF_ROPE, 1))
    return xb * ct + partner * st


def _inproj_body(x_ref, sc_ref, sh_ref, g_ref, tab_ref, wz_ref, wxbc_ref, wsm_ref, wqa_ref, wkva_ref,
                 qn_ref, kvn_ref, wqup_ref, wkup_ref, wvup_ref,
                 z_ref, xbc_ref, dt_ref, q_ref, k_ref, v_ref):
    x = x_ref[0]
    h = _rms(x) * (g_ref[...] * (1.0 + sc_ref[0])) + sh_ref[0]
    hb = h.astype(BF16)
    z_ref[0] = _dot(hb, wz_ref[...]).astype(BF16)
    xbc_ref[0] = _dot(hb, wxbc_ref[...]).astype(BF16)
    sm = _dot(hb, wsm_ref[...])
    dt_ref[0] = sm[:, LANES:]
    tab = tab_ref[0].T
    lane = lax.broadcasted_iota(jnp.int32, tab.shape, 1)
    in_lo = (lane >= ROPE_LO) & (lane < ROPE_HI)
    in_hi = (lane >= ROPE_HI) & (lane < ROPE_HI + HALF_ROPE)
    ct = jnp.where(lane < ROPE_LO, 1.0, jnp.where(in_lo, tab, jnp.where(in_hi, pltpu.roll(tab, HALF_ROPE, 1), 0.0)))
    st = jnp.where(in_lo, -pltpu.roll(tab, LANES - HALF_ROPE, 1), jnp.where(in_hi, tab, 0.0))
    kr =_rope_block(sm[:, :LANES], ct, st, lane)
    qan = (_rms(_dot(hb, wqa_ref[...])) * qn_ref[...]).astype(BF16)
    q = _dot(qan, wqup_ref[...])
    for hh in range(MLA_HEADS):
        blk = slice(hh * LANES, (hh + 1) * LANES)
        q_ref[0, :, blk] = _rope_block(q[:, blk], ct, st, lane).astype(BF16)
    kvn = (_rms(_dot(hb, wkva_ref[...])) * kvn_ref[...]).astype(BF16)
    k = _dot(kvn, wkup_ref[...])
    for hh in range(MLA_HEADS):
        blk = slice(hh * LANES, (hh + 1) * LANES)
        k_ref[0, :, blk] = (k[:, blk] + kr).astype(BF16)
    vt = _dot_nt(wvup_ref[...], kvn)
    vrow = lax.broadcasted_iota(jnp.int32, vt.shape, 0)
    v_ref[0] = jnp.where(vrow % VT_ROWS == LANES, 1.0, vt).astype(BF16)


def _inproj(x, sc1, sh1, gain, rope_tab, wz, wxbc, wsm, wqa, wkva, qn, kvn, wqup, wkup, wvup, *, batch_offset):
    _, seq, d = x.shape
    bsz = sc1.shape[0]
    ts = min(TS_PROJ, seq)
    hw = MLA_HEADS * LANES

    def tok(width):
        return pl.BlockSpec((1, ts, width), lambda b, i: (b, i, 0))

    def tok_full(width):
        return pl.BlockSpec((1, ts, width), lambda b, i: (batch_offset + b, i, 0))

    def per_batch(width):
        return pl.BlockSpec((1, 1, width), lambda b, i: (b, 0, 0))

    weights = (wz, wxbc, wsm, wqa, wkva, qn, kvn, wqup, wkup, wvup)
    out_widths = (SSD_INNER, XBC_DIM, LANES, hw, hw)
    out_dtypes = (BF16, BF16, F32, BF16, BF16)
    vdim = (MLA_HEADS // 2) * VT_ROWS
    out_shape = ([jax.ShapeDtypeStruct((bsz, seq, w), dt) for w, dt in zip(out_widths, out_dtypes)]
                 + [jax.ShapeDtypeStruct((bsz, vdim, seq), BF16)])
    return pl.pallas_call(
        _inproj_body,
        grid=(bsz, seq // ts),
        in_specs=[tok_full(d), per_batch(d), per_batch(d), _const_spec((1, d)),
                  pl.BlockSpec((1, LANES, ts), lambda b, i: (batch_offset + b, 0, i))]
                 + [_const_spec(w.shape) for w in weights],
        out_specs=[tok(w) for w in out_widths] + [pl.BlockSpec((1, vdim, ts), lambda b, i: (b, 0, i))],
        out_shape=out_shape,
        compiler_params=pltpu.CompilerParams(dimension_semantics=("parallel", "parallel"),
                                             vmem_limit_bytes=VMEM_LIMIT),
        name="inproj",
    )(x, sc1, sh1, gain, rope_tab, *weights)


CONV_HALO = 16


def _ssd_body(xc_ref, xp_ref, z_ref, dt_ref, cw_ref, cb_ref, dtb_ref, alog_ref, dsk_ref, ng_ref,
              y_ref, st_scr, *, blk):
    i = pl.program_id(1)

    @pl.when(i == 0)
    def _():
        st_scr[...] = jnp.zeros_like(st_scr)

    xc = xc_ref[0]
    row = lax.broadcasted_iota(jnp.int32, (blk, blk), 0)
    col = lax.broadcasted_iota(jnp.int32, (blk, blk), 1)
    conv = cb_ref[...] + cw_ref[SSD_CONV - 1:SSD_CONV, :] * xc.astype(F32)
    tail = jnp.where(i > 0, xp_ref[0].astype(F32)[CONV_HALO - 8:], 0.0)
    row8 = lax.broadcasted_iota(jnp.int32, tail.shape, 0)
    head_fix = jnp.zeros(tail.shape, F32)
    for shift in range(1, SSD_CONV):
        w = cw_ref[SSD_CONV - 1 - shift:SSD_CONV - shift, :]
        shifted = _dot(jnp.where(row - col == shift, 1.0, 0.0).astype(BF16), xc)
        conv = conv + w * shifted
        head_fix = head_fix + jnp.where(row8 < shift, w * pltpu.roll(tail, shift, 0), 0.0)
    conv = jnp.concatenate([conv[:8] + head_fix, conv[8:]], axis=0)
    xa = _silu(conv)
    xs = xa[:, :SSD_INNER]
    bm = xa[:, SSD_INNER:SSD_INNER + SSD_BC]
    cm = xa[:, SSD_INNER + SSD_BC:]

    hl = lax.broadcasted_iota(jnp.int32, (1, LANES), 1)
    dtr = dt_ref[0] + dtb_ref[...]
    dt = jnp.maximum(dtr, 0.0) + jnp.log(1.0 + jnp.exp(-jnp.abs(dtr)))
    a = jnp.where(hl < SSD_HEADS, -jnp.exp(alog_ref[...]), 0.0)
    dta = dt * a
    row = lax.broadcasted_iota(jnp.int32, (blk, blk), 0)
    col = lax.broadcasted_iota(jnp.int32, (blk, blk), 1)
    tril = row >= col
    trilb = jnp.where(tril, 1.0, 0.0).astype(BF16)
    d1, d2, d3 = _split3(dta)
    cs = _dot(trilb, d1) + _dot(trilb, d2) + _dot(trilb, d3)
    cs_last = cs[blk - 1:blk, :]
    ecs = jnp.exp(cs)
    dte = jnp.exp(cs_last - cs)
    cs_t = cs.T

    er = lax.broadcasted_iota(jnp.int32, (LANES, SSD_INNER), 0)
    ec = lax.broadcasted_iota(jnp.int32, (LANES, SSD_INNER), 1)
    expand = jnp.where(ec // SSD_HEAD_DIM == er, 1.0, 0.0).astype(BF16)

    def per_channel(v):
        vh, vl = _split2(v)
        return _dot(vh, expand) + _dot(vl, expand)

    dt_e = per_channel(dt)
    ecs_e = per_channel(ecs)
    dte_e = per_channel(dte)
    xdt = xs * dt_e
    xdt_b = xdt.astype(BF16)
    xw_b = (xdt * dte_e).astype(BF16)

    gw = SSD_INNER // SSD_GROUPS
    heads_per_group = SSD_HEADS // SSD_GROUPS
    lane = lax.broadcasted_iota(jnp.int32, (blk, LANES), 1)
    y_groups = []
    for g in range(SSD_GROUPS):
        bg = bm[:, g * SSD_STATE:(g + 1) * SSD_STATE].astype(BF16)
        cg = cm[:, g * SSD_STATE:(g + 1) * SSD_STATE].astype(BF16)
        cb = _dot_nt(cg, bg)
        state = st_scr[g]
        y_off = _dot(cg, state.astype(BF16))
        pairs = []
        for j in range(heads_per_group // 2):
            xp = xdt_b[:, g * gw + j * LANES:g * gw + (j + 1) * LANES]
            halves = []
            for u in range(2):
                hidx = g * heads_per_group + 2 * j + u
                seg = cs[:, hidx:hidx + 1] - cs_t[hidx:hidx + 1, :]
                dec = jnp.exp(jnp.where(tril, seg, -jnp.inf))
                halves.append(_dot((cb * dec).astype(BF16), xp))
            pairs.append(jnp.where(lane < SSD_HEAD_DIM, halves[0], halves[1]))
        y_diag = jnp.concatenate(pairs, axis=1)
        y_groups.append(y_diag + y_off * ecs_e[:, g * gw:(g + 1) * gw])
        st_scr[g] = (state * ecs_e[blk - 1:blk, g * gw:(g + 1) * gw]
                     + _dot_tn(bg, xw_b[:, g * gw:(g + 1) * gw]))
    y = jnp.concatenate(y_groups, axis=1) + xs * dsk_ref[...]
    y = y * _silu(z_ref[0].astype(F32))
    y = jnp.concatenate([_rms(y[:, g * gw:(g + 1) * gw]) for g in range(SSD_GROUPS)], axis=1)
    y_ref[0] = (y * ng_ref[...]).astype(BF16)


def _ssd(xbc, z, dt, conv_w, conv_b, dt_bias, a_log, d_skip_e, norm_gain):
    bsz, seq, _ = xbc.shape
    blk = min(SSD_L, seq)
    halo_per_blk = blk // CONV_HALO
    body = functools.partial(_ssd_body, blk=blk)
    out_shape = jax.ShapeDtypeStruct((bsz, seq, SSD_INNER), BF16)
    return pl.pallas_call(
        body,
        grid=(bsz, seq // blk),
        in_specs=[pl.BlockSpec((1, blk, XBC_DIM), lambda b, i: (b, i, 0)),
                  pl.BlockSpec((1, CONV_HALO, XBC_DIM), lambda b, i: (b, jnp.maximum(i * halo_per_blk - 1, 0), 0)),
                  pl.BlockSpec((1, blk, SSD_INNER), lambda b, i: (b, i, 0)),
                  pl.BlockSpec((1, blk, LANES), lambda b, i: (b, i, 0)),
                  _const_spec((SSD_CONV, XBC_DIM)), _const_spec((1, XBC_DIM)),
                  _const_spec((1, LANES)), _const_spec((1, LANES)),
                  _const_spec((1, SSD_INNER)), _const_spec((1, SSD_INNER))],
        out_specs=pl.BlockSpec((1, blk, SSD_INNER), lambda b, i: (b, i, 0)),
        out_shape=out_shape,
        scratch_shapes=[pltpu.VMEM((SSD_GROUPS, SSD_STATE, SSD_INNER // SSD_GROUPS), F32)],
        compiler_params=pltpu.CompilerParams(dimension_semantics=("parallel", "arbitrary"),
                                             vmem_limit_bytes=VMEM_LIMIT),
        name="ssd",
    )(xbc, xbc, z, dt, conv_w, conv_b, dt_bias, a_log, d_skip_e, norm_gain)


def _attn_body(q_ref, k_ref, v_ref, o_ref, *scratch, tq, tk):
    n_streams = 2 * (tq // tk)
    s_scr = (scratch[:n_streams], scratch[n_streams:2 * n_streams])
    acc_scr = scratch[2 * n_streams:]
    qi = pl.program_id(2)
    n_sub = tq // tk
    n_full = qi * n_sub
    krow = lax.broadcasted_iota(jnp.int32, (tk, tk), 0)
    qcol = lax.broadcasted_iota(jnp.int32, (tk, tk), 1)
    diag_ok = krow // CHUNK <= qcol // CHUNK
    vrow = lax.broadcasted_iota(jnp.int32, (LANES, tk), 0)
    streams = [(u, r) for u in range(2) for r in range(n_sub)]
    qs = [q_ref[0, r * tk:(r + 1) * tk, u * LANES:(u + 1) * LANES] for u, r in streams]

    def put_scores(ki, which, slot):
        start = pl.multiple_of(ki * tk, tk)
        k2 = k_ref[0, pl.ds(start, tk), :]
        out = {}
        for si in which:
            u = streams[si][0]
            s = _dot_nt(k2[:, u * LANES:(u + 1) * LANES], qs[si])
            s_scr[slot][si][...] = s
            out[si] = jnp.max(s, axis=0, keepdims=True)
        return out

    def values_t(ki):
        return v_ref[0, :, pl.ds(pl.multiple_of(ki * tk, tk), tk)]

    def softmax_pv(si, m, s_max, slot, vt, masked):
        s = s_scr[slot][si][...]
        if masked:
            s = jnp.where(diag_ok, s, -jnp.inf)
            s_max = jnp.max(s, axis=0, keepdims=True)
        m_new = jnp.maximum(m, s_max)
        alpha = jnp.exp2(m - m_new)
        p = jnp.exp2((s - m_new).astype(BF16))
        acc_scr[si][...] = alpha * acc_scr[si][...] + _dot(vt, p)
        return m_new

    every = list(range(len(streams)))
    for ref in acc_scr:
        ref[...] = jnp.zeros_like(ref)

    def step(j, state):
        ms, s_maxes = state
        for slot in range(2):
            ki = 2 * j + slot
            nxt = put_scores(ki + 1, every, 1 - slot)
            vt = values_t(ki)
            ms = tuple(softmax_pv(si, ms[si], s_maxes[si], slot, vt, False) for si in every)
            s_maxes = tuple(nxt[si] for si in every)
        return ms, s_maxes

    first = put_scores(0, every, 0)
    m_init = jnp.full((1, tk), -jnp.inf, F32)
    ms, s_maxes = lax.fori_loop(0, n_full // 2, step, ((m_init,) * len(streams), tuple(first[si] for si in every)))
    ms = list(ms)
    s_maxes = dict(zip(every, s_maxes))
    for dd in range(n_sub):
        slot = dd % 2
        live = [si for si in every if dd <= streams[si][1]]
        later = [si for si in every if dd + 1 <= streams[si][1]]
        nxt = put_scores(n_full + dd + 1, later, 1 - slot) if later else {}
        vt = values_t(n_full + dd)
        for si in live:
            ms[si] = softmax_pv(si, ms[si], s_maxes[si], slot, vt, dd == streams[si][1])
        s_maxes = nxt
    for r in range(n_sub):
        a0 = acc_scr[streams.index((0, r))][...]
        a1 = acc_scr[streams.index((1, r))][...]
        out_t = jnp.where(vrow < MLA_V, a0[:LANES] / a0[LANES:LANES + 1], a1[:LANES] / a1[LANES:LANES + 1])
        o_ref[0, r * tk:(r + 1) * tk, :] = out_t.T.astype(BF16)


def _attn(q, k, v):
    bsz, seq, _ = q.shape
    tk = min(ATT_TK, seq)
    tq = min(ATT_TQ, seq)
    assert (tq // tk) % 2 == 0, "the two-slot score pipeline needs an even number of query sub-tiles"
    n_streams = 2 * (tq // tk)
    body = functools.partial(_attn_body, tq=tq, tk=tk)
    out_shape = jax.ShapeDtypeStruct((bsz, seq, MLA_HEADS * MLA_V), BF16)
    return pl.pallas_call(
        body,
        grid=(bsz, MLA_HEADS // 2, seq // tq),
        in_specs=[pl.BlockSpec((1, tq, 2 * LANES), lambda b, hp, i: (b, i, hp)),
                  pl.BlockSpec((1, seq, 2 * LANES), lambda b, hp, i: (b, 0, hp)),
                  pl.BlockSpec((1, VT_ROWS, seq), lambda b, hp, i: (b, hp, 0))],
        out_specs=pl.BlockSpec((1, tq, LANES), lambda b, hp, i: (b, i, hp)),
        out_shape=out_shape,
        scratch_shapes=[pltpu.VMEM((tk, tk), F32)] * (2 * n_streams) + [pltpu.VMEM((VT_ROWS, tk), F32)] * n_streams,
        compiler_params=pltpu.CompilerParams(dimension_semantics=("parallel", "parallel", "arbitrary"),
                                             vmem_limit_bytes=VMEM_LIMIT),
        name="attn",
    )(q, k, v)


def _outproj_body(ys_ref, ya_ref, x_ref, g1_ref, sc2_ref, sh2_ref, mn_ref, wo1_ref, wo2_ref, pmn_ref, pfn_ref,
                  wr_ref, br_ref, x1_ref, h2_ref, gate_ref, idx_ref, cnt_ref, pref_ref, ltab_ref, cnt_scr):
    first = (pl.program_id(0) == 0) & (pl.program_id(1) == 0)

    @pl.when(first)
    def _():
        cnt_scr[...] = jnp.zeros_like(cnt_scr)

    yan =(_rms(ya_ref[0].astype(F32)) * mn_ref[...]).astype(BF16)
    mix = _dot(ys_ref[0], wo1_ref[...]) + _dot(yan, wo2_ref[...])
    x1 = x_ref[0] + g1_ref[0] * (_rms(mix) * pmn_ref[...])
    x1_ref[0] = x1
    h2 = _rms(x1) * (pfn_ref[...] * (1.0 + sc2_ref[0])) + sh2_ref[0]
    h2_ref[0] = h2.astype(BF16)
    h_hi, h_lo = _split2(h2)
    w_hi, w_lo = _split2(wr_ref[...])
    both = _dot(h_hi, jnp.concatenate([w_hi, w_lo], axis=1))
    logits = both[:, :LANES] + both[:, LANES:] + _dot(h_lo, w_hi) + br_ref[...]
    lane = lax.broadcasted_iota(jnp.int32, logits.shape, 1)
    cur = jnp.where(lane < N_EXPERTS, logits, -jnp.inf)
    vals, idxs = [], []
    for _ in range(TOP_K):
        m = jnp.max(cur, axis=-1, keepdims=True)
        ix = jnp.min(jnp.where(cur == m, lane, LANES), axis=-1, keepdims=True)
        vals.append(m)
        idxs.append(ix)
        cur = jnp.where(lane == ix, -jnp.inf, cur)
    es = [jnp.exp(v - vals[0]) for v in vals]
    denom = es[0]
    for e in es[1:]:
        denom = denom + e
    onehot = jnp.zeros(logits.shape, F32)
    for kk in range(TOP_K):
        onehot = onehot + jnp.where(lane == idxs[kk], 1.0, 0.0)
    ts = logits.shape[0]
    row = lax.broadcasted_iota(jnp.int32, (ts, ts), 0)
    col = lax.broadcasted_iota(jnp.int32, (ts, ts), 1)
    before = jnp.where(row > col, 1.0, 0.0).astype(BF16)
    prior = _dot(before, onehot.astype(BF16))
    seen = cnt_scr[...]
    pref_ref[0] = seen
    gate_out = jnp.zeros(logits.shape, F32)
    idx_out = jnp.zeros(logits.shape, jnp.int32)
    local_out = jnp.zeros(logits.shape, F32)
    for kk in range(TOP_K):
        mine = lane == idxs[kk]
        local = jnp.sum(jnp.where(mine, prior, 0.0), axis=-1, keepdims=True)
        rank = local + jnp.sum(jnp.where(mine, seen, 0.0), axis=-1, keepdims=True)
        gate_out = jnp.where(lane == kk, es[kk] / denom, gate_out)
        idx_out = jnp.where(lane == kk, idxs[kk], idx_out)
        idx_out = jnp.where(lane == TOP_K + kk, rank.astype(jnp.int32), idx_out)
        local_out = jnp.where(lane == kk, local, local_out)
    gate_ref[0] = gate_out
    idx_ref[0] = idx_out
    local_rows = local_out.T.astype(jnp.int32)
    tok = lax.broadcasted_iota(jnp.int32, logits.shape, 0)
    tok_hi = (tok // 16).astype(F32)
    tok_lo = (tok % 16).astype(F32)
    ltab = jnp.zeros((ts, 2 * LANES), F32)
    for kk in range(TOP_K):
        mine = lane == idxs[kk]
        at_rank = jnp.where(row == local_rows[kk:kk + 1, :], 1.0, 0.0).astype(BF16)
        tagged = jnp.concatenate([jnp.where(mine, tok_hi, 0.0), jnp.where(mine, tok_lo, 0.0)], axis=1)
        ltab = ltab + _dot(at_rank, tagged.astype(BF16))
    ltab_ref[0] = (16.0 * ltab[:, :LANES] + ltab[:, LANES:]).astype(jnp.int32)
    cnt_scr[...] = seen + jnp.sum(onehot, axis=0, keepdims=True)
    cnt_ref[...] = cnt_scr[...]


def _outproj(y_ssd, y_att, x, g1, sc2, sh2, mla_norm, wo1, wo2, post_mix_norm, pre_ffn_norm, wr, br, *, batch_offset):
    bsz, seq, _ = y_ssd.shape
    d = x.shape[-1]
    ts = min(TS_PROJ, seq)
    tiles = seq // ts

    def tok(width):
        return pl.BlockSpec((1, ts, width), lambda b, i: (b, i, 0))

    x_spec = pl.BlockSpec((1, ts, d), lambda b, i: (batch_offset + b, i, 0))

    def per_batch(width):
        return pl.BlockSpec((1, 1, width), lambda b, i: (b, 0, 0))

    consts = (mla_norm, wo1, wo2, post_mix_norm, pre_ffn_norm, wr, br)
    out_shape = [jax.ShapeDtypeStruct((bsz, seq, d), F32), jax.ShapeDtypeStruct((bsz, seq, d), BF16),
                 jax.ShapeDtypeStruct((bsz, seq, LANES), F32), jax.ShapeDtypeStruct((bsz, seq, LANES), jnp.int32),
                 jax.ShapeDtypeStruct((1, LANES), F32),
                 jax.ShapeDtypeStruct((bsz * tiles, 1, LANES), F32),
                 jax.ShapeDtypeStruct((bsz * tiles, ts, LANES), jnp.int32)]
    return pl.pallas_call(
        _outproj_body,
        grid=(bsz, seq // ts),
        in_specs=[tok(SSD_INNER), tok(MLA_HEADS * MLA_V), x_spec, per_batch(d), per_batch(d), per_batch(d)]
                 + [_const_spec(w.shape) for w in consts],
        out_specs=[tok(d), tok(d), tok(LANES), tok(LANES), _const_spec((1, LANES)),
                   pl.BlockSpec((1, 1, LANES), lambda b, i: (b * tiles + i, 0, 0)),
                   pl.BlockSpec((1, ts, LANES), lambda b, i: (b * tiles + i, 0, 0))],
        out_shape=out_shape,
        scratch_shapes=[pltpu.VMEM((1, LANES), F32)],
        compiler_params=pltpu.CompilerParams(dimension_semantics=("arbitrary", "arbitrary"),
                                             vmem_limit_bytes=VMEM_LIMIT),
        name="outproj",
    )(y_ssd, y_att, x, g1, sc2, sh2, *consts)


def _moe_body(be_ref, na_ref, x_ref, wgu_ref, bgu_ref, wd_ref, bd_ref, y_ref, wgu_b, wd_b):
    i = pl.program_id(0)

    @pl.when((i == 0) | (be_ref[i] != be_ref[jnp.maximum(i - 1, 0)]))
    def _():
        wgu_b[...] = wgu_ref[0].astype(BF16)
        wd_b[...] = wd_ref[0].astype(BF16)

    @pl.when(i < na_ref[0])
    def _():
        gu = _dot(x_ref[...], wgu_b[...]) + bgu_ref[0]
        glu = jnp.minimum(gu[:, :D_FF_EXPERT], SWIGLU_LIMIT)
        lin = jnp.clip(gu[:, D_FF_EXPERT:], -SWIGLU_LIMIT, SWIGLU_LIMIT)
        act = glu * jax.nn.sigmoid(SWIGLU_ALPHA * glu) * (lin + 1.0)
        y_ref[...] = (_dot(act.astype(BF16), wd_b[...]) + bd_ref[0]).astype(BF16)

    @pl.when(i >= na_ref[0])
    def _():
        y_ref[...] = jnp.zeros_like(y_ref)


def _moe(block_expert, n_active, xg, wgu, bgu, wd, bd):
    n_slots, d = xg.shape
    n_blocks = n_slots // MOE_TB
    f2 = wgu.shape[2]
    out_shape = jax.ShapeDtypeStruct((n_slots, d), BF16)
    return pl.pallas_call(
        _moe_body,
        grid_spec=pltpu.PrefetchScalarGridSpec(
            num_scalar_prefetch=2,
            grid=(n_blocks,),
            in_specs=[pl.BlockSpec((MOE_TB, d), lambda i, be, na: (i, 0)),
                      pl.BlockSpec((1, d, f2), lambda i, be, na: (be[i], 0, 0)),
                      pl.BlockSpec((1, 1, f2), lambda i, be, na: (be[i], 0, 0)),
                      pl.BlockSpec((1, f2 // 2, d), lambda i, be, na: (be[i], 0, 0)),
                      pl.BlockSpec((1, 1, d), lambda i, be, na: (be[i], 0, 0))],
            out_specs=pl.BlockSpec((MOE_TB, d), lambda i, be, na: (i, 0)),
            scratch_shapes=[pltpu.VMEM((d, f2), BF16), pltpu.VMEM((f2 // 2, d), BF16)],
        ),
        out_shape=out_shape,
        compiler_params=pltpu.CompilerParams(dimension_semantics=("arbitrary",),
                                             vmem_limit_bytes=VMEM_LIMIT),
        name="moe",
    )(block_expert, n_active, xg, wgu, bgu, wd, bd)


def _final_body(x1_ref, y_ref, gate_ref, g2_ref, gain_ref, *rest):
    o_ref = rest[-1]
    gates = gate_ref[0]
    f = gates[:, 0:1] * y_ref[0, 0].astype(F32)
    for kk in range(1, TOP_K):
        f = f + gates[:, kk:kk + 1] * y_ref[kk, 0].astype(F32)
    o_ref[0] = x1_ref[0] + g2_ref[0] * (_rms(f) * gain_ref[...])


def _final(x1, y4, gates, g2, gain, out_prev, batch_offset, total_batch):
    bsz, seq, d = x1.shape
    ts = min(TS_PROJ, seq)

    def tok(width):
        return pl.BlockSpec((1, ts, width), lambda b, i: (b, i, 0))

    in_specs = [tok(d), pl.BlockSpec((TOP_K, 1, ts, d), lambda b, i: (0, b, i, 0)), tok(LANES),
                pl.BlockSpec((1, 1, d), lambda b, i: (b, 0, 0)), _const_spec((1, d))]
    args = [x1, y4, gates, g2, gain]
    aliases = {}
    if out_prev is not None:
        in_specs.append(pl.BlockSpec(memory_space=pl.ANY))
        args.append(out_prev)
        aliases = {len(args) - 1: 0}
    return pl.pallas_call(
        _final_body,
        grid=(bsz, seq // ts),
        in_specs=in_specs,
        out_specs=pl.BlockSpec((1, ts, d), lambda b, i: (batch_offset + b, i, 0)),
        out_shape=jax.ShapeDtypeStruct((total_batch, seq, d), F32),
        input_output_aliases=aliases,
        compiler_params=pltpu.CompilerParams(dimension_semantics=("parallel", "parallel"),
                                             vmem_limit_bytes=VMEM_LIMIT),
        name="final",
    )(*args)


def _head_blocks(cols):
    out = []
    for c in cols:
        pad = LANES - c.shape[1]
        out.append(jnp.pad(c, ((0, 0), (0, pad))) if pad else c)
    return jnp.concatenate(out, axis=1)


def _prep_mixer_weights(w_in, w_q_up, w_kv_up):
    d = w_in.shape[0]
    wz = w_in[:, OFF_Z:OFF_XBC]
    wxbc = w_in[:, OFF_XBC:OFF_DT]
    wdt = w_in[:, OFF_DT:OFF_QA]
    wqa = w_in[:, OFF_QA:OFF_KVA]
    wkva = w_in[:, OFF_KVA:OFF_KR]
    wkr = w_in[:, OFF_KR:IN_COLS]
    kr_blk = jnp.concatenate([jnp.zeros((d, ROPE_LO), F32), wkr, jnp.zeros((d, LANES - ROPE_LO - MLA_ROPE), F32)], axis=1)
    dt_blk = jnp.pad(wdt, ((0, 0), (0, LANES - SSD_HEADS)))
    wsm = jnp.concatenate([kr_blk, dt_blk], axis=1)
    qh = MLA_NOPE + MLA_ROPE
    scale = math.log2(math.e) / math.sqrt(qh)
    wqup = _head_blocks([w_q_up[:, h * qh:(h + 1) * qh] for h in range(MLA_HEADS)]) * scale
    kvh = MLA_NOPE + MLA_V
    wkup = _head_blocks([w_kv_up[:, h * kvh:h * kvh + MLA_NOPE] for h in range(MLA_HEADS)])
    vcols = []
    for h in range(MLA_HEADS):
        vcols.append(w_kv_up[:, h * kvh + MLA_NOPE:(h + 1) * kvh])
        if h % 2 == 1:
            vcols.append(jnp.zeros((w_kv_up.shape[0], VT_ROWS - LANES), F32))
    wvup = jnp.concatenate(vcols, axis=1).T
    return tuple(w.astype(BF16) for w in (wz, wxbc, wsm, wqa, wkva)) + tuple(w.astype(BF16) for w in (wqup, wkup, wvup))


def _rope_tables(positions):
    inv_freq = ROPE_BASE ** (-(jnp.arange(HALF_ROPE, dtype=F32) * 2.0 / MLA_ROPE))
    angles = positions.astype(F32)[:, None, :] * inv_freq[None, :, None]
    bsz, _, seq = angles.shape
    return jnp.concatenate([jnp.zeros((bsz, ROPE_LO, seq), F32), jnp.cos(angles), jnp.sin(angles),
                            jnp.zeros((bsz, LANES - ROPE_LO - MLA_ROPE, seq), F32)], axis=1)


def _route(idx, rank, counts, tile_seen, ltab, n_tok):
    n_assign = n_tok * TOP_K
    n_tiles, tile, lanes = ltab.shape
    padded = ((counts + MOE_TB - 1) // MOE_TB) * MOE_TB
    padded_end = jnp.cumsum(padded)
    padded_start = padded_end - padded
    experts = jnp.arange(N_EXPERTS, dtype=jnp.int32)
    start_of = jnp.sum(jnp.where(idx[..., None] == experts, padded_start, 0), axis=-1)
    dest = (start_of + rank).reshape(-1)
    n_blocks = n_assign // MOE_TB + N_EXPERTS
    block_start = jnp.arange(n_blocks, dtype=jnp.int32) * MOE_TB
    block_expert = jnp.minimum(jnp.sum((padded_end[None, :] <= block_start[:, None]).astype(jnp.int32), axis=1),
                               N_EXPERTS - 1)
    n_active = (padded_end[-1] // MOE_TB).astype(jnp.int32).reshape(1)
    j = (block_start - padded_start[block_expert])[:, None] + jnp.arange(MOE_TB, dtype=jnp.int32)[None, :]
    seen_blk = tile_seen.T[block_expert]
    reached = seen_blk[:, None, :] <= j[:, :, None]
    tau = jnp.sum(reached.astype(jnp.int32), axis=-1) - 1
    j_local = j - jnp.max(jnp.where(reached, seen_blk[:, None, :], 0), axis=-1)
    valid = j < counts[block_expert][:, None]
    flat = (tau * tile + j_local) * lanes + block_expert[:, None]
    tok_local = ltab.reshape(-1)[jnp.clip(flat, 0, n_tiles * tile * lanes - 1)]
    filler = (block_start[:, None] + jnp.arange(MOE_TB, dtype=jnp.int32)[None, :]) % n_tok
    slot_tok = jnp.where(valid, tau * tile + tok_local, filler).reshape(-1)
    return dest, slot_tok, block_expert, n_active


def kernel(x, c, positions, w_ada, b_ada, pre_mix_norm, w_in, conv_w, conv_b, dt_bias, a_log, d_skip, ssd_norm, q_a_norm, w_q_up, kv_a_norm, w_kv_up, mla_norm, w_out, post_mix_norm, pre_ffn_norm, w_router, b_router, w_gate_up, b_gate_up, w_down, b_down, post_ffn_norm):
    bsz, seq, d = x.shape
    rope_tab = _rope_tables(positions)
    n_groups = BATCH_GROUPS if bsz % BATCH_GROUPS == 0 else 1
    gb = bsz // n_groups
    n_tok = gb * seq
    assert d == D_MODEL and seq % min(TS_PROJ, seq) == 0 and seq % min(SSD_L, seq) == 0
    assert seq % min(ATT_TQ, seq) == 0 and min(ATT_TQ, seq) % min(ATT_TK, seq) == 0
    assert (n_tok * TOP_K) % MOE_TB == 0
    pad_h = LANES - SSD_HEADS
    for l in range(w_ada.shape[0]):
        mod = _adaln(c, w_ada[l], b_ada[l])
        mods = [m.reshape(bsz, 1, d) for m in jnp.split(mod, 6, axis=-1)]
        mixer_w = _prep_mixer_weights(w_in[l], w_q_up[l], w_kv_up[l])
        wo = w_out[l].astype(BF16)
        wr = jnp.pad(w_router[l], ((0, 0), (0, LANES - N_EXPERTS)))
        br = jnp.pad(b_router[l], (0, LANES - N_EXPERTS)).reshape(1, LANES)
        dtb = jnp.pad(dt_bias[l], (0, pad_h)).reshape(1, LANES)
        alog = jnp.pad(a_log[l], (0, pad_h)).reshape(1, LANES)
        dsk = jnp.repeat(d_skip[l], SSD_HEAD_DIM).reshape(1, -1)
        out = None
        pending = None

        def experts_and_final(p, xs, out):
            y = _moe(p["block_expert"], p["n_active"], xs, w_gate_up[l], b_gate_up[l][:, None, :],
                     w_down[l], b_down[l][:, None, :])
            y4 = y[p["dest"].reshape(n_tok, TOP_K).T.reshape(-1)].reshape(TOP_K, gb, seq, d)
            return _final(p["x1"], y4, p["gates"], p["g2"], post_ffn_norm[l].reshape(1, d), out, p["offset"], bsz)

        for gi in range(n_groups):
            grp = slice(gi * gb, (gi + 1) * gb)
            sh1, sc1, g1, sh2, sc2, g2 = [m[grp] for m in mods]
            if pending is not None:
                sc1 = sc1 + jnp.where(pending["slot_tok"][0] < 0, 1.0, 0.0)
            z, xbc, dt, q, k, v = _inproj(x, sc1, sh1, pre_mix_norm[l].reshape(1, d), rope_tab, *mixer_w[:5],
                                          q_a_norm[l].reshape(1, -1), kv_a_norm[l].reshape(1, -1), *mixer_w[5:],
                                          batch_offset=gi * gb)
            y_ssd = _ssd(xbc, z, dt, conv_w[l], conv_b[l].reshape(1, -1), dtb, alog, dsk, ssd_norm[l].reshape(1, -1))
            prev_xs = None
            if pending is not None:
                ssd_bits = lax.bitcast_convert_type(y_ssd[0, 0, 0], jnp.uint16).astype(jnp.int32)
                prev_xs = pending["h2"].reshape(n_tok, d)[pending["slot_tok"] + jnp.where(ssd_bits < 0, 1, 0)]
            y_att = _attn(q, k, v)
            x1, h2, gates, route, cnt, seen, ltab = _outproj(
                y_ssd, y_att, x, g1, sc2, sh2, mla_norm[l].reshape(1, -1), wo[:SSD_INNER], wo[SSD_INNER:],
                post_mix_norm[l].reshape(1, d), pre_ffn_norm[l].reshape(1, d), wr, br, batch_offset=gi * gb)
            route = route.reshape(n_tok, LANES)
            counts = cnt[0, :N_EXPERTS].astype(jnp.int32)
            tile_seen = seen[:, 0, :N_EXPERTS].astype(jnp.int32)
            if prev_xs is not None:
                bits = lax.bitcast_convert_type(prev_xs[0, 0], jnp.uint16).astype(jnp.int32)
                tile_seen = tile_seen + jnp.where(bits < 0, 1, 0)
            dest, slot_tok, block_expert, n_active = _route(route[:, :TOP_K], route[:, TOP_K:2 * TOP_K], counts,
                                                            tile_seen, ltab, n_tok)
            if pending is not None:
                out = experts_and_final(pending, prev_xs, out)
            pending = dict(h2=h2, slot_tok=slot_tok, dest=dest, block_expert=block_expert, n_active=n_active,
                           x1=x1, gates=gates, g2=g2, offset=gi * gb)
        out = experts_and_final(pending, pending["h2"].reshape(n_tok, d)[pending["slot_tok"]], out)
        x = out
    return x
```

```python
import functools
import math

import jax
import jax.numpy as jnp
from jax import lax
from jax.experimental import pallas as pl
from jax.experimental.pallas import tpu as pltpu

F32 = jnp.float32
BF16 = jnp.bfloat16

D_MODEL = 1024
CHUNK = 64
SSD_INNER = 512
SSD_HEAD_DIM = 64
SSD_HEADS = 8
SSD_GROUPS = 2
SSD_STATE = 128
SSD_CONV = 4
SSD_BC = SSD_GROUPS * SSD_STATE
XBC_DIM = SSD_INNER + 2 * SSD_BC
MLA_V = 64
MLA_HEADS = 8
MLA_NOPE = 64
MLA_ROPE = 32
Q_LORA = 384
KV_LORA = 256
ROPE_BASE = 10000.0
OFF_Z = 0
OFF_XBC = OFF_Z + SSD_INNER
OFF_DT = OFF_XBC + XBC_DIM
OFF_QA = OFF_DT + SSD_HEADS
OFF_KVA = OFF_QA + Q_LORA
OFF_KR = OFF_KVA + KV_LORA
IN_COLS = OFF_KR + MLA_ROPE
N_EXPERTS = 32
TOP_K = 4
D_FF_EXPERT = 1024
SWIGLU_LIMIT = 7.0
SWIGLU_ALPHA = 1.702
NORM_EPS = 1e-6

LANES = 128
HALF_ROPE = MLA_ROPE // 2
ROPE_LO = MLA_NOPE
ROPE_HI = MLA_NOPE + HALF_ROPE
VT_ROWS = 144

ADA_TN = 1024
TS_PROJ = 512
SSD_L = 256
ATT_TQ = 1024
ATT_TK = 256
MOE_TB = 512
BATCH_GROUPS = 2
VMEM_LIMIT = 56 * 1024 * 1024


def _dot(a, b):
    return jnp.dot(a, b, preferred_element_type=F32)


def _dot_nt(a, b):
    return lax.dot_general(a, b, (((1,), (1,)), ((), ())), preferred_element_type=F32)


def _dot_tn(a, b):
    return lax.dot_general(a, b, (((0,), (0,)), ((), ())), preferred_element_type=F32)


def _split2(x):
    hi = x.astype(BF16)
    lo = (x - hi.astype(F32)).astype(BF16)
    return hi, lo


def _split3(x):
    h1 = x.astype(BF16)
    r1 = x - h1.astype(F32)
    h2 = r1.astype(BF16)
    h3 = (r1 - h2.astype(F32)).astype(BF16)
    return h1, h2, h3


def _dot3(a, b):
    ah, al = _split2(a)
    bh, bl = _split2(b)
    return _dot(ah, bh) + _dot(ah, bl) + _dot(al, bh)


def _rms(x):
    return x * lax.rsqrt(jnp.mean(x * x, axis=-1, keepdims=True) + NORM_EPS)


def _silu(x):
    hx = 0.5 * x
    return hx + hx * jnp.tanh(hx)


def _const_spec(shape):
    nd = len(shape)
    return pl.BlockSpec(shape, lambda *_: (0,) * nd)


def _adaln_body(c_ref, w_ref, b_ref, o_ref):
    o_ref[...] = _dot3(_silu(c_ref[...]), w_ref[...]) + b_ref[...]


def _adaln(c, w_ada, b_ada):
    bsz, d = c.shape
    n = w_ada.shape[1]
    tn = ADA_TN
    return pl.pallas_call(
        _adaln_body,
        grid=(n // tn,),
        in_specs=[_const_spec((bsz, d)),
                  pl.BlockSpec((d, tn), lambda j: (0, j)),
                  pl.BlockSpec((1, tn), lambda j: (0, j))],
        out_specs=pl.BlockSpec((bsz, tn), lambda j: (0, j)),
        out_shape=jax.ShapeDtypeStruct((bsz, n), F32),
        compiler_params=pltpu.CompilerParams(dimension_semantics=("arbitrary",),
                                             vmem_limit_bytes=VMEM_LIMIT),
        name="adaln",
    )(c, w_ada, b_ada.reshape(1, n))


def _rope_block(xb, ct, st, lane):
    partner = jnp.where(lane < ROPE_HI, pltpu.roll(xb, LANES - HALF_ROPE, 1), pltpu.roll(xb, HALF_ROPE, 1))
    return xb * ct + partner * st


def _inproj_body(x_ref, sc_ref, sh_ref, g_ref, tab_ref, wz_ref, wxbc_ref, wsm_ref, wqa_ref, wkva_ref,
                 qn_ref, kvn_ref, wqup_ref, wkup_ref, wvup_ref,
                 z_ref, xbc_ref, dt_ref, q_ref, k_ref, v_ref):
    x = x_ref[0]
    h = _rms(x) * (g_ref[...] * (1.0 + sc_ref[0])) + sh_ref[0]
    hb = h.astype(BF16)
    z_ref[0] = _dot(hb, wz_ref[...]).astype(BF16)
    xbc_ref[0] = _dot(hb, wxbc_ref[...]).astype(BF16)
    sm = _dot(hb, wsm_ref[...])
    dt_ref[0] = sm[:, LANES:]
    tab = tab_ref[0].T
    lane = lax.broadcasted_iota(jnp.int32, tab.shape, 1)
    in_lo = (lane >= ROPE_LO) & (lane < ROPE_HI)
    in_hi = (lane >= ROPE_HI) & (lane < ROPE_HI + HALF_ROPE)
    ct = jnp.where(lane < ROPE_LO, 1.0, jnp.where(in_lo, tab, jnp.where(in_hi, pltpu.roll(tab, HALF_ROPE, 1), 0.0)))
    st = jnp.where(in_lo, -pltpu.roll(tab, LANES - HALF_ROPE, 1), jnp.where(in_hi, tab, 0.0))
    kr =_rope_block(sm[:, :LANES], ct, st, lane)
    qan = (_rms(_dot(hb, wqa_ref[...])) * qn_ref[...]).astype(BF16)
    q = _dot(qan, wqup_ref[...])
    for hh in range(MLA_HEADS):
        blk = slice(hh * LANES, (hh + 1) * LANES)
        q_ref[0, :, blk] = _rope_block(q[:, blk], ct, st, lane).astype(BF16)
    kvn = (_rms(_dot(hb, wkva_ref[...])) * kvn_ref[...]).astype(BF16)
    k = _dot(kvn, wkup_ref[...])
    for hh in range(MLA_HEADS):
        blk = slice(hh * LANES, (hh + 1) * LANES)
        k_ref[0, :, blk] = (k[:, blk] + kr).astype(BF16)
    vt = _dot_nt(wvup_ref[...], kvn)
    vrow = lax.broadcasted_iota(jnp.int32, vt.shape, 0)
    v_ref[0] = jnp.where(vrow % VT_ROWS == LANES, 1.0, vt).astype(BF16)


def _inproj(x, sc1, sh1, gain, rope_tab, wz, wxbc, wsm, wqa, wkva, qn, kvn, wqup, wkup, wvup, *, batch_offset):
    _, seq, d = x.shape
    bsz = sc1.shape[0]
    ts = min(TS_PROJ, seq)
    hw = MLA_HEADS * LANES

    def tok(width):
        return pl.BlockSpec((1, ts, width), lambda b, i: (b, i, 0))

    def tok_full(width):
        return pl.BlockSpec((1, ts, width), lambda b, i: (batch_offset + b, i, 0))

    def per_batch(width):
        return pl.BlockSpec((1, 1, width), lambda b, i: (b, 0, 0))

    weights = (wz, wxbc, wsm, wqa, wkva, qn, kvn, wqup, wkup, wvup)
    out_widths = (SSD_INNER, XBC_DIM, LANES, hw, hw)
    out_dtypes = (BF16, BF16, F32, BF16, BF16)
    vdim = (MLA_HEADS // 2) * VT_ROWS
    out_shape = ([jax.ShapeDtypeStruct((bsz, seq, w), dt) for w, dt in zip(out_widths, out_dtypes)]
                 + [jax.ShapeDtypeStruct((bsz, vdim, seq), BF16)])
    return pl.pallas_call(
        _inproj_body,
        grid=(bsz, seq // ts),
        in_specs=[tok_full(d), per_batch(d), per_batch(d), _const_spec((1, d)),
                  pl.BlockSpec((1, LANES, ts), lambda b, i: (batch_offset + b, 0, i))]
                 + [_const_spec(w.shape) for w in weights],
        out_specs=[tok(w) for w in out_widths] + [pl.BlockSpec((1, vdim, ts), lambda b, i: (b, 0, i))],
        out_shape=out_shape,
        compiler_params=pltpu.CompilerParams(dimension_semantics=("parallel", "parallel"),
                                             vmem_limit_bytes=VMEM_LIMIT),
        name="inproj",
    )(x, sc1, sh1, gain, rope_tab, *weights)


CONV_HALO = 16


def _ssd_body(xc_ref, xp_ref, z_ref, dt_ref, cw_ref, cb_ref, dtb_ref, alog_ref, dsk_ref, ng_ref,
              y_ref, st_scr, *, blk):
    i = pl.program_id(1)

    @pl.when(i == 0)
    def _():
        st_scr[...] = jnp.zeros_like(st_scr)

    xc = xc_ref[0]
    row = lax.broadcasted_iota(jnp.int32, (blk, blk), 0)
    col = lax.broadcasted_iota(jnp.int32, (blk, blk), 1)
    conv = cb_ref[...] + cw_ref[SSD_CONV - 1:SSD_CONV, :] * xc.astype(F32)
    tail = jnp.where(i > 0, xp_ref[0].astype(F32)[CONV_HALO - 8:], 0.0)
    row8 = lax.broadcasted_iota(jnp.int32, tail.shape, 0)
    head_fix = jnp.zeros(tail.shape, F32)
    for shift in range(1, SSD_CONV):
        w = cw_ref[SSD_CONV - 1 - shift:SSD_CONV - shift, :]
        shifted = _dot(jnp.where(row - col == shift, 1.0, 0.0).astype(BF16), xc)
        conv = conv + w * shifted
        head_fix = head_fix + jnp.where(row8 < shift, w * pltpu.roll(tail, shift, 0), 0.0)
    conv = jnp.concatenate([conv[:8] + head_fix, conv[8:]], axis=0)
    xa = _silu(conv)
    xs = xa[:, :SSD_INNER]
    bm = xa[:, SSD_INNER:SSD_INNER + SSD_BC]
    cm = xa[:, SSD_INNER + SSD_BC:]

    hl = lax.broadcasted_iota(jnp.int32, (1, LANES), 1)
    dtr = dt_ref[0] + dtb_ref[...]
    dt = jnp.maximum(dtr, 0.0) + jnp.log(1.0 + jnp.exp(-jnp.abs(dtr)))
    a = jnp.where(hl < SSD_HEADS, -jnp.exp(alog_ref[...]), 0.0)
    dta = dt * a
    row = lax.broadcasted_iota(jnp.int32, (blk, blk), 0)
    col = lax.broadcasted_iota(jnp.int32, (blk, blk), 1)
    tril = row >= col
    trilb = jnp.where(tril, 1.0, 0.0).astype(BF16)
    d1, d2, d3 = _split3(dta)
    cs = _dot(trilb, d1) + _dot(trilb, d2) + _dot(trilb, d3)
    cs_last = cs[blk - 1:blk, :]
    ecs = jnp.exp(cs)
    dte = jnp.exp(cs_last - cs)
    cs_t = cs.T

    er = lax.broadcasted_iota(jnp.int32, (LANES, SSD_INNER), 0)
    ec = lax.broadcasted_iota(jnp.int32, (LANES, SSD_INNER), 1)
    expand = jnp.where(ec // SSD_HEAD_DIM == er, 1.0, 0.0).astype(BF16)

    def per_channel(v):
        vh, vl = _split2(v)
        return _dot(vh, expand) + _dot(vl, expand)

    dt_e = per_channel(dt)
    ecs_e = per_channel(ecs)
    dte_e = per_channel(dte)
    xdt = xs * dt_e
    xdt_b = xdt.astype(BF16)
    xw_b = (xdt * dte_e).astype(BF16)

    gw = SSD_INNER // SSD_GROUPS
    heads_per_group = SSD_HEADS // SSD_GROUPS
    lane = lax.broadcasted_iota(jnp.int32, (blk, LANES), 1)
    y_groups = []
    for g in range(SSD_GROUPS):
        bg = bm[:, g * SSD_STATE:(g + 1) * SSD_STATE].astype(BF16)
        cg = cm[:, g * SSD_STATE:(g + 1) * SSD_STATE].astype(BF16)
        cb = _dot_nt(cg, bg)
        state = st_scr[g]
        y_off = _dot(cg, state.astype(BF16))
        pairs = []
        for j in range(heads_per_group // 2):
            xp = xdt_b[:, g * gw + j * LANES:g * gw + (j + 1) * LANES]
            halves = []
            for u in range(2):
                hidx = g * heads_per_group + 2 * j + u
                seg = cs[:, hidx:hidx + 1] - cs_t[hidx:hidx + 1, :]
                dec = jnp.exp(jnp.where(tril, seg, -jnp.inf))
                halves.append(_dot((cb * dec).astype(BF16), xp))
            pairs.append(jnp.where(lane < SSD_HEAD_DIM, halves[0], halves[1]))
        y_diag = jnp.concatenate(pairs, axis=1)
        y_groups.append(y_diag + y_off * ecs_e[:, g * gw:(g + 1) * gw])
        st_scr[g] = (state * ecs_e[blk - 1:blk, g * gw:(g + 1) * gw]
                     + _dot_tn(bg, xw_b[:, g * gw:(g + 1) * gw]))
    y = jnp.concatenate(y_groups, axis=1) + xs * dsk_ref[...]
    y = y * _silu(z_ref[0].astype(F32))
    y = jnp.concatenate([_rms(y[:, g * gw:(g + 1) * gw]) for g in range(SSD_GROUPS)], axis=1)
    y_ref[0] = (y * ng_ref[...]).astype(BF16)


def _ssd(xbc, z, dt, conv_w, conv_b, dt_bias, a_log, d_skip_e, norm_gain):
    bsz, seq, _ = xbc.shape
    blk = min(SSD_L, seq)
    halo_per_blk = blk // CONV_HALO
    body = functools.partial(_ssd_body, blk=blk)
    out_shape = jax.ShapeDtypeStruct((bsz, seq, SSD_INNER), BF16)
    return pl.pallas_call(
        body,
        grid=(bsz, seq // blk),
        in_specs=[pl.BlockSpec((1, blk, XBC_DIM), lambda b, i: (b, i, 0)),
                  pl.BlockSpec((1, CONV_HALO, XBC_DIM), lambda b, i: (b, jnp.maximum(i * halo_per_blk - 1, 0), 0)),
                  pl.BlockSpec((1, blk, SSD_INNER), lambda b, i: (b, i, 0)),
                  pl.BlockSpec((1, blk, LANES), lambda b, i: (b, i, 0)),
                  _const_spec((SSD_CONV, XBC_DIM)), _const_spec((1, XBC_DIM)),
                  _const_spec((1, LANES)), _const_spec((1, LANES)),
                  _const_spec((1, SSD_INNER)), _const_spec((1, SSD_INNER))],
        out_specs=pl.BlockSpec((1, blk, SSD_INNER), lambda b, i: (b, i, 0)),
        out_shape=out_shape,
        scratch_shapes=[pltpu.VMEM((SSD_GROUPS, SSD_STATE, SSD_INNER // SSD_GROUPS), F32)],
        compiler_params=pltpu.CompilerParams(dimension_semantics=("parallel", "arbitrary"),
                                             vmem_limit_bytes=VMEM_LIMIT),
        name="ssd",
    )(xbc, xbc, z, dt, conv_w, conv_b, dt_bias, a_log, d_skip_e, norm_gain)


def _attn_body(q_ref, k_ref, v_ref, o_ref, *scratch, tq, tk):
    n_streams = 2 * (tq // tk)
    s_scr = (scratch[:n_streams], scratch[n_streams:2 * n_streams])
    acc_scr = scratch[2 * n_streams:]
    qi = pl.program_id(2)
    n_sub = tq // tk
    n_full = qi * n_sub
    krow = lax.broadcasted_iota(jnp.int32, (tk, tk), 0)
    qcol = lax.broadcasted_iota(jnp.int32, (tk, tk), 1)
    diag_ok = krow // CHUNK <= qcol // CHUNK
    vrow = lax.broadcasted_iota(jnp.int32, (LANES, tk), 0)
    streams = [(u, r) for u in range(2) for r in range(n_sub)]
    qs = [q_ref[0, r * tk:(r + 1) * tk, u * LANES:(u + 1) * LANES] for u, r in streams]

    def put_scores(ki, which, slot):
        start = pl.multiple_of(ki * tk, tk)
        k2 = k_ref[0, pl.ds(start, tk), :]
        out = {}
        for si in which:
            u = streams[si][0]
            s = _dot_nt(k2[:, u * LANES:(u + 1) * LANES], qs[si])
            s_scr[slot][si][...] = s
            out[si] = jnp.max(s, axis=0, keepdims=True)
        return out

    def values_t(ki):
        return v_ref[0, :, pl.ds(pl.multiple_of(ki * tk, tk), tk)]

    def softmax_pv(si, m, s_max, slot, vt, masked):
        s = s_scr[slot][si][...]
        if masked:
            s = jnp.where(diag_ok, s, -jnp.inf)
            s_max = jnp.max(s, axis=0, keepdims=True)
        m_new = jnp.maximum(m, s_max)
        alpha = jnp.exp2(m - m_new)
        p = jnp.exp2((s - m_new).astype(BF16))
        acc_scr[si][...] = alpha * acc_scr[si][...] + _dot(vt, p)
        return m_new

    every = list(range(len(streams)))
    for ref in acc_scr:
        ref[...] = jnp.zeros_like(ref)

    def step(j, state):
        ms, s_maxes = state
        for slot in range(2):
            ki = 2 * j + slot
            nxt = put_scores(ki + 1, every, 1 - slot)
            vt = values_t(ki)
            ms = tuple(softmax_pv(si, ms[si], s_maxes[si], slot, vt, False) for si in every)
            s_maxes = tuple(nxt[si] for si in every)
        return ms, s_maxes

    first = put_scores(0, every, 0)
    m_init = jnp.full((1, tk), -jnp.inf, F32)
    ms, s_maxes = lax.fori_loop(0, n_full // 2, step, ((m_init,) * len(streams), tuple(first[si] for si in every)))
    ms = list(ms)
    s_maxes = dict(zip(every, s_maxes))
    for dd in range(n_sub):
        slot = dd % 2
        live = [si for si in every if dd <= streams[si][1]]
        later = [si for si in every if dd + 1 <= streams[si][1]]
        nxt = put_scores(n_full + dd + 1, later, 1 - slot) if later else {}
        vt = values_t(n_full + dd)
        for si in live:
            ms[si] = softmax_pv(si, ms[si], s_maxes[si], slot, vt, dd == streams[si][1])
        s_maxes = nxt
    for r in range(n_sub):
        a0 = acc_scr[streams.index((0, r))][...]
        a1 = acc_scr[streams.index((1, r))][...]
        out_t = jnp.where(vrow < MLA_V, a0[:LANES] / a0[LANES:LANES + 1], a1[:LANES] / a1[LANES:LANES + 1])
        o_ref[0, r * tk:(r + 1) * tk, :] = out_t.T.astype(BF16)


def _attn(q, k, v):
    bsz, seq, _ = q.shape
    tk = min(ATT_TK, seq)
    tq = min(ATT_TQ, seq)
    assert (tq // tk) % 2 == 0, "the two-slot score pipeline needs an even number of query sub-tiles"
    n_streams = 2 * (tq // tk)
    body = functools.partial(_attn_body, tq=tq, tk=tk)
    out_shape = jax.ShapeDtypeStruct((bsz, seq, MLA_HEADS * MLA_V), BF16)
    return pl.pallas_call(
        body,
        grid=(bsz, MLA_HEADS // 2, seq // tq),
        in_specs=[pl.BlockSpec((1, tq, 2 * LANES), lambda b, hp, i: (b, i, hp)),
                  pl.BlockSpec((1, seq, 2 * LANES), lambda b, hp, i: (b, 0, hp)),
                  pl.BlockSpec((1, VT_ROWS, seq), lambda b, hp, i: (b, hp, 0))],
        out_specs=pl.BlockSpec((1, tq, LANES), lambda b, hp, i: (b, i, hp)),
        out_shape=out_shape,
        scratch_shapes=[pltpu.VMEM((tk, tk), F32)] * (2 * n_streams) + [pltpu.VMEM((VT_ROWS, tk), F32)] * n_streams,
        compiler_params=pltpu.CompilerParams(dimension_semantics=("parallel", "parallel", "arbitrary"),
                                             vmem_limit_bytes=VMEM_LIMIT),
        name="attn",
    )(q, k, v)


def _outproj_body(ys_ref, ya_ref, x_ref, g1_ref, sc2_ref, sh2_ref, mn_ref, wo1_ref, wo2_ref, pmn_ref, pfn_ref,
                  wr_ref, br_ref, x1_ref, h2_ref, gate_ref, idx_ref, cnt_ref, pref_ref, ltab_ref, cnt_scr):
    first = (pl.program_id(0) == 0) & (pl.program_id(1) == 0)

    @pl.when(first)
    def _():
        cnt_scr[...] = jnp.zeros_like(cnt_scr)

    yan =(_rms(ya_ref[0].astype(F32)) * mn_ref[...]).astype(BF16)
    mix = _dot(ys_ref[0], wo1_ref[...]) + _dot(yan, wo2_ref[...])
    x1 = x_ref[0] + g1_ref[0] * (_rms(mix) * pmn_ref[...])
    x1_ref[0] = x1
    h2 = _rms(x1) * (pfn_ref[...] * (1.0 + sc2_ref[0])) + sh2_ref[0]
    h2_ref[0] = h2.astype(BF16)
    h_hi, h_lo = _split2(h2)
    w_hi, w_lo = _split2(wr_ref[...])
    both = _dot(h_hi, jnp.concatenate([w_hi, w_lo], axis=1))
    logits = both[:, :LANES] + both[:, LANES:] + _dot(h_lo, w_hi) + br_ref[...]
    lane = lax.broadcasted_iota(jnp.int32, logits.shape, 1)
    cur = jnp.where(lane < N_EXPERTS, logits, -jnp.inf)
    vals, idxs = [], []
    for _ in range(TOP_K):
        m = jnp.max(cur, axis=-1, keepdims=True)
        ix = jnp.min(jnp.where(cur == m, lane, LANES), axis=-1, keepdims=True)
        vals.append(m)
        idxs.append(ix)
        cur = jnp.where(lane == ix, -jnp.inf, cur)
    es = [jnp.exp(v - vals[0]) for v in vals]
    denom = es[0]
    for e in es[1:]:
        denom = denom + e
    onehot = jnp.zeros(logits.shape, F32)
    for kk in range(TOP_K):
        onehot = onehot + jnp.where(lane == idxs[kk], 1.0, 0.0)
    ts = logits.shape[0]
    row = lax.broadcasted_iota(jnp.int32, (ts, ts), 0)
    col = lax.broadcasted_iota(jnp.int32, (ts, ts), 1)
    before = jnp.where(row > col, 1.0, 0.0).astype(BF16)
    prior = _dot(before, onehot.astype(BF16))
    seen = cnt_scr[...]
    pref_ref[0] = seen
    gate_out = jnp.zeros(logits.shape, F32)
    idx_out = jnp.zeros(logits.shape, jnp.int32)
    local_out = jnp.zeros(logits.shape, F32)
    for kk in range(TOP_K):
        mine = lane == idxs[kk]
        local = jnp.sum(jnp.where(mine, prior, 0.0), axis=-1, keepdims=True)
        rank = local + jnp.sum(jnp.where(mine, seen, 0.0), axis=-1, keepdims=True)
        gate_out = jnp.where(lane == kk, es[kk] / denom, gate_out)
        idx_out = jnp.where(lane == kk, idxs[kk], idx_out)
        idx_out = jnp.where(lane == TOP_K + kk, rank.astype(jnp.int32), idx_out)
        local_out = jnp.where(lane == kk, local, local_out)
    gate_ref[0] = gate_out
    idx_ref[0] = idx_out
    local_rows = local_out.T.astype(jnp.int32)
    tok = lax.broadcasted_iota(jnp.int32, logits.shape, 0)
    tok_hi = (tok // 16).astype(F32)
    tok_lo = (tok % 16).astype(F32)
    ltab = jnp.zeros((ts, 2 * LANES), F32)
    for kk in range(TOP_K):
        mine = lane == idxs[kk]
        at_rank = jnp.where(row == local_rows[kk:kk + 1, :], 1.0, 0.0).astype(BF16)
        tagged = jnp.concatenate([jnp.where(mine, tok_hi, 0.0), jnp.where(mine, tok_lo, 0.0)], axis=1)
        ltab = ltab + _dot(at_rank, tagged.astype(BF16))
    ltab_ref[0] = (16.0 * ltab[:, :LANES] + ltab[:, LANES:]).astype(jnp.int32)
    cnt_scr[...] = seen + jnp.sum(onehot, axis=0, keepdims=True)
    cnt_ref[...] = cnt_scr[...]


def _outproj(y_ssd, y_att, x, g1, sc2, sh2, mla_norm, wo1, wo2, post_mix_norm, pre_ffn_norm, wr, br, *, batch_offset):
    bsz, seq, _ = y_ssd.shape
    d = x.shape[-1]
    ts = min(TS_PROJ, seq)
    tiles = seq // ts

    def tok(width):
        return pl.BlockSpec((1, ts, width), lambda b, i: (b, i, 0))

    x_spec = pl.BlockSpec((1, ts, d), lambda b, i: (batch_offset + b, i, 0))

    def per_batch(width):
        return pl.BlockSpec((1, 1, width), lambda b, i: (b, 0, 0))

    consts = (mla_norm, wo1, wo2, post_mix_norm, pre_ffn_norm, wr, br)
    out_shape = [jax.ShapeDtypeStruct((bsz, seq, d), F32), jax.ShapeDtypeStruct((bsz, seq, d), BF16),
                 jax.ShapeDtypeStruct((bsz, seq, LANES), F32), jax.ShapeDtypeStruct((bsz, seq, LANES), jnp.int32),
                 jax.ShapeDtypeStruct((1, LANES), F32),
                 jax.ShapeDtypeStruct((bsz * tiles, 1, LANES), F32),
                 jax.ShapeDtypeStruct((bsz * tiles, ts, LANES), jnp.int32)]
    return pl.pallas_call(
        _outproj_body,
        grid=(bsz, seq // ts),
        in_specs=[tok(SSD_INNER), tok(MLA_HEADS * MLA_V), x_spec, per_batch(d), per_batch(d), per_batch(d)]
                 + [_const_spec(w.shape) for w in consts],
        out_specs=[tok(d), tok(d), tok(LANES), tok(LANES), _const_spec((1, LANES)),
                   pl.BlockSpec((1, 1, LANES), lambda b, i: (b * tiles + i, 0, 0)),
                   pl.BlockSpec((1, ts, LANES), lambda b, i: (b * tiles + i, 0, 0))],
        out_shape=out_shape,
        scratch_shapes=[pltpu.VMEM((1, LANES), F32)],
        compiler_params=pltpu.CompilerParams(dimension_semantics=("arbitrary", "arbitrary"),
                                             vmem_limit_bytes=VMEM_LIMIT),
        name="outproj",
    )(y_ssd, y_att, x, g1, sc2, sh2, *consts)


def _moe_body(be_ref, na_ref, x_ref, wgu_ref, bgu_ref, wd_ref, bd_ref, y_ref, wgu_b, wd_b):
    i = pl.program_id(0)

    @pl.when((i == 0) | (be_ref[i] != be_ref[jnp.maximum(i - 1, 0)]))
    def _():
        wgu_b[...] = wgu_ref[0].astype(BF16)
        wd_b[...] = wd_ref[0].astype(BF16)

    @pl.when(i < na_ref[0])
    def _():
        gu = _dot(x_ref[...], wgu_b[...]) + bgu_ref[0]
        glu = jnp.minimum(gu[:, :D_FF_EXPERT], SWIGLU_LIMIT)
        lin = jnp.clip(gu[:, D_FF_EXPERT:], -SWIGLU_LIMIT, SWIGLU_LIMIT)
        act = glu * jax.nn.sigmoid(SWIGLU_ALPHA * glu) * (lin + 1.0)
        y_ref[...] = (_dot(act.astype(BF16), wd_b[...]) + bd_ref[0]).astype(BF16)

    @pl.when(i >= na_ref[0])
    def _():
        y_ref[...] = jnp.zeros_like(y_ref)


def _moe(block_expert, n_active, xg, wgu, bgu, wd, bd):
    n_slots, d = xg.shape
    n_blocks = n_slots // MOE_TB
    f2 = wgu.shape[2]
    out_shape = jax.ShapeDtypeStruct((n_slots, d), BF16)
    return pl.pallas_call(
        _moe_body,
        grid_spec=pltpu.PrefetchScalarGridSpec(
            num_scalar_prefetch=2,
            grid=(n_blocks,),
            in_specs=[pl.BlockSpec((MOE_TB, d), lambda i, be, na: (i, 0)),
                      pl.BlockSpec((1, d, f2), lambda i, be, na: (be[i], 0, 0)),
                      pl.BlockSpec((1, 1, f2), lambda i, be, na: (be[i], 0, 0)),
                      pl.BlockSpec((1, f2 // 2, d), lambda i, be, na: (be[i], 0, 0)),
                      pl.BlockSpec((1, 1, d), lambda i, be, na: (be[i], 0, 0))],
            out_specs=pl.BlockSpec((MOE_TB, d), lambda i, be, na: (i, 0)),
            scratch_shapes=[pltpu.VMEM((d, f2), BF16), pltpu.VMEM((f2 // 2, d), BF16)],
        ),
        out_shape=out_shape,
        compiler_params=pltpu.CompilerParams(dimension_semantics=("arbitrary",),
                                             vmem_limit_bytes=VMEM_LIMIT),
        name="moe",
    )(block_expert, n_active, xg, wgu, bgu, wd, bd)


def _final_body(x1_ref, y_ref, gate_ref, g2_ref, gain_ref, *rest, group):
    o_ref = rest[-1]

    @pl.when(pl.program_id(0) < group)
    def _():
        gates = gate_ref[0]
        f = gates[:, 0:1] * y_ref[0, 0].astype(F32)
        for kk in range(1, TOP_K):
            f = f + gates[:, kk:kk + 1] * y_ref[kk, 0].astype(F32)
        o_ref[0] = x1_ref[0] + g2_ref[0] * (_rms(f) * gain_ref[...])

    @pl.when(pl.program_id(0) >= group)
    def _():
        o_ref[...] = jnp.zeros_like(o_ref)


def _final(x1, y4, gates, g2, gain, out_prev, batch_offset, total_batch):
    bsz, seq, d = x1.shape
    ts = min(TS_PROJ, seq)
    tiles = seq // ts
    assert out_prev is not None or batch_offset == 0
    rows = bsz if out_prev is not None else total_batch

    def tok(width):
        return pl.BlockSpec((1, ts, width),
                            lambda b, i: (jnp.minimum(b, bsz - 1), jnp.where(b < bsz, i, tiles - 1), 0))

    in_specs = [tok(d),
                pl.BlockSpec((TOP_K, 1, ts, d),
                             lambda b, i: (0, jnp.minimum(b, bsz - 1), jnp.where(b < bsz, i, tiles - 1), 0)),
                tok(LANES), pl.BlockSpec((1, 1, d), lambda b, i: (jnp.minimum(b, bsz - 1), 0, 0)),
                _const_spec((1, d))]
    args = [x1, y4, gates, g2, gain]
    aliases = {}
    if out_prev is not None:
        in_specs.append(pl.BlockSpec(memory_space=pl.ANY))
        args.append(out_prev)
        aliases = {len(args) - 1: 0}
    return pl.pallas_call(
        functools.partial(_final_body, group=bsz),
        grid=(rows, tiles),
        in_specs=in_specs,
        out_specs=pl.BlockSpec((1, ts, d), lambda b, i: (batch_offset + b, i, 0)),
        out_shape=jax.ShapeDtypeStruct((total_batch, seq, d), F32),
        input_output_aliases=aliases,
        compiler_params=pltpu.CompilerParams(dimension_semantics=("parallel", "parallel"),
                                             vmem_limit_bytes=VMEM_LIMIT),
        name="final",
    )(*args)


def _head_blocks(cols):
    out = []
    for c in cols:
        pad = LANES - c.shape[1]
        out.append(jnp.pad(c, ((0, 0), (0, pad))) if pad else c)
    return jnp.concatenate(out, axis=1)


def _prep_mixer_weights(w_in, w_q_up, w_kv_up):
    d = w_in.shape[0]
    wz = w_in[:, OFF_Z:OFF_XBC]
    wxbc = w_in[:, OFF_XBC:OFF_DT]
    wdt = w_in[:, OFF_DT:OFF_QA]
    wqa = w_in[:, OFF_QA:OFF_KVA]
    wkva = w_in[:, OFF_KVA:OFF_KR]
    wkr = w_in[:, OFF_KR:IN_COLS]
    kr_blk = jnp.concatenate([jnp.zeros((d, ROPE_LO), F32), wkr, jnp.zeros((d, LANES - ROPE_LO - MLA_ROPE), F32)], axis=1)
    dt_blk = jnp.pad(wdt, ((0, 0), (0, LANES - SSD_HEADS)))
    wsm = jnp.concatenate([kr_blk, dt_blk], axis=1)
    qh = MLA_NOPE + MLA_ROPE
    scale = math.log2(math.e) / math.sqrt(qh)
    wqup = _head_blocks([w_q_up[:, h * qh:(h + 1) * qh] for h in range(MLA_HEADS)]) * scale
    kvh = MLA_NOPE + MLA_V
    wkup = _head_blocks([w_kv_up[:, h * kvh:h * kvh + MLA_NOPE] for h in range(MLA_HEADS)])
    vcols = []
    for h in range(MLA_HEADS):
        vcols.append(w_kv_up[:, h * kvh + MLA_NOPE:(h + 1) * kvh])
        if h % 2 == 1:
            vcols.append(jnp.zeros((w_kv_up.shape[0], VT_ROWS - LANES), F32))
    wvup = jnp.concatenate(vcols, axis=1).T
    return tuple(w.astype(BF16) for w in (wz, wxbc, wsm, wqa, wkva)) + tuple(w.astype(BF16) for w in (wqup, wkup, wvup))


def _rope_tables(positions):
    inv_freq = ROPE_BASE ** (-(jnp.arange(HALF_ROPE, dtype=F32) * 2.0 / MLA_ROPE))
    angles = positions.astype(F32)[:, None, :] * inv_freq[None, :, None]
    bsz, _, seq = angles.shape
    return jnp.concatenate([jnp.zeros((bsz, ROPE_LO, seq), F32), jnp.cos(angles), jnp.sin(angles),
                            jnp.zeros((bsz, LANES - ROPE_LO - MLA_ROPE, seq), F32)], axis=1)


def _route(idx, rank, counts, tile_seen, ltab, n_tok):
    n_assign = n_tok * TOP_K
    n_tiles, tile, lanes = ltab.shape
    padded = ((counts + MOE_TB - 1) // MOE_TB) * MOE_TB
    padded_end = jnp.cumsum(padded)
    padded_start = padded_end - padded
    experts = jnp.arange(N_EXPERTS, dtype=jnp.int32)
    start_of = jnp.sum(jnp.where(idx[..., None] == experts, padded_start, 0), axis=-1)
    dest = (start_of + rank).reshape(-1)
    n_blocks = n_assign // MOE_TB + N_EXPERTS
    block_start = jnp.arange(n_blocks, dtype=jnp.int32) * MOE_TB
    block_expert = jnp.minimum(jnp.sum((padded_end[None, :] <= block_start[:, None]).astype(jnp.int32), axis=1),
                               N_EXPERTS - 1)
    n_active = (padded_end[-1] // MOE_TB).astype(jnp.int32).reshape(1)
    j = (block_start - padded_start[block_expert])[:, None] + jnp.arange(MOE_TB, dtype=jnp.int32)[None, :]
    seen_blk = tile_seen.T[block_expert]
    reached = seen_blk[:, None, :] <= j[:, :, None]
    tau = jnp.sum(reached.astype(jnp.int32), axis=-1) - 1
    j_local = j - jnp.max(jnp.where(reached, seen_blk[:, None, :], 0), axis=-1)
    valid = j < counts[block_expert][:, None]
    flat = (tau * tile + j_local) * lanes + block_expert[:, None]
    tok_local = ltab.reshape(-1)[jnp.clip(flat, 0, n_tiles * tile * lanes - 1)]
    filler = (block_start[:, None] + jnp.arange(MOE_TB, dtype=jnp.int32)[None, :]) % n_tok
    slot_tok = jnp.where(valid, tau * tile + tok_local, filler).reshape(-1)
    return dest, slot_tok, block_expert, n_active


def kernel(x, c, positions, w_ada, b_ada, pre_mix_norm, w_in, conv_w, conv_b, dt_bias, a_log, d_skip, ssd_norm, q_a_norm, w_q_up, kv_a_norm, w_kv_up, mla_norm, w_out, post_mix_norm, pre_ffn_norm, w_router, b_router, w_gate_up, b_gate_up, w_down, b_down, post_ffn_norm):
    bsz, seq, d = x.shape
    rope_tab = _rope_tables(positions)
    n_groups = BATCH_GROUPS if bsz % BATCH_GROUPS == 0 else 1
    gb = bsz // n_groups
    n_tok = gb * seq
    assert d == D_MODEL and seq % min(TS_PROJ, seq) == 0 and seq % min(SSD_L, seq) == 0
    assert seq % min(ATT_TQ, seq) == 0 and min(ATT_TQ, seq) % min(ATT_TK, seq) == 0
    assert (n_tok * TOP_K) % MOE_TB == 0
    pad_h = LANES - SSD_HEADS
    for l in range(w_ada.shape[0]):
        mod = _adaln(c, w_ada[l], b_ada[l])
        mods = [m.reshape(bsz, 1, d) for m in jnp.split(mod, 6, axis=-1)]
        mixer_w = _prep_mixer_weights(w_in[l], w_q_up[l], w_kv_up[l])
        wo = w_out[l].astype(BF16)
        wr = jnp.pad(w_router[l], ((0, 0), (0, LANES - N_EXPERTS)))
        br = jnp.pad(b_router[l], (0, LANES - N_EXPERTS)).reshape(1, LANES)
        dtb = jnp.pad(dt_bias[l], (0, pad_h)).reshape(1, LANES)
        alog = jnp.pad(a_log[l], (0, pad_h)).reshape(1, LANES)
        dsk = jnp.repeat(d_skip[l], SSD_HEAD_DIM).reshape(1, -1)
        out = None
        pending = None

        def experts_and_final(p, xs, out):
            y = _moe(p["block_expert"], p["n_active"], xs, w_gate_up[l], b_gate_up[l][:, None, :],
                     w_down[l], b_down[l][:, None, :])
            y4 = y[p["dest"].reshape(n_tok, TOP_K).T.reshape(-1)].reshape(TOP_K, gb, seq, d)
            return _final(p["x1"], y4, p["gates"], p["g2"], post_ffn_norm[l].reshape(1, d), out, p["offset"], bsz)

        for gi in range(n_groups):
            grp = slice(gi * gb, (gi + 1) * gb)
            sh1, sc1, g1, sh2, sc2, g2 = [m[grp] for m in mods]
            if pending is not None:
                sc1 = sc1 + jnp.where(pending["slot_tok"][0] < 0, 1.0, 0.0)
            z, xbc, dt, q, k, v = _inproj(x, sc1, sh1, pre_mix_norm[l].reshape(1, d), rope_tab, *mixer_w[:5],
                                          q_a_norm[l].reshape(1, -1), kv_a_norm[l].reshape(1, -1), *mixer_w[5:],
                                          batch_offset=gi * gb)
            y_ssd = _ssd(xbc, z, dt, conv_w[l], conv_b[l].reshape(1, -1), dtb, alog, dsk, ssd_norm[l].reshape(1, -1))
            prev_xs = None
            if pending is not None:
                ssd_bits = lax.bitcast_convert_type(y_ssd[0, 0, 0], jnp.uint16).astype(jnp.int32)
                prev_xs = pending["h2"].reshape(n_tok, d)[pending["slot_tok"] + jnp.where(ssd_bits < 0, 1, 0)]
            y_att = _attn(q, k, v)
            x1, h2, gates, route, cnt, seen, ltab = _outproj(
                y_ssd, y_att, x, g1, sc2, sh2, mla_norm[l].reshape(1, -1), wo[:SSD_INNER], wo[SSD_INNER:],
                post_mix_norm[l].reshape(1, d), pre_ffn_norm[l].reshape(1, d), wr, br, batch_offset=gi * gb)
            route = route.reshape(n_tok, LANES)
            counts = cnt[0, :N_EXPERTS].astype(jnp.int32)
            tile_seen = seen[:, 0, :N_EXPERTS].astype(jnp.int32)
            if prev_xs is not None:
                bits = lax.bitcast_convert_type(prev_xs[0, 0], jnp.uint16).astype(jnp.int32)
                tile_seen = tile_seen + jnp.where(bits < 0, 1, 0)
            dest, slot_tok, block_expert, n_active = _route(route[:, :TOP_K], route[:, TOP_K:2 * TOP_K], counts,
                                                            tile_seen, ltab, n_tok)
            if pending is not None:
                out = experts_and_final(pending, prev_xs, out)
            pending = dict(h2=h2, slot_tok=slot_tok, dest=dest, block_expert=block_expert, n_active=n_active,
                           x1=x1, gates=gates, g2=g2, offset=gi * gb)
        out = experts_and_final(pending, pending["h2"].reshape(n_tok, d)[pending["slot_tok"]], out)
        x = out
    return x
```

```python
import functools
import math

import jax
import jax.numpy as jnp
from jax import lax
from jax.experimental import pallas as pl
from jax.experimental.pallas import tpu as pltpu

F32 = jnp.float32
BF16 = jnp.bfloat16

D_MODEL = 1024
CHUNK = 64
SSD_INNER = 512
SSD_HEAD_DIM = 64
SSD_HEADS = 8
SSD_GROUPS = 2
SSD_STATE = 128
SSD_CONV = 4
SSD_BC = SSD_GROUPS * SSD_STATE
XBC_DIM = SSD_INNER + 2 * SSD_BC
MLA_V = 64
MLA_HEADS = 8
MLA_NOPE = 64
MLA_ROPE = 32
Q_LORA = 384
KV_LORA = 256
ROPE_BASE = 10000.0
OFF_Z = 0
OFF_XBC = OFF_Z + SSD_INNER
OFF_DT = OFF_XBC + XBC_DIM
OFF_QA = OFF_DT + SSD_HEADS
OFF_KVA = OFF_QA + Q_LORA
OFF_KR = OFF_KVA + KV_LORA
IN_COLS = OFF_KR + MLA_ROPE
N_EXPERTS = 32
TOP_K = 4
D_FF_EXPERT = 1024
SWIGLU_LIMIT = 7.0
SWIGLU_ALPHA = 1.702
NORM_EPS = 1e-6

LANES = 128
HALF_ROPE = MLA_ROPE // 2
ROPE_LO = MLA_NOPE
ROPE_HI = MLA_NOPE + HALF_ROPE
VT_ROWS = 144

ADA_TN = 1024
TS_PROJ = 512
SSD_L = 256
ATT_TQ = 2048
ATT_TK = 256
MOE_TB = 512
BATCH_GROUPS = 2
VMEM_LIMIT = 56 * 1024 * 1024


def _dot(a, b):
    return jnp.dot(a, b, preferred_element_type=F32)


def _dot_nt(a, b):
    return lax.dot_general(a, b, (((1,), (1,)), ((), ())), preferred_element_type=F32)


def _dot_tn(a, b):
    return lax.dot_general(a, b, (((0,), (0,)), ((), ())), preferred_element_type=F32)


def _split2(x):
    hi = x.astype(BF16)
    lo = (x - hi.astype(F32)).astype(BF16)
    return hi, lo


def _split3(x):
    h1 = x.astype(BF16)
    r1 = x - h1.astype(F32)
    h2 = r1.astype(BF16)
    h3 = (r1 - h2.astype(F32)).astype(BF16)
    return h1, h2, h3


def _dot3(a, b):
    ah, al = _split2(a)
    bh, bl = _split2(b)
    return _dot(ah, bh) + _dot(ah, bl) + _dot(al, bh)


def _rms(x):
    return x * lax.rsqrt(jnp.mean(x * x, axis=-1, keepdims=True) + NORM_EPS)


def _silu(x):
    hx = 0.5 * x
    return hx + hx * jnp.tanh(hx)


def _const_spec(shape):
    nd = len(shape)
    return pl.BlockSpec(shape, lambda *_: (0,) * nd)


def _adaln_body(c_ref, w_ref, b_ref, o_ref):
    o_ref[...] = _dot3(_silu(c_ref[...]), w_ref[...]) + b_ref[...]


def _adaln(c, w_ada, b_ada):
    bsz, d = c.shape
    n = w_ada.shape[1]
    tn = ADA_TN
    return pl.pallas_call(
        _adaln_body,
        grid=(n // tn,),
        in_specs=[_const_spec((bsz, d)),
                  pl.BlockSpec((d, tn), lambda j: (0, j)),
                  pl.BlockSpec((1, tn), lambda j: (0, j))],
        out_specs=pl.BlockSpec((bsz, tn), lambda j: (0, j)),
        out_shape=jax.ShapeDtypeStruct((bsz, n), F32),
        compiler_params=pltpu.CompilerParams(dimension_semantics=("arbitrary",),
                                             vmem_limit_bytes=VMEM_LIMIT),
        name="adaln",
    )(c, w_ada, b_ada.reshape(1, n))


def _rope_block(xb, ct, st, lane):
    partner = jnp.where(lane < ROPE_HI, pltpu.roll(xb, LANES - HALF_ROPE, 1), pltpu.roll(xb, HALF_ROPE, 1))
    return xb * ct + partner * st


def _inproj_body(x_ref, sc_ref, sh_ref, g_ref, tab_ref, wz_ref, wxbc_ref, wsm_ref, wqa_ref, wkva_ref,
                 qn_ref, kvn_ref, wqup_ref, wkup_ref, wvup_ref,
                 z_ref, xbc_ref, dt_ref, q_ref, k_ref, v_ref):
    x = x_ref[0]
    h = _rms(x) * (g_ref[...] * (1.0 + sc_ref[0])) + sh_ref[0]
    hb = h.astype(BF16)
    z_ref[0] = _dot(hb, wz_ref[...]).astype(BF16)
    xbc_ref[0] = _dot(hb, wxbc_ref[...]).astype(BF16)
    sm = _dot(hb, wsm_ref[...])
    dt_ref[0] = sm[:, LANES:]
    tab = tab_ref[0].T
    lane = lax.broadcasted_iota(jnp.int32, tab.shape, 1)
    in_lo = (lane >= ROPE_LO) & (lane < ROPE_HI)
    in_hi = (lane >= ROPE_HI) & (lane < ROPE_HI + HALF_ROPE)
    ct = jnp.where(lane < ROPE_LO, 1.0, jnp.where(in_lo, tab, jnp.where(in_hi, pltpu.roll(tab, HALF_ROPE, 1), 0.0)))
    st = jnp.where(in_lo, -pltpu.roll(tab, LANES - HALF_ROPE, 1), jnp.where(in_hi, tab, 0.0))
    kr =_rope_block(sm[:, :LANES], ct, st, lane)
    qan = (_rms(_dot(hb, wqa_ref[...])) * qn_ref[...]).astype(BF16)
    q = _dot(qan, wqup_ref[...])
    for hh in range(MLA_HEADS):
        blk = slice(hh * LANES, (hh + 1) * LANES)
        q_ref[0, :, blk] = _rope_block(q[:, blk], ct, st, lane).astype(BF16)
    kvn = (_rms(_dot(hb, wkva_ref[...])) * kvn_ref[...]).astype(BF16)
    k = _dot(kvn, wkup_ref[...])
    for hh in range(MLA_HEADS):
        blk = slice(hh * LANES, (hh + 1) * LANES)
        k_ref[0, :, blk] = (k[:, blk] + kr).astype(BF16)
    vt = _dot_nt(wvup_ref[...], kvn)
    vrow = lax.broadcasted_iota(jnp.int32, vt.shape, 0)
    v_ref[0] = jnp.where(vrow % VT_ROWS == LANES, 1.0, vt).astype(BF16)


def _inproj(x, sc1, sh1, gain, rope_tab, wz, wxbc, wsm, wqa, wkva, qn, kvn, wqup, wkup, wvup, *, batch_offset):
    _, seq, d = x.shape
    bsz = sc1.shape[0]
    ts = min(TS_PROJ, seq)
    hw = MLA_HEADS * LANES

    def tok(width):
        return pl.BlockSpec((1, ts, width), lambda b, i: (b, i, 0))

    def tok_full(width):
        return pl.BlockSpec((1, ts, width), lambda b, i: (batch_offset + b, i, 0))

    def per_batch(width):
        return pl.BlockSpec((1, 1, width), lambda b, i: (b, 0, 0))

    weights = (wz, wxbc, wsm, wqa, wkva, qn, kvn, wqup, wkup, wvup)
    out_widths = (SSD_INNER, XBC_DIM, LANES, hw, hw)
    out_dtypes = (BF16, BF16, F32, BF16, BF16)
    vdim = (MLA_HEADS // 2) * VT_ROWS
    out_shape = ([jax.ShapeDtypeStruct((bsz, seq, w), dt) for w, dt in zip(out_widths, out_dtypes)]
                 + [jax.ShapeDtypeStruct((bsz, vdim, seq), BF16)])
    return pl.pallas_call(
        _inproj_body,
        grid=(bsz, seq // ts),
        in_specs=[tok_full(d), per_batch(d), per_batch(d), _const_spec((1, d)),
                  pl.BlockSpec((1, LANES, ts), lambda b, i: (batch_offset + b, 0, i))]
                 + [_const_spec(w.shape) for w in weights],
        out_specs=[tok(w) for w in out_widths] + [pl.BlockSpec((1, vdim, ts), lambda b, i: (b, 0, i))],
        out_shape=out_shape,
        compiler_params=pltpu.CompilerParams(dimension_semantics=("parallel", "parallel"),
                                             vmem_limit_bytes=VMEM_LIMIT),
        name="inproj",
    )(x, sc1, sh1, gain, rope_tab, *weights)


CONV_HALO = 16


def _ssd_body(xc_ref, xp_ref, z_ref, dt_ref, cw_ref, cb_ref, dtb_ref, alog_ref, dsk_ref, ng_ref,
              y_ref, st_scr, *, blk):
    i = pl.program_id(1)

    @pl.when(i == 0)
    def _():
        st_scr[...] = jnp.zeros_like(st_scr)

    xc = xc_ref[0]
    row = lax.broadcasted_iota(jnp.int32, (blk, blk), 0)
    col = lax.broadcasted_iota(jnp.int32, (blk, blk), 1)
    conv = cb_ref[...] + cw_ref[SSD_CONV - 1:SSD_CONV, :] * xc.astype(F32)
    tail = jnp.where(i > 0, xp_ref[0].astype(F32)[CONV_HALO - 8:], 0.0)
    row8 = lax.broadcasted_iota(jnp.int32, tail.shape, 0)
    head_fix = jnp.zeros(tail.shape, F32)
    for shift in range(1, SSD_CONV):
        w = cw_ref[SSD_CONV - 1 - shift:SSD_CONV - shift, :]
        shifted = _dot(jnp.where(row - col == shift, 1.0, 0.0).astype(BF16), xc)
        conv = conv + w * shifted
        head_fix = head_fix + jnp.where(row8 < shift, w * pltpu.roll(tail, shift, 0), 0.0)
    conv = jnp.concatenate([conv[:8] + head_fix, conv[8:]], axis=0)
    xa = _silu(conv)
    xs = xa[:, :SSD_INNER]
    bm = xa[:, SSD_INNER:SSD_INNER + SSD_BC]
    cm = xa[:, SSD_INNER + SSD_BC:]

    hl = lax.broadcasted_iota(jnp.int32, (1, LANES), 1)
    dtr = dt_ref[0] + dtb_ref[...]
    dt = jnp.maximum(dtr, 0.0) + jnp.log(1.0 + jnp.exp(-jnp.abs(dtr)))
    a = jnp.where(hl < SSD_HEADS, -jnp.exp(alog_ref[...]), 0.0)
    dta = dt * a
    row = lax.broadcasted_iota(jnp.int32, (blk, blk), 0)
    col = lax.broadcasted_iota(jnp.int32, (blk, blk), 1)
    tril = row >= col
    trilb = jnp.where(tril, 1.0, 0.0).astype(BF16)
    d1, d2, d3 = _split3(dta)
    cs = _dot(trilb, d1) + _dot(trilb, d2) + _dot(trilb, d3)
    cs_last = cs[blk - 1:blk, :]
    ecs = jnp.exp(cs)
    dte = jnp.exp(cs_last - cs)
    cs_t = cs.T

    er = lax.broadcasted_iota(jnp.int32, (LANES, SSD_INNER), 0)
    ec = lax.broadcasted_iota(jnp.int32, (LANES, SSD_INNER), 1)
    expand = jnp.where(ec // SSD_HEAD_DIM == er, 1.0, 0.0).astype(BF16)

    def per_channel(v):
        vh, vl = _split2(v)
        return _dot(vh, expand) + _dot(vl, expand)

    dt_e = per_channel(dt)
    ecs_e = per_channel(ecs)
    dte_e = per_channel(dte)
    xdt = xs * dt_e
    xdt_b = xdt.astype(BF16)
    xw_b = (xdt * dte_e).astype(BF16)

    gw = SSD_INNER // SSD_GROUPS
    heads_per_group = SSD_HEADS // SSD_GROUPS
    lane = lax.broadcasted_iota(jnp.int32, (blk, LANES), 1)
    y_groups = []
    for g in range(SSD_GROUPS):
        bg = bm[:, g * SSD_STATE:(g + 1) * SSD_STATE].astype(BF16)
        cg = cm[:, g * SSD_STATE:(g + 1) * SSD_STATE].astype(BF16)
        cb = _dot_nt(cg, bg)
        state = st_scr[g]
        y_off = _dot(cg, state.astype(BF16))
        pairs = []
        for j in range(heads_per_group // 2):
            xp = xdt_b[:, g * gw + j * LANES:g * gw + (j + 1) * LANES]
            halves = []
            for u in range(2):
                hidx = g * heads_per_group + 2 * j + u
                seg = cs[:, hidx:hidx + 1] - cs_t[hidx:hidx + 1, :]
                dec = jnp.exp(jnp.where(tril, seg, -jnp.inf))
                halves.append(_dot((cb * dec).astype(BF16), xp))
            pairs.append(jnp.where(lane < SSD_HEAD_DIM, halves[0], halves[1]))
        y_diag = jnp.concatenate(pairs, axis=1)
        y_groups.append(y_diag + y_off * ecs_e[:, g * gw:(g + 1) * gw])
        st_scr[g] = (state * ecs_e[blk - 1:blk, g * gw:(g + 1) * gw]
                     + _dot_tn(bg, xw_b[:, g * gw:(g + 1) * gw]))
    y = jnp.concatenate(y_groups, axis=1) + xs * dsk_ref[...]
    y = y * _silu(z_ref[0].astype(F32))
    y = jnp.concatenate([_rms(y[:, g * gw:(g + 1) * gw]) for g in range(SSD_GROUPS)], axis=1)
    y_ref[0] = (y * ng_ref[...]).astype(BF16)


def _ssd(xbc, z, dt, conv_w, conv_b, dt_bias, a_log, d_skip_e, norm_gain):
    bsz, seq, _ = xbc.shape
    blk = min(SSD_L, seq)
    halo_per_blk = blk // CONV_HALO
    body = functools.partial(_ssd_body, blk=blk)
    out_shape = jax.ShapeDtypeStruct((bsz, seq, SSD_INNER), BF16)
    return pl.pallas_call(
        body,
        grid=(bsz, seq // blk),
        in_specs=[pl.BlockSpec((1, blk, XBC_DIM), lambda b, i: (b, i, 0)),
                  pl.BlockSpec((1, CONV_HALO, XBC_DIM), lambda b, i: (b, jnp.maximum(i * halo_per_blk - 1, 0), 0)),
                  pl.BlockSpec((1, blk, SSD_INNER), lambda b, i: (b, i, 0)),
                  pl.BlockSpec((1, blk, LANES), lambda b, i: (b, i, 0)),
                  _const_spec((SSD_CONV, XBC_DIM)), _const_spec((1, XBC_DIM)),
                  _const_spec((1, LANES)), _const_spec((1, LANES)),
                  _const_spec((1, SSD_INNER)), _const_spec((1, SSD_INNER))],
        out_specs=pl.BlockSpec((1, blk, SSD_INNER), lambda b, i: (b, i, 0)),
        out_shape=out_shape,
        scratch_shapes=[pltpu.VMEM((SSD_GROUPS, SSD_STATE, SSD_INNER // SSD_GROUPS), F32)],
        compiler_params=pltpu.CompilerParams(dimension_semantics=("parallel", "arbitrary"),
                                             vmem_limit_bytes=VMEM_LIMIT),
        name="ssd",
    )(xbc, xbc, z, dt, conv_w, conv_b, dt_bias, a_log, d_skip_e, norm_gain)


def _attn_body(q_ref, k_ref, v_ref, o_ref, *scratch, tq, tk):
    n_streams = 2 * (tq // tk)
    s_scr = (scratch[:n_streams], scratch[n_streams:2 * n_streams])
    acc_scr = scratch[2 * n_streams:]
    qi = pl.program_id(2)
    n_sub = tq // tk
    n_full = qi * n_sub
    krow = lax.broadcasted_iota(jnp.int32, (tk, tk), 0)
    qcol = lax.broadcasted_iota(jnp.int32, (tk, tk), 1)
    diag_ok = krow // CHUNK <= qcol // CHUNK
    vrow = lax.broadcasted_iota(jnp.int32, (LANES, tk), 0)
    streams = [(u, r) for u in range(2) for r in range(n_sub)]
    qs = [q_ref[0, r * tk:(r + 1) * tk, u * LANES:(u + 1) * LANES] for u, r in streams]

    def put_scores(ki, which, slot):
        start = pl.multiple_of(ki * tk, tk)
        k2 = k_ref[0, pl.ds(start, tk), :]
        out = {}
        for si in which:
            u = streams[si][0]
            s = _dot_nt(k2[:, u * LANES:(u + 1) * LANES], qs[si])
            s_scr[slot][si][...] = s
            out[si] = jnp.max(s, axis=0, keepdims=True)
        return out

    def values_t(ki):
        return v_ref[0, :, pl.ds(pl.multiple_of(ki * tk, tk), tk)]

    def softmax_pv(si, m, s_max, slot, vt, masked):
        s = s_scr[slot][si][...]
        if masked:
            s = jnp.where(diag_ok, s, -jnp.inf)
            s_max = jnp.max(s, axis=0, keepdims=True)
        m_new = jnp.maximum(m, s_max)
        alpha = jnp.exp2(m - m_new)
        p = jnp.exp2((s - m_new).astype(BF16))
        acc_scr[si][...] = alpha * acc_scr[si][...] + _dot(vt, p)
        return m_new

    every = list(range(len(streams)))
    for ref in acc_scr:
        ref[...] = jnp.zeros_like(ref)

    def step(j, state):
        ms, s_maxes = state
        for slot in range(2):
            ki = 2 * j + slot
            nxt = put_scores(ki + 1, every, 1 - slot)
            vt = values_t(ki)
            ms = tuple(softmax_pv(si, ms[si], s_maxes[si], slot, vt, False) for si in every)
            s_maxes = tuple(nxt[si] for si in every)
        return ms, s_maxes

    first = put_scores(0, every, 0)
    m_init = jnp.full((1, tk), -jnp.inf, F32)
    ms, s_maxes = lax.fori_loop(0, n_full // 2, step, ((m_init,) * len(streams), tuple(first[si] for si in every)))
    ms = list(ms)
    s_maxes = dict(zip(every, s_maxes))
    for dd in range(n_sub):
        slot = dd % 2
        live = [si for si in every if dd <= streams[si][1]]
        later = [si for si in every if dd + 1 <= streams[si][1]]
        nxt = put_scores(n_full + dd + 1, later, 1 - slot) if later else {}
        vt = values_t(n_full + dd)
        for si in live:
            ms[si] = softmax_pv(si, ms[si], s_maxes[si], slot, vt, dd == streams[si][1])
        s_maxes = nxt
    for r in range(n_sub):
        a0 = acc_scr[streams.index((0, r))][...]
        a1 = acc_scr[streams.index((1, r))][...]
        out_t = jnp.where(vrow < MLA_V, a0[:LANES] / a0[LANES:LANES + 1], a1[:LANES] / a1[LANES:LANES + 1])
        o_ref[0, r * tk:(r + 1) * tk, :] = out_t.T.astype(BF16)


def _attn(q, k, v):
    bsz, seq, _ = q.shape
    tk = min(ATT_TK, seq)
    tq = min(ATT_TQ, seq)
    assert (tq // tk) % 2 == 0, "the two-slot score pipeline needs an even number of query sub-tiles"
    n_streams = 2 * (tq // tk)
    body = functools.partial(_attn_body, tq=tq, tk=tk)
    out_shape = jax.ShapeDtypeStruct((bsz, seq, MLA_HEADS * MLA_V), BF16)
    return pl.pallas_call(
        body,
        grid=(bsz, MLA_HEADS // 2, seq // tq),
        in_specs=[pl.BlockSpec((1, tq, 2 * LANES), lambda b, hp, i: (b, i, hp)),
                  pl.BlockSpec((1, seq, 2 * LANES), lambda b, hp, i: (b, 0, hp)),
                  pl.BlockSpec((1, VT_ROWS, seq), lambda b, hp, i: (b, hp, 0))],
        out_specs=pl.BlockSpec((1, tq, LANES), lambda b, hp, i: (b, i, hp)),
        out_shape=out_shape,
        scratch_shapes=[pltpu.VMEM((tk, tk), F32)] * (2 * n_streams) + [pltpu.VMEM((VT_ROWS, tk), F32)] * n_streams,
        compiler_params=pltpu.CompilerParams(dimension_semantics=("parallel", "parallel", "arbitrary"),
                                             vmem_limit_bytes=VMEM_LIMIT),
        name="attn",
    )(q, k, v)


def _outproj_body(ys_ref, ya_ref, x_ref, g1_ref, sc2_ref, sh2_ref, mn_ref, wo1_ref, wo2_ref, pmn_ref, pfn_ref,
                  wr_ref, br_ref, x1_ref, h2_ref, gate_ref, idx_ref, cnt_ref, pref_ref, ltab_ref, cnt_scr):
    first = (pl.program_id(0) == 0) & (pl.program_id(1) == 0)

    @pl.when(first)
    def _():
        cnt_scr[...] = jnp.zeros_like(cnt_scr)

    yan =(_rms(ya_ref[0].astype(F32)) * mn_ref[...]).astype(BF16)
    mix = _dot(ys_ref[0], wo1_ref[...]) + _dot(yan, wo2_ref[...])
    x1 = x_ref[0] + g1_ref[0] * (_rms(mix) * pmn_ref[...])
    x1_ref[0] = x1
    h2 = _rms(x1) * (pfn_ref[...] * (1.0 + sc2_ref[0])) + sh2_ref[0]
    h2_ref[0] = h2.astype(BF16)
    h_hi, h_lo = _split2(h2)
    w_hi, w_lo = _split2(wr_ref[...])
    both = _dot(h_hi, jnp.concatenate([w_hi, w_lo], axis=1))
    logits = both[:, :LANES] + both[:, LANES:] + _dot(h_lo, w_hi) + br_ref[...]
    lane = lax.broadcasted_iota(jnp.int32, logits.shape, 1)
    cur = jnp.where(lane < N_EXPERTS, logits, -jnp.inf)
    vals, idxs = [], []
    for _ in range(TOP_K):
        m = jnp.max(cur, axis=-1, keepdims=True)
        ix = jnp.min(jnp.where(cur == m, lane, LANES), axis=-1, keepdims=True)
        vals.append(m)
        idxs.append(ix)
        cur = jnp.where(lane == ix, -jnp.inf, cur)
    es = [jnp.exp(v - vals[0]) for v in vals]
    denom = es[0]
    for e in es[1:]:
        denom = denom + e
    onehot = jnp.zeros(logits.shape, F32)
    for kk in range(TOP_K):
        onehot = onehot + jnp.where(lane == idxs[kk], 1.0, 0.0)
    ts = logits.shape[0]
    row = lax.broadcasted_iota(jnp.int32, (ts, ts), 0)
    col = lax.broadcasted_iota(jnp.int32, (ts, ts), 1)
    before = jnp.where(row > col, 1.0, 0.0).astype(BF16)
    prior = _dot(before, onehot.astype(BF16))
    seen = cnt_scr[...]
    pref_ref[0] = seen
    gate_out = jnp.zeros(logits.shape, F32)
    idx_out = jnp.zeros(logits.shape, jnp.int32)
    local_out = jnp.zeros(logits.shape, F32)
    for kk in range(TOP_K):
        mine = lane == idxs[kk]
        local = jnp.sum(jnp.where(mine, prior, 0.0), axis=-1, keepdims=True)
        rank = local + jnp.sum(jnp.where(mine, seen, 0.0), axis=-1, keepdims=True)
        gate_out = jnp.where(lane == kk, es[kk] / denom, gate_out)
        idx_out = jnp.where(lane == kk, idxs[kk], idx_out)
        idx_out = jnp.where(lane == TOP_K + kk, rank.astype(jnp.int32), idx_out)
        local_out = jnp.where(lane == kk, local, local_out)
    gate_ref[0] = gate_out
    idx_ref[0] = idx_out
    local_rows = local_out.T.astype(jnp.int32)
    tok = lax.broadcasted_iota(jnp.int32, logits.shape, 0)
    tok_hi = (tok // 16).astype(F32)
    tok_lo = (tok % 16).astype(F32)
    ltab = jnp.zeros((ts, 2 * LANES), F32)
    for kk in range(TOP_K):
        mine = lane == idxs[kk]
        at_rank = jnp.where(row == local_rows[kk:kk + 1, :], 1.0, 0.0).astype(BF16)
        tagged = jnp.concatenate([jnp.where(mine, tok_hi, 0.0), jnp.where(mine, tok_lo, 0.0)], axis=1)
        ltab = ltab + _dot(at_rank, tagged.astype(BF16))
    ltab_ref[0] = (16.0 * ltab[:, :LANES] + ltab[:, LANES:]).astype(jnp.int32)
    cnt_scr[...] = seen + jnp.sum(onehot, axis=0, keepdims=True)
    cnt_ref[...] = cnt_scr[...]


def _outproj(y_ssd, y_att, x, g1, sc2, sh2, mla_norm, wo1, wo2, post_mix_norm, pre_ffn_norm, wr, br, *, batch_offset):
    bsz, seq, _ = y_ssd.shape
    d = x.shape[-1]
    ts = min(TS_PROJ, seq)
    tiles = seq // ts

    def tok(width):
        return pl.BlockSpec((1, ts, width), lambda b, i: (b, i, 0))

    x_spec = pl.BlockSpec((1, ts, d), lambda b, i: (batch_offset + b, i, 0))

    def per_batch(width):
        return pl.BlockSpec((1, 1, width), lambda b, i: (b, 0, 0))

    consts = (mla_norm, wo1, wo2, post_mix_norm, pre_ffn_norm, wr, br)
    out_shape = [jax.ShapeDtypeStruct((bsz, seq, d), F32), jax.ShapeDtypeStruct((bsz, seq, d), BF16),
                 jax.ShapeDtypeStruct((bsz, seq, LANES), F32), jax.ShapeDtypeStruct((bsz, seq, LANES), jnp.int32),
                 jax.ShapeDtypeStruct((1, LANES), F32),
                 jax.ShapeDtypeStruct((bsz * tiles, 1, LANES), F32),
                 jax.ShapeDtypeStruct((bsz * tiles, ts, LANES), jnp.int32)]
    return pl.pallas_call(
        _outproj_body,
        grid=(bsz, seq // ts),
        in_specs=[tok(SSD_INNER), tok(MLA_HEADS * MLA_V), x_spec, per_batch(d), per_batch(d), per_batch(d)]
                 + [_const_spec(w.shape) for w in consts],
        out_specs=[tok(d), tok(d), tok(LANES), tok(LANES), _const_spec((1, LANES)),
                   pl.BlockSpec((1, 1, LANES), lambda b, i: (b * tiles + i, 0, 0)),
                   pl.BlockSpec((1, ts, LANES), lambda b, i: (b * tiles + i, 0, 0))],
        out_shape=out_shape,
        scratch_shapes=[pltpu.VMEM((1, LANES), F32)],
        compiler_params=pltpu.CompilerParams(dimension_semantics=("arbitrary", "arbitrary"),
                                             vmem_limit_bytes=VMEM_LIMIT),
        name="outproj",
    )(y_ssd, y_att, x, g1, sc2, sh2, *consts)


def _moe_body(be_ref, na_ref, x_ref, wgu_ref, bgu_ref, wd_ref, bd_ref, y_ref, wgu_b, wd_b):
    i = pl.program_id(0)

    @pl.when((i == 0) | (be_ref[i] != be_ref[jnp.maximum(i - 1, 0)]))
    def _():
        wgu_b[...] = wgu_ref[0].astype(BF16)
        wd_b[...] = wd_ref[0].astype(BF16)

    @pl.when(i < na_ref[0])
    def _():
        gu = _dot(x_ref[...], wgu_b[...]) + bgu_ref[0]
        glu = jnp.minimum(gu[:, :D_FF_EXPERT], SWIGLU_LIMIT)
        lin = jnp.clip(gu[:, D_FF_EXPERT:], -SWIGLU_LIMIT, SWIGLU_LIMIT)
        act = glu * jax.nn.sigmoid(SWIGLU_ALPHA * glu) * (lin + 1.0)
        y_ref[...] = (_dot(act.astype(BF16), wd_b[...]) + bd_ref[0]).astype(BF16)

    @pl.when(i >= na_ref[0])
    def _():
        y_ref[...] = jnp.zeros_like(y_ref)


def _moe(block_expert, n_active, xg, wgu, bgu, wd, bd):
    n_slots, d = xg.shape
    n_blocks = n_slots // MOE_TB
    f2 = wgu.shape[2]
    out_shape = jax.ShapeDtypeStruct((n_slots, d), BF16)
    return pl.pallas_call(
        _moe_body,
        grid_spec=pltpu.PrefetchScalarGridSpec(
            num_scalar_prefetch=2,
            grid=(n_blocks,),
            in_specs=[pl.BlockSpec((MOE_TB, d), lambda i, be, na: (i, 0)),
                      pl.BlockSpec((1, d, f2), lambda i, be, na: (be[i], 0, 0)),
                      pl.BlockSpec((1, 1, f2), lambda i, be, na: (be[i], 0, 0)),
                      pl.BlockSpec((1, f2 // 2, d), lambda i, be, na: (be[i], 0, 0)),
                      pl.BlockSpec((1, 1, d), lambda i, be, na: (be[i], 0, 0))],
            out_specs=pl.BlockSpec((MOE_TB, d), lambda i, be, na: (i, 0)),
            scratch_shapes=[pltpu.VMEM((d, f2), BF16), pltpu.VMEM((f2 // 2, d), BF16)],
        ),
        out_shape=out_shape,
        compiler_params=pltpu.CompilerParams(dimension_semantics=("arbitrary",),
                                             vmem_limit_bytes=VMEM_LIMIT),
        name="moe",
    )(block_expert, n_active, xg, wgu, bgu, wd, bd)


def _final_body(x1_ref, y_ref, gate_ref, g2_ref, gain_ref, *rest, group):
    o_ref = rest[-1]

    @pl.when(pl.program_id(0) < group)
    def _():
        gates = gate_ref[0]
        f = gates[:, 0:1] * y_ref[0, 0].astype(F32)
        for kk in range(1, TOP_K):
            f = f + gates[:, kk:kk + 1] * y_ref[kk, 0].astype(F32)
        o_ref[0] = x1_ref[0] + g2_ref[0] * (_rms(f) * gain_ref[...])

    @pl.when(pl.program_id(0) >= group)
    def _():
        o_ref[...] = jnp.zeros_like(o_ref)


def _final(x1, y4, gates, g2, gain, out_prev, batch_offset, total_batch):
    bsz, seq, d = x1.shape
    ts = min(TS_PROJ, seq)
    tiles = seq // ts
    assert out_prev is not None or batch_offset == 0
    rows = bsz if out_prev is not None else total_batch

    def tok(width):
        return pl.BlockSpec((1, ts, width),
                            lambda b, i: (jnp.minimum(b, bsz - 1), jnp.where(b < bsz, i, tiles - 1), 0))

    in_specs = [tok(d),
                pl.BlockSpec((TOP_K, 1, ts, d),
                             lambda b, i: (0, jnp.minimum(b, bsz - 1), jnp.where(b < bsz, i, tiles - 1), 0)),
                tok(LANES), pl.BlockSpec((1, 1, d), lambda b, i: (jnp.minimum(b, bsz - 1), 0, 0)),
                _const_spec((1, d))]
    args = [x1, y4, gates, g2, gain]
    aliases = {}
    if out_prev is not None:
        in_specs.append(pl.BlockSpec(memory_space=pl.ANY))
        args.append(out_prev)
        aliases = {len(args) - 1: 0}
    return pl.pallas_call(
        functools.partial(_final_body, group=bsz),
        grid=(rows, tiles),
        in_specs=in_specs,
        out_specs=pl.BlockSpec((1, ts, d), lambda b, i: (batch_offset + b, i, 0)),
        out_shape=jax.ShapeDtypeStruct((total_batch, seq, d), F32),
        input_output_aliases=aliases,
        compiler_params=pltpu.CompilerParams(dimension_semantics=("parallel", "parallel"),
                                             vmem_limit_bytes=VMEM_LIMIT),
        name="final",
    )(*args)


def _head_blocks(cols):
    out = []
    for c in cols:
        pad = LANES - c.shape[1]
        out.append(jnp.pad(c, ((0, 0), (0, pad))) if pad else c)
    return jnp.concatenate(out, axis=1)


def _prep_mixer_weights(w_in, w_q_up, w_kv_up):
    d = w_in.shape[0]
    wz = w_in[:, OFF_Z:OFF_XBC]
    wxbc = w_in[:, OFF_XBC:OFF_DT]
    wdt = w_in[:, OFF_DT:OFF_QA]
    wqa = w_in[:, OFF_QA:OFF_KVA]
    wkva = w_in[:, OFF_KVA:OFF_KR]
    wkr = w_in[:, OFF_KR:IN_COLS]
    kr_blk = jnp.concatenate([jnp.zeros((d, ROPE_LO), F32), wkr, jnp.zeros((d, LANES - ROPE_LO - MLA_ROPE), F32)], axis=1)
    dt_blk = jnp.pad(wdt, ((0, 0), (0, LANES - SSD_HEADS)))
    wsm = jnp.concatenate([kr_blk, dt_blk], axis=1)
    qh = MLA_NOPE + MLA_ROPE
    scale = math.log2(math.e) / math.sqrt(qh)
    wqup = _head_blocks([w_q_up[:, h * qh:(h + 1) * qh] for h in range(MLA_HEADS)]) * scale
    kvh = MLA_NOPE + MLA_V
    wkup = _head_blocks([w_kv_up[:, h * kvh:h * kvh + MLA_NOPE] for h in range(MLA_HEADS)])
    vcols = []
    for h in range(MLA_HEADS):
        vcols.append(w_kv_up[:, h * kvh + MLA_NOPE:(h + 1) * kvh])
        if h % 2 == 1:
            vcols.append(jnp.zeros((w_kv_up.shape[0], VT_ROWS - LANES), F32))
    wvup = jnp.concatenate(vcols, axis=1).T
    return tuple(w.astype(BF16) for w in (wz, wxbc, wsm, wqa, wkva)) + tuple(w.astype(BF16) for w in (wqup, wkup, wvup))


def _rope_tables(positions):
    inv_freq = ROPE_BASE ** (-(jnp.arange(HALF_ROPE, dtype=F32) * 2.0 / MLA_ROPE))
    angles = positions.astype(F32)[:, None, :] * inv_freq[None, :, None]
    bsz, _, seq = angles.shape
    return jnp.concatenate([jnp.zeros((bsz, ROPE_LO, seq), F32), jnp.cos(angles), jnp.sin(angles),
                            jnp.zeros((bsz, LANES - ROPE_LO - MLA_ROPE, seq), F32)], axis=1)


def _route(idx, rank, counts, tile_seen, ltab, n_tok):
    n_assign = n_tok * TOP_K
    n_tiles, tile, lanes = ltab.shape
    padded = ((counts + MOE_TB - 1) // MOE_TB) * MOE_TB
    padded_end = jnp.cumsum(padded)
    padded_start = padded_end - padded
    experts = jnp.arange(N_EXPERTS, dtype=jnp.int32)
    start_of = jnp.sum(jnp.where(idx[..., None] == experts, padded_start, 0), axis=-1)
    dest = (start_of + rank).reshape(-1)
    n_blocks = n_assign // MOE_TB + N_EXPERTS
    block_start = jnp.arange(n_blocks, dtype=jnp.int32) * MOE_TB
    block_expert = jnp.minimum(jnp.sum((padded_end[None, :] <= block_start[:, None]).astype(jnp.int32), axis=1),
                               N_EXPERTS - 1)
    n_active = (padded_end[-1] // MOE_TB).astype(jnp.int32).reshape(1)
    j = (block_start - padded_start[block_expert])[:, None] + jnp.arange(MOE_TB, dtype=jnp.int32)[None, :]
    seen_blk = tile_seen.T[block_expert]
    reached = seen_blk[:, None, :] <= j[:, :, None]
    tau = jnp.sum(reached.astype(jnp.int32), axis=-1) - 1
    j_local = j - jnp.max(jnp.where(reached, seen_blk[:, None, :], 0), axis=-1)
    valid = j < counts[block_expert][:, None]
    flat = (tau * tile + j_local) * lanes + block_expert[:, None]
    tok_local = ltab.reshape(-1)[jnp.clip(flat, 0, n_tiles * tile * lanes - 1)]
    filler = (block_start[:, None] + jnp.arange(MOE_TB, dtype=jnp.int32)[None, :]) % n_tok
    slot_tok = jnp.where(valid, tau * tile + tok_local, filler).reshape(-1)
    return dest, slot_tok, block_expert, n_active


def kernel(x, c, positions, w_ada, b_ada, pre_mix_norm, w_in, conv_w, conv_b, dt_bias, a_log, d_skip, ssd_norm, q_a_norm, w_q_up, kv_a_norm, w_kv_up, mla_norm, w_out, post_mix_norm, pre_ffn_norm, w_router, b_router, w_gate_up, b_gate_up, w_down, b_down, post_ffn_norm):
    bsz, seq, d = x.shape
    rope_tab = _rope_tables(positions)
    n_groups = BATCH_GROUPS if bsz % BATCH_GROUPS == 0 else 1
    gb = bsz // n_groups
    n_tok = gb * seq
    assert d == D_MODEL and seq % min(TS_PROJ, seq) == 0 and seq % min(SSD_L, seq) == 0
    assert seq % min(ATT_TQ, seq) == 0 and min(ATT_TQ, seq) % min(ATT_TK, seq) == 0
    assert (n_tok * TOP_K) % MOE_TB == 0
    pad_h = LANES - SSD_HEADS
    for l in range(w_ada.shape[0]):
        mod = _adaln(c, w_ada[l], b_ada[l])
        mods = [m.reshape(bsz, 1, d) for m in jnp.split(mod, 6, axis=-1)]
        mixer_w = _prep_mixer_weights(w_in[l], w_q_up[l], w_kv_up[l])
        wo = w_out[l].astype(BF16)
        wr = jnp.pad(w_router[l], ((0, 0), (0, LANES - N_EXPERTS)))
        br = jnp.pad(b_router[l], (0, LANES - N_EXPERTS)).reshape(1, LANES)
        dtb = jnp.pad(dt_bias[l], (0, pad_h)).reshape(1, LANES)
        alog = jnp.pad(a_log[l], (0, pad_h)).reshape(1, LANES)
        dsk = jnp.repeat(d_skip[l], SSD_HEAD_DIM).reshape(1, -1)
        out = None
        pending = None

        def experts_and_final(p, xs, out):
            y = _moe(p["block_expert"], p["n_active"], xs, w_gate_up[l], b_gate_up[l][:, None, :],
                     w_down[l], b_down[l][:, None, :])
            y4 = y[p["dest"].reshape(n_tok, TOP_K).T.reshape(-1)].reshape(TOP_K, gb, seq, d)
            return _final(p["x1"], y4, p["gates"], p["g2"], post_ffn_norm[l].reshape(1, d), out, p["offset"], bsz)

        for gi in range(n_groups):
            grp = slice(gi * gb, (gi + 1) * gb)
            sh1, sc1, g1, sh2, sc2, g2 = [m[grp] for m in mods]
            if pending is not None:
                sc1 = sc1 + jnp.where(pending["slot_tok"][0] < 0, 1.0, 0.0)
            z, xbc, dt, q, k, v = _inproj(x, sc1, sh1, pre_mix_norm[l].reshape(1, d), rope_tab, *mixer_w[:5],
                                          q_a_norm[l].reshape(1, -1), kv_a_norm[l].reshape(1, -1), *mixer_w[5:],
                                          batch_offset=gi * gb)
            y_ssd = _ssd(xbc, z, dt, conv_w[l], conv_b[l].reshape(1, -1), dtb, alog, dsk, ssd_norm[l].reshape(1, -1))
            prev_xs = None
            if pending is not None:
                ssd_bits = lax.bitcast_convert_type(y_ssd[0, 0, 0], jnp.uint16).astype(jnp.int32)
                prev_xs = pending["h2"].reshape(n_tok, d)[pending["slot_tok"] + jnp.where(ssd_bits < 0, 1, 0)]
            y_att = _attn(q, k, v)
            x1, h2, gates, route, cnt, seen, ltab = _outproj(
                y_ssd, y_att, x, g1, sc2, sh2, mla_norm[l].reshape(1, -1), wo[:SSD_INNER], wo[SSD_INNER:],
                post_mix_norm[l].reshape(1, d), pre_ffn_norm[l].reshape(1, d), wr, br, batch_offset=gi * gb)
            route = route.reshape(n_tok, LANES)
            counts = cnt[0, :N_EXPERTS].astype(jnp.int32)
            tile_seen = seen[:, 0, :N_EXPERTS].astype(jnp.int32)
            if prev_xs is not None:
                bits = lax.bitcast_convert_type(prev_xs[0, 0], jnp.uint16).astype(jnp.int32)
                tile_seen = tile_seen + jnp.where(bits < 0, 1, 0)
            dest, slot_tok, block_expert, n_active = _route(route[:, :TOP_K], route[:, TOP_K:2 * TOP_K], counts,
                                                            tile_seen, ltab, n_tok)
            if pending is not None:
                out = experts_and_final(pending, prev_xs, out)
            pending = dict(h2=h2, slot_tok=slot_tok, dest=dest, block_expert=block_expert, n_active=n_active,
                           x1=x1, gates=gates, g2=g2, offset=gi * gb)
        out = experts_and_final(pending, pending["h2"].reshape(n_tok, d)[pending["slot_tok"]], out)
        x = out
    return x
```

```python
import functools
import math

import jax
import jax.numpy as jnp
from jax import lax
from jax.experimental import pallas as pl
from jax.experimental.pallas import tpu as pltpu

F32 = jnp.float32
BF16 = jnp.bfloat16

D_MODEL = 1024
CHUNK = 64
SSD_INNER = 512
SSD_HEAD_DIM = 64
SSD_HEADS = 8
SSD_GROUPS = 2
SSD_STATE = 128
SSD_CONV = 4
SSD_BC = SSD_GROUPS * SSD_STATE
XBC_DIM = SSD_INNER + 2 * SSD_BC
MLA_V = 64
MLA_HEADS = 8
MLA_NOPE = 64
MLA_ROPE = 32
Q_LORA = 384
KV_LORA = 256
ROPE_BASE = 10000.0
OFF_Z = 0
OFF_XBC = OFF_Z + SSD_INNER
OFF_DT = OFF_XBC + XBC_DIM
OFF_QA = OFF_DT + SSD_HEADS
OFF_KVA = OFF_QA + Q_LORA
OFF_KR = OFF_KVA + KV_LORA
IN_COLS = OFF_KR + MLA_ROPE
N_EXPERTS = 32
TOP_K = 4
D_FF_EXPERT = 1024
SWIGLU_LIMIT = 7.0
SWIGLU_ALPHA = 1.702
NORM_EPS = 1e-6

LANES = 128
HALF_ROPE = MLA_ROPE // 2
ROPE_LO = MLA_NOPE
ROPE_HI = MLA_NOPE + HALF_ROPE
VT_ROWS = 144

ADA_TN = 1024
TS_PROJ = 512
SSD_L = 256
ATT_TQ = 4096
ATT_TK = 256
MOE_TB = 512
BATCH_GROUPS = 2
VMEM_LIMIT = 56 * 1024 * 1024


def _dot(a, b):
    return jnp.dot(a, b, preferred_element_type=F32)


def _dot_nt(a, b):
    return lax.dot_general(a, b, (((1,), (1,)), ((), ())), preferred_element_type=F32)


def _dot_tn(a, b):
    return lax.dot_general(a, b, (((0,), (0,)), ((), ())), preferred_element_type=F32)


def _split2(x):
    hi = x.astype(BF16)
    lo = (x - hi.astype(F32)).astype(BF16)
    return hi, lo


def _split3(x):
    h1 = x.astype(BF16)
    r1 = x - h1.astype(F32)
    h2 = r1.astype(BF16)
    h3 = (r1 - h2.astype(F32)).astype(BF16)
    return h1, h2, h3


def _dot3(a, b):
    ah, al = _split2(a)
    bh, bl = _split2(b)
    return _dot(ah, bh) + _dot(ah, bl) + _dot(al, bh)


def _rms(x):
    return x * lax.rsqrt(jnp.mean(x * x, axis=-1, keepdims=True) + NORM_EPS)


def _silu(x):
    hx = 0.5 * x
    return hx + hx * jnp.tanh(hx)


def _const_spec(shape):
    nd = len(shape)
    return pl.BlockSpec(shape, lambda *_: (0,) * nd)


def _adaln_body(c_ref, w_ref, b_ref, o_ref):
    o_ref[...] = _dot3(_silu(c_ref[...]), w_ref[...]) + b_ref[...]


def _adaln(c, w_ada, b_ada):
    bsz, d = c.shape
    n = w_ada.shape[1]
    tn = ADA_TN
    return pl.pallas_call(
        _adaln_body,
        grid=(n // tn,),
        in_specs=[_const_spec((bsz, d)),
                  pl.BlockSpec((d, tn), lambda j: (0, j)),
                  pl.BlockSpec((1, tn), lambda j: (0, j))],
        out_specs=pl.BlockSpec((bsz, tn), lambda j: (0, j)),
        out_shape=jax.ShapeDtypeStruct((bsz, n), F32),
        compiler_params=pltpu.CompilerParams(dimension_semantics=("arbitrary",),
                                             vmem_limit_bytes=VMEM_LIMIT),
        name="adaln",
    )(c, w_ada, b_ada.reshape(1, n))


def _rope_block(xb, ct, st, lane):
    partner = jnp.where(lane < ROPE_HI, pltpu.roll(xb, LANES - HALF_ROPE, 1), pltpu.roll(xb, HALF_ROPE, 1))
    return xb * ct + partner * st


def _inproj_body(x_ref, sc_ref, sh_ref, g_ref, tab_ref, wz_ref, wxbc_ref, wsm_ref, wqa_ref, wkva_ref,
                 qn_ref, kvn_ref, wqup_ref, wkup_ref, wvup_ref,
                 z_ref, xbc_ref, dt_ref, q_ref, k_ref, v_ref):
    x = x_ref[0]
    h = _rms(x) * (g_ref[...] * (1.0 + sc_ref[0])) + sh_ref[0]
    hb = h.astype(BF16)
    z_ref[0] = _dot(hb, wz_ref[...]).astype(BF16)
    xbc_ref[0] = _dot(hb, wxbc_ref[...]).astype(BF16)
    sm = _dot(hb, wsm_ref[...])
    dt_ref[0] = sm[:, LANES:]
    tab = tab_ref[0].T
    lane = lax.broadcasted_iota(jnp.int32, tab.shape, 1)
    in_lo = (lane >= ROPE_LO) & (lane < ROPE_HI)
    in_hi = (lane >= ROPE_HI) & (lane < ROPE_HI + HALF_ROPE)
    ct = jnp.where(lane < ROPE_LO, 1.0, jnp.where(in_lo, tab, jnp.where(in_hi, pltpu.roll(tab, HALF_ROPE, 1), 0.0)))
    st = jnp.where(in_lo, -pltpu.roll(tab, LANES - HALF_ROPE, 1), jnp.where(in_hi, tab, 0.0))
    kr =_rope_block(sm[:, :LANES], ct, st, lane)
    qan = (_rms(_dot(hb, wqa_ref[...])) * qn_ref[...]).astype(BF16)
    q = _dot(qan, wqup_ref[...])
    for hh in range(MLA_HEADS):
        blk = slice(hh * LANES, (hh + 1) * LANES)
        q_ref[0, :, blk] = _rope_block(q[:, blk], ct, st, lane).astype(BF16)
    kvn = (_rms(_dot(hb, wkva_ref[...])) * kvn_ref[...]).astype(BF16)
    k = _dot(kvn, wkup_ref[...])
    for hh in range(MLA_HEADS):
        blk = slice(hh * LANES, (hh + 1) * LANES)
        k_ref[0, :, blk] = (k[:, blk] + kr).astype(BF16)
    vt = _dot_nt(wvup_ref[...], kvn)
    vrow = lax.broadcasted_iota(jnp.int32, vt.shape, 0)
    v_ref[0] = jnp.where(vrow % VT_ROWS == LANES, 1.0, vt).astype(BF16)


def _inproj(x, sc1, sh1, gain, rope_tab, wz, wxbc, wsm, wqa, wkva, qn, kvn, wqup, wkup, wvup, *, batch_offset):
    _, seq, d = x.shape
    bsz = sc1.shape[0]
    ts = min(TS_PROJ, seq)
    hw = MLA_HEADS * LANES

    def tok(width):
        return pl.BlockSpec((1, ts, width), lambda b, i: (b, i, 0))

    def tok_full(width):
        return pl.BlockSpec((1, ts, width), lambda b, i: (batch_offset + b, i, 0))

    def per_batch(width):
        return pl.BlockSpec((1, 1, width), lambda b, i: (b, 0, 0))

    weights = (wz, wxbc, wsm, wqa, wkva, qn, kvn, wqup, wkup, wvup)
    out_widths = (SSD_INNER, XBC_DIM, LANES, hw, hw)
    out_dtypes = (BF16, BF16, F32, BF16, BF16)
    vdim = (MLA_HEADS // 2) * VT_ROWS
    out_shape = ([jax.ShapeDtypeStruct((bsz, seq, w), dt) for w, dt in zip(out_widths, out_dtypes)]
                 + [jax.ShapeDtypeStruct((bsz, vdim, seq), BF16)])
    return pl.pallas_call(
        _inproj_body,
        grid=(bsz, seq // ts),
        in_specs=[tok_full(d), per_batch(d), per_batch(d), _const_spec((1, d)),
                  pl.BlockSpec((1, LANES, ts), lambda b, i: (batch_offset + b, 0, i))]
                 + [_const_spec(w.shape) for w in weights],
        out_specs=[tok(w) for w in out_widths] + [pl.BlockSpec((1, vdim, ts), lambda b, i: (b, 0, i))],
        out_shape=out_shape,
        compiler_params=pltpu.CompilerParams(dimension_semantics=("parallel", "parallel"),
                                             vmem_limit_bytes=VMEM_LIMIT),
        name="inproj",
    )(x, sc1, sh1, gain, rope_tab, *weights)


CONV_HALO = 16


def _ssd_body(xc_ref, xp_ref, z_ref, dt_ref, cw_ref, cb_ref, dtb_ref, alog_ref, dsk_ref, ng_ref,
              y_ref, st_scr, *, blk):
    i = pl.program_id(1)

    @pl.when(i == 0)
    def _():
        st_scr[...] = jnp.zeros_like(st_scr)

    xc = xc_ref[0]
    row = lax.broadcasted_iota(jnp.int32, (blk, blk), 0)
    col = lax.broadcasted_iota(jnp.int32, (blk, blk), 1)
    conv = cb_ref[...] + cw_ref[SSD_CONV - 1:SSD_CONV, :] * xc.astype(F32)
    tail = jnp.where(i > 0, xp_ref[0].astype(F32)[CONV_HALO - 8:], 0.0)
    row8 = lax.broadcasted_iota(jnp.int32, tail.shape, 0)
    head_fix = jnp.zeros(tail.shape, F32)
    for shift in range(1, SSD_CONV):
        w = cw_ref[SSD_CONV - 1 - shift:SSD_CONV - shift, :]
        shifted = _dot(jnp.where(row - col == shift, 1.0, 0.0).astype(BF16), xc)
        conv = conv + w * shifted
        head_fix = head_fix + jnp.where(row8 < shift, w * pltpu.roll(tail, shift, 0), 0.0)
    conv = jnp.concatenate([conv[:8] + head_fix, conv[8:]], axis=0)
    xa = _silu(conv)
    xs = xa[:, :SSD_INNER]
    bm = xa[:, SSD_INNER:SSD_INNER + SSD_BC]
    cm = xa[:, SSD_INNER + SSD_BC:]

    hl = lax.broadcasted_iota(jnp.int32, (1, LANES), 1)
    dtr = dt_ref[0] + dtb_ref[...]
    dt = jnp.maximum(dtr, 0.0) + jnp.log(1.0 + jnp.exp(-jnp.abs(dtr)))
    a = jnp.where(hl < SSD_HEADS, -jnp.exp(alog_ref[...]), 0.0)
    dta = dt * a
    row = lax.broadcasted_iota(jnp.int32, (blk, blk), 0)
    col = lax.broadcasted_iota(jnp.int32, (blk, blk), 1)
    tril = row >= col
    trilb = jnp.where(tril, 1.0, 0.0).astype(BF16)
    d1, d2, d3 = _split3(dta)
    cs = _dot(trilb, d1) + _dot(trilb, d2) + _dot(trilb, d3)
    cs_last = cs[blk - 1:blk, :]
    ecs = jnp.exp(cs)
    dte = jnp.exp(cs_last - cs)
    cs_t = cs.T

    er = lax.broadcasted_iota(jnp.int32, (LANES, SSD_INNER), 0)
    ec = lax.broadcasted_iota(jnp.int32, (LANES, SSD_INNER), 1)
    expand = jnp.where(ec // SSD_HEAD_DIM == er, 1.0, 0.0).astype(BF16)

    def per_channel(v):
        vh, vl = _split2(v)
        return _dot(vh, expand) + _dot(vl, expand)

    dt_e = per_channel(dt)
    ecs_e = per_channel(ecs)
    dte_e = per_channel(dte)
    xdt = xs * dt_e
    xdt_b = xdt.astype(BF16)
    xw_b = (xdt * dte_e).astype(BF16)

    gw = SSD_INNER // SSD_GROUPS
    heads_per_group = SSD_HEADS // SSD_GROUPS
    lane = lax.broadcasted_iota(jnp.int32, (blk, LANES), 1)
    y_groups = []
    for g in range(SSD_GROUPS):
        bg = bm[:, g * SSD_STATE:(g + 1) * SSD_STATE].astype(BF16)
        cg = cm[:, g * SSD_STATE:(g + 1) * SSD_STATE].astype(BF16)
        cb = _dot_nt(cg, bg)
        state = st_scr[g]
        y_off = _dot(cg, state.astype(BF16))
        pairs = []
        for j in range(heads_per_group // 2):
            xp = xdt_b[:, g * gw + j * LANES:g * gw + (j + 1) * LANES]
            halves = []
            for u in range(2):
                hidx = g * heads_per_group + 2 * j + u
                seg = cs[:, hidx:hidx + 1] - cs_t[hidx:hidx + 1, :]
                dec = jnp.exp(jnp.where(tril, seg, -jnp.inf))
                halves.append(_dot((cb * dec).astype(BF16), xp))
            pairs.append(jnp.where(lane < SSD_HEAD_DIM, halves[0], halves[1]))
        y_diag = jnp.concatenate(pairs, axis=1)
        y_groups.append(y_diag + y_off * ecs_e[:, g * gw:(g + 1) * gw])
        st_scr[g] = (state * ecs_e[blk - 1:blk, g * gw:(g + 1) * gw]
                     + _dot_tn(bg, xw_b[:, g * gw:(g + 1) * gw]))
    y = jnp.concatenate(y_groups, axis=1) + xs * dsk_ref[...]
    y = y * _silu(z_ref[0].astype(F32))
    y = jnp.concatenate([_rms(y[:, g * gw:(g + 1) * gw]) for g in range(SSD_GROUPS)], axis=1)
    y_ref[0] = (y * ng_ref[...]).astype(BF16)


def _ssd(xbc, z, dt, conv_w, conv_b, dt_bias, a_log, d_skip_e, norm_gain):
    bsz, seq, _ = xbc.shape
    blk = min(SSD_L, seq)
    halo_per_blk = blk // CONV_HALO
    body = functools.partial(_ssd_body, blk=blk)
    out_shape = jax.ShapeDtypeStruct((bsz, seq, SSD_INNER), BF16)
    return pl.pallas_call(
        body,
        grid=(bsz, seq // blk),
        in_specs=[pl.BlockSpec((1, blk, XBC_DIM), lambda b, i: (b, i, 0)),
                  pl.BlockSpec((1, CONV_HALO, XBC_DIM), lambda b, i: (b, jnp.maximum(i * halo_per_blk - 1, 0), 0)),
                  pl.BlockSpec((1, blk, SSD_INNER), lambda b, i: (b, i, 0)),
                  pl.BlockSpec((1, blk, LANES), lambda b, i: (b, i, 0)),
                  _const_spec((SSD_CONV, XBC_DIM)), _const_spec((1, XBC_DIM)),
                  _const_spec((1, LANES)), _const_spec((1, LANES)),
                  _const_spec((1, SSD_INNER)), _const_spec((1, SSD_INNER))],
        out_specs=pl.BlockSpec((1, blk, SSD_INNER), lambda b, i: (b, i, 0)),
        out_shape=out_shape,
        scratch_shapes=[pltpu.VMEM((SSD_GROUPS, SSD_STATE, SSD_INNER // SSD_GROUPS), F32)],
        compiler_params=pltpu.CompilerParams(dimension_semantics=("parallel", "arbitrary"),
                                             vmem_limit_bytes=VMEM_LIMIT),
        name="ssd",
    )(xbc, xbc, z, dt, conv_w, conv_b, dt_bias, a_log, d_skip_e, norm_gain)


def _attn_body(q_ref, k_ref, v_ref, o_ref, *scratch, tq, tk):
    n_streams = 2 * (tq // tk)
    s_scr = (scratch[:n_streams], scratch[n_streams:2 * n_streams])
    acc_scr = scratch[2 * n_streams:]
    qi = pl.program_id(2)
    n_sub = tq // tk
    n_full = qi * n_sub
    krow = lax.broadcasted_iota(jnp.int32, (tk, tk), 0)
    qcol = lax.broadcasted_iota(jnp.int32, (tk, tk), 1)
    diag_ok = krow // CHUNK <= qcol // CHUNK
    vrow = lax.broadcasted_iota(jnp.int32, (LANES, tk), 0)
    streams = [(u, r) for u in range(2) for r in range(n_sub)]
    qs = [q_ref[0, r * tk:(r + 1) * tk, u * LANES:(u + 1) * LANES] for u, r in streams]

    def put_scores(ki, which, slot):
        start = pl.multiple_of(ki * tk, tk)
        k2 = k_ref[0, pl.ds(start, tk), :]
        out = {}
        for si in which:
            u = streams[si][0]
            s = _dot_nt(k2[:, u * LANES:(u + 1) * LANES], qs[si])
            s_scr[slot][si][...] = s
            out[si] = jnp.max(s, axis=0, keepdims=True)
        return out

    def values_t(ki):
        return v_ref[0, :, pl.ds(pl.multiple_of(ki * tk, tk), tk)]

    def softmax_pv(si, m, s_max, slot, vt, masked):
        s = s_scr[slot][si][...]
        if masked:
            s = jnp.where(diag_ok, s, -jnp.inf)
            s_max = jnp.max(s, axis=0, keepdims=True)
        m_new = jnp.maximum(m, s_max)
        alpha = jnp.exp2(m - m_new)
        p = jnp.exp2((s - m_new).astype(BF16))
        acc_scr[si][...] = alpha * acc_scr[si][...] + _dot(vt, p)
        return m_new

    every = list(range(len(streams)))
    for ref in acc_scr:
        ref[...] = jnp.zeros_like(ref)

    def step(j, state):
        ms, s_maxes = state
        for slot in range(2):
            ki = 2 * j + slot
            nxt = put_scores(ki + 1, every, 1 - slot)
            vt = values_t(ki)
            ms = tuple(softmax_pv(si, ms[si], s_maxes[si], slot, vt, False) for si in every)
            s_maxes = tuple(nxt[si] for si in every)
        return ms, s_maxes

    first = put_scores(0, every, 0)
    m_init = jnp.full((1, tk), -jnp.inf, F32)
    ms, s_maxes = lax.fori_loop(0, n_full // 2, step, ((m_init,) * len(streams), tuple(first[si] for si in every)))
    ms = list(ms)
    s_maxes = dict(zip(every, s_maxes))
    for dd in range(n_sub):
        slot = dd % 2
        live = [si for si in every if dd <= streams[si][1]]
        later = [si for si in every if dd + 1 <= streams[si][1]]
        nxt = put_scores(n_full + dd + 1, later, 1 - slot) if later else {}
        vt = values_t(n_full + dd)
        for si in live:
            ms[si] = softmax_pv(si, ms[si], s_maxes[si], slot, vt, dd == streams[si][1])
        s_maxes = nxt
    for r in range(n_sub):
        a0 = acc_scr[streams.index((0, r))][...]
        a1 = acc_scr[streams.index((1, r))][...]
        out_t = jnp.where(vrow < MLA_V, a0[:LANES] / a0[LANES:LANES + 1], a1[:LANES] / a1[LANES:LANES + 1])
        o_ref[0, r * tk:(r + 1) * tk, :] = out_t.T.astype(BF16)


def _attn(q, k, v):
    bsz, seq, _ = q.shape
    tk = min(ATT_TK, seq)
    tq = min(ATT_TQ, seq)
    assert (tq // tk) % 2 == 0, "the two-slot score pipeline needs an even number of query sub-tiles"
    n_streams = 2 * (tq // tk)
    body = functools.partial(_attn_body, tq=tq, tk=tk)
    out_shape = jax.ShapeDtypeStruct((bsz, seq, MLA_HEADS * MLA_V), BF16)
    return pl.pallas_call(
        body,
        grid=(bsz, MLA_HEADS // 2, seq // tq),
        in_specs=[pl.BlockSpec((1, tq, 2 * LANES), lambda b, hp, i: (b, i, hp)),
                  pl.BlockSpec((1, seq, 2 * LANES), lambda b, hp, i: (b, 0, hp)),
                  pl.BlockSpec((1, VT_ROWS, seq), lambda b, hp, i: (b, hp, 0))],
        out_specs=pl.BlockSpec((1, tq, LANES), lambda b, hp, i: (b, i, hp)),
        out_shape=out_shape,
        scratch_shapes=[pltpu.VMEM((tk, tk), F32)] * (2 * n_streams) + [pltpu.VMEM((VT_ROWS, tk), F32)] * n_streams,
        compiler_params=pltpu.CompilerParams(dimension_semantics=("parallel", "parallel", "arbitrary"),
                                             vmem_limit_bytes=VMEM_LIMIT),
        name="attn",
    )(q, k, v)


def _outproj_body(ys_ref, ya_ref, x_ref, g1_ref, sc2_ref, sh2_ref, mn_ref, wo1_ref, wo2_ref, pmn_ref, pfn_ref,
                  wr_ref, br_ref, x1_ref, h2_ref, gate_ref, idx_ref, cnt_ref, pref_ref, ltab_ref, cnt_scr):
    first = (pl.program_id(0) == 0) & (pl.program_id(1) == 0)

    @pl.when(first)
    def _():
        cnt_scr[...] = jnp.zeros_like(cnt_scr)

    yan =(_rms(ya_ref[0].astype(F32)) * mn_ref[...]).astype(BF16)
    mix = _dot(ys_ref[0], wo1_ref[...]) + _dot(yan, wo2_ref[...])
    x1 = x_ref[0] + g1_ref[0] * (_rms(mix) * pmn_ref[...])
    x1_ref[0] = x1
    h2 = _rms(x1) * (pfn_ref[...] * (1.0 + sc2_ref[0])) + sh2_ref[0]
    h2_ref[0] = h2.astype(BF16)
    h_hi, h_lo = _split2(h2)
    w_hi, w_lo = _split2(wr_ref[...])
    both = _dot(h_hi, jnp.concatenate([w_hi, w_lo], axis=1))
    logits = both[:, :LANES] + both[:, LANES:] + _dot(h_lo, w_hi) + br_ref[...]
    lane = lax.broadcasted_iota(jnp.int32, logits.shape, 1)
    cur = jnp.where(lane < N_EXPERTS, logits, -jnp.inf)
    vals, idxs = [], []
    for _ in range(TOP_K):
        m = jnp.max(cur, axis=-1, keepdims=True)
        ix = jnp.min(jnp.where(cur == m, lane, LANES), axis=-1, keepdims=True)
        vals.append(m)
        idxs.append(ix)
        cur = jnp.where(lane == ix, -jnp.inf, cur)
    es = [jnp.exp(v - vals[0]) for v in vals]
    denom = es[0]
    for e in es[1:]:
        denom = denom + e
    onehot = jnp.zeros(logits.shape, F32)
    for kk in range(TOP_K):
        onehot = onehot + jnp.where(lane == idxs[kk], 1.0, 0.0)
    ts = logits.shape[0]
    row = lax.broadcasted_iota(jnp.int32, (ts, ts), 0)
    col = lax.broadcasted_iota(jnp.int32, (ts, ts), 1)
    before = jnp.where(row > col, 1.0, 0.0).astype(BF16)
    prior = _dot(before, onehot.astype(BF16))
    seen = cnt_scr[...]
    pref_ref[0] = seen
    gate_out = jnp.zeros(logits.shape, F32)
    idx_out = jnp.zeros(logits.shape, jnp.int32)
    local_out = jnp.zeros(logits.shape, F32)
    for kk in range(TOP_K):
        mine = lane == idxs[kk]
        local = jnp.sum(jnp.where(mine, prior, 0.0), axis=-1, keepdims=True)
        rank = local + jnp.sum(jnp.where(mine, seen, 0.0), axis=-1, keepdims=True)
        gate_out = jnp.where(lane == kk, es[kk] / denom, gate_out)
        idx_out = jnp.where(lane == kk, idxs[kk], idx_out)
        idx_out = jnp.where(lane == TOP_K + kk, rank.astype(jnp.int32), idx_out)
        local_out = jnp.where(lane == kk, local, local_out)
    gate_ref[0] = gate_out
    idx_ref[0] = idx_out
    local_rows = local_out.T.astype(jnp.int32)
    tok = lax.broadcasted_iota(jnp.int32, logits.shape, 0)
    tok_hi = (tok // 16).astype(F32)
    tok_lo = (tok % 16).astype(F32)
    ltab = jnp.zeros((ts, 2 * LANES), F32)
    for kk in range(TOP_K):
        mine = lane == idxs[kk]
        at_rank = jnp.where(row == local_rows[kk:kk + 1, :], 1.0, 0.0).astype(BF16)
        tagged = jnp.concatenate([jnp.where(mine, tok_hi, 0.0), jnp.where(mine, tok_lo, 0.0)], axis=1)
        ltab = ltab + _dot(at_rank, tagged.astype(BF16))
    ltab_ref[0] = (16.0 * ltab[:, :LANES] + ltab[:, LANES:]).astype(jnp.int32)
    cnt_scr[...] = seen + jnp.sum(onehot, axis=0, keepdims=True)
    cnt_ref[...] = cnt_scr[...]


def _outproj(y_ssd, y_att, x, g1, sc2, sh2, mla_norm, wo1, wo2, post_mix_norm, pre_ffn_norm, wr, br, *, batch_offset):
    bsz, seq, _ = y_ssd.shape
    d = x.shape[-1]
    ts = min(TS_PROJ, seq)
    tiles = seq // ts

    def tok(width):
        return pl.BlockSpec((1, ts, width), lambda b, i: (b, i, 0))

    x_spec = pl.BlockSpec((1, ts, d), lambda b, i: (batch_offset + b, i, 0))

    def per_batch(width):
        return pl.BlockSpec((1, 1, width), lambda b, i: (b, 0, 0))

    consts = (mla_norm, wo1, wo2, post_mix_norm, pre_ffn_norm, wr, br)
    out_shape = [jax.ShapeDtypeStruct((bsz, seq, d), F32), jax.ShapeDtypeStruct((bsz, seq, d), BF16),
                 jax.ShapeDtypeStruct((bsz, seq, LANES), F32), jax.ShapeDtypeStruct((bsz, seq, LANES), jnp.int32),
                 jax.ShapeDtypeStruct((1, LANES), F32),
                 jax.ShapeDtypeStruct((bsz * tiles, 1, LANES), F32),
                 jax.ShapeDtypeStruct((bsz * tiles, ts, LANES), jnp.int32)]
    return pl.pallas_call(
        _outproj_body,
        grid=(bsz, seq // ts),
        in_specs=[tok(SSD_INNER), tok(MLA_HEADS * MLA_V), x_spec, per_batch(d), per_batch(d), per_batch(d)]
                 + [_const_spec(w.shape) for w in consts],
        out_specs=[tok(d), tok(d), tok(LANES), tok(LANES), _const_spec((1, LANES)),
                   pl.BlockSpec((1, 1, LANES), lambda b, i: (b * tiles + i, 0, 0)),
                   pl.BlockSpec((1, ts, LANES), lambda b, i: (b * tiles + i, 0, 0))],
        out_shape=out_shape,
        scratch_shapes=[pltpu.VMEM((1, LANES), F32)],
        compiler_params=pltpu.CompilerParams(dimension_semantics=("arbitrary", "arbitrary"),
                                             vmem_limit_bytes=VMEM_LIMIT),
        name="outproj",
    )(y_ssd, y_att, x, g1, sc2, sh2, *consts)


def _moe_body(be_ref, na_ref, x_ref, wgu_ref, bgu_ref, wd_ref, bd_ref, y_ref, wgu_b, wd_b):
    i = pl.program_id(0)

    @pl.when((i == 0) | (be_ref[i] != be_ref[jnp.maximum(i - 1, 0)]))
    def _():
        wgu_b[...] = wgu_ref[0].astype(BF16)
        wd_b[...] = wd_ref[0].astype(BF16)

    @pl.when(i < na_ref[0])
    def _():
        gu = _dot(x_ref[...], wgu_b[...]) + bgu_ref[0]
        glu = jnp.minimum(gu[:, :D_FF_EXPERT], SWIGLU_LIMIT)
        lin = jnp.clip(gu[:, D_FF_EXPERT:], -SWIGLU_LIMIT, SWIGLU_LIMIT)
        act = glu * jax.nn.sigmoid(SWIGLU_ALPHA * glu) * (lin + 1.0)
        y_ref[...] = (_dot(act.astype(BF16), wd_b[...]) + bd_ref[0]).astype(BF16)

    @pl.when(i >= na_ref[0])
    def _():
        y_ref[...] = jnp.zeros_like(y_ref)


def _moe(block_expert, n_active, xg, wgu, bgu, wd, bd):
    n_slots, d = xg.shape
    n_blocks = n_slots // MOE_TB
    f2 = wgu.shape[2]
    out_shape = jax.ShapeDtypeStruct((n_slots, d), BF16)
    return pl.pallas_call(
        _moe_body,
        grid_spec=pltpu.PrefetchScalarGridSpec(
            num_scalar_prefetch=2,
            grid=(n_blocks,),
            in_specs=[pl.BlockSpec((MOE_TB, d), lambda i, be, na: (i, 0)),
                      pl.BlockSpec((1, d, f2), lambda i, be, na: (be[i], 0, 0)),
                      pl.BlockSpec((1, 1, f2), lambda i, be, na: (be[i], 0, 0)),
                      pl.BlockSpec((1, f2 // 2, d), lambda i, be, na: (be[i], 0, 0)),
                      pl.BlockSpec((1, 1, d), lambda i, be, na: (be[i], 0, 0))],
            out_specs=pl.BlockSpec((MOE_TB, d), lambda i, be, na: (i, 0)),
            scratch_shapes=[pltpu.VMEM((d, f2), BF16), pltpu.VMEM((f2 // 2, d), BF16)],
        ),
        out_shape=out_shape,
        compiler_params=pltpu.CompilerParams(dimension_semantics=("arbitrary",),
                                             vmem_limit_bytes=VMEM_LIMIT),
        name="moe",
    )(block_expert, n_active, xg, wgu, bgu, wd, bd)


def _final_body(x1_ref, y_ref, gate_ref, g2_ref, gain_ref, *rest, group):
    o_ref = rest[-1]

    @pl.when(pl.program_id(0) < group)
    def _():
        gates = gate_ref[0]
        f = gates[:, 0:1] * y_ref[0, 0].astype(F32)
        for kk in range(1, TOP_K):
            f = f + gates[:, kk:kk + 1] * y_ref[kk, 0].astype(F32)
        o_ref[0] = x1_ref[0] + g2_ref[0] * (_rms(f) * gain_ref[...])

    @pl.when(pl.program_id(0) >= group)
    def _():
        o_ref[...] = jnp.zeros_like(o_ref)


def _final(x1, y4, gates, g2, gain, out_prev, batch_offset, total_batch):
    bsz, seq, d = x1.shape
    ts = min(TS_PROJ, seq)
    tiles = seq // ts
    assert out_prev is not None or batch_offset == 0
    rows = bsz if out_prev is not None else total_batch

    def tok(width):
        return pl.BlockSpec((1, ts, width),
                            lambda b, i: (jnp.minimum(b, bsz - 1), jnp.where(b < bsz, i, tiles - 1), 0))

    in_specs = [tok(d),
                pl.BlockSpec((TOP_K, 1, ts, d),
                             lambda b, i: (0, jnp.minimum(b, bsz - 1), jnp.where(b < bsz, i, tiles - 1), 0)),
                tok(LANES), pl.BlockSpec((1, 1, d), lambda b, i: (jnp.minimum(b, bsz - 1), 0, 0)),
                _const_spec((1, d))]
    args = [x1, y4, gates, g2, gain]
    aliases = {}
    if out_prev is not None:
        in_specs.append(pl.BlockSpec(memory_space=pl.ANY))
        args.append(out_prev)
        aliases = {len(args) - 1: 0}
    return pl.pallas_call(
        functools.partial(_final_body, group=bsz),
        grid=(rows, tiles),
        in_specs=in_specs,
        out_specs=pl.BlockSpec((1, ts, d), lambda b, i: (batch_offset + b, i, 0)),
        out_shape=jax.ShapeDtypeStruct((total_batch, seq, d), F32),
        input_output_aliases=aliases,
        compiler_params=pltpu.CompilerParams(dimension_semantics=("parallel", "parallel"),
                                             vmem_limit_bytes=VMEM_LIMIT),
        name="final",
    )(*args)


def _head_blocks(cols):
    out = []
    for c in cols:
        pad = LANES - c.shape[1]
        out.append(jnp.pad(c, ((0, 0), (0, pad))) if pad else c)
    return jnp.concatenate(out, axis=1)


def _prep_mixer_weights(w_in, w_q_up, w_kv_up):
    d = w_in.shape[0]
    wz = w_in[:, OFF_Z:OFF_XBC]
    wxbc = w_in[:, OFF_XBC:OFF_DT]
    wdt = w_in[:, OFF_DT:OFF_QA]
    wqa = w_in[:, OFF_QA:OFF_KVA]
    wkva = w_in[:, OFF_KVA:OFF_KR]
    wkr = w_in[:, OFF_KR:IN_COLS]
    kr_blk = jnp.concatenate([jnp.zeros((d, ROPE_LO), F32), wkr, jnp.zeros((d, LANES - ROPE_LO - MLA_ROPE), F32)], axis=1)
    dt_blk = jnp.pad(wdt, ((0, 0), (0, LANES - SSD_HEADS)))
    wsm = jnp.concatenate([kr_blk, dt_blk], axis=1)
    qh = MLA_NOPE + MLA_ROPE
    scale = math.log2(math.e) / math.sqrt(qh)
    wqup = _head_blocks([w_q_up[:, h * qh:(h + 1) * qh] for h in range(MLA_HEADS)]) * scale
    kvh = MLA_NOPE + MLA_V
    wkup = _head_blocks([w_kv_up[:, h * kvh:h * kvh + MLA_NOPE] for h in range(MLA_HEADS)])
    vcols = []
    for h in range(MLA_HEADS):
        vcols.append(w_kv_up[:, h * kvh + MLA_NOPE:(h + 1) * kvh])
        if h % 2 == 1:
            vcols.append(jnp.zeros((w_kv_up.shape[0], VT_ROWS - LANES), F32))
    wvup = jnp.concatenate(vcols, axis=1).T
    return tuple(w.astype(BF16) for w in (wz, wxbc, wsm, wqa, wkva)) + tuple(w.astype(BF16) for w in (wqup, wkup, wvup))


def _rope_tables(positions):
    inv_freq = ROPE_BASE ** (-(jnp.arange(HALF_ROPE, dtype=F32) * 2.0 / MLA_ROPE))
    angles = positions.astype(F32)[:, None, :] * inv_freq[None, :, None]
    bsz, _, seq = angles.shape
    return jnp.concatenate([jnp.zeros((bsz, ROPE_LO, seq), F32), jnp.cos(angles), jnp.sin(angles),
                            jnp.zeros((bsz, LANES - ROPE_LO - MLA_ROPE, seq), F32)], axis=1)


def _route(idx, rank, counts, tile_seen, ltab, n_tok):
    n_assign = n_tok * TOP_K
    n_tiles, tile, lanes = ltab.shape
    padded = ((counts + MOE_TB - 1) // MOE_TB) * MOE_TB
    padded_end = jnp.cumsum(padded)
    padded_start = padded_end - padded
    experts = jnp.arange(N_EXPERTS, dtype=jnp.int32)
    start_of = jnp.sum(jnp.where(idx[..., None] == experts, padded_start, 0), axis=-1)
    dest = (start_of + rank).reshape(-1)
    n_blocks = n_assign // MOE_TB + N_EXPERTS
    block_start = jnp.arange(n_blocks, dtype=jnp.int32) * MOE_TB
    block_expert = jnp.minimum(jnp.sum((padded_end[None, :] <= block_start[:, None]).astype(jnp.int32), axis=1),
                               N_EXPERTS - 1)
    n_active = (padded_end[-1] // MOE_TB).astype(jnp.int32).reshape(1)
    j = (block_start - padded_start[block_expert])[:, None] + jnp.arange(MOE_TB, dtype=jnp.int32)[None, :]
    seen_blk = tile_seen.T[block_expert]
    reached = seen_blk[:, None, :] <= j[:, :, None]
    tau = jnp.sum(reached.astype(jnp.int32), axis=-1) - 1
    j_local = j - jnp.max(jnp.where(reached, seen_blk[:, None, :], 0), axis=-1)
    valid = j < counts[block_expert][:, None]
    flat = (tau * tile + j_local) * lanes + block_expert[:, None]
    tok_local = ltab.reshape(-1)[jnp.clip(flat, 0, n_tiles * tile * lanes - 1)]
    filler = (block_start[:, None] + jnp.arange(MOE_TB, dtype=jnp.int32)[None, :]) % n_tok
    slot_tok = jnp.where(valid, tau * tile + tok_local, filler).reshape(-1)
    return dest, slot_tok, block_expert, n_active


def kernel(x, c, positions, w_ada, b_ada, pre_mix_norm, w_in, conv_w, conv_b, dt_bias, a_log, d_skip, ssd_norm, q_a_norm, w_q_up, kv_a_norm, w_kv_up, mla_norm, w_out, post_mix_norm, pre_ffn_norm, w_router, b_router, w_gate_up, b_gate_up, w_down, b_down, post_ffn_norm):
    bsz, seq, d = x.shape
    rope_tab = _rope_tables(positions)
    n_groups = BATCH_GROUPS if bsz % BATCH_GROUPS == 0 else 1
    gb = bsz // n_groups
    n_tok = gb * seq
    assert d == D_MODEL and seq % min(TS_PROJ, seq) == 0 and seq % min(SSD_L, seq) == 0
    assert seq % min(ATT_TQ, seq) == 0 and min(ATT_TQ, seq) % min(ATT_TK, seq) == 0
    assert (n_tok * TOP_K) % MOE_TB == 0
    pad_h = LANES - SSD_HEADS
    for l in range(w_ada.shape[0]):
        mod = _adaln(c, w_ada[l], b_ada[l])
        mods = [m.reshape(bsz, 1, d) for m in jnp.split(mod, 6, axis=-1)]
        mixer_w = _prep_mixer_weights(w_in[l], w_q_up[l], w_kv_up[l])
        wo = w_out[l].astype(BF16)
        wr = jnp.pad(w_router[l], ((0, 0), (0, LANES - N_EXPERTS)))
        br = jnp.pad(b_router[l], (0, LANES - N_EXPERTS)).reshape(1, LANES)
        dtb = jnp.pad(dt_bias[l], (0, pad_h)).reshape(1, LANES)
        alog = jnp.pad(a_log[l], (0, pad_h)).reshape(1, LANES)
        dsk = jnp.repeat(d_skip[l], SSD_HEAD_DIM).reshape(1, -1)
        out = None
        pending = None

        def experts_and_final(p, xs, out):
            y = _moe(p["block_expert"], p["n_active"], xs, w_gate_up[l], b_gate_up[l][:, None, :],
                     w_down[l], b_down[l][:, None, :])
            y4 = y[p["dest"].reshape(n_tok, TOP_K).T.reshape(-1)].reshape(TOP_K, gb, seq, d)
            return _final(p["x1"], y4, p["gates"], p["g2"], post_ffn_norm[l].reshape(1, d), out, p["offset"], bsz)

        for gi in range(n_groups):
            grp = slice(gi * gb, (gi + 1) * gb)
            sh1, sc1, g1, sh2, sc2, g2 = [m[grp] for m in mods]
            if pending is not None:
                sc1 = sc1 + jnp.where(pending["slot_tok"][0] < 0, 1.0, 0.0)
            z, xbc, dt, q, k, v = _inproj(x, sc1, sh1, pre_mix_norm[l].reshape(1, d), rope_tab, *mixer_w[:5],
                                          q_a_norm[l].reshape(1, -1), kv_a_norm[l].reshape(1, -1), *mixer_w[5:],
                                          batch_offset=gi * gb)
            y_ssd = _ssd(xbc, z, dt, conv_w[l], conv_b[l].reshape(1, -1), dtb, alog, dsk, ssd_norm[l].reshape(1, -1))
            prev_xs = None
            if pending is not None:
                ssd_bits = lax.bitcast_convert_type(y_ssd[0, 0, 0], jnp.uint16).astype(jnp.int32)
                prev_xs = pending["h2"].reshape(n_tok, d)[pending["slot_tok"] + jnp.where(ssd_bits < 0, 1, 0)]
            y_att = _attn(q, k, v)
            x1, h2, gates, route, cnt, seen, ltab = _outproj(
                y_ssd, y_att, x, g1, sc2, sh2, mla_norm[l].reshape(1, -1), wo[:SSD_INNER], wo[SSD_INNER:],
                post_mix_norm[l].reshape(1, d), pre_ffn_norm[l].reshape(1, d), wr, br, batch_offset=gi * gb)
            route = route.reshape(n_tok, LANES)
            counts = cnt[0, :N_EXPERTS].astype(jnp.int32)
            tile_seen = seen[:, 0, :N_EXPERTS].astype(jnp.int32)
            if prev_xs is not None:
                bits = lax.bitcast_convert_type(prev_xs[0, 0], jnp.uint16).astype(jnp.int32)
                tile_seen = tile_seen + jnp.where(bits < 0, 1, 0)
            dest, slot_tok, block_expert, n_active = _route(route[:, :TOP_K], route[:, TOP_K:2 * TOP_K], counts,
                                                            tile_seen, ltab, n_tok)
            if pending is not None:
                out = experts_and_final(pending, prev_xs, out)
            pending = dict(h2=h2, slot_tok=slot_tok, dest=dest, block_expert=block_expert, n_active=n_active,
                           x1=x1, gates=gates, g2=g2, offset=gi * gb)
        out = experts_and_final(pending, pending["h2"].reshape(n_tok, d)[pending["slot_tok"]], out)
        x = out
    return x
```

```python
import functools
import math

import jax
import jax.numpy as jnp
from jax import lax
from jax.experimental import pallas as pl
from jax.experimental.pallas import tpu as pltpu

F32 = jnp.float32
BF16 = jnp.bfloat16

D_MODEL = 1024
CHUNK = 64
SSD_INNER = 512
SSD_HEAD_DIM = 64
SSD_HEADS = 8
SSD_GROUPS = 2
SSD_STATE = 128
SSD_CONV = 4
SSD_BC = SSD_GROUPS * SSD_STATE
XBC_DIM = SSD_INNER + 2 * SSD_BC
MLA_V = 64
MLA_HEADS = 8
MLA_NOPE = 64
MLA_ROPE = 32
Q_LORA = 384
KV_LORA = 256
ROPE_BASE = 10000.0
OFF_Z = 0
OFF_XBC = OFF_Z + SSD_INNER
OFF_DT = OFF_XBC + XBC_DIM
OFF_QA = OFF_DT + SSD_HEADS
OFF_KVA = OFF_QA + Q_LORA
OFF_KR = OFF_KVA + KV_LORA
IN_COLS = OFF_KR + MLA_ROPE
N_EXPERTS = 32
TOP_K = 4
D_FF_EXPERT = 1024
SWIGLU_LIMIT = 7.0
SWIGLU_ALPHA = 1.702
NORM_EPS = 1e-6

LANES = 128
HALF_ROPE = MLA_ROPE // 2
ROPE_LO = MLA_NOPE
ROPE_HI = MLA_NOPE + HALF_ROPE
VT_ROWS = 144

ADA_TN = 1024
TS_PROJ = 512
SSD_L = 256
ATT_TQ = 4096
ATT_TK = 256
MOE_TB = 512
GROUP_SHARES = (5, 8)
VMEM_LIMIT = 56 * 1024 * 1024


def _dot(a, b):
    return jnp.dot(a, b, preferred_element_type=F32)


def _dot_nt(a, b):
    return lax.dot_general(a, b, (((1,), (1,)), ((), ())), preferred_element_type=F32)


def _dot_tn(a, b):
    return lax.dot_general(a, b, (((0,), (0,)), ((), ())), preferred_element_type=F32)


def _split2(x):
    hi = x.astype(BF16)
    lo = (x - hi.astype(F32)).astype(BF16)
    return hi, lo


def _split3(x):
    h1 = x.astype(BF16)
    r1 = x - h1.astype(F32)
    h2 = r1.astype(BF16)
    h3 = (r1 - h2.astype(F32)).astype(BF16)
    return h1, h2, h3


def _dot3(a, b):
    ah, al = _split2(a)
    bh, bl = _split2(b)
    return _dot(ah, bh) + _dot(ah, bl) + _dot(al, bh)


def _rms(x):
    return x * lax.rsqrt(jnp.mean(x * x, axis=-1, keepdims=True) + NORM_EPS)


def _silu(x):
    hx = 0.5 * x
    return hx + hx * jnp.tanh(hx)


def _const_spec(shape):
    nd = len(shape)
    return pl.BlockSpec(shape, lambda *_: (0,) * nd)


def _adaln_body(c_ref, w_ref, b_ref, o_ref):
    o_ref[...] = _dot3(_silu(c_ref[...]), w_ref[...]) + b_ref[...]


def _adaln(c, w_ada, b_ada):
    bsz, d = c.shape
    n = w_ada.shape[1]
    tn = ADA_TN
    return pl.pallas_call(
        _adaln_body,
        grid=(n // tn,),
        in_specs=[_const_spec((bsz, d)),
                  pl.BlockSpec((d, tn), lambda j: (0, j)),
                  pl.BlockSpec((1, tn), lambda j: (0, j))],
        out_specs=pl.BlockSpec((bsz, tn), lambda j: (0, j)),
        out_shape=jax.ShapeDtypeStruct((bsz, n), F32),
        compiler_params=pltpu.CompilerParams(dimension_semantics=("arbitrary",),
                                             vmem_limit_bytes=VMEM_LIMIT),
        name="adaln",
    )(c, w_ada, b_ada.reshape(1, n))


def _rope_block(xb, ct, st, lane):
    partner = jnp.where(lane < ROPE_HI, pltpu.roll(xb, LANES - HALF_ROPE, 1), pltpu.roll(xb, HALF_ROPE, 1))
    return xb * ct + partner * st


def _inproj_body(x_ref, sc_ref, sh_ref, g_ref, tab_ref, wz_ref, wxbc_ref, wsm_ref, wqa_ref, wkva_ref,
                 qn_ref, kvn_ref, wqup_ref, wkup_ref, wvup_ref,
                 z_ref, xbc_ref, dt_ref, q_ref, k_ref, v_ref):
    x = x_ref[0]
    h = _rms(x) * (g_ref[...] * (1.0 + sc_ref[0])) + sh_ref[0]
    hb = h.astype(BF16)
    z_ref[0] = _dot(hb, wz_ref[...]).astype(BF16)
    xbc_ref[0] = _dot(hb, wxbc_ref[...]).astype(BF16)
    sm = _dot(hb, wsm_ref[...])
    dt_ref[0] = sm[:, LANES:]
    tab = tab_ref[0].T
    lane = lax.broadcasted_iota(jnp.int32, tab.shape, 1)
    in_lo = (lane >= ROPE_LO) & (lane < ROPE_HI)
    in_hi = (lane >= ROPE_HI) & (lane < ROPE_HI + HALF_ROPE)
    ct = jnp.where(lane < ROPE_LO, 1.0, jnp.where(in_lo, tab, jnp.where(in_hi, pltpu.roll(tab, HALF_ROPE, 1), 0.0)))
    st = jnp.where(in_lo, -pltpu.roll(tab, LANES - HALF_ROPE, 1), jnp.where(in_hi, tab, 0.0))
    kr =_rope_block(sm[:, :LANES], ct, st, lane)
    qan = (_rms(_dot(hb, wqa_ref[...])) * qn_ref[...]).astype(BF16)
    q = _dot(qan, wqup_ref[...])
    for hh in range(MLA_HEADS):
        blk = slice(hh * LANES, (hh + 1) * LANES)
        q_ref[0, :, blk] = _rope_block(q[:, blk], ct, st, lane).astype(BF16)
    kvn = (_rms(_dot(hb, wkva_ref[...])) * kvn_ref[...]).astype(BF16)
    k = _dot(kvn, wkup_ref[...])
    for hh in range(MLA_HEADS):
        blk = slice(hh * LANES, (hh + 1) * LANES)
        k_ref[0, :, blk] = (k[:, blk] + kr).astype(BF16)
    vt = _dot_nt(wvup_ref[...], kvn)
    vrow = lax.broadcasted_iota(jnp.int32, vt.shape, 0)
    v_ref[0] = jnp.where(vrow % VT_ROWS == LANES, 1.0, vt).astype(BF16)


def _inproj(x, sc1, sh1, gain, rope_tab, wz, wxbc, wsm, wqa, wkva, qn, kvn, wqup, wkup, wvup, *, batch_offset):
    _, seq, d = x.shape
    bsz = sc1.shape[0]
    ts = min(TS_PROJ, seq)
    hw = MLA_HEADS * LANES

    def tok(width):
        return pl.BlockSpec((1, ts, width), lambda b, i: (b, i, 0))

    def tok_full(width):
        return pl.BlockSpec((1, ts, width), lambda b, i: (batch_offset + b, i, 0))

    def per_batch(width):
        return pl.BlockSpec((1, 1, width), lambda b, i: (b, 0, 0))

    weights = (wz, wxbc, wsm, wqa, wkva, qn, kvn, wqup, wkup, wvup)
    out_widths = (SSD_INNER, XBC_DIM, LANES, hw, hw)
    out_dtypes = (BF16, BF16, F32, BF16, BF16)
    vdim = (MLA_HEADS // 2) * VT_ROWS
    out_shape = ([jax.ShapeDtypeStruct((bsz, seq, w), dt) for w, dt in zip(out_widths, out_dtypes)]
                 + [jax.ShapeDtypeStruct((bsz, vdim, seq), BF16)])
    return pl.pallas_call(
        _inproj_body,
        grid=(bsz, seq // ts),
        in_specs=[tok_full(d), per_batch(d), per_batch(d), _const_spec((1, d)),
                  pl.BlockSpec((1, LANES, ts), lambda b, i: (batch_offset + b, 0, i))]
                 + [_const_spec(w.shape) for w in weights],
        out_specs=[tok(w) for w in out_widths] + [pl.BlockSpec((1, vdim, ts), lambda b, i: (b, 0, i))],
        out_shape=out_shape,
        compiler_params=pltpu.CompilerParams(dimension_semantics=("parallel", "parallel"),
                                             vmem_limit_bytes=VMEM_LIMIT),
        name="inproj",
    )(x, sc1, sh1, gain, rope_tab, *weights)


CONV_HALO = 16


def _ssd_body(xc_ref, xp_ref, z_ref, dt_ref, cw_ref, cb_ref, dtb_ref, alog_ref, dsk_ref, ng_ref,
              y_ref, st_scr, *, blk):
    i = pl.program_id(1)

    @pl.when(i == 0)
    def _():
        st_scr[...] = jnp.zeros_like(st_scr)

    xc = xc_ref[0]
    row = lax.broadcasted_iota(jnp.int32, (blk, blk), 0)
    col = lax.broadcasted_iota(jnp.int32, (blk, blk), 1)
    conv = cb_ref[...] + cw_ref[SSD_CONV - 1:SSD_CONV, :] * xc.astype(F32)
    tail = jnp.where(i > 0, xp_ref[0].astype(F32)[CONV_HALO - 8:], 0.0)
    row8 = lax.broadcasted_iota(jnp.int32, tail.shape, 0)
    head_fix = jnp.zeros(tail.shape, F32)
    for shift in range(1, SSD_CONV):
        w = cw_ref[SSD_CONV - 1 - shift:SSD_CONV - shift, :]
        shifted = _dot(jnp.where(row - col == shift, 1.0, 0.0).astype(BF16), xc)
        conv = conv + w * shifted
        head_fix = head_fix + jnp.where(row8 < shift, w * pltpu.roll(tail, shift, 0), 0.0)
    conv = jnp.concatenate([conv[:8] + head_fix, conv[8:]], axis=0)
    xa = _silu(conv)
    xs = xa[:, :SSD_INNER]
    bm = xa[:, SSD_INNER:SSD_INNER + SSD_BC]
    cm = xa[:, SSD_INNER + SSD_BC:]

    hl = lax.broadcasted_iota(jnp.int32, (1, LANES), 1)
    dtr = dt_ref[0] + dtb_ref[...]
    dt = jnp.maximum(dtr, 0.0) + jnp.log(1.0 + jnp.exp(-jnp.abs(dtr)))
    a = jnp.where(hl < SSD_HEADS, -jnp.exp(alog_ref[...]), 0.0)
    dta = dt * a
    row = lax.broadcasted_iota(jnp.int32, (blk, blk), 0)
    col = lax.broadcasted_iota(jnp.int32, (blk, blk), 1)
    tril = row >= col
    trilb = jnp.where(tril, 1.0, 0.0).astype(BF16)
    d1, d2, d3 = _split3(dta)
    cs = _dot(trilb, d1) + _dot(trilb, d2) + _dot(trilb, d3)
    cs_last = cs[blk - 1:blk, :]
    ecs = jnp.exp(cs)
    dte = jnp.exp(cs_last - cs)
    cs_t = cs.T

    er = lax.broadcasted_iota(jnp.int32, (LANES, SSD_INNER), 0)
    ec = lax.broadcasted_iota(jnp.int32, (LANES, SSD_INNER), 1)
    expand = jnp.where(ec // SSD_HEAD_DIM == er, 1.0, 0.0).astype(BF16)

    def per_channel(v):
        vh, vl = _split2(v)
        return _dot(vh, expand) + _dot(vl, expand)

    dt_e = per_channel(dt)
    ecs_e = per_channel(ecs)
    dte_e = per_channel(dte)
    xdt = xs * dt_e
    xdt_b = xdt.astype(BF16)
    xw_b = (xdt * dte_e).astype(BF16)

    gw = SSD_INNER // SSD_GROUPS
    heads_per_group = SSD_HEADS // SSD_GROUPS
    lane = lax.broadcasted_iota(jnp.int32, (blk, LANES), 1)
    y_groups = []
    for g in range(SSD_GROUPS):
        bg = bm[:, g * SSD_STATE:(g + 1) * SSD_STATE].astype(BF16)
        cg = cm[:, g * SSD_STATE:(g + 1) * SSD_STATE].astype(BF16)
        cb = _dot_nt(cg, bg)
        state = st_scr[g]
        y_off = _dot(cg, state.astype(BF16))
        pairs = []
        for j in range(heads_per_group // 2):
            xp = xdt_b[:, g * gw + j * LANES:g * gw + (j + 1) * LANES]
            halves = []
            for u in range(2):
                hidx = g * heads_per_group + 2 * j + u
                seg = cs[:, hidx:hidx + 1] - cs_t[hidx:hidx + 1, :]
                dec = jnp.exp(jnp.where(tril, seg, -jnp.inf))
                halves.append(_dot((cb * dec).astype(BF16), xp))
            pairs.append(jnp.where(lane < SSD_HEAD_DIM, halves[0], halves[1]))
        y_diag = jnp.concatenate(pairs, axis=1)
        y_groups.append(y_diag + y_off * ecs_e[:, g * gw:(g + 1) * gw])
        st_scr[g] = (state * ecs_e[blk - 1:blk, g * gw:(g + 1) * gw]
                     + _dot_tn(bg, xw_b[:, g * gw:(g + 1) * gw]))
    y = jnp.concatenate(y_groups, axis=1) + xs * dsk_ref[...]
    y = y * _silu(z_ref[0].astype(F32))
    y = jnp.concatenate([_rms(y[:, g * gw:(g + 1) * gw]) for g in range(SSD_GROUPS)], axis=1)
    y_ref[0] = (y * ng_ref[...]).astype(BF16)


def _ssd(xbc, z, dt, conv_w, conv_b, dt_bias, a_log, d_skip_e, norm_gain):
    bsz, seq, _ = xbc.shape
    blk = min(SSD_L, seq)
    halo_per_blk = blk // CONV_HALO
    body = functools.partial(_ssd_body, blk=blk)
    out_shape = jax.ShapeDtypeStruct((bsz, seq, SSD_INNER), BF16)
    return pl.pallas_call(
        body,
        grid=(bsz, seq // blk),
        in_specs=[pl.BlockSpec((1, blk, XBC_DIM), lambda b, i: (b, i, 0)),
                  pl.BlockSpec((1, CONV_HALO, XBC_DIM), lambda b, i: (b, jnp.maximum(i * halo_per_blk - 1, 0), 0)),
                  pl.BlockSpec((1, blk, SSD_INNER), lambda b, i: (b, i, 0)),
                  pl.BlockSpec((1, blk, LANES), lambda b, i: (b, i, 0)),
                  _const_spec((SSD_CONV, XBC_DIM)), _const_spec((1, XBC_DIM)),
                  _const_spec((1, LANES)), _const_spec((1, LANES)),
                  _const_spec((1, SSD_INNER)), _const_spec((1, SSD_INNER))],
        out_specs=pl.BlockSpec((1, blk, SSD_INNER), lambda b, i: (b, i, 0)),
        out_shape=out_shape,
        scratch_shapes=[pltpu.VMEM((SSD_GROUPS, SSD_STATE, SSD_INNER // SSD_GROUPS), F32)],
        compiler_params=pltpu.CompilerParams(dimension_semantics=("parallel", "arbitrary"),
                                             vmem_limit_bytes=VMEM_LIMIT),
        name="ssd",
    )(xbc, xbc, z, dt, conv_w, conv_b, dt_bias, a_log, d_skip_e, norm_gain)


def _attn_body(q_ref, k_ref, v_ref, o_ref, *scratch, tq, tk):
    n_streams = 2 * (tq // tk)
    s_scr = (scratch[:n_streams], scratch[n_streams:2 * n_streams])
    acc_scr = scratch[2 * n_streams:]
    qi = pl.program_id(2)
    n_sub = tq // tk
    n_full = qi * n_sub
    krow = lax.broadcasted_iota(jnp.int32, (tk, tk), 0)
    qcol = lax.broadcasted_iota(jnp.int32, (tk, tk), 1)
    diag_ok = krow // CHUNK <= qcol // CHUNK
    vrow = lax.broadcasted_iota(jnp.int32, (LANES, tk), 0)
    streams = [(u, r) for u in range(2) for r in range(n_sub)]
    qs = [q_ref[0, r * tk:(r + 1) * tk, u * LANES:(u + 1) * LANES] for u, r in streams]

    def put_scores(ki, which, slot):
        start = pl.multiple_of(ki * tk, tk)
        k2 = k_ref[0, pl.ds(start, tk), :]
        out = {}
        for si in which:
            u = streams[si][0]
            s = _dot_nt(k2[:, u * LANES:(u + 1) * LANES], qs[si])
            s_scr[slot][si][...] = s
            out[si] = jnp.max(s, axis=0, keepdims=True)
        return out

    def values_t(ki):
        return v_ref[0, :, pl.ds(pl.multiple_of(ki * tk, tk), tk)]

    def softmax_pv(si, m, s_max, slot, vt, masked):
        s = s_scr[slot][si][...]
        if masked:
            s = jnp.where(diag_ok, s, -jnp.inf)
            s_max = jnp.max(s, axis=0, keepdims=True)
        m_new = jnp.maximum(m, s_max)
        alpha = jnp.exp2(m - m_new)
        p = jnp.exp2((s - m_new).astype(BF16))
        acc_scr[si][...] = alpha * acc_scr[si][...] + _dot(vt, p)
        return m_new

    every = list(range(len(streams)))
    for ref in acc_scr:
        ref[...] = jnp.zeros_like(ref)

    def step(j, state):
        ms, s_maxes = state
        for slot in range(2):
            ki = 2 * j + slot
            nxt = put_scores(ki + 1, every, 1 - slot)
            vt = values_t(ki)
            ms = tuple(softmax_pv(si, ms[si], s_maxes[si], slot, vt, False) for si in every)
            s_maxes = tuple(nxt[si] for si in every)
        return ms, s_maxes

    first = put_scores(0, every, 0)
    m_init = jnp.full((1, tk), -jnp.inf, F32)
    ms, s_maxes = lax.fori_loop(0, n_full // 2, step, ((m_init,) * len(streams), tuple(first[si] for si in every)))
    ms = list(ms)
    s_maxes = dict(zip(every, s_maxes))
    for dd in range(n_sub):
        slot = dd % 2
        live = [si for si in every if dd <= streams[si][1]]
        later = [si for si in every if dd + 1 <= streams[si][1]]
        nxt = put_scores(n_full + dd + 1, later, 1 - slot) if later else {}
        vt = values_t(n_full + dd)
        for si in live:
            ms[si] = softmax_pv(si, ms[si], s_maxes[si], slot, vt, dd == streams[si][1])
        s_maxes = nxt
    for r in range(n_sub):
        a0 = acc_scr[streams.index((0, r))][...]
        a1 = acc_scr[streams.index((1, r))][...]
        out_t = jnp.where(vrow < MLA_V, a0[:LANES] / a0[LANES:LANES + 1], a1[:LANES] / a1[LANES:LANES + 1])
        o_ref[0, r * tk:(r + 1) * tk, :] = out_t.T.astype(BF16)


def _attn(q, k, v):
    bsz, seq, _ = q.shape
    tk = min(ATT_TK, seq)
    tq = min(ATT_TQ, seq)
    assert (tq // tk) % 2 == 0, "the two-slot score pipeline needs an even number of query sub-tiles"
    n_streams = 2 * (tq // tk)
    body = functools.partial(_attn_body, tq=tq, tk=tk)
    out_shape = jax.ShapeDtypeStruct((bsz, seq, MLA_HEADS * MLA_V), BF16)
    return pl.pallas_call(
        body,
        grid=(bsz, MLA_HEADS // 2, seq // tq),
        in_specs=[pl.BlockSpec((1, tq, 2 * LANES), lambda b, hp, i: (b, i, hp)),
                  pl.BlockSpec((1, seq, 2 * LANES), lambda b, hp, i: (b, 0, hp)),
                  pl.BlockSpec((1, VT_ROWS, seq), lambda b, hp, i: (b, hp, 0))],
        out_specs=pl.BlockSpec((1, tq, LANES), lambda b, hp, i: (b, i, hp)),
        out_shape=out_shape,
        scratch_shapes=[pltpu.VMEM((tk, tk), F32)] * (2 * n_streams) + [pltpu.VMEM((VT_ROWS, tk), F32)] * n_streams,
        compiler_params=pltpu.CompilerParams(dimension_semantics=("parallel", "parallel", "arbitrary"),
                                             vmem_limit_bytes=VMEM_LIMIT),
        name="attn",
    )(q, k, v)


def _outproj_body(ys_ref, ya_ref, x_ref, g1_ref, sc2_ref, sh2_ref, mn_ref, wo1_ref, wo2_ref, pmn_ref, pfn_ref,
                  wr_ref, br_ref, x1_ref, h2_ref, gate_ref, idx_ref, cnt_ref, pref_ref, ltab_ref, cnt_scr):
    first = (pl.program_id(0) == 0) & (pl.program_id(1) == 0)

    @pl.when(first)
    def _():
        cnt_scr[...] = jnp.zeros_like(cnt_scr)

    yan =(_rms(ya_ref[0].astype(F32)) * mn_ref[...]).astype(BF16)
    mix = _dot(ys_ref[0], wo1_ref[...]) + _dot(yan, wo2_ref[...])
    x1 = x_ref[0] + g1_ref[0] * (_rms(mix) * pmn_ref[...])
    x1_ref[0] = x1
    h2 = _rms(x1) * (pfn_ref[...] * (1.0 + sc2_ref[0])) + sh2_ref[0]
    h2_ref[0] = h2.astype(BF16)
    h_hi, h_lo = _split2(h2)
    w_hi, w_lo = _split2(wr_ref[...])
    both = _dot(h_hi, jnp.concatenate([w_hi, w_lo], axis=1))
    logits = both[:, :LANES] + both[:, LANES:] + _dot(h_lo, w_hi) + br_ref[...]
    lane = lax.broadcasted_iota(jnp.int32, logits.shape, 1)
    cur = jnp.where(lane < N_EXPERTS, logits, -jnp.inf)
    vals, idxs = [], []
    for _ in range(TOP_K):
        m = jnp.max(cur, axis=-1, keepdims=True)
        ix = jnp.min(jnp.where(cur == m, lane, LANES), axis=-1, keepdims=True)
        vals.append(m)
        idxs.append(ix)
        cur = jnp.where(lane == ix, -jnp.inf, cur)
    es = [jnp.exp(v - vals[0]) for v in vals]
    denom = es[0]
    for e in es[1:]:
        denom = denom + e
    onehot = jnp.zeros(logits.shape, F32)
    for kk in range(TOP_K):
        onehot = onehot + jnp.where(lane == idxs[kk], 1.0, 0.0)
    ts = logits.shape[0]
    row = lax.broadcasted_iota(jnp.int32, (ts, ts), 0)
    col = lax.broadcasted_iota(jnp.int32, (ts, ts), 1)
    before = jnp.where(row > col, 1.0, 0.0).astype(BF16)
    prior = _dot(before, onehot.astype(BF16))
    seen = cnt_scr[...]
    pref_ref[0] = seen
    gate_out = jnp.zeros(logits.shape, F32)
    idx_out = jnp.zeros(logits.shape, jnp.int32)
    local_out = jnp.zeros(logits.shape, F32)
    for kk in range(TOP_K):
        mine = lane == idxs[kk]
        local = jnp.sum(jnp.where(mine, prior, 0.0), axis=-1, keepdims=True)
        rank = local + jnp.sum(jnp.where(mine, seen, 0.0), axis=-1, keepdims=True)
        gate_out = jnp.where(lane == kk, es[kk] / denom, gate_out)
        idx_out = jnp.where(lane == kk, idxs[kk], idx_out)
        idx_out = jnp.where(lane == TOP_K + kk, rank.astype(jnp.int32), idx_out)
        local_out = jnp.where(lane == kk, local, local_out)
    gate_ref[0] = gate_out
    idx_ref[0] = idx_out
    local_rows = local_out.T.astype(jnp.int32)
    tok = lax.broadcasted_iota(jnp.int32, logits.shape, 0)
    tok_hi = (tok // 16).astype(F32)
    tok_lo = (tok % 16).astype(F32)
    ltab = jnp.zeros((ts, 2 * LANES), F32)
    for kk in range(TOP_K):
        mine = lane == idxs[kk]
        at_rank = jnp.where(row == local_rows[kk:kk + 1, :], 1.0, 0.0).astype(BF16)
        tagged = jnp.concatenate([jnp.where(mine, tok_hi, 0.0), jnp.where(mine, tok_lo, 0.0)], axis=1)
        ltab = ltab + _dot(at_rank, tagged.astype(BF16))
    ltab_ref[0] = (16.0 * ltab[:, :LANES] + ltab[:, LANES:]).astype(jnp.int32)
    cnt_scr[...] = seen + jnp.sum(onehot, axis=0, keepdims=True)
    cnt_ref[...] = cnt_scr[...]


def _outproj(y_ssd, y_att, x, g1, sc2, sh2, mla_norm, wo1, wo2, post_mix_norm, pre_ffn_norm, wr, br, *, batch_offset):
    bsz, seq, _ = y_ssd.shape
    d = x.shape[-1]
    ts = min(TS_PROJ, seq)
    tiles = seq // ts

    def tok(width):
        return pl.BlockSpec((1, ts, width), lambda b, i: (b, i, 0))

    x_spec = pl.BlockSpec((1, ts, d), lambda b, i: (batch_offset + b, i, 0))

    def per_batch(width):
        return pl.BlockSpec((1, 1, width), lambda b, i: (b, 0, 0))

    consts = (mla_norm, wo1, wo2, post_mix_norm, pre_ffn_norm, wr, br)
    out_shape = [jax.ShapeDtypeStruct((bsz, seq, d), F32), jax.ShapeDtypeStruct((bsz, seq, d), BF16),
                 jax.ShapeDtypeStruct((bsz, seq, LANES), F32), jax.ShapeDtypeStruct((bsz, seq, LANES), jnp.int32),
                 jax.ShapeDtypeStruct((1, LANES), F32),
                 jax.ShapeDtypeStruct((bsz * tiles, 1, LANES), F32),
                 jax.ShapeDtypeStruct((bsz * tiles, ts, LANES), jnp.int32)]
    return pl.pallas_call(
        _outproj_body,
        grid=(bsz, seq // ts),
        in_specs=[tok(SSD_INNER), tok(MLA_HEADS * MLA_V), x_spec, per_batch(d), per_batch(d), per_batch(d)]
                 + [_const_spec(w.shape) for w in consts],
        out_specs=[tok(d), tok(d), tok(LANES), tok(LANES), _const_spec((1, LANES)),
                   pl.BlockSpec((1, 1, LANES), lambda b, i: (b * tiles + i, 0, 0)),
                   pl.BlockSpec((1, ts, LANES), lambda b, i: (b * tiles + i, 0, 0))],
        out_shape=out_shape,
        scratch_shapes=[pltpu.VMEM((1, LANES), F32)],
        compiler_params=pltpu.CompilerParams(dimension_semantics=("arbitrary", "arbitrary"),
                                             vmem_limit_bytes=VMEM_LIMIT),
        name="outproj",
    )(y_ssd, y_att, x, g1, sc2, sh2, *consts)


def _moe_body(be_ref, na_ref, x_ref, wgu_ref, bgu_ref, wd_ref, bd_ref, y_ref, wgu_b, wd_b):
    i = pl.program_id(0)

    @pl.when((i == 0) | (be_ref[i] != be_ref[jnp.maximum(i - 1, 0)]))
    def _():
        wgu_b[...] = wgu_ref[0].astype(BF16)
        wd_b[...] = wd_ref[0].astype(BF16)

    @pl.when(i < na_ref[0])
    def _():
        gu = _dot(x_ref[...], wgu_b[...]) + bgu_ref[0]
        glu = jnp.minimum(gu[:, :D_FF_EXPERT], SWIGLU_LIMIT)
        lin = jnp.clip(gu[:, D_FF_EXPERT:], -SWIGLU_LIMIT, SWIGLU_LIMIT)
        act = glu * jax.nn.sigmoid(SWIGLU_ALPHA * glu) * (lin + 1.0)
        y_ref[...] = (_dot(act.astype(BF16), wd_b[...]) + bd_ref[0]).astype(BF16)

    @pl.when(i >= na_ref[0])
    def _():
        y_ref[...] = jnp.zeros_like(y_ref)


def _moe(block_expert, n_active, xg, wgu, bgu, wd, bd):
    n_slots, d = xg.shape
    n_blocks = n_slots // MOE_TB
    f2 = wgu.shape[2]
    out_shape = jax.ShapeDtypeStruct((n_slots, d), BF16)
    return pl.pallas_call(
        _moe_body,
        grid_spec=pltpu.PrefetchScalarGridSpec(
            num_scalar_prefetch=2,
            grid=(n_blocks,),
            in_specs=[pl.BlockSpec((MOE_TB, d), lambda i, be, na: (i, 0)),
                      pl.BlockSpec((1, d, f2), lambda i, be, na: (be[i], 0, 0)),
                      pl.BlockSpec((1, 1, f2), lambda i, be, na: (be[i], 0, 0)),
                      pl.BlockSpec((1, f2 // 2, d), lambda i, be, na: (be[i], 0, 0)),
                      pl.BlockSpec((1, 1, d), lambda i, be, na: (be[i], 0, 0))],
            out_specs=pl.BlockSpec((MOE_TB, d), lambda i, be, na: (i, 0)),
            scratch_shapes=[pltpu.VMEM((d, f2), BF16), pltpu.VMEM((f2 // 2, d), BF16)],
        ),
        out_shape=out_shape,
        compiler_params=pltpu.CompilerParams(dimension_semantics=("arbitrary",),
                                             vmem_limit_bytes=VMEM_LIMIT),
        name="moe",
    )(block_expert, n_active, xg, wgu, bgu, wd, bd)


def _final_body(x1_ref, y_ref, gate_ref, g2_ref, gain_ref, *rest, group):
    o_ref = rest[-1]

    @pl.when(pl.program_id(0) < group)
    def _():
        gates = gate_ref[0]
        f = gates[:, 0:1] * y_ref[0, 0].astype(F32)
        for kk in range(1, TOP_K):
            f = f + gates[:, kk:kk + 1] * y_ref[kk, 0].astype(F32)
        o_ref[0] = x1_ref[0] + g2_ref[0] * (_rms(f) * gain_ref[...])

    @pl.when(pl.program_id(0) >= group)
    def _():
        o_ref[...] = jnp.zeros_like(o_ref)


def _final(x1, y4, gates, g2, gain, out_prev, batch_offset, total_batch):
    bsz, seq, d = x1.shape
    ts = min(TS_PROJ, seq)
    tiles = seq // ts
    assert out_prev is not None or batch_offset == 0
    rows = bsz if out_prev is not None else total_batch

    def tok(width):
        return pl.BlockSpec((1, ts, width),
                            lambda b, i: (jnp.minimum(b, bsz - 1), jnp.where(b < bsz, i, tiles - 1), 0))

    in_specs = [tok(d),
                pl.BlockSpec((TOP_K, 1, ts, d),
                             lambda b, i: (0, jnp.minimum(b, bsz - 1), jnp.where(b < bsz, i, tiles - 1), 0)),
                tok(LANES), pl.BlockSpec((1, 1, d), lambda b, i: (jnp.minimum(b, bsz - 1), 0, 0)),
                _const_spec((1, d))]
    args = [x1, y4, gates, g2, gain]
    aliases = {}
    if out_prev is not None:
        in_specs.append(pl.BlockSpec(memory_space=pl.ANY))
        args.append(out_prev)
        aliases = {len(args) - 1: 0}
    return pl.pallas_call(
        functools.partial(_final_body, group=bsz),
        grid=(rows, tiles),
        in_specs=in_specs,
        out_specs=pl.BlockSpec((1, ts, d), lambda b, i: (batch_offset + b, i, 0)),
        out_shape=jax.ShapeDtypeStruct((total_batch, seq, d), F32),
        input_output_aliases=aliases,
        compiler_params=pltpu.CompilerParams(dimension_semantics=("parallel", "parallel"),
                                             vmem_limit_bytes=VMEM_LIMIT),
        name="final",
    )(*args)


def _head_blocks(cols):
    out = []
    for c in cols:
        pad = LANES - c.shape[1]
        out.append(jnp.pad(c, ((0, 0), (0, pad))) if pad else c)
    return jnp.concatenate(out, axis=1)


def _prep_mixer_weights(w_in, w_q_up, w_kv_up):
    d = w_in.shape[0]
    wz = w_in[:, OFF_Z:OFF_XBC]
    wxbc = w_in[:, OFF_XBC:OFF_DT]
    wdt = w_in[:, OFF_DT:OFF_QA]
    wqa = w_in[:, OFF_QA:OFF_KVA]
    wkva = w_in[:, OFF_KVA:OFF_KR]
    wkr = w_in[:, OFF_KR:IN_COLS]
    kr_blk = jnp.concatenate([jnp.zeros((d, ROPE_LO), F32), wkr, jnp.zeros((d, LANES - ROPE_LO - MLA_ROPE), F32)], axis=1)
    dt_blk = jnp.pad(wdt, ((0, 0), (0, LANES - SSD_HEADS)))
    wsm = jnp.concatenate([kr_blk, dt_blk], axis=1)
    qh = MLA_NOPE + MLA_ROPE
    scale = math.log2(math.e) / math.sqrt(qh)
    wqup = _head_blocks([w_q_up[:, h * qh:(h + 1) * qh] for h in range(MLA_HEADS)]) * scale
    kvh = MLA_NOPE + MLA_V
    wkup = _head_blocks([w_kv_up[:, h * kvh:h * kvh + MLA_NOPE] for h in range(MLA_HEADS)])
    vcols = []
    for h in range(MLA_HEADS):
        vcols.append(w_kv_up[:, h * kvh + MLA_NOPE:(h + 1) * kvh])
        if h % 2 == 1:
            vcols.append(jnp.zeros((w_kv_up.shape[0], VT_ROWS - LANES), F32))
    wvup = jnp.concatenate(vcols, axis=1).T
    return tuple(w.astype(BF16) for w in (wz, wxbc, wsm, wqa, wkva)) + tuple(w.astype(BF16) for w in (wqup, wkup, wvup))


def _rope_tables(positions):
    inv_freq = ROPE_BASE ** (-(jnp.arange(HALF_ROPE, dtype=F32) * 2.0 / MLA_ROPE))
    angles = positions.astype(F32)[:, None, :] * inv_freq[None, :, None]
    bsz, _, seq = angles.shape
    return jnp.concatenate([jnp.zeros((bsz, ROPE_LO, seq), F32), jnp.cos(angles), jnp.sin(angles),
                            jnp.zeros((bsz, LANES - ROPE_LO - MLA_ROPE, seq), F32)], axis=1)


def _route(idx, rank, counts, tile_seen, ltab, n_tok):
    n_assign = n_tok * TOP_K
    n_tiles, tile, lanes = ltab.shape
    padded = ((counts + MOE_TB - 1) // MOE_TB) * MOE_TB
    padded_end = jnp.cumsum(padded)
    padded_start = padded_end - padded
    experts = jnp.arange(N_EXPERTS, dtype=jnp.int32)
    start_of = jnp.sum(jnp.where(idx[..., None] == experts, padded_start, 0), axis=-1)
    dest = (start_of + rank).reshape(-1)
    n_blocks = n_assign // MOE_TB + N_EXPERTS
    block_start = jnp.arange(n_blocks, dtype=jnp.int32) * MOE_TB
    block_expert = jnp.minimum(jnp.sum((padded_end[None, :] <= block_start[:, None]).astype(jnp.int32), axis=1),
                               N_EXPERTS - 1)
    n_active = (padded_end[-1] // MOE_TB).astype(jnp.int32).reshape(1)
    j = (block_start - padded_start[block_expert])[:, None] + jnp.arange(MOE_TB, dtype=jnp.int32)[None, :]
    seen_blk = tile_seen.T[block_expert]
    reached = seen_blk[:, None, :] <= j[:, :, None]
    tau = jnp.sum(reached.astype(jnp.int32), axis=-1) - 1
    j_local = j - jnp.max(jnp.where(reached, seen_blk[:, None, :], 0), axis=-1)
    valid = j < counts[block_expert][:, None]
    flat = (tau * tile + j_local) * lanes + block_expert[:, None]
    tok_local = ltab.reshape(-1)[jnp.clip(flat, 0, n_tiles * tile * lanes - 1)]
    filler = (block_start[:, None] + jnp.arange(MOE_TB, dtype=jnp.int32)[None, :]) % n_tok
    slot_tok = jnp.where(valid, tau * tile + tok_local, filler).reshape(-1)
    return dest, slot_tok, block_expert, n_active


def kernel(x, c, positions, w_ada, b_ada, pre_mix_norm, w_in, conv_w, conv_b, dt_bias, a_log, d_skip, ssd_norm, q_a_norm, w_q_up, kv_a_norm, w_kv_up, mla_norm, w_out, post_mix_norm, pre_ffn_norm, w_router, b_router, w_gate_up, b_gate_up, w_down, b_down, post_ffn_norm):
    bsz, seq, d = x.shape
    rope_tab = _rope_tables(positions)
    if bsz % GROUP_SHARES[1] == 0 and bsz >= GROUP_SHARES[1]:
        first = bsz * GROUP_SHARES[0] // GROUP_SHARES[1]
        sizes = [first, bsz - first]
    else:
        sizes = [bsz]
    n_groups = len(sizes)
    offsets = [sum(sizes[:gi]) for gi in range(n_groups)]
    assert d == D_MODEL and seq % min(TS_PROJ, seq) == 0 and seq % min(SSD_L, seq) == 0
    assert seq % min(ATT_TQ, seq) == 0 and min(ATT_TQ, seq) % min(ATT_TK, seq) == 0
    assert all((g * seq * TOP_K) % MOE_TB == 0 for g in sizes)
    pad_h = LANES - SSD_HEADS
    for l in range(w_ada.shape[0]):
        mod = _adaln(c, w_ada[l], b_ada[l])
        mods = [m.reshape(bsz, 1, d) for m in jnp.split(mod, 6, axis=-1)]
        mixer_w = _prep_mixer_weights(w_in[l], w_q_up[l], w_kv_up[l])
        wo = w_out[l].astype(BF16)
        wr = jnp.pad(w_router[l], ((0, 0), (0, LANES - N_EXPERTS)))
        br = jnp.pad(b_router[l], (0, LANES - N_EXPERTS)).reshape(1, LANES)
        dtb = jnp.pad(dt_bias[l], (0, pad_h)).reshape(1, LANES)
        alog = jnp.pad(a_log[l], (0, pad_h)).reshape(1, LANES)
        dsk = jnp.repeat(d_skip[l], SSD_HEAD_DIM).reshape(1, -1)
        out = None
        pending = None

        def experts_and_final(p, xs, out):
            y = _moe(p["block_expert"], p["n_active"], xs, w_gate_up[l], b_gate_up[l][:, None, :],
                     w_down[l], b_down[l][:, None, :])
            y4 = y[p["dest"].reshape(p["n_tok"], TOP_K).T.reshape(-1)].reshape(TOP_K, p["gb"], seq, d)
            return _final(p["x1"], y4, p["gates"], p["g2"], post_ffn_norm[l].reshape(1, d), out, p["offset"], bsz)

        for gi in range(n_groups):
            gb, off = sizes[gi], offsets[gi]
            n_tok = gb * seq
            grp = slice(off, off + gb)
            sh1, sc1, g1, sh2, sc2, g2 = [m[grp] for m in mods]
            if pending is not None:
                sc1 = sc1 + jnp.where(pending["slot_tok"][0] < 0, 1.0, 0.0)
            z, xbc, dt, q, k, v = _inproj(x, sc1, sh1, pre_mix_norm[l].reshape(1, d), rope_tab, *mixer_w[:5],
                                          q_a_norm[l].reshape(1, -1), kv_a_norm[l].reshape(1, -1), *mixer_w[5:],
                                          batch_offset=off)
            y_ssd = _ssd(xbc, z, dt, conv_w[l], conv_b[l].reshape(1, -1), dtb, alog, dsk, ssd_norm[l].reshape(1, -1))
            prev_xs = None
            if pending is not None:
                ssd_bits = lax.bitcast_convert_type(y_ssd[0, 0, 0], jnp.uint16).astype(jnp.int32)
                prev_xs = pending["h2"].reshape(pending["n_tok"], d)[pending["slot_tok"] + jnp.where(ssd_bits < 0, 1, 0)]
            y_att = _attn(q, k, v)
            x1, h2, gates, route, cnt, seen, ltab = _outproj(
                y_ssd, y_att, x, g1, sc2, sh2, mla_norm[l].reshape(1, -1), wo[:SSD_INNER], wo[SSD_INNER:],
                post_mix_norm[l].reshape(1, d), pre_ffn_norm[l].reshape(1, d), wr, br, batch_offset=off)
            route = route.reshape(n_tok, LANES)
            counts = cnt[0, :N_EXPERTS].astype(jnp.int32)
            tile_seen = seen[:, 0, :N_EXPERTS].astype(jnp.int32)
            if prev_xs is not None:
                bits = lax.bitcast_convert_type(prev_xs[0, 0], jnp.uint16).astype(jnp.int32)
                tile_seen = tile_seen + jnp.where(bits < 0, 1, 0)
            dest, slot_tok, block_expert, n_active = _route(route[:, :TOP_K], route[:, TOP_K:2 * TOP_K], counts,
                                                            tile_seen, ltab, n_tok)
            if pending is not None:
                out = experts_and_final(pending, prev_xs, out)
            pending = dict(h2=h2, slot_tok=slot_tok, dest=dest, block_expert=block_expert, n_active=n_active,
                           x1=x1, gates=gates, g2=g2, offset=off, gb=gb, n_tok=n_tok)
        out = experts_and_final(pending, pending["h2"].reshape(pending["n_tok"], d)[pending["slot_tok"]], out)
        x = out
    return x
```
